```python
import math
import jax, jax.numpy as jnp
from jax import lax
import numpy as np

D_MODEL = 2048
BATCH = 4
SEQ = 2048
DEPTH = 2
DEC_BATCH = 32
DEC_SEQ = 8
PAST_LEN = 16384
PAGE_SIZE = 128

N_EVEN = (DEPTH + 1) // 2
N_ODD = DEPTH // 2
SSM_HEADS = 32
SSM_HEAD_DIM = 64
SSM_INNER = SSM_HEADS * SSM_HEAD_DIM
SSM_GROUPS = 8
SSM_STATE = 128
SSM_CONV = 4
SSM_CHUNK = 128
SSM_CONV_DIM = SSM_INNER + 2 * SSM_GROUPS * SSM_STATE
LRU_WIDTH = D_MODEL
LRU_BLOCKS = 8
LRU_BLOCK = LRU_WIDTH // LRU_BLOCKS
LRU_CONV = 4
LRU_C = 8.0
ATT_HEADS = 32
KV_HEADS = 8
HEAD_DIM = 64
Q_PER_KV = ATT_HEADS // KV_HEADS
WINDOW = 128
ROPE_THETA = 10000.0
QKV_COLS = (ATT_HEADS + 2 * KV_HEADS) * HEAD_DIM
D_FF = 4 * D_MODEL
EPS = 1e-6
HYB_COLS = SSM_INNER + SSM_CONV_DIM + SSM_HEADS + 2 * LRU_WIDTH
HYB_SPLITS = [SSM_INNER, SSM_INNER + SSM_CONV_DIM, SSM_INNER + SSM_CONV_DIM + SSM_HEADS,
              SSM_INNER + SSM_CONV_DIM + SSM_HEADS + LRU_WIDTH]
MIX_HYB = SSM_INNER + LRU_WIDTH

kernel_name = 'hybrid_ssd_rglru_swa_sink_step'


def rmsnorm(x, g):
    xf = x.astype(jnp.float32)
    y = xf * lax.rsqrt(jnp.mean(xf * xf, axis=-1, keepdims=True) + EPS)
    return (y * g.astype(jnp.float32)).astype(x.dtype)


def adaln(c, w_mod, b_mod):
    m = jax.nn.silu(c) @ w_mod + b_mod
    return jnp.split(m[:, None, :], 6, axis=-1)


def causal_dwconv(x, buf, w, b):
    k = w.shape[0]
    l = x.shape[1]
    xp = jnp.concatenate([buf.astype(x.dtype), x], axis=1)
    y = b + xp[:, k - 1:k - 1 + l] * w[k - 1]
    for j in range(k - 1):
        y = y + xp[:, j:j + l] * w[j]
    return y, xp[:, xp.shape[1] - (k - 1):]


def ssd_chunked(x, dt, a, bmat, cmat, h0):
    f32 = jnp.float32
    b, l, nh, p = x.shape
    g, n = bmat.shape[-2:]
    r = nh // g
    q = math.gcd(l, SSM_CHUNK)
    nc = l // q
    xs = (x.astype(f32) * dt[..., None]).reshape(b, nc, q, g, r, p)
    acs = jnp.cumsum((dt * a).reshape(b, nc, q, g, r), axis=2)
    bc = bmat.astype(f32).reshape(b, nc, q, g, n)
    cc = cmat.astype(f32).reshape(b, nc, q, g, n)
    causal = jnp.tril(jnp.ones((q, q), dtype=bool))[:, :, None, None]
    seg = jnp.exp(jnp.where(causal, acs[:, :, :, None] - acs[:, :, None, :], -jnp.inf))
    cb = jnp.einsum('bctgn,bcsgn->bctsg', cc, bc)
    y_diag = jnp.einsum('bctsgr,bcsgrp->bctgrp', cb[..., None] * seg, xs)
    xw = xs * jnp.exp(acs[:, :, -1:] - acs)[..., None]
    chunk_states = jnp.einsum('bcsgn,bcsgrp->bcgrpn', bc, xw)
    chunk_decay = jnp.exp(acs[:, :, -1])

    def step(h, inp):
        st, dec = inp
        return h * dec[..., None, None] + st, h

    h_last, h_in = lax.scan(step, h0.astype(f32).reshape(b, g, r, p, n),
                            (jnp.moveaxis(chunk_states, 1, 0), jnp.moveaxis(chunk_decay, 1, 0)))
    h_in = jnp.moveaxis(h_in, 0, 1)
    y_off = jnp.einsum('bctgn,bcgrpn->bctgrp', cc, h_in) * jnp.exp(acs)[..., None]
    return (y_diag + y_off).reshape(b, l, nh, p), h_last.reshape(b, nh, p, n)


def mamba2_branch(z, xbc, dt_raw, conv_buf, h0, conv_w, conv_b, dt_bias, a_log, d_skip, norm_g):
    f32 = jnp.float32
    b, l, _ = z.shape
    xbc, new_buf = causal_dwconv(xbc, conv_buf, conv_w, conv_b)
    xbc = jax.nn.silu(xbc)
    xs, bm, cm = jnp.split(xbc, [SSM_INNER, SSM_INNER + SSM_GROUPS * SSM_STATE], axis=-1)
    xs = xs.reshape(b, l, SSM_HEADS, SSM_HEAD_DIM)
    bm = bm.reshape(b, l, SSM_GROUPS, SSM_STATE)
    cm = cm.reshape(b, l, SSM_GROUPS, SSM_STATE)
    dt = jax.nn.softplus(dt_raw.astype(f32) + dt_bias.astype(f32))
    a = -jnp.exp(a_log.astype(f32))
    y, h_last = ssd_chunked(xs, dt, a, bm, cm, h0)
    y = (y + xs.astype(f32) * d_skip.astype(f32)[:, None]).reshape(b, l, SSM_INNER)
    y = (y * jax.nn.silu(z.astype(f32))).reshape(b, l, SSM_GROUPS, SSM_INNER // SSM_GROUPS)
    y = y * lax.rsqrt(jnp.mean(y * y, axis=-1, keepdims=True) + EPS)
    y = y.reshape(b, l, SSM_INNER) * norm_g.astype(f32)
    return y.astype(z.dtype), new_buf, h_last.astype(z.dtype)


def rglru_branch(gate_in, xr, conv_buf, h0, conv_w, conv_b, w_a, b_a, w_x, b_x, lam):
    f32 = jnp.float32
    b, l, _ = xr.shape
    xc, new_buf = causal_dwconv(xr, conv_buf, conv_w, conv_b)
    xb = xc.astype(f32).reshape(b, l, LRU_BLOCKS, LRU_BLOCK)
    gate_r = jax.nn.sigmoid(jnp.einsum('blki,kij->blkj', xb, w_a.astype(f32)) + b_a.astype(f32))
    gate_i = jax.nn.sigmoid(jnp.einsum('blki,kij->blkj', xb, w_x.astype(f32)) + b_x.astype(f32))
    log_a = -LRU_C * gate_r * jax.nn.softplus(-lam.astype(f32)).reshape(LRU_BLOCKS, LRU_BLOCK)
    a = jnp.exp(log_a).reshape(b, l, LRU_WIDTH)
    u = (jnp.sqrt(-jnp.expm1(2.0 * log_a)) * gate_i * xb).reshape(b, l, LRU_WIDTH)

    def step(h, inp):
        a_t, u_t = inp
        h = a_t * h + u_t
        return h, h

    h_last, hs = lax.scan(step, h0.astype(f32), (jnp.moveaxis(a, 1, 0), jnp.moveaxis(u, 1, 0)))
    y = jnp.moveaxis(hs, 0, 1) * jax.nn.gelu(gate_in.astype(f32))
    return y.astype(xr.dtype), new_buf, h_last.astype(xr.dtype)


def rope(x, pos):
    half = x.shape[-1] // 2
    inv_freq = ROPE_THETA ** (-jnp.arange(half, dtype=jnp.float32) / half)
    ang = pos.astype(jnp.float32)[:, None] * inv_freq[None, :]
    cos = jnp.cos(ang)[None, :, None, :]
    sin = jnp.sin(ang)[None, :, None, :]
    xf = x.astype(jnp.float32)
    x1, x2 = xf[..., :half], xf[..., half:]
    return jnp.concatenate([x1 * cos - x2 * sin, x2 * cos + x1 * sin], axis=-1).astype(x.dtype)


def sink_attention(q, k, v, q_pos, k_pos, sinks):
    f32 = jnp.float32
    s = jnp.einsum('bnqhrd,bnkhd->bnhrqk', q.astype(f32), k.astype(f32)) * (HEAD_DIM ** -0.5)
    rel = q_pos[:, :, None] - k_pos[:, None, :]
    valid = (rel >= 0) & (rel < WINDOW)
    s = jnp.where(valid[None, :, None, None], s, -jnp.inf)
    sink = jnp.broadcast_to(sinks.astype(f32).reshape(KV_HEADS, Q_PER_KV)[None, None, :, :, None, None],
                            s.shape[:-1] + (1,))
    p = jax.nn.softmax(jnp.concatenate([s, sink], axis=-1), axis=-1)[..., :-1]
    return jnp.einsum('bnhrqk,bnkhd->bnqhrd', p, v.astype(f32))


def prev_block(t):
    return jnp.concatenate([jnp.zeros_like(t[:, :1]), t[:, :-1]], axis=1)


def attention_mixer(h, pos, kv, w_qkv, b_qkv, sinks, w_out):
    b, l, _ = h.shape
    qkv = h @ w_qkv + b_qkv
    q, k, v = jnp.split(qkv, [ATT_HEADS * HEAD_DIM, (ATT_HEADS + KV_HEADS) * HEAD_DIM], axis=-1)
    q = rope(q.reshape(b, l, ATT_HEADS, HEAD_DIM), pos)
    k = rope(k.reshape(b, l, KV_HEADS, HEAD_DIM), pos)
    v = v.reshape(b, l, KV_HEADS, HEAD_DIM)
    if kv is None:
        nb = l // WINDOW
        qb = q.reshape(b, nb, WINDOW, KV_HEADS, Q_PER_KV, HEAD_DIM)
        kb = k.reshape(b, nb, WINDOW, KV_HEADS, HEAD_DIM)
        vb = v.reshape(b, nb, WINDOW, KV_HEADS, HEAD_DIM)
        kb = jnp.concatenate([prev_block(kb), kb], axis=2)
        vb = jnp.concatenate([prev_block(vb), vb], axis=2)
        qpos = pos.reshape(nb, WINDOW)
        kpos = jnp.concatenate([qpos - WINDOW, qpos], axis=1)
        o = sink_attention(qb, kb, vb, qpos, kpos, sinks)
        n_keep = min(WINDOW, l)
        k_new, v_new = k[:, l - n_keep:], v[:, l - n_keep:]
    else:
        k_buf, v_buf = kv
        n_keep = k_buf.shape[1]
        k_all = jnp.concatenate([k_buf.astype(k.dtype), k], axis=1)
        v_all = jnp.concatenate([v_buf.astype(v.dtype), v], axis=1)
        kpos = jnp.concatenate([pos[0] - n_keep + jnp.arange(n_keep, dtype=pos.dtype), pos])
        o = sink_attention(q.reshape(b, 1, l, KV_HEADS, Q_PER_KV, HEAD_DIM), k_all[:, None], v_all[:, None],
                           pos[None], kpos[None], sinks)
        k_new = k_all[:, k_all.shape[1] - n_keep:]
        v_new = v_all[:, v_all.shape[1] - n_keep:]
    o = o.reshape(b, l, ATT_HEADS * HEAD_DIM).astype(h.dtype)
    return o @ w_out, k_new, v_new


def squared_relu_mlp(h, w_up, w_down):
    return jnp.square(jax.nn.relu(h @ w_up)) @ w_down


def run_trunk(x, c, pos, st, pr):
    b = x.shape[0]
    new = ([], [], [], [], [], [])
    for layer in range(DEPTH):
        i = layer // 2
        sh1, sc1, gt1, sh2, sc2, gt2 = adaln(c, pr['w_mod'][layer], pr['b_mod'][layer])
        g = pr['norms'][layer]
        h = rmsnorm(x, g[0]) * (1 + sc1) + sh1
        if layer % 2 == 0:
            if st is None:
                ssm_buf = jnp.zeros((b, SSM_CONV - 1, SSM_CONV_DIM), x.dtype)
                ssm_h = jnp.zeros((b, SSM_HEADS, SSM_HEAD_DIM, SSM_STATE), x.dtype)
                lru_buf = jnp.zeros((b, LRU_CONV - 1, LRU_WIDTH), x.dtype)
                lru_h = jnp.zeros((b, LRU_WIDTH), x.dtype)
            else:
                ssm_buf, ssm_h = st['ssm_conv'][i], st['ssm'][i]
                lru_buf, lru_h = st['lru_conv'][i], st['lru'][i]
            proj = h @ pr['w_in_hyb'][i]
            z, xbc, dt_raw, gate_in, xr = jnp.split(proj, HYB_SPLITS, axis=-1)
            y_ssm, ssm_buf, ssm_h = mamba2_branch(z, xbc, dt_raw, ssm_buf, ssm_h, pr['ssm_conv_w'][i],
                                                  pr['ssm_conv_b'][i], pr['ssm_dt_bias'][i], pr['ssm_a_log'][i],
                                                  pr['ssm_d'][i], pr['ssm_norm'][i])
            y_lru, lru_buf, lru_h = rglru_branch(gate_in, xr, lru_buf, lru_h, pr['lru_conv_w'][i],
                                                 pr['lru_conv_b'][i], pr['lru_w_a'][i], pr['lru_b_a'][i],
                                                 pr['lru_w_x'][i], pr['lru_b_x'][i], pr['lru_lambda'][i])
            mix = jnp.concatenate([y_ssm, y_lru], axis=-1) @ pr['w_out_hyb'][i]
            new[0].append(ssm_buf)
            new[1].append(ssm_h)
            new[2].append(lru_buf)
            new[3].append(lru_h)
        else:
            kv = None if st is None else (st['k'][i], st['v'][i])
            mix, k_new, v_new = attention_mixer(h, pos, kv, pr['w_qkv'][i], pr['b_qkv'][i],
                                                pr['attn_sinks'][i], pr['w_out_attn'][i])
            new[4].append(k_new)
            new[5].append(v_new)
        x = x + gt1 * rmsnorm(mix, g[1])
        h = rmsnorm(x, g[2]) * (1 + sc2) + sh2
        x = x + gt2 * rmsnorm(squared_relu_mlp(h, pr['w_up'][layer], pr['w_down'][layer]), g[3])
    return x, tuple(jnp.stack(v) for v in new)


def setup_inputs(seed: int = 0) -> dict:
    key = jax.random.key(seed)
    ks = iter(jax.random.split(key, 64))
    f32 = jnp.float32

    def nrm(shape, scale):
        return jax.random.normal(next(ks), shape, f32) * scale

    n_win = min(WINDOW, PAST_LEN)
    dt0 = jnp.exp(jax.random.uniform(next(ks), (N_EVEN, SSM_HEADS), f32, math.log(1e-3), math.log(1e-1)))
    dt_bias = dt0 + jnp.log(-jnp.expm1(-dt0))
    a_log = jnp.log(jax.random.uniform(next(ks), (N_EVEN, SSM_HEADS), f32, 1.0, 16.0))
    a0 = jax.random.uniform(next(ks), (N_EVEN, LRU_WIDTH), f32, 0.9, 0.999)
    s0 = a0 ** (1.0 / LRU_C)
    lam = jnp.log(s0) - jnp.log1p(-s0)
    return {
        'x_prompt': nrm((BATCH, SEQ, D_MODEL), 1.0),
        'x_sample': nrm((DEC_BATCH, DEC_SEQ, D_MODEL), 1.0),
        'state_ssm_conv': nrm((N_EVEN, DEC_BATCH, SSM_CONV - 1, SSM_CONV_DIM), 1.0),
        'state_ssm': nrm((N_EVEN, DEC_BATCH, SSM_HEADS, SSM_HEAD_DIM, SSM_STATE), 0.1),
        'state_lru_conv': nrm((N_EVEN, DEC_BATCH, LRU_CONV - 1, LRU_WIDTH), 1.0),
        'state_lru': nrm((N_EVEN, DEC_BATCH, LRU_WIDTH), 0.5),
        'cache_k': nrm((N_ODD, DEC_BATCH, n_win, KV_HEADS, HEAD_DIM), 1.0),
        'cache_v': nrm((N_ODD, DEC_BATCH, n_win, KV_HEADS, HEAD_DIM), 1.0),
        'c_prompt': nrm((BATCH, D_MODEL), 1.0),
        'c_sample': nrm((DEC_BATCH, D_MODEL), 1.0),
        'w_mod': nrm((DEPTH, D_MODEL, 6 * D_MODEL), 0.5 * D_MODEL ** -0.5),
        'b_mod': nrm((DEPTH, 6 * D_MODEL), 0.01),
        'norms': 1.0 + nrm((DEPTH, 4, D_MODEL), 0.05),
        'w_in_hyb': nrm((N_EVEN, D_MODEL, HYB_COLS), D_MODEL ** -0.5),
        'ssm_conv_w': nrm((N_EVEN, SSM_CONV, SSM_CONV_DIM), SSM_CONV ** -0.5),
        'ssm_conv_b': nrm((N_EVEN, SSM_CONV_DIM), 0.01),
        'ssm_dt_bias': dt_bias,
        'ssm_a_log': a_log,
        'ssm_d': 1.0 + nrm((N_EVEN, SSM_HEADS), 0.05),
        'ssm_norm': 1.0 + nrm((N_EVEN, SSM_INNER), 0.05),
        'lru_conv_w': nrm((N_EVEN, LRU_CONV, LRU_WIDTH), LRU_CONV ** -0.5),
        'lru_conv_b': nrm((N_EVEN, LRU_WIDTH), 0.01),
        'lru_w_a': nrm((N_EVEN, LRU_BLOCKS, LRU_BLOCK, LRU_BLOCK), LRU_BLOCK ** -0.5),
        'lru_b_a': nrm((N_EVEN, LRU_BLOCKS, LRU_BLOCK), 0.01),
        'lru_w_x': nrm((N_EVEN, LRU_BLOCKS, LRU_BLOCK, LRU_BLOCK), LRU_BLOCK ** -0.5),
        'lru_b_x': nrm((N_EVEN, LRU_BLOCKS, LRU_BLOCK), 0.01),
        'lru_lambda': lam,
        'w_out_hyb': nrm((N_EVEN, MIX_HYB, D_MODEL), MIX_HYB ** -0.5),
        'w_qkv': nrm((N_ODD, D_MODEL, QKV_COLS), D_MODEL ** -0.5),
        'b_qkv': nrm((N_ODD, QKV_COLS), 0.01),
        'attn_sinks': nrm((N_ODD, ATT_HEADS), 0.5),
        'w_out_attn': nrm((N_ODD, ATT_HEADS * HEAD_DIM, D_MODEL), (ATT_HEADS * HEAD_DIM) ** -0.5),
        'w_up': nrm((DEPTH, D_MODEL, D_FF), D_MODEL ** -0.5),
        'w_down': nrm((DEPTH, D_FF, D_MODEL), D_FF ** -0.5),
    }


def reference(x_prompt, x_sample, state_ssm_conv, state_ssm, state_lru_conv, state_lru, cache_k, cache_v,
              c_prompt, c_sample, w_mod, b_mod, norms, w_in_hyb, ssm_conv_w, ssm_conv_b, ssm_dt_bias,
              ssm_a_log, ssm_d, ssm_norm, lru_conv_w, lru_conv_b, lru_w_a, lru_b_a, lru_w_x, lru_b_x,
              lru_lambda, w_out_hyb, w_qkv, b_qkv, attn_sinks, w_out_attn, w_up, w_down):
    params = {'w_mod': w_mod, 'b_mod': b_mod, 'norms': norms, 'w_in_hyb': w_in_hyb,
              'ssm_conv_w': ssm_conv_w, 'ssm_conv_b': ssm_conv_b, 'ssm_dt_bias': ssm_dt_bias,
              'ssm_a_log': ssm_a_log, 'ssm_d': ssm_d, 'ssm_norm': ssm_norm,
              'lru_conv_w': lru_conv_w, 'lru_conv_b': lru_conv_b, 'lru_w_a': lru_w_a, 'lru_b_a': lru_b_a,
              'lru_w_x': lru_w_x, 'lru_b_x': lru_b_x, 'lru_lambda': lru_lambda, 'w_out_hyb': w_out_hyb,
              'w_qkv': w_qkv, 'b_qkv': b_qkv, 'attn_sinks': attn_sinks, 'w_out_attn': w_out_attn,
              'w_up': w_up, 'w_down': w_down}
    pos_p = jnp.arange(x_prompt.shape[1], dtype=jnp.int32)
    pos_s = PAST_LEN + jnp.arange(x_sample.shape[1], dtype=jnp.int32)
    y_prompt, (p_ssm_conv, p_ssm, p_lru_conv, p_lru, p_k, p_v) = run_trunk(x_prompt, c_prompt, pos_p, None, params)
    states = {'ssm_conv': state_ssm_conv, 'ssm': state_ssm, 'lru_conv': state_lru_conv, 'lru': state_lru,
              'k': cache_k, 'v': cache_v}
    y_sample, (s_ssm_conv, s_ssm, s_lru_conv, s_lru, s_k, s_v) = run_trunk(x_sample, c_sample, pos_s, states, params)
    return (y_prompt, y_sample, p_ssm_conv, p_ssm, p_lru_conv, p_lru, p_k, p_v,
            s_ssm_conv, s_ssm, s_lru_conv, s_lru, s_k, s_v)
```

```python
import functools
import math

import jax
import jax.numpy as jnp
from jax import lax
from jax.experimental import pallas as pl
from jax.experimental.pallas import tpu as pltpu

F32 = jnp.float32
BF16 = jnp.bfloat16

D_MODEL = 2048
PAST_LEN = 16384
SSM_HEADS = 32
SSM_HEAD_DIM = 64
SSM_INNER = SSM_HEADS * SSM_HEAD_DIM
SSM_GROUPS = 8
SSM_STATE = 128
SSM_CONV = 4
SSM_CHUNK = 128
SSM_CONV_DIM = SSM_INNER + 2 * SSM_GROUPS * SSM_STATE
HEADS_PER_GROUP = SSM_HEADS // SSM_GROUPS
GROUP_WIDTH = HEADS_PER_GROUP * SSM_HEAD_DIM
LRU_WIDTH = D_MODEL
LRU_BLOCKS = 8
LRU_BLOCK = LRU_WIDTH // LRU_BLOCKS
LRU_C = 8.0
ATT_HEADS = 32
KV_HEADS = 8
HEAD_DIM = 64
Q_PER_KV = ATT_HEADS // KV_HEADS
WINDOW = 128
ROPE_THETA = 10000.0
D_FF = 4 * D_MODEL
EPS = 1e-6

LANES = 128
SUBLANES = 8
TAIL_ROWS = SUBLANES
VMEM_LIMIT = 56 * 1024 * 1024


def _cparams(sem):
    return pltpu.CompilerParams(dimension_semantics=sem, vmem_limit_bytes=VMEM_LIMIT)


def _rms(x, g):
    return x * lax.rsqrt(jnp.mean(x * x, axis=-1, keepdims=True) + EPS) * g


def _silu(x):
    return x * jax.nn.sigmoid(x)


def _softplus(x):
    return jnp.maximum(x, 0.0) + jnp.log1p(jnp.exp(-jnp.abs(x)))


def _neg_expm1(x):
    t = jnp.tanh(0.5 * x)
    return -2.0 * t / (1.0 - t)


def _adaln_kernel(c_ref, w_ref, b_ref, o_ref):
    s = _silu(c_ref[...]).astype(BF16)
    o_ref[0] = jnp.dot(s, w_ref[0].astype(BF16), preferred_element_type=F32) + b_ref[0]


def _adaln(c_all, w_mod, b_mod):
    nb, d = c_all.shape
    depth, _, n = w_mod.shape
    bn = 1024
    return pl.pallas_call(
        _adaln_kernel,
        grid=(depth, n // bn),
        in_specs=[pl.BlockSpec((nb, d), lambda l, j: (0, 0)),
                  pl.BlockSpec((1, d, bn), lambda l, j: (l, 0, j)),
                  pl.BlockSpec((1, 1, bn), lambda l, j: (l, 0, j))],
        out_specs=pl.BlockSpec((1, nb, bn), lambda l, j: (l, 0, j)),
        out_shape=jax.ShapeDtypeStruct((depth, nb, n), F32),
        compiler_params=_cparams(("parallel", "parallel")),
        name="adaln",
    )(c_all, w_mod, b_mod.reshape(depth, 1, n))


def _mod_spec(mod, bm, k):
    if mod.shape[1] == 1:
        return pl.BlockSpec((1, 1, D_MODEL), lambda g, i, j: (g, 0, k))
    return pl.BlockSpec((1, bm, D_MODEL), lambda g, i, j: (g, i, k))


def _inproj_kernel(x_ref, g_ref, sh_ref, sc_ref, w_ref, wdt_ref, o_ref, dt_ref, h_scr):
    @pl.when(pl.program_id(2) == 0)
    def _():
        h = _rms(x_ref[0], g_ref[...]) * (1.0 + sc_ref[0]) + sh_ref[0]
        hb = h.astype(BF16)
        h_scr[...] = hb
        dt_ref[0] = jnp.dot(hb, wdt_ref[...], preferred_element_type=F32)

    o_ref[0] = jnp.dot(h_scr[...], w_ref[...], preferred_element_type=F32)


def _inproj(x, gain, mod, w_main, w_dt, bm, bn):
    g_, m, d = x.shape
    n = w_main.shape[1]
    return pl.pallas_call(
        _inproj_kernel,
        grid=(g_, m // bm, n // bn),
        in_specs=[pl.BlockSpec((1, bm, d), lambda g, i, j: (g, i, 0)),
                  pl.BlockSpec((1, d), lambda g, i, j: (0, 0)),
                  _mod_spec(mod, bm, 0), _mod_spec(mod, bm, 1),
                  pl.BlockSpec((d, bn), lambda g, i, j: (0, j)),
                  pl.BlockSpec((d, LANES), lambda g, i, j: (0, 0))],
        out_specs=[pl.BlockSpec((1, bm, bn), lambda g, i, j: (g, i, j)),
                   pl.BlockSpec((1, bm, LANES), lambda g, i, j: (g, i, 0))],
        out_shape=[jax.ShapeDtypeStruct((g_, m, n), F32),
                   jax.ShapeDtypeStruct((g_, m, LANES), F32)],
        scratch_shapes=[pltpu.VMEM((bm, d), BF16)],
        compiler_params=_cparams(("parallel", "parallel", "arbitrary")),
        name="inproj",
    )(x, gain, mod, mod, w_main, w_dt)


def _swap_halves(x):
    lane = lax.broadcasted_iota(jnp.int32, x.shape, 1)
    first = (lane % HEAD_DIM) < (HEAD_DIM // 2)
    return jnp.where(first, pltpu.roll(x, LANES - HEAD_DIM // 2, 1), pltpu.roll(x, HEAD_DIM // 2, 1))


def _qkv_kernel(x_ref, g_ref, sh_ref, sc_ref, w_ref, b_ref, cos_ref, sin_ref, o_ref, h_scr, *,
                rope_blocks):
    j = pl.program_id(2)

    @pl.when(j == 0)
    def _():
        h = _rms(x_ref[0], g_ref[...]) * (1.0 + sc_ref[0]) + sh_ref[0]
        h_scr[...] = h.astype(BF16)

    acc = jnp.dot(h_scr[...], w_ref[...], preferred_element_type=F32) + b_ref[...]

    @pl.when(j < rope_blocks)
    def _():
        cos = cos_ref[...]
        sin = sin_ref[...]
        for c in range(acc.shape[1] // LANES):
            a = acc[:, c * LANES:(c + 1) * LANES]
            o_ref[0, :, c * LANES:(c + 1) * LANES] = a * cos + _swap_halves(a) * sin

    @pl.when(j >= rope_blocks)
    def _():
        o_ref[0] = acc


def _qkv(x, gain, mod, w, b, cos_t, sin_t, bm, bn):
    g_, m, d = x.shape
    n = w.shape[1]
    rope_blocks = ((ATT_HEADS + KV_HEADS) * HEAD_DIM) // bn
    rows_per_g = cos_t.shape[0] // bm
    return pl.pallas_call(
        functools.partial(_qkv_kernel, rope_blocks=rope_blocks),
        grid=(g_, m // bm, n // bn),
        in_specs=[pl.BlockSpec((1, bm, d), lambda g, i, j: (g, i, 0)),
                  pl.BlockSpec((1, d), lambda g, i, j: (0, 0)),
                  _mod_spec(mod, bm, 0), _mod_spec(mod, bm, 1),
                  pl.BlockSpec((d, bn), lambda g, i, j: (0, j)),
                  pl.BlockSpec((1, bn), lambda g, i, j: (0, j)),
                  pl.BlockSpec((bm, LANES), lambda g, i, j: (i % rows_per_g, 0)),
                  pl.BlockSpec((bm, LANES), lambda g, i, j: (i % rows_per_g, 0))],
        out_specs=pl.BlockSpec((1, bm, bn), lambda g, i, j: (g, i, j)),
        out_shape=jax.ShapeDtypeStruct((g_, m, n), F32),
        scratch_shapes=[pltpu.VMEM((bm, d), BF16)],
        compiler_params=_cparams(("parallel", "parallel", "arbitrary")),
        name="qkv",
    )(x, gain, mod, mod, w, b, cos_t, sin_t)


def _outproj_kernel(*refs, n_lhs):
    lhs = refs[:n_lhs]
    ws = refs[n_lhs:2 * n_lhs]
    x_ref, g_ref, gt_ref, o_ref = refs[2 * n_lhs:]
    acc = jnp.dot(lhs[0][0], ws[0][...], preferred_element_type=F32)
    for a, w in zip(lhs[1:], ws[1:]):
        acc = acc + jnp.dot(a[0], w[...], preferred_element_type=F32)
    o_ref[0] = x_ref[0] + gt_ref[0] * _rms(acc, g_ref[...])


def _outproj(lhs_list, w_list, x, gain, mod, bm):
    g_, m, d = x.shape
    n_lhs = len(lhs_list)
    lhs_specs = [pl.BlockSpec((1, bm, a.shape[2]), lambda g, i: (g, i, 0)) for a in lhs_list]
    w_specs = [pl.BlockSpec(w.shape, lambda g, i: (0, 0), pipeline_mode=pl.Buffered(1)) for w in w_list]
    if mod.shape[1] == 1:
        gt_spec = pl.BlockSpec((1, 1, d), lambda g, i: (g, 0, 2))
    else:
        gt_spec = pl.BlockSpec((1, bm, d), lambda g, i: (g, i, 2))
    return pl.pallas_call(
        functools.partial(_outproj_kernel, n_lhs=n_lhs),
        grid=(g_, m // bm),
        in_specs=lhs_specs + w_specs + [pl.BlockSpec((1, bm, d), lambda g, i: (g, i, 0)),
                                        pl.BlockSpec((1, d), lambda g, i: (0, 0)), gt_spec],
        out_specs=pl.BlockSpec((1, bm, d), lambda g, i: (g, i, 0)),
        out_shape=jax.ShapeDtypeStruct((g_, m, d), F32),
        compiler_params=_cparams(("parallel", "parallel")),
        name="outproj",
    )(*lhs_list, *w_list, x, gain, mod)


def _mlp_kernel(x_ref, g2_ref, sh_ref, sc_ref, gt_ref, wu_ref, wd_ref, g3_ref, o_ref, h_scr, acc_scr):
    f = pl.program_id(2)

    @pl.when(f == 0)
    def _():
        h = _rms(x_ref[0], g2_ref[...]) * (1.0 + sc_ref[0]) + sh_ref[0]
        h_scr[...] = h.astype(BF16)

    u = jnp.dot(h_scr[...], wu_ref[...], preferred_element_type=F32)
    u = jnp.square(jnp.maximum(u, 0.0)).astype(BF16)
    part = jnp.dot(u, wd_ref[...], preferred_element_type=F32)

    @pl.when(f == 0)
    def _():
        acc_scr[...] = part

    @pl.when(f > 0)
    def _():
        acc_scr[...] += part

    @pl.when(f == pl.num_programs(2) - 1)
    def _():
        o_ref[0] = x_ref[0] + gt_ref[0] * _rms(acc_scr[...], g3_ref[...])


def _mlp(x, g2, g3, mod, w_up, w_down, bm, bf):
    g_, m, d = x.shape
    dff = w_up.shape[1]
    if mod.shape[1] == 1:
        mspec = lambda k: pl.BlockSpec((1, 1, d), lambda g, i, f: (g, 0, k))
    else:
        mspec = lambda k: pl.BlockSpec((1, bm, d), lambda g, i, f: (g, i, k))
    return pl.pallas_call(
        _mlp_kernel,
        grid=(g_, m // bm, dff // bf),
        in_specs=[pl.BlockSpec((1, bm, d), lambda g, i, f: (g, i, 0)),
                  pl.BlockSpec((1, d), lambda g, i, f: (0, 0)),
                  mspec(3), mspec(4), mspec(5),
                  pl.BlockSpec((d, bf), lambda g, i, f: (0, f)),
                  pl.BlockSpec((bf, d), lambda g, i, f: (f, 0)),
                  pl.BlockSpec((1, d), lambda g, i, f: (0, 0))],
        out_specs=pl.BlockSpec((1, bm, d), lambda g, i, f: (g, i, 0)),
        out_shape=jax.ShapeDtypeStruct((g_, m, d), F32),
        scratch_shapes=[pltpu.VMEM((bm, d), BF16), pltpu.VMEM((bm, d), F32)],
        compiler_params=_cparams(("parallel", "parallel", "arbitrary")),
        name="mlp",
    )(x, g2, mod, mod, mod, w_up, w_down, g3)


def _dwconv(raw, tail, w, b):
    t = raw.shape[0]
    k = w.shape[0]
    ext = jnp.concatenate([tail, raw], axis=0)
    y = b + raw * w[k - 1:k]
    for j in range(k - 1):
        off = TAIL_ROWS - (k - 1) + j
        y = y + ext[off:off + t] * w[j:j + 1]
    return y, ext[t:t + TAIL_ROWS]


def _cumsum_rows(x):
    n = x.shape[0]
    row = lax.broadcasted_iota(jnp.int32, x.shape, 0)
    d = 1
    while d < n:
        x = x + jnp.where(row >= d, pltpu.roll(x, d, 0), 0.0)
        d *= 2
    return x


def _ssd_kernel(z_ref, xs_ref, bc_ref, dt_ref, buf_ref, h0_ref, cw_ref, cb_ref, dtb_ref, alog_ref,
                dsk_ref, ng_ref, exp_ref, y_ref, st_ref, tail_scr, *, q_valid):
    q = SSM_CHUNK
    c = pl.program_id(1)

    @pl.when(c == 0)
    def _():
        tail_scr[...] = buf_ref[0]
        st_ref[0] = h0_ref[0]

    def padded(v):
        if q_valid == q:
            return v
        return jnp.concatenate([v, jnp.zeros((q - q_valid, v.shape[1]), v.dtype)], axis=0)

    cw = cw_ref[...]
    cb = cb_ref[...]
    xs, tail_x = _dwconv(xs_ref[0], tail_scr[:, :SSM_INNER], cw[:, :SSM_INNER], cb[:, :SSM_INNER])
    bcm, tail_b = _dwconv(bc_ref[0], tail_scr[:, SSM_INNER:], cw[:, SSM_INNER:], cb[:, SSM_INNER:])
    tail_scr[:, :SSM_INNER] = tail_x
    tail_scr[:, SSM_INNER:] = tail_b
    xs = padded(_silu(xs))
    bcm = padded(_silu(bcm))
    z = padded(z_ref[0])

    dt = _softplus(dt_ref[0] + dtb_ref[...])
    dt = padded(dt)
    a = -jnp.exp(alog_ref[...])
    acs = _cumsum_rows(dt * a)
    acs_t = acs.T

    expand = exp_ref[...]
    hp = lax.Precision.HIGHEST
    dt_full = jnp.dot(dt, expand, precision=hp, preferred_element_type=F32)
    acs_full = jnp.dot(acs, expand, precision=hp, preferred_element_type=F32)
    xdt = xs * dt_full
    dec_out = jnp.exp(acs_full)
    xw = (xdt * jnp.exp(acs_full[q - 1:q] - acs_full)).astype(BF16)
    xdt_b = xdt.astype(BF16)
    bc_b = bcm.astype(BF16)

    row = lax.broadcasted_iota(jnp.int32, (q, q), 0)
    col = lax.broadcasted_iota(jnp.int32, (q, q), 1)
    causal = row >= col
    lane = lax.broadcasted_iota(jnp.int32, (q, LANES), 1)
    lo_half = lane < SSM_HEAD_DIM
    nt = (((1,), (1,)), ((), ()))
    tn = (((0,), (0,)), ((), ()))
    n_bc = SSM_GROUPS * SSM_STATE

    for g in range(SSM_GROUPS):
        b_g = bc_b[:, g * SSM_STATE:(g + 1) * SSM_STATE]
        c_g = bc_b[:, n_bc + g * SSM_STATE:n_bc + (g + 1) * SSM_STATE]
        cbm = lax.dot_general(c_g, b_g, nt, preferred_element_type=F32)
        g0 = g * GROUP_WIDTH
        h_g = st_ref[0, g0:g0 + GROUP_WIDTH, :]
        y_off = lax.dot_general(c_g, h_g.astype(BF16), nt, preferred_element_type=F32)
        y_g = y_off * dec_out[:, g0:g0 + GROUP_WIDTH]
        pieces = []
        for pair in range(HEADS_PER_GROUP // 2):
            x_pair = xdt_b[:, g0 + pair * LANES:g0 + (pair + 1) * LANES]
            acc = None
            for e in range(2):
                h = g * HEADS_PER_GROUP + pair * 2 + e
                seg = jnp.exp(jnp.where(causal, acs[:, h:h + 1] - acs_t[h:h + 1, :], -jnp.inf))
                m = (cbm * seg).astype(BF16)
                x_h = jnp.where(lo_half if e == 0 else jnp.logical_not(lo_half), x_pair, jnp.zeros_like(x_pair))
                part = jnp.dot(m, x_h, preferred_element_type=F32)
                acc = part if acc is None else acc + part
            pieces.append(acc)
        y_g = y_g + jnp.concatenate(pieces, axis=1)
        y_g = y_g + xs[:, g0:g0 + GROUP_WIDTH] * dsk_ref[:, g0:g0 + GROUP_WIDTH]
        y_g = y_g * _silu(z[:, g0:g0 + GROUP_WIDTH])
        y_g = y_g * lax.rsqrt(jnp.mean(y_g * y_g, axis=-1, keepdims=True) + EPS)
        y_g = y_g * ng_ref[:, g0:g0 + GROUP_WIDTH]
        y_ref[0, :, g0:g0 + GROUP_WIDTH] = y_g[:q_valid].astype(y_ref.dtype)

        st_new = lax.dot_general(xw[:, g0:g0 + GROUP_WIDTH], b_g, tn, preferred_element_type=F32)
        for r in range(HEADS_PER_GROUP):
            h = g * HEADS_PER_GROUP + r
            decay = jnp.exp(acs_t[h:h + 1, q - 1:q])
            r0 = g0 + r * SSM_HEAD_DIM
            st_ref[0, r0:r0 + SSM_HEAD_DIM, :] = (
                h_g[r * SSM_HEAD_DIM:(r + 1) * SSM_HEAD_DIM] * decay
                + st_new[r * SSM_HEAD_DIM:(r + 1) * SSM_HEAD_DIM])


def _ssd(proj, dtp, buf8, h0, conv_w, conv_b, dt_bias, a_log, d_full, norm_g, expand, rows):
    b_, l, _ = proj.shape
    nc = l // rows
    st_rows = SSM_INNER
    return pl.pallas_call(
        functools.partial(_ssd_kernel, q_valid=rows),
        grid=(b_, nc),
        in_specs=[pl.BlockSpec((1, rows, SSM_INNER), lambda b, c: (b, c, 0)),
                  pl.BlockSpec((1, rows, SSM_INNER), lambda b, c: (b, c, 1)),
                  pl.BlockSpec((1, rows, SSM_INNER), lambda b, c: (b, c, 2)),
                  pl.BlockSpec((1, rows, LANES), lambda b, c: (b, c, 0)),
                  pl.BlockSpec((1, TAIL_ROWS, SSM_CONV_DIM), lambda b, c: (b, 0, 0)),
                  pl.BlockSpec((1, st_rows, SSM_STATE), lambda b, c: (b, 0, 0)),
                  pl.BlockSpec((SSM_CONV, SSM_CONV_DIM), lambda b, c: (0, 0)),
                  pl.BlockSpec((1, SSM_CONV_DIM), lambda b, c: (0, 0)),
                  pl.BlockSpec((1, LANES), lambda b, c: (0, 0)),
                  pl.BlockSpec((1, LANES), lambda b, c: (0, 0)),
                  pl.BlockSpec((1, SSM_INNER), lambda b, c: (0, 0)),
                  pl.BlockSpec((1, SSM_INNER), lambda b, c: (0, 0)),
                  pl.BlockSpec((LANES, SSM_INNER), lambda b, c: (0, 0))],
        out_specs=[pl.BlockSpec((1, rows, SSM_INNER), lambda b, c: (b, c, 0)),
                   pl.BlockSpec((1, st_rows, SSM_STATE), lambda b, c: (b, 0, 0))],
        out_shape=[jax.ShapeDtypeStruct((b_, l, SSM_INNER), BF16),
                   jax.ShapeDtypeStruct((b_, st_rows, SSM_STATE), F32)],
        scratch_shapes=[pltpu.VMEM((TAIL_ROWS, SSM_CONV_DIM), F32)],
        compiler_params=_cparams(("parallel", "arbitrary")),
        name="ssd",
    )(proj, proj, proj, dtp, buf8, h0, conv_w, conv_b, dt_bias, a_log, d_full, norm_g, expand)


def _lru_kernel(gate_ref, xr_ref, buf_ref, h0_ref, cw_ref, cb_ref, wa_ref, ba_ref, wx_ref, bx_ref,
                lam_ref, y_ref, hl_ref, tail_scr, a_scr, u_scr):
    t = xr_ref.shape[1]
    c = pl.program_id(1)

    @pl.when(c == 0)
    def _():
        tail_scr[...] = buf_ref[0]
        hl_ref[0] = h0_ref[0]

    xc, tail = _dwconv(xr_ref[0], tail_scr[...], cw_ref[...], cb_ref[...])
    tail_scr[...] = tail
    sp = _softplus(-lam_ref[...])
    for k in range(LRU_BLOCKS):
        k0 = k * LRU_BLOCK
        xk = xc[:, k0:k0 + LRU_BLOCK]
        xkb = xk.astype(BF16)
        gr = jax.nn.sigmoid(jnp.dot(xkb, wa_ref[k], preferred_element_type=F32) + ba_ref[:, k0:k0 + LRU_BLOCK])
        gi = jax.nn.sigmoid(jnp.dot(xkb, wx_ref[k], preferred_element_type=F32) + bx_ref[:, k0:k0 + LRU_BLOCK])
        log_a = -LRU_C * gr * sp[:, k0:k0 + LRU_BLOCK]
        a_scr[:, k0:k0 + LRU_BLOCK] = jnp.exp(log_a)
        u_scr[:, k0:k0 + LRU_BLOCK] = jnp.sqrt(_neg_expm1(2.0 * log_a)) * gi * xk

    row = lax.broadcasted_iota(jnp.int32, (SUBLANES, LRU_WIDTH), 0)

    def tile(i, h_prev):
        r0 = pl.multiple_of(i * SUBLANES, SUBLANES)
        a = a_scr[pl.ds(r0, SUBLANES), :]
        u = u_scr[pl.ds(r0, SUBLANES), :]
        d = 1
        while d < SUBLANES:
            keep = row >= d
            u = u + a * jnp.where(keep, pltpu.roll(u, d, 0), 0.0)
            a = a * jnp.where(keep, pltpu.roll(a, d, 0), 1.0)
            d *= 2
        h = u + a * h_prev
        u_scr[pl.ds(r0, SUBLANES), :] = h
        return h[SUBLANES - 1:SUBLANES, :]

    h_last = lax.fori_loop(0, t // SUBLANES, tile, hl_ref[0])
    hl_ref[0] = h_last
    y_ref[0] = (u_scr[...] * jax.nn.gelu(gate_ref[0])).astype(y_ref.dtype)


def _lru(proj, buf8, h0, conv_w, conv_b, w_a, b_a, w_x, b_x, lam, rows, col0):
    b_, l, _ = proj.shape
    w = LRU_WIDTH
    full2 = lambda shape: pl.BlockSpec(shape, lambda b, c: (0,) * len(shape))
    return pl.pallas_call(
        _lru_kernel,
        grid=(b_, l // rows),
        in_specs=[pl.BlockSpec((1, rows, w), lambda b, c: (b, c, col0)),
                  pl.BlockSpec((1, rows, w), lambda b, c: (b, c, col0 + 1)),
                  pl.BlockSpec((1, TAIL_ROWS, w), lambda b, c: (b, 0, 0)),
                  pl.BlockSpec((1, 1, w), lambda b, c: (b, 0, 0)),
                  full2((SSM_CONV, w)), full2((1, w)),
                  full2((LRU_BLOCKS, LRU_BLOCK, LRU_BLOCK)), full2((1, w)),
                  full2((LRU_BLOCKS, LRU_BLOCK, LRU_BLOCK)), full2((1, w)),
                  full2((1, w))],
        out_specs=[pl.BlockSpec((1, rows, w), lambda b, c: (b, c, 0)),
                   pl.BlockSpec((1, 1, w), lambda b, c: (b, 0, 0))],
        out_shape=[jax.ShapeDtypeStruct((b_, l, w), BF16),
                   jax.ShapeDtypeStruct((b_, 1, w), F32)],
        scratch_shapes=[pltpu.VMEM((TAIL_ROWS, w), F32), pltpu.VMEM((rows, w), F32),
                        pltpu.VMEM((rows, w), F32)],
        compiler_params=_cparams(("parallel", "arbitrary")),
        name="lru",
    )(proj, proj, buf8, h0, conv_w, conv_b, w_a, b_a, w_x, b_x, lam)


def _softmax_pv(s, valid, sink_col, v_b):
    s = jnp.where(valid, s, -jnp.inf)
    m = jnp.maximum(jnp.max(s, axis=1, keepdims=True), sink_col)
    p = jnp.exp(s - m)
    den = jnp.sum(p, axis=1, keepdims=True) + jnp.exp(sink_col - m)
    return jnp.dot(p.astype(BF16), v_b, preferred_element_type=F32) / den


def _attn_prompt_kernel(sink_ref, q_ref, kc_ref, kp_ref, vc_ref, vp_ref, o_ref):
    n = pl.program_id(1)
    hp = pl.program_id(2)
    w = WINDOW
    has_prev = n > 0
    k2 = jnp.concatenate([jnp.where(has_prev, kp_ref[0], 0.0), kc_ref[0]], axis=0).astype(BF16)
    v2 = jnp.concatenate([jnp.where(has_prev, vp_ref[0], 0.0), vc_ref[0]], axis=0).astype(BF16)
    qb = q_ref[0].astype(BF16)
    rows = Q_PER_KV * w
    t = lax.broadcasted_iota(jnp.int32, (rows, 2 * w), 0) % w
    j = lax.broadcasted_iota(jnp.int32, (rows, 2 * w), 1)
    valid = (j > t) & (j <= t + w)
    nt = (((1,), (1,)), ((), ()))
    outs = []
    for e in range(2):
        k_e = k2[:, e * HEAD_DIM:(e + 1) * HEAD_DIM]
        v_e = v2[:, e * HEAD_DIM:(e + 1) * HEAD_DIM]
        q_e = jnp.concatenate([qb[:, (e * Q_PER_KV + r) * HEAD_DIM:(e * Q_PER_KV + r + 1) * HEAD_DIM]
                               for r in range(Q_PER_KV)], axis=0)
        s = lax.dot_general(q_e, k_e, nt, preferred_element_type=F32) * (HEAD_DIM ** -0.5)
        sink_col = jnp.concatenate(
            [jnp.full((w, 1), sink_ref[hp * 2 * Q_PER_KV + e * Q_PER_KV + r], F32) for r in range(Q_PER_KV)], axis=0)
        o = _softmax_pv(s, valid, sink_col, v_e)
        outs.extend(o[r * w:(r + 1) * w] for r in range(Q_PER_KV))
    o_ref[0] = jnp.concatenate(outs, axis=1).astype(o_ref.dtype)


def _attn_prompt(qkv, sinks):
    b_, l, _ = qkv.shape
    w = WINDOW
    qcols = 2 * Q_PER_KV * HEAD_DIM
    k_blk0 = (ATT_HEADS * HEAD_DIM) // LANES
    v_blk0 = ((ATT_HEADS + KV_HEADS) * HEAD_DIM) // LANES
    return pl.pallas_call(
        _attn_prompt_kernel,
        grid=(b_, l // w, KV_HEADS // 2),
        in_specs=[pl.BlockSpec(memory_space=pltpu.SMEM),
                  pl.BlockSpec((1, w, qcols), lambda b, n, h: (b, n, h)),
                  pl.BlockSpec((1, w, LANES), lambda b, n, h: (b, n, k_blk0 + h)),
                  pl.BlockSpec((1, w, LANES), lambda b, n, h: (b, jnp.maximum(n - 1, 0), k_blk0 + h)),
                  pl.BlockSpec((1, w, LANES), lambda b, n, h: (b, n, v_blk0 + h)),
                  pl.BlockSpec((1, w, LANES), lambda b, n, h: (b, jnp.maximum(n - 1, 0), v_blk0 + h))],
        out_specs=pl.BlockSpec((1, w, qcols), lambda b, n, h: (b, n, h)),
        out_shape=jax.ShapeDtypeStruct((b_, l, ATT_HEADS * HEAD_DIM), BF16),
        compiler_params=_cparams(("parallel", "parallel", "parallel")),
        name="attn_prompt",
    )(sinks, qkv, qkv, qkv, qkv, qkv)


def _attn_sample_kernel(sink_ref, qkv_ref, ck_ref, cv_ref, o_ref, nk_ref, nv_ref):
    l = qkv_ref.shape[1]
    n_keep = ck_ref.shape[1]
    kcol = ATT_HEADS * HEAD_DIM
    vcol = kcol + KV_HEADS * HEAD_DIM
    qkv = qkv_ref[0]
    k_all = jnp.concatenate([ck_ref[0], qkv[:, kcol:vcol]], axis=0)
    v_all = jnp.concatenate([cv_ref[0], qkv[:, vcol:]], axis=0)
    nk_ref[0] = k_all[l:]
    nv_ref[0] = v_all[l:]
    kb = k_all.astype(BF16)
    vb = v_all.astype(BF16)
    qb = qkv[:, :kcol].astype(BF16)
    rows = Q_PER_KV * l
    t = lax.broadcasted_iota(jnp.int32, (rows, n_keep + l), 0) % l
    j = lax.broadcasted_iota(jnp.int32, (rows, n_keep + l), 1)
    rel = jnp.where(j < n_keep, t + n_keep - j, t - (j - n_keep))
    valid = (rel >= 0) & (rel < WINDOW)
    nt = (((1,), (1,)), ((), ()))
    outs = []
    for e in range(KV_HEADS):
        k_e = kb[:, e * HEAD_DIM:(e + 1) * HEAD_DIM]
        v_e = vb[:, e * HEAD_DIM:(e + 1) * HEAD_DIM]
        q_e = jnp.concatenate([qb[:, (e * Q_PER_KV + r) * HEAD_DIM:(e * Q_PER_KV + r + 1) * HEAD_DIM]
                               for r in range(Q_PER_KV)], axis=0)
        s = lax.dot_general(q_e, k_e, nt, preferred_element_type=F32) * (HEAD_DIM ** -0.5)
        sink_col = jnp.concatenate(
            [jnp.full((l, 1), sink_ref[e * Q_PER_KV + r], F32) for r in range(Q_PER_KV)], axis=0)
        o = _softmax_pv(s, valid, sink_col, v_e)
        outs.extend(o[r * l:(r + 1) * l] for r in range(Q_PER_KV))
    o_ref[0] = jnp.concatenate(outs, axis=1).astype(o_ref.dtype)


def _attn_sample(qkv, cache_k, cache_v, sinks):
    b_, l, n = qkv.shape
    n_keep, kvw = cache_k.shape[1:]
    return pl.pallas_call(
        _attn_sample_kernel,
        grid=(b_,),
        in_specs=[pl.BlockSpec(memory_space=pltpu.SMEM),
                  pl.BlockSpec((1, l, n), lambda b: (b, 0, 0)),
                  pl.BlockSpec((1, n_keep, kvw), lambda b: (b, 0, 0)),
                  pl.BlockSpec((1, n_keep, kvw), lambda b: (b, 0, 0))],
        out_specs=[pl.BlockSpec((1, l, ATT_HEADS * HEAD_DIM), lambda b: (b, 0, 0)),
                   pl.BlockSpec((1, n_keep, kvw), lambda b: (b, 0, 0)),
                   pl.BlockSpec((1, n_keep, kvw), lambda b: (b, 0, 0))],
        out_shape=[jax.ShapeDtypeStruct((b_, l, ATT_HEADS * HEAD_DIM), BF16),
                   jax.ShapeDtypeStruct((b_, n_keep, kvw), F32),
                   jax.ShapeDtypeStruct((b_, n_keep, kvw), F32)],
        compiler_params=_cparams(("parallel",)),
        name="attn_sample",
    )(sinks, qkv, cache_k, cache_v)


def _rope_tables(pos):
    half = HEAD_DIM // 2
    inv_freq = ROPE_THETA ** (-jnp.arange(half, dtype=F32) / half)
    ang = pos.astype(F32)[:, None] * inv_freq[None, :]
    cos = jnp.cos(ang)
    sin = jnp.sin(ang)
    reps = LANES // HEAD_DIM
    return (jnp.tile(jnp.concatenate([cos, cos], axis=1), (1, reps)),
            jnp.tile(jnp.concatenate([-sin, sin], axis=1), (1, reps)))


def _tail_pad(buf):
    return jnp.pad(buf, ((0, 0), (TAIL_ROWS - buf.shape[1], 0), (0, 0)))


def _trunk(x, mods, pos, st, pw, seq_len):
    g_, m, d = x.shape
    nseq = g_ * m // seq_len
    bm = min(m, 512)
    chunk = min(seq_len, SSM_CHUNK)
    lru_rows = min(seq_len, 256)

    proj, dtp = _inproj(x, pw['norms'][0, 0:1], mods[0], pw['w_in_main'], pw['w_in_dt'], bm, 1024)
    n_main = proj.shape[-1]
    proj_s = proj.reshape(nseq, seq_len, n_main)
    dtp_s = dtp.reshape(nseq, seq_len, LANES)
    y_ssm, ssm_h = _ssd(proj_s, dtp_s, _tail_pad(st['ssm_conv']), st['ssm'].reshape(nseq, SSM_INNER, SSM_STATE),
                        pw['ssm_conv_w'], pw['ssm_conv_b'], pw['ssm_dt_bias'], pw['ssm_a_log'], pw['ssm_d_full'],
                        pw['ssm_norm'], pw['expand'], chunk)
    y_lru, lru_h = _lru(proj_s, _tail_pad(st['lru_conv']), st['lru'].reshape(nseq, 1, LRU_WIDTH),
                        pw['lru_conv_w'], pw['lru_conv_b'], pw['lru_w_a'], pw['lru_b_a'], pw['lru_w_x'],
                        pw['lru_b_x'], pw['lru_lambda'], lru_rows, 3)
    keep = SSM_CONV - 1
    new_ssm_conv = proj_s[:, seq_len - keep:, SSM_INNER:SSM_INNER + SSM_CONV_DIM]
    new_lru_conv = proj_s[:, seq_len - keep:, n_main - LRU_WIDTH:]
    x = _outproj([y_ssm.reshape(g_, m, SSM_INNER), y_lru.reshape(g_, m, LRU_WIDTH)],
                 [pw['w_out_ssm'], pw['w_out_lru']], x, pw['norms'][0, 1:2], mods[0], bm)
    x = _mlp(x, pw['norms'][0, 2:3], pw['norms'][0, 3:4], mods[0], pw['w_up'][0], pw['w_down'][0], bm, 1024)

    cos_t, sin_t = _rope_tables(pos)
    if seq_len < bm:
        cos_t = jnp.tile(cos_t, (bm // seq_len, 1))
        sin_t = jnp.tile(sin_t, (bm // seq_len, 1))
    qkv = _qkv(x, pw['norms'][1, 0:1], mods[1], pw['w_qkv'], pw['b_qkv'], cos_t, sin_t, bm, 512)
    qkv_s = qkv.reshape(nseq, seq_len, qkv.shape[-1])
    kcol = ATT_HEADS * HEAD_DIM
    vcol = kcol + KV_HEADS * HEAD_DIM
    if st['k'] is None:
        o = _attn_prompt(qkv_s, pw['attn_sinks'])
        n_keep = min(WINDOW, seq_len)
        k_new = qkv_s[:, seq_len - n_keep:, kcol:vcol]
        v_new = qkv_s[:, seq_len - n_keep:, vcol:]
    else:
        n_keep = st['k'].shape[1]
        o, k_new, v_new = _attn_sample(qkv_s, st['k'].reshape(nseq, n_keep, KV_HEADS * HEAD_DIM),
                                       st['v'].reshape(nseq, n_keep, KV_HEADS * HEAD_DIM), pw['attn_sinks'])
    k_new = k_new.reshape(nseq, n_keep, KV_HEADS, HEAD_DIM)
    v_new = v_new.reshape(nseq, n_keep, KV_HEADS, HEAD_DIM)
    x = _outproj([o.reshape(g_, m, kcol)], [pw['w_out_attn']], x, pw['norms'][1, 1:2], mods[1], bm)
    x = _mlp(x, pw['norms'][1, 2:3], pw['norms'][1, 3:4], mods[1], pw['w_up'][1], pw['w_down'][1], bm, 1024)

    states = (new_ssm_conv[None], ssm_h.reshape(1, nseq, SSM_HEADS, SSM_HEAD_DIM, SSM_STATE),
              new_lru_conv[None], lru_h.reshape(1, nseq, LRU_WIDTH), k_new[None], v_new[None])
    return x, states


def kernel(x_prompt, x_sample, state_ssm_conv, state_ssm, state_lru_conv, state_lru, cache_k, cache_v, c_prompt, c_sample, w_mod, b_mod, norms, w_in_hyb, ssm_conv_w, ssm_conv_b, ssm_dt_bias, ssm_a_log, ssm_d, ssm_norm, lru_conv_w, lru_conv_b, lru_w_a, lru_b_a, lru_w_x, lru_b_x, lru_lambda, w_out_hyb, w_qkv, b_qkv, attn_sinks, w_out_attn, w_up, w_down):
    bp, lp, d = x_prompt.shape
    bs, ls, _ = x_sample.shape
    depth = w_mod.shape[0]

    dt0 = SSM_INNER + SSM_CONV_DIM
    w_in = w_in_hyb[0]
    pad_lanes = lambda v: jnp.pad(v.reshape(1, -1), ((0, 0), (0, LANES - v.size)))
    head_of_lane = jnp.arange(SSM_INNER) // SSM_HEAD_DIM
    pw = {
        'norms': norms,
        'w_in_main': jnp.concatenate([w_in[:, :dt0], w_in[:, dt0 + SSM_HEADS:]], axis=1).astype(BF16),
        'w_in_dt': jnp.pad(w_in[:, dt0:dt0 + SSM_HEADS], ((0, 0), (0, LANES - SSM_HEADS))).astype(BF16),
        'ssm_conv_w': ssm_conv_w[0], 'ssm_conv_b': ssm_conv_b[0].reshape(1, -1),
        'ssm_dt_bias': pad_lanes(ssm_dt_bias[0]), 'ssm_a_log': pad_lanes(ssm_a_log[0]),
        'ssm_d_full': jnp.repeat(ssm_d[0], SSM_HEAD_DIM).reshape(1, -1),
        'ssm_norm': ssm_norm[0].reshape(1, -1),
        'expand': (jnp.arange(LANES)[:, None] == head_of_lane[None, :]).astype(F32),
        'lru_conv_w': lru_conv_w[0], 'lru_conv_b': lru_conv_b[0].reshape(1, -1),
        'lru_w_a': lru_w_a[0].astype(BF16), 'lru_b_a': lru_b_a[0].reshape(1, -1),
        'lru_w_x': lru_w_x[0].astype(BF16), 'lru_b_x': lru_b_x[0].reshape(1, -1),
        'lru_lambda': lru_lambda[0].reshape(1, -1),
        'w_out_ssm': w_out_hyb[0, :SSM_INNER].astype(BF16), 'w_out_lru': w_out_hyb[0, SSM_INNER:].astype(BF16),
        'w_qkv': w_qkv[0].astype(BF16), 'b_qkv': b_qkv[0].reshape(1, -1),
        'attn_sinks': attn_sinks[0],
        'w_out_attn': w_out_attn[0].astype(BF16),
        'w_up': w_up.astype(BF16), 'w_down': w_down.astype(BF16),
    }

    mod = _adaln(jnp.concatenate([c_prompt, c_sample], axis=0), w_mod, b_mod)
    mods_p = [mod[l, :bp][:, None, :] for l in range(depth)]
    mods_s = [jnp.repeat(mod[l, bp:], ls, axis=0)[None] for l in range(depth)]

    zeros = lambda *shape: jnp.zeros(shape, F32)
    st_p = {'ssm_conv': zeros(bp, SSM_CONV - 1, SSM_CONV_DIM), 'ssm': zeros(bp, SSM_INNER, SSM_STATE),
            'lru_conv': zeros(bp, SSM_CONV - 1, LRU_WIDTH), 'lru': zeros(bp, LRU_WIDTH), 'k': None, 'v': None}
    st_s = {'ssm_conv': state_ssm_conv[0], 'ssm': state_ssm[0], 'lru_conv': state_lru_conv[0],
            'lru': state_lru[0], 'k': cache_k[0], 'v': cache_v[0]}

    pos_p = jnp.arange(lp, dtype=jnp.int32)
    pos_s = PAST_LEN + jnp.arange(ls, dtype=jnp.int32)
    y_p, sp = _trunk(x_prompt, mods_p, pos_p, st_p, pw, lp)
    y_s, ss = _trunk(x_sample.reshape(1, bs * ls, d), mods_s, pos_s, st_s, pw, ls)
    return (y_p, y_s.reshape(bs, ls, d)) + sp + ss
```

```python
import functools
import math

import jax
import jax.numpy as jnp
from jax import lax
from jax.experimental import pallas as pl
from jax.experimental.pallas import tpu as pltpu

F32 = jnp.float32
BF16 = jnp.bfloat16

D_MODEL = 2048
PAST_LEN = 16384
SSM_HEADS = 32
SSM_HEAD_DIM = 64
SSM_INNER = SSM_HEADS * SSM_HEAD_DIM
SSM_GROUPS = 8
SSM_STATE = 128
SSM_CONV = 4
SSM_CHUNK = 128
SSM_CONV_DIM = SSM_INNER + 2 * SSM_GROUPS * SSM_STATE
HEADS_PER_GROUP = SSM_HEADS // SSM_GROUPS
GROUP_WIDTH = HEADS_PER_GROUP * SSM_HEAD_DIM
LRU_WIDTH = D_MODEL
LRU_BLOCKS = 8
LRU_BLOCK = LRU_WIDTH // LRU_BLOCKS
LRU_C = 8.0
ATT_HEADS = 32
KV_HEADS = 8
HEAD_DIM = 64
Q_PER_KV = ATT_HEADS // KV_HEADS
WINDOW = 128
ROPE_THETA = 10000.0
D_FF = 4 * D_MODEL
EPS = 1e-6

LANES = 128
SUBLANES = 8
TAIL_ROWS = SUBLANES
VMEM_LIMIT = 56 * 1024 * 1024


INPROJ_COLS = 1024
MLP_COLS = 512
QKV_COLS = 512
LRU_ROWS = 256


def _block_rows(m):
    return {'inproj': min(m, 1024), 'mlp': min(m, 1024), 'qkv': min(m, 512), 'outproj': min(m, 512)}


def _cparams(sem):
    return pltpu.CompilerParams(dimension_semantics=sem, vmem_limit_bytes=VMEM_LIMIT)


def _rms(x, g):
    return x * lax.rsqrt(jnp.mean(x * x, axis=-1, keepdims=True) + EPS) * g


def _silu(x):
    return x * jax.nn.sigmoid(x)


def _softplus(x):
    return jnp.maximum(x, 0.0) + jnp.log1p(jnp.exp(-jnp.abs(x)))


def _neg_expm1(x):
    t = jnp.tanh(0.5 * x)
    return -2.0 * t / (1.0 - t)


def _adaln_kernel(c_ref, w_ref, b_ref, o_ref):
    s = _silu(c_ref[...]).astype(BF16)
    o_ref[0] = jnp.dot(s, w_ref[0].astype(BF16), preferred_element_type=F32) + b_ref[0]


def _adaln(c_all, w_mod, b_mod):
    nb, d = c_all.shape
    depth, _, n = w_mod.shape
    bn = 1024
    return pl.pallas_call(
        _adaln_kernel,
        grid=(depth, n // bn),
        in_specs=[pl.BlockSpec((nb, d), lambda l, j: (0, 0)),
                  pl.BlockSpec((1, d, bn), lambda l, j: (l, 0, j)),
                  pl.BlockSpec((1, 1, bn), lambda l, j: (l, 0, j))],
        out_specs=pl.BlockSpec((1, nb, bn), lambda l, j: (l, 0, j)),
        out_shape=jax.ShapeDtypeStruct((depth, nb, n), F32),
        compiler_params=_cparams(("parallel", "parallel")),
        name="adaln",
    )(c_all, w_mod, b_mod.reshape(depth, 1, n))


def _mod_spec(mod, bm, k):
    if mod.shape[1] == 1:
        return pl.BlockSpec((1, 1, D_MODEL), lambda g, i, *_: (g, 0, k))
    return pl.BlockSpec((1, bm, D_MODEL), lambda g, i, *_: (g, i, k))


def _inproj_kernel(x_ref, g_ref, sh_ref, sc_ref, w_ref, wdt_ref, o_ref, dt_ref, h_scr):
    @pl.when(pl.program_id(2) == 0)
    def _():
        h = _rms(x_ref[0], g_ref[...]) * (1.0 + sc_ref[0]) + sh_ref[0]
        hb = h.astype(BF16)
        h_scr[...] = hb
        dt_ref[0] = jnp.dot(hb, wdt_ref[...], preferred_element_type=F32)

    o_ref[0] = jnp.dot(h_scr[...], w_ref[...], preferred_element_type=F32)


def _inproj(x, gain, mod, w_main, w_dt, bm, bn):
    g_, m, d = x.shape
    n = w_main.shape[1]
    return pl.pallas_call(
        _inproj_kernel,
        grid=(g_, m // bm, n // bn),
        in_specs=[pl.BlockSpec((1, bm, d), lambda g, i, j: (g, i, 0)),
                  pl.BlockSpec((1, d), lambda g, i, j: (0, 0)),
                  _mod_spec(mod, bm, 0), _mod_spec(mod, bm, 1),
                  pl.BlockSpec((d, bn), lambda g, i, j: (0, j)),
                  pl.BlockSpec((d, LANES), lambda g, i, j: (0, 0))],
        out_specs=[pl.BlockSpec((1, bm, bn), lambda g, i, j: (g, i, j)),
                   pl.BlockSpec((1, bm, LANES), lambda g, i, j: (g, i, 0))],
        out_shape=[jax.ShapeDtypeStruct((g_, m, n), F32),
                   jax.ShapeDtypeStruct((g_, m, LANES), F32)],
        scratch_shapes=[pltpu.VMEM((bm, d), BF16)],
        compiler_params=_cparams(("parallel", "parallel", "arbitrary")),
        name="inproj",
    )(x, gain, mod, mod, w_main, w_dt)


def _swap_halves(x):
    lane = lax.broadcasted_iota(jnp.int32, x.shape, 1)
    first = (lane % HEAD_DIM) < (HEAD_DIM // 2)
    return jnp.where(first, pltpu.roll(x, LANES - HEAD_DIM // 2, 1), pltpu.roll(x, HEAD_DIM // 2, 1))


def _qkv_kernel(x_ref, g_ref, sh_ref, sc_ref, w_ref, b_ref, cos_ref, sin_ref, o_ref, *, rope_cols, bn):
    h = (_rms(x_ref[0], g_ref[...]) * (1.0 + sc_ref[0]) + sh_ref[0]).astype(BF16)
    cos = cos_ref[...]
    sin = sin_ref[...]
    for c0 in range(0, w_ref.shape[1], bn):
        acc = jnp.dot(h, w_ref[:, c0:c0 + bn], preferred_element_type=F32) + b_ref[:, c0:c0 + bn]
        if c0 < rope_cols:
            for c in range(c0, c0 + bn, LANES):
                a = acc[:, c - c0:c - c0 + LANES]
                o_ref[0, :, c:c + LANES] = a * cos + _swap_halves(a) * sin
        else:
            o_ref[0, :, c0:c0 + bn] = acc


def _qkv(x, gain, mod, w, b, cos_t, sin_t, bm, bn):
    g_, m, d = x.shape
    n = w.shape[1]
    rope_cols = (ATT_HEADS + KV_HEADS) * HEAD_DIM
    rows_per_g = cos_t.shape[0] // bm
    return pl.pallas_call(
        functools.partial(_qkv_kernel, rope_cols=rope_cols, bn=bn),
        grid=(g_, m // bm),
        in_specs=[pl.BlockSpec((1, bm, d), lambda g, i: (g, i, 0)),
                  pl.BlockSpec((1, d), lambda g, i: (0, 0)),
                  _mod_spec(mod, bm, 0), _mod_spec(mod, bm, 1),
                  pl.BlockSpec((d, n), lambda g, i: (0, 0), pipeline_mode=pl.Buffered(1)),
                  pl.BlockSpec((1, n), lambda g, i: (0, 0)),
                  pl.BlockSpec((bm, LANES), lambda g, i: (i % rows_per_g, 0)),
                  pl.BlockSpec((bm, LANES), lambda g, i: (i % rows_per_g, 0))],
        out_specs=pl.BlockSpec((1, bm, n), lambda g, i: (g, i, 0)),
        out_shape=jax.ShapeDtypeStruct((g_, m, n), F32),
        compiler_params=_cparams(("parallel", "parallel")),
        name="qkv",
    )(x, gain, mod, mod, w, b, cos_t, sin_t)


def _outproj_kernel(*refs, n_lhs):
    lhs = refs[:n_lhs]
    ws = refs[n_lhs:2 * n_lhs]
    x_ref, g_ref, gt_ref, o_ref = refs[2 * n_lhs:]
    acc = jnp.dot(lhs[0][0], ws[0][...], preferred_element_type=F32)
    for a, w in zip(lhs[1:], ws[1:]):
        acc = acc + jnp.dot(a[0], w[...], preferred_element_type=F32)
    o_ref[0] = x_ref[0] + gt_ref[0] * _rms(acc, g_ref[...])


def _outproj(lhs_list, w_list, x, gain, mod, bm):
    g_, m, d = x.shape
    n_lhs = len(lhs_list)
    lhs_specs = [pl.BlockSpec((1, bm, a.shape[2]), lambda g, i: (g, i, 0)) for a in lhs_list]
    w_specs = [pl.BlockSpec(w.shape, lambda g, i: (0, 0), pipeline_mode=pl.Buffered(1)) for w in w_list]
    if mod.shape[1] == 1:
        gt_spec = pl.BlockSpec((1, 1, d), lambda g, i: (g, 0, 2))
    else:
        gt_spec = pl.BlockSpec((1, bm, d), lambda g, i: (g, i, 2))
    return pl.pallas_call(
        functools.partial(_outproj_kernel, n_lhs=n_lhs),
        grid=(g_, m // bm),
        in_specs=lhs_specs + w_specs + [pl.BlockSpec((1, bm, d), lambda g, i: (g, i, 0)),
                                        pl.BlockSpec((1, d), lambda g, i: (0, 0)), gt_spec],
        out_specs=pl.BlockSpec((1, bm, d), lambda g, i: (g, i, 0)),
        out_shape=jax.ShapeDtypeStruct((g_, m, d), F32),
        compiler_params=_cparams(("parallel", "parallel")),
        name="outproj",
    )(*lhs_list, *w_list, x, gain, mod)


def _mlp_kernel(x_ref, g2_ref, sh_ref, sc_ref, gt_ref, wu_ref, wd_ref, g3_ref, o_ref, h_scr):
    f = pl.program_id(2)

    @pl.when(f == 0)
    def _():
        h = _rms(x_ref[0], g2_ref[...]) * (1.0 + sc_ref[0]) + sh_ref[0]
        h_scr[...] = h.astype(BF16)
        o_ref[0] = jnp.zeros(o_ref.shape[1:], F32)

    u = jnp.dot(h_scr[...], wu_ref[...], preferred_element_type=F32)
    u = jnp.square(jnp.maximum(u, 0.0)).astype(BF16)
    o_ref[0] += jnp.dot(u, wd_ref[...], preferred_element_type=F32)

    @pl.when(f == pl.num_programs(2) - 1)
    def _():
        o_ref[0] = x_ref[0] + gt_ref[0] * _rms(o_ref[0], g3_ref[...])


def _mlp(x, g2, g3, mod, w_up, w_down, bm, bf):
    g_, m, d = x.shape
    dff = w_up.shape[1]
    if mod.shape[1] == 1:
        mspec = lambda k: pl.BlockSpec((1, 1, d), lambda g, i, f: (g, 0, k))
    else:
        mspec = lambda k: pl.BlockSpec((1, bm, d), lambda g, i, f: (g, i, k))
    return pl.pallas_call(
        _mlp_kernel,
        grid=(g_, m // bm, dff // bf),
        in_specs=[pl.BlockSpec((1, bm, d), lambda g, i, f: (g, i, 0), pipeline_mode=pl.Buffered(1)),
                  pl.BlockSpec((1, d), lambda g, i, f: (0, 0)),
                  mspec(3), mspec(4), mspec(5),
                  pl.BlockSpec((d, bf), lambda g, i, f: (0, f)),
                  pl.BlockSpec((bf, d), lambda g, i, f: (f, 0)),
                  pl.BlockSpec((1, d), lambda g, i, f: (0, 0))],
        out_specs=pl.BlockSpec((1, bm, d), lambda g, i, f: (g, i, 0)),
        out_shape=jax.ShapeDtypeStruct((g_, m, d), F32),
        scratch_shapes=[pltpu.VMEM((bm, d), BF16)],
        compiler_params=_cparams(("parallel", "parallel", "arbitrary")),
        name="mlp",
    )(x, g2, mod, mod, mod, w_up, w_down, g3)


def _dwconv(raw, tail, w, b):
    t = raw.shape[0]
    k = w.shape[0]
    ext = jnp.concatenate([tail, raw], axis=0)
    y = b + raw * w[k - 1:k]
    for j in range(k - 1):
        off = TAIL_ROWS - (k - 1) + j
        y = y + ext[off:off + t] * w[j:j + 1]
    return y, ext[t:t + TAIL_ROWS]


def _cumsum_rows(x):
    n = x.shape[0]
    row = lax.broadcasted_iota(jnp.int32, x.shape, 0)
    d = 1
    while d < n:
        x = x + jnp.where(row >= d, pltpu.roll(x, d, 0), 0.0)
        d *= 2
    return x


def _ssd_kernel(z_ref, xs_ref, bc_ref, dt_ref, buf_ref, h0_ref, cw_ref, cb_ref, dtb_ref, alog_ref,
                dsk_ref, ng_ref, exp_ref, y_ref, st_ref, tail_scr, *, q_valid):
    q = SSM_CHUNK
    c = pl.program_id(1)

    @pl.when(c == 0)
    def _():
        tail_scr[...] = buf_ref[0]
        st_ref[0] = h0_ref[0]

    def padded(v):
        if q_valid == q:
            return v
        return jnp.concatenate([v, jnp.zeros((q - q_valid, v.shape[1]), v.dtype)], axis=0)

    cw = cw_ref[...]
    cb = cb_ref[...]
    xs, tail_x = _dwconv(xs_ref[0], tail_scr[:, :SSM_INNER], cw[:, :SSM_INNER], cb[:, :SSM_INNER])
    bcm, tail_b = _dwconv(bc_ref[0], tail_scr[:, SSM_INNER:], cw[:, SSM_INNER:], cb[:, SSM_INNER:])
    tail_scr[:, :SSM_INNER] = tail_x
    tail_scr[:, SSM_INNER:] = tail_b
    xs = padded(_silu(xs))
    bcm = padded(_silu(bcm))
    z = padded(z_ref[0])

    dt = _softplus(dt_ref[0] + dtb_ref[...])
    dt = padded(dt)
    a = -jnp.exp(alog_ref[...])
    acs = _cumsum_rows(dt * a)
    acs_t = acs.T

    expand = exp_ref[...]
    hp = lax.Precision.HIGHEST
    dt_full = jnp.dot(dt, expand, precision=hp, preferred_element_type=F32)
    acs_full = jnp.dot(acs, expand, precision=hp, preferred_element_type=F32)
    xdt = xs * dt_full
    dec_out = jnp.exp(acs_full)
    xw = (xdt * jnp.exp(acs_full[q - 1:q] - acs_full)).astype(BF16)
    xdt_b = xdt.astype(BF16)
    bc_b = bcm.astype(BF16)

    row = lax.broadcasted_iota(jnp.int32, (q, q), 0)
    col = lax.broadcasted_iota(jnp.int32, (q, q), 1)
    causal = row >= col
    lane = lax.broadcasted_iota(jnp.int32, (q, LANES), 1)
    lo_half = lane < SSM_HEAD_DIM
    nt = (((1,), (1,)), ((), ()))
    tn = (((0,), (0,)), ((), ()))
    n_bc = SSM_GROUPS * SSM_STATE

    for g in range(SSM_GROUPS):
        b_g = bc_b[:, g * SSM_STATE:(g + 1) * SSM_STATE]
        c_g = bc_b[:, n_bc + g * SSM_STATE:n_bc + (g + 1) * SSM_STATE]
        cbm = lax.dot_general(c_g, b_g, nt, preferred_element_type=F32)
        g0 = g * GROUP_WIDTH
        h_g = st_ref[0, g0:g0 + GROUP_WIDTH, :]
        y_off = lax.dot_general(c_g, h_g.astype(BF16), nt, preferred_element_type=F32)
        y_g = y_off * dec_out[:, g0:g0 + GROUP_WIDTH]
        pieces = []
        for pair in range(HEADS_PER_GROUP // 2):
            x_pair = xdt_b[:, g0 + pair * LANES:g0 + (pair + 1) * LANES]
            acc = None
            for e in range(2):
                h = g * HEADS_PER_GROUP + pair * 2 + e
                seg = jnp.exp(jnp.where(causal, acs[:, h:h + 1] - acs_t[h:h + 1, :], -jnp.inf))
                m = (cbm * seg).astype(BF16)
                x_h = jnp.where(lo_half if e == 0 else jnp.logical_not(lo_half), x_pair, jnp.zeros_like(x_pair))
                part = jnp.dot(m, x_h, preferred_element_type=F32)
                acc = part if acc is None else acc + part
            pieces.append(acc)
        y_g = y_g + jnp.concatenate(pieces, axis=1)
        y_g = y_g + xs[:, g0:g0 + GROUP_WIDTH] * dsk_ref[:, g0:g0 + GROUP_WIDTH]
        y_g = y_g * _silu(z[:, g0:g0 + GROUP_WIDTH])
        y_g = y_g * lax.rsqrt(jnp.mean(y_g * y_g, axis=-1, keepdims=True) + EPS)
        y_g = y_g * ng_ref[:, g0:g0 + GROUP_WIDTH]
        y_ref[0, :, g0:g0 + GROUP_WIDTH] = y_g[:q_valid].astype(y_ref.dtype)

        st_new = lax.dot_general(xw[:, g0:g0 + GROUP_WIDTH], b_g, tn, preferred_element_type=F32)
        for r in range(HEADS_PER_GROUP):
            h = g * HEADS_PER_GROUP + r
            decay = jnp.exp(acs_t[h:h + 1, q - 1:q])
            r0 = g0 + r * SSM_HEAD_DIM
            st_ref[0, r0:r0 + SSM_HEAD_DIM, :] = (
                h_g[r * SSM_HEAD_DIM:(r + 1) * SSM_HEAD_DIM] * decay
                + st_new[r * SSM_HEAD_DIM:(r + 1) * SSM_HEAD_DIM])


def _ssd(proj, dtp, buf8, h0, conv_w, conv_b, dt_bias, a_log, d_full, norm_g, expand, rows):
    b_, l, _ = proj.shape
    nc = l // rows
    st_rows = SSM_INNER
    return pl.pallas_call(
        functools.partial(_ssd_kernel, q_valid=rows),
        grid=(b_, nc),
        in_specs=[pl.BlockSpec((1, rows, SSM_INNER), lambda b, c: (b, c, 0)),
                  pl.BlockSpec((1, rows, SSM_INNER), lambda b, c: (b, c, 1)),
                  pl.BlockSpec((1, rows, SSM_INNER), lambda b, c: (b, c, 2)),
                  pl.BlockSpec((1, rows, LANES), lambda b, c: (b, c, 0)),
                  pl.BlockSpec((1, TAIL_ROWS, SSM_CONV_DIM), lambda b, c: (b, 0, 0)),
                  pl.BlockSpec((1, st_rows, SSM_STATE), lambda b, c: (b, 0, 0)),
                  pl.BlockSpec((SSM_CONV, SSM_CONV_DIM), lambda b, c: (0, 0)),
                  pl.BlockSpec((1, SSM_CONV_DIM), lambda b, c: (0, 0)),
                  pl.BlockSpec((1, LANES), lambda b, c: (0, 0)),
                  pl.BlockSpec((1, LANES), lambda b, c: (0, 0)),
                  pl.BlockSpec((1, SSM_INNER), lambda b, c: (0, 0)),
                  pl.BlockSpec((1, SSM_INNER), lambda b, c: (0, 0)),
                  pl.BlockSpec((LANES, SSM_INNER), lambda b, c: (0, 0))],
        out_specs=[pl.BlockSpec((1, rows, SSM_INNER), lambda b, c: (b, c, 0)),
                   pl.BlockSpec((1, st_rows, SSM_STATE), lambda b, c: (b, 0, 0))],
        out_shape=[jax.ShapeDtypeStruct((b_, l, SSM_INNER), BF16),
                   jax.ShapeDtypeStruct((b_, st_rows, SSM_STATE), F32)],
        scratch_shapes=[pltpu.VMEM((TAIL_ROWS, SSM_CONV_DIM), F32)],
        compiler_params=_cparams(("parallel", "arbitrary")),
        name="ssd",
    )(proj, proj, proj, dtp, buf8, h0, conv_w, conv_b, dt_bias, a_log, d_full, norm_g, expand)


def _lru_kernel(gate_ref, xr_ref, buf_ref, h0_ref, cw_ref, cb_ref, wa_ref, ba_ref, wx_ref, bx_ref,
                lam_ref, y_ref, hl_ref, tail_scr, a_scr, u_scr):
    t = xr_ref.shape[1]
    c = pl.program_id(1)

    @pl.when(c == 0)
    def _():
        tail_scr[...] = buf_ref[0]
        hl_ref[0] = h0_ref[0]

    xc, tail = _dwconv(xr_ref[0], tail_scr[...], cw_ref[...], cb_ref[...])
    tail_scr[...] = tail
    sp = _softplus(-lam_ref[...])
    for k in range(LRU_BLOCKS):
        k0 = k * LRU_BLOCK
        xk = xc[:, k0:k0 + LRU_BLOCK]
        xkb = xk.astype(BF16)
        gr = jax.nn.sigmoid(jnp.dot(xkb, wa_ref[k], preferred_element_type=F32) + ba_ref[:, k0:k0 + LRU_BLOCK])
        gi = jax.nn.sigmoid(jnp.dot(xkb, wx_ref[k], preferred_element_type=F32) + bx_ref[:, k0:k0 + LRU_BLOCK])
        log_a = -LRU_C * gr * sp[:, k0:k0 + LRU_BLOCK]
        a_scr[:, k0:k0 + LRU_BLOCK] = jnp.exp(log_a)
        u_scr[:, k0:k0 + LRU_BLOCK] = jnp.sqrt(_neg_expm1(2.0 * log_a)) * gi * xk

    row = lax.broadcasted_iota(jnp.int32, (SUBLANES, LRU_WIDTH), 0)

    def tile(i, h_prev):
        r0 = pl.multiple_of(i * SUBLANES, SUBLANES)
        a = a_scr[pl.ds(r0, SUBLANES), :]
        u = u_scr[pl.ds(r0, SUBLANES), :]
        d = 1
        while d < SUBLANES:
            keep = row >= d
            u = u + a * jnp.where(keep, pltpu.roll(u, d, 0), 0.0)
            a = a * jnp.where(keep, pltpu.roll(a, d, 0), 1.0)
            d *= 2
        h = u + a * h_prev
        u_scr[pl.ds(r0, SUBLANES), :] = h
        return h[SUBLANES - 1:SUBLANES, :]

    h_last = lax.fori_loop(0, t // SUBLANES, tile, hl_ref[0])
    hl_ref[0] = h_last
    y_ref[0] = (u_scr[...] * jax.nn.gelu(gate_ref[0])).astype(y_ref.dtype)


def _lru(proj, buf8, h0, conv_w, conv_b, w_a, b_a, w_x, b_x, lam, rows, col0):
    b_, l, _ = proj.shape
    w = LRU_WIDTH
    full2 = lambda shape: pl.BlockSpec(shape, lambda b, c: (0,) * len(shape))
    return pl.pallas_call(
        _lru_kernel,
        grid=(b_, l // rows),
        in_specs=[pl.BlockSpec((1, rows, w), lambda b, c: (b, c, col0)),
                  pl.BlockSpec((1, rows, w), lambda b, c: (b, c, col0 + 1)),
                  pl.BlockSpec((1, TAIL_ROWS, w), lambda b, c: (b, 0, 0)),
                  pl.BlockSpec((1, 1, w), lambda b, c: (b, 0, 0)),
                  full2((SSM_CONV, w)), full2((1, w)),
                  full2((LRU_BLOCKS, LRU_BLOCK, LRU_BLOCK)), full2((1, w)),
                  full2((LRU_BLOCKS, LRU_BLOCK, LRU_BLOCK)), full2((1, w)),
                  full2((1, w))],
        out_specs=[pl.BlockSpec((1, rows, w), lambda b, c: (b, c, 0)),
                   pl.BlockSpec((1, 1, w), lambda b, c: (b, 0, 0))],
        out_shape=[jax.ShapeDtypeStruct((b_, l, w), BF16),
                   jax.ShapeDtypeStruct((b_, 1, w), F32)],
        scratch_shapes=[pltpu.VMEM((TAIL_ROWS, w), F32), pltpu.VMEM((rows, w), F32),
                        pltpu.VMEM((rows, w), F32)],
        compiler_params=_cparams(("parallel", "arbitrary")),
        name="lru",
    )(proj, proj, buf8, h0, conv_w, conv_b, w_a, b_a, w_x, b_x, lam)


def _softmax_pv(s, valid, sink_col, v_b):
    s = jnp.where(valid, s, -jnp.inf)
    m = jnp.maximum(jnp.max(s, axis=1, keepdims=True), sink_col)
    p = jnp.exp(s - m)
    den = jnp.sum(p, axis=1, keepdims=True) + jnp.exp(sink_col - m)
    return jnp.dot(p.astype(BF16), v_b, preferred_element_type=F32) / den


def _head_variants(slab, half):
    lane = lax.broadcasted_iota(jnp.int32, slab.shape, 1)
    keep = (lane < HEAD_DIM) if half == 0 else (lane >= HEAD_DIM)
    own = jnp.where(keep, slab, 0.0)
    other = pltpu.roll(own, HEAD_DIM, 1)
    return (own, other) if half == 0 else (other, own)


def _attn_prompt_kernel(sink_ref, q_ref, kc_ref, kp_ref, vc_ref, vp_ref, o_ref):
    n = pl.program_id(1)
    w = WINDOW
    has_prev = n > 0
    k2 = jnp.concatenate([jnp.where(has_prev, kp_ref[0], 0.0), kc_ref[0]], axis=0)
    v2 = jnp.concatenate([jnp.where(has_prev, vp_ref[0], 0.0), vc_ref[0]], axis=0)
    t = lax.broadcasted_iota(jnp.int32, (2 * w, 2 * w), 0) % w
    j = lax.broadcasted_iota(jnp.int32, (2 * w, 2 * w), 1)
    valid = (j > t) & (j <= t + w)
    nt = (((1,), (1,)), ((), ()))
    scale = HEAD_DIM ** -0.5
    heads = range(KV_HEADS)
    halves = [(e, c) for e in heads for c in range(2)]
    vvs, ss = [], []
    for e in heads:
        slab, half = divmod(e, 2)
        k_lo, k_hi = _head_variants(k2[:, slab * LANES:(slab + 1) * LANES], half)
        v_lo, v_hi = _head_variants(v2[:, slab * LANES:(slab + 1) * LANES], half)
        kk = jnp.concatenate([k_lo, k_hi], axis=0).astype(BF16)
        vvs.append(jnp.concatenate([v_lo, v_hi], axis=0).astype(BF16))
        c0 = 2 * e * LANES
        qq = jnp.concatenate([q_ref[0, :, c0:c0 + LANES], q_ref[0, :, c0 + LANES:c0 + 2 * LANES]],
                             axis=0).astype(BF16)
        ss.append(lax.dot_general(qq, kk, nt, preferred_element_type=F32) * scale)
    scs = [jnp.where(valid, ss[e][:, c * 2 * w:(c + 1) * 2 * w], -jnp.inf) for e, c in halves]
    sinks = [jnp.concatenate([jnp.full((w, 1), sink_ref[Q_PER_KV * e + c], F32),
                              jnp.full((w, 1), sink_ref[Q_PER_KV * e + 2 + c], F32)], axis=0) for e, c in halves]
    ms = [jnp.maximum(jnp.max(sc, axis=1, keepdims=True), sk) for sc, sk in zip(scs, sinks)]
    ps = [jnp.exp(sc - m) for sc, m in zip(scs, ms)]
    dens = [jnp.sum(p, axis=1, keepdims=True) + jnp.exp(sk - m) for p, sk, m in zip(ps, sinks, ms)]
    pn = [(p * (1.0 / den)).astype(BF16) for p, den in zip(ps, dens)]
    for e in heads:
        o = jnp.dot(jnp.concatenate(pn[2 * e:2 * e + 2], axis=1), vvs[e], preferred_element_type=F32)
        c0 = 2 * e * LANES
        o_ref[0, :, c0:c0 + LANES] = o[:w].astype(o_ref.dtype)
        o_ref[0, :, c0 + LANES:c0 + 2 * LANES] = o[w:].astype(o_ref.dtype)


def _attn_prompt(qkv, sinks):
    b_, l, _ = qkv.shape
    w = WINDOW
    qw = ATT_HEADS * HEAD_DIM
    kvw = KV_HEADS * HEAD_DIM
    k_blk = qw // kvw
    prev = lambda n: jnp.maximum(n - 1, 0)
    return pl.pallas_call(
        _attn_prompt_kernel,
        grid=(b_, l // w),
        in_specs=[pl.BlockSpec(memory_space=pltpu.SMEM),
                  pl.BlockSpec((1, w, qw), lambda b, n: (b, n, 0)),
                  pl.BlockSpec((1, w, kvw), lambda b, n: (b, n, k_blk)),
                  pl.BlockSpec((1, w, kvw), lambda b, n: (b, prev(n), k_blk)),
                  pl.BlockSpec((1, w, kvw), lambda b, n: (b, n, k_blk + 1)),
                  pl.BlockSpec((1, w, kvw), lambda b, n: (b, prev(n), k_blk + 1))],
        out_specs=pl.BlockSpec((1, w, qw), lambda b, n: (b, n, 0)),
        out_shape=jax.ShapeDtypeStruct((b_, l, qw), BF16),
        compiler_params=_cparams(("parallel", "parallel")),
        name="attn_prompt",
    )(sinks, qkv, qkv, qkv, qkv, qkv)


def _attn_sample_kernel(sink_ref, qkv_ref, ck_ref, cv_ref, o_ref, nk_ref, nv_ref):
    l = qkv_ref.shape[1]
    n_keep = ck_ref.shape[1]
    kcol = ATT_HEADS * HEAD_DIM
    vcol = kcol + KV_HEADS * HEAD_DIM
    qkv = qkv_ref[0]
    k_all = jnp.concatenate([ck_ref[0], qkv[:, kcol:vcol]], axis=0)
    v_all = jnp.concatenate([cv_ref[0], qkv[:, vcol:]], axis=0)
    nk_ref[0] = k_all[l:]
    nv_ref[0] = v_all[l:]
    kb = k_all.astype(BF16)
    vb = v_all.astype(BF16)
    qb = qkv[:, :kcol].astype(BF16)
    rows = Q_PER_KV * l
    t = lax.broadcasted_iota(jnp.int32, (rows, n_keep + l), 0) % l
    j = lax.broadcasted_iota(jnp.int32, (rows, n_keep + l), 1)
    rel = jnp.where(j < n_keep, t + n_keep - j, t - (j - n_keep))
    valid = (rel >= 0) & (rel < WINDOW)
    nt = (((1,), (1,)), ((), ()))
    outs = []
    for e in range(KV_HEADS):
        k_e = kb[:, e * HEAD_DIM:(e + 1) * HEAD_DIM]
        v_e = vb[:, e * HEAD_DIM:(e + 1) * HEAD_DIM]
        q_e = jnp.concatenate([qb[:, (e * Q_PER_KV + r) * HEAD_DIM:(e * Q_PER_KV + r + 1) * HEAD_DIM]
                               for r in range(Q_PER_KV)], axis=0)
        s = lax.dot_general(q_e, k_e, nt, preferred_element_type=F32) * (HEAD_DIM ** -0.5)
        sink_col = jnp.concatenate(
            [jnp.full((l, 1), sink_ref[e * Q_PER_KV + r], F32) for r in range(Q_PER_KV)], axis=0)
        o = _softmax_pv(s, valid, sink_col, v_e)
        outs.extend(o[r * l:(r + 1) * l] for r in range(Q_PER_KV))
    o_ref[0] = jnp.concatenate(outs, axis=1).astype(o_ref.dtype)


def _attn_sample(qkv, cache_k, cache_v, sinks):
    b_, l, n = qkv.shape
    n_keep, kvw = cache_k.shape[1:]
    return pl.pallas_call(
        _attn_sample_kernel,
        grid=(b_,),
        in_specs=[pl.BlockSpec(memory_space=pltpu.SMEM),
                  pl.BlockSpec((1, l, n), lambda b: (b, 0, 0)),
                  pl.BlockSpec((1, n_keep, kvw), lambda b: (b, 0, 0)),
                  pl.BlockSpec((1, n_keep, kvw), lambda b: (b, 0, 0))],
        out_specs=[pl.BlockSpec((1, l, ATT_HEADS * HEAD_DIM), lambda b: (b, 0, 0)),
                   pl.BlockSpec((1, n_keep, kvw), lambda b: (b, 0, 0)),
                   pl.BlockSpec((1, n_keep, kvw), lambda b: (b, 0, 0))],
        out_shape=[jax.ShapeDtypeStruct((b_, l, ATT_HEADS * HEAD_DIM), BF16),
                   jax.ShapeDtypeStruct((b_, n_keep, kvw), F32),
                   jax.ShapeDtypeStruct((b_, n_keep, kvw), F32)],
        compiler_params=_cparams(("parallel",)),
        name="attn_sample",
    )(sinks, qkv, cache_k, cache_v)


def _rope_tables(pos):
    half = HEAD_DIM // 2
    inv_freq = ROPE_THETA ** (-jnp.arange(half, dtype=F32) / half)
    ang = pos.astype(F32)[:, None] * inv_freq[None, :]
    cos = jnp.cos(ang)
    sin = jnp.sin(ang)
    reps = LANES // HEAD_DIM
    return (jnp.tile(jnp.concatenate([cos, cos], axis=1), (1, reps)),
            jnp.tile(jnp.concatenate([-sin, sin], axis=1), (1, reps)))


def _tail_pad(buf):
    return jnp.pad(buf, ((0, 0), (TAIL_ROWS - buf.shape[1], 0), (0, 0)))


def _trunk(x, mods, pos, st, pw, seq_len):
    g_, m, d = x.shape
    nseq = g_ * m // seq_len
    blk = _block_rows(m)
    chunk = min(seq_len, SSM_CHUNK)
    lru_rows = min(seq_len, LRU_ROWS)

    proj, dtp = _inproj(x, pw['norms'][0, 0:1], mods[0], pw['w_in_main'], pw['w_in_dt'], blk['inproj'],
                        INPROJ_COLS)
    n_main = proj.shape[-1]
    proj_s = proj.reshape(nseq, seq_len, n_main)
    dtp_s = dtp.reshape(nseq, seq_len, LANES)
    y_ssm, ssm_h = _ssd(proj_s, dtp_s, _tail_pad(st['ssm_conv']), st['ssm'].reshape(nseq, SSM_INNER, SSM_STATE),
                        pw['ssm_conv_w'], pw['ssm_conv_b'], pw['ssm_dt_bias'], pw['ssm_a_log'], pw['ssm_d_full'],
                        pw['ssm_norm'], pw['expand'], chunk)
    y_lru, lru_h = _lru(proj_s, _tail_pad(st['lru_conv']), st['lru'].reshape(nseq, 1, LRU_WIDTH),
                        pw['lru_conv_w'], pw['lru_conv_b'], pw['lru_w_a'], pw['lru_b_a'], pw['lru_w_x'],
                        pw['lru_b_x'], pw['lru_lambda'], lru_rows, 3)
    keep = SSM_CONV - 1
    new_ssm_conv = proj_s[:, seq_len - keep:, SSM_INNER:SSM_INNER + SSM_CONV_DIM]
    new_lru_conv = proj_s[:, seq_len - keep:, n_main - LRU_WIDTH:]
    x = _outproj([y_ssm.reshape(g_, m, SSM_INNER), y_lru.reshape(g_, m, LRU_WIDTH)],
                 [pw['w_out_ssm'], pw['w_out_lru']], x, pw['norms'][0, 1:2], mods[0], blk['outproj'])
    x = _mlp(x, pw['norms'][0, 2:3], pw['norms'][0, 3:4], mods[0], pw['w_up'][0], pw['w_down'][0], blk['mlp'],
             MLP_COLS)

    cos_t, sin_t = _rope_tables(pos)
    if seq_len < blk['qkv']:
        cos_t = jnp.tile(cos_t, (blk['qkv'] // seq_len, 1))
        sin_t = jnp.tile(sin_t, (blk['qkv'] // seq_len, 1))
    qkv = _qkv(x, pw['norms'][1, 0:1], mods[1], pw['w_qkv'], pw['b_qkv'], cos_t, sin_t, blk['qkv'], QKV_COLS)
    qkv_s = qkv.reshape(nseq, seq_len, qkv.shape[-1])
    kcol = ATT_HEADS * HEAD_DIM
    vcol = kcol + KV_HEADS * HEAD_DIM
    if st['k'] is None:
        o = _attn_prompt(qkv_s, pw['attn_sinks'])
        n_keep = min(WINDOW, seq_len)
        k_new = qkv_s[:, seq_len - n_keep:, kcol:vcol]
        v_new = qkv_s[:, seq_len - n_keep:, vcol:]
    else:
        n_keep = st['k'].shape[1]
        o, k_new, v_new = _attn_sample(qkv_s, st['k'].reshape(nseq, n_keep, KV_HEADS * HEAD_DIM),
                                       st['v'].reshape(nseq, n_keep, KV_HEADS * HEAD_DIM), pw['attn_sinks'])
    k_new = k_new.reshape(nseq, n_keep, KV_HEADS, HEAD_DIM)
    v_new = v_new.reshape(nseq, n_keep, KV_HEADS, HEAD_DIM)
    x = _outproj([o.reshape(g_, m, kcol)], [pw['w_out_attn']], x, pw['norms'][1, 1:2], mods[1], blk['outproj'])
    x = _mlp(x, pw['norms'][1, 2:3], pw['norms'][1, 3:4], mods[1], pw['w_up'][1], pw['w_down'][1], blk['mlp'],
             MLP_COLS)

    states = (new_ssm_conv[None], ssm_h.reshape(1, nseq, SSM_HEADS, SSM_HEAD_DIM, SSM_STATE),
              new_lru_conv[None], lru_h.reshape(1, nseq, LRU_WIDTH), k_new[None], v_new[None])
    return x, states


def kernel(x_prompt, x_sample, state_ssm_conv, state_ssm, state_lru_conv, state_lru, cache_k, cache_v, c_prompt, c_sample, w_mod, b_mod, norms, w_in_hyb, ssm_conv_w, ssm_conv_b, ssm_dt_bias, ssm_a_log, ssm_d, ssm_norm, lru_conv_w, lru_conv_b, lru_w_a, lru_b_a, lru_w_x, lru_b_x, lru_lambda, w_out_hyb, w_qkv, b_qkv, attn_sinks, w_out_attn, w_up, w_down):
    bp, lp, d = x_prompt.shape
    bs, ls, _ = x_sample.shape
    depth = w_mod.shape[0]

    dt0 = SSM_INNER + SSM_CONV_DIM
    w_in = w_in_hyb[0]
    pad_lanes = lambda v: jnp.pad(v.reshape(1, -1), ((0, 0), (0, LANES - v.size)))
    head_of_lane = jnp.arange(SSM_INNER) // SSM_HEAD_DIM
    pw = {
        'norms': norms,
        'w_in_main': jnp.concatenate([w_in[:, :dt0], w_in[:, dt0 + SSM_HEADS:]], axis=1).astype(BF16),
        'w_in_dt': jnp.pad(w_in[:, dt0:dt0 + SSM_HEADS], ((0, 0), (0, LANES - SSM_HEADS))).astype(BF16),
        'ssm_conv_w': ssm_conv_w[0], 'ssm_conv_b': ssm_conv_b[0].reshape(1, -1),
        'ssm_dt_bias': pad_lanes(ssm_dt_bias[0]), 'ssm_a_log': pad_lanes(ssm_a_log[0]),
        'ssm_d_full': jnp.repeat(ssm_d[0], SSM_HEAD_DIM).reshape(1, -1),
        'ssm_norm': ssm_norm[0].reshape(1, -1),
        'expand': (jnp.arange(LANES)[:, None] == head_of_lane[None, :]).astype(F32),
        'lru_conv_w': lru_conv_w[0], 'lru_conv_b': lru_conv_b[0].reshape(1, -1),
        'lru_w_a': lru_w_a[0].astype(BF16), 'lru_b_a': lru_b_a[0].reshape(1, -1),
        'lru_w_x': lru_w_x[0].astype(BF16), 'lru_b_x': lru_b_x[0].reshape(1, -1),
        'lru_lambda': lru_lambda[0].reshape(1, -1),
        'w_out_ssm': w_out_hyb[0, :SSM_INNER].astype(BF16), 'w_out_lru': w_out_hyb[0, SSM_INNER:].astype(BF16),
        'w_qkv': w_qkv[0].astype(BF16), 'b_qkv': b_qkv[0].reshape(1, -1),
        'attn_sinks': attn_sinks[0],
        'w_out_attn': w_out_attn[0].astype(BF16),
        'w_up': w_up.astype(BF16), 'w_down': w_down.astype(BF16),
    }

    mod = _adaln(jnp.concatenate([c_prompt, c_sample], axis=0), w_mod, b_mod)
    mods_p = [mod[l, :bp][:, None, :] for l in range(depth)]
    mods_s = [jnp.repeat(mod[l, bp:], ls, axis=0)[None] for l in range(depth)]

    zeros = lambda *shape: jnp.zeros(shape, F32)
    st_p = {'ssm_conv': zeros(bp, SSM_CONV - 1, SSM_CONV_DIM), 'ssm': zeros(bp, SSM_INNER, SSM_STATE),
            'lru_conv': zeros(bp, SSM_CONV - 1, LRU_WIDTH), 'lru': zeros(bp, LRU_WIDTH), 'k': None, 'v': None}
    st_s = {'ssm_conv': state_ssm_conv[0], 'ssm': state_ssm[0], 'lru_conv': state_lru_conv[0],
            'lru': state_lru[0], 'k': cache_k[0], 'v': cache_v[0]}

    pos_p = jnp.arange(lp, dtype=jnp.int32)
    pos_s = PAST_LEN + jnp.arange(ls, dtype=jnp.int32)
    y_p, sp = _trunk(x_prompt, mods_p, pos_p, st_p, pw, lp)
    y_s, ss = _trunk(x_sample.reshape(1, bs * ls, d), mods_s, pos_s, st_s, pw, ls)
    return (y_p, y_s.reshape(bs, ls, d)) + sp + ss
```

```python
import functools
import math

import jax
import jax.numpy as jnp
from jax import lax
from jax.experimental import pallas as pl
from jax.experimental.pallas import tpu as pltpu

F32 = jnp.float32
BF16 = jnp.bfloat16

D_MODEL = 2048
PAST_LEN = 16384
SSM_HEADS = 32
SSM_HEAD_DIM = 64
SSM_INNER = SSM_HEADS * SSM_HEAD_DIM
SSM_GROUPS = 8
SSM_STATE = 128
SSM_CONV = 4
SSM_CHUNK = 128
SSM_CONV_DIM = SSM_INNER + 2 * SSM_GROUPS * SSM_STATE
HEADS_PER_GROUP = SSM_HEADS // SSM_GROUPS
GROUP_WIDTH = HEADS_PER_GROUP * SSM_HEAD_DIM
LRU_WIDTH = D_MODEL
LRU_BLOCKS = 8
LRU_BLOCK = LRU_WIDTH // LRU_BLOCKS
LRU_C = 8.0
ATT_HEADS = 32
KV_HEADS = 8
HEAD_DIM = 64
Q_PER_KV = ATT_HEADS // KV_HEADS
WINDOW = 128
ROPE_THETA = 10000.0
D_FF = 4 * D_MODEL
EPS = 1e-6

LANES = 128
SUBLANES = 8
TAIL_ROWS = SUBLANES
VMEM_LIMIT = 56 * 1024 * 1024


INPROJ_COLS = 1024
MLP_COLS = 512
QKV_COLS = 512
LRU_ROWS = 256


def _block_rows(m):
    return {'inproj': min(m, 1024), 'mlp': min(m, 1024), 'qkv': min(m, 512), 'outproj': min(m, 512)}


def _cparams(sem):
    return pltpu.CompilerParams(dimension_semantics=sem, vmem_limit_bytes=VMEM_LIMIT)


def _rms(x, g):
    return x * lax.rsqrt(jnp.mean(x * x, axis=-1, keepdims=True) + EPS) * g


ROW_CHUNK = 16
ROW_UNROLL = 4


def _for_row_chunks(n_rows, body):
    def step(i, carry):
        body(pl.multiple_of(i * ROW_CHUNK, ROW_CHUNK))
        return carry

    lax.fori_loop(0, n_rows // ROW_CHUNK, step, 0, unroll=ROW_UNROLL)


def _mod_rows(ref, r0):
    return ref[0] if ref.shape[1] == 1 else ref[0, pl.ds(r0, ROW_CHUNK), :]


def _prenorm_to(h_scr, x_ref, g_ref, sc_ref, sh_ref):
    def body(r0):
        x = x_ref[0, pl.ds(r0, ROW_CHUNK), :]
        h = _rms(x, g_ref[...]) * (1.0 + _mod_rows(sc_ref, r0)) + _mod_rows(sh_ref, r0)
        h_scr[pl.ds(r0, ROW_CHUNK), :] = h.astype(BF16)

    _for_row_chunks(x_ref.shape[1], body)


def _sigmoid(x):
    return 0.5 * (jnp.tanh(0.5 * x) + 1.0)


def _silu(x):
    return x * _sigmoid(x)


def _softplus(x):
    return jnp.maximum(x, 0.0) + jnp.log1p(jnp.exp(-jnp.abs(x)))


def _sqrt_neg_expm1(x):
    s = -jnp.tanh(0.5 * x)
    return jnp.sqrt(2.0 * s) * lax.rsqrt(1.0 + s)


def _split3(x):
    hi = x.astype(BF16)
    r = x - hi.astype(F32)
    mid = r.astype(BF16)
    lo = (r - mid.astype(F32)).astype(BF16)
    return [hi, mid, lo]


def _adaln_kernel(c_ref, w_ref, b_ref, o_ref):
    s = _silu(c_ref[...]).astype(BF16)
    o_ref[0] = jnp.dot(s, w_ref[0].astype(BF16), preferred_element_type=F32) + b_ref[0]


def _adaln(c_all, w_mod, b_mod):
    nb, d = c_all.shape
    depth, _, n = w_mod.shape
    bn = 1024
    return pl.pallas_call(
        _adaln_kernel,
        grid=(depth, n // bn),
        in_specs=[pl.BlockSpec((nb, d), lambda l, j: (0, 0)),
                  pl.BlockSpec((1, d, bn), lambda l, j: (l, 0, j)),
                  pl.BlockSpec((1, 1, bn), lambda l, j: (l, 0, j))],
        out_specs=pl.BlockSpec((1, nb, bn), lambda l, j: (l, 0, j)),
        out_shape=jax.ShapeDtypeStruct((depth, nb, n), F32),
        compiler_params=_cparams(("parallel", "parallel")),
        name="adaln",
    )(c_all, w_mod, b_mod.reshape(depth, 1, n))


def _mod_spec(mod, bm, k):
    if mod.shape[1] == 1:
        return pl.BlockSpec((1, 1, D_MODEL), lambda g, i, *_: (g, 0, k))
    return pl.BlockSpec((1, bm, D_MODEL), lambda g, i, *_: (g, i, k))


def _inproj_kernel(x_ref, g_ref, sh_ref, sc_ref, wa_ref, wb_ref, wdt_ref, o_ref, dt_ref, h_scr, *, na):
    j = pl.program_id(2)

    @pl.when(j == 0)
    def _():
        _prenorm_to(h_scr, x_ref, g_ref, sc_ref, sh_ref)
        dt_ref[0] = jnp.dot(h_scr[...], wdt_ref[...], preferred_element_type=F32)

    @pl.when(j < na)
    def _():
        o_ref[0] = jnp.dot(h_scr[...], wa_ref[...], preferred_element_type=F32)

    @pl.when(j >= na)
    def _():
        o_ref[0] = jnp.dot(h_scr[...], wb_ref[...], preferred_element_type=F32)


def _inproj(x, gain, mod, w_a, w_b, w_dt, bm, bn):
    g_, m, d = x.shape
    na = w_a.shape[1] // bn
    n = w_a.shape[1] + w_b.shape[1]
    return pl.pallas_call(
        functools.partial(_inproj_kernel, na=na),
        grid=(g_, m // bm, n // bn),
        in_specs=[pl.BlockSpec((1, bm, d), lambda g, i, j: (g, i, 0)),
                  pl.BlockSpec((1, d), lambda g, i, j: (0, 0)),
                  _mod_spec(mod, bm, 0), _mod_spec(mod, bm, 1),
                  pl.BlockSpec((d, bn), lambda g, i, j: (0, jnp.minimum(j, na - 1))),
                  pl.BlockSpec((d, bn), lambda g, i, j: (0, jnp.maximum(j - na, 0))),
                  pl.BlockSpec((d, LANES), lambda g, i, j: (0, 0))],
        out_specs=[pl.BlockSpec((1, bm, bn), lambda g, i, j: (g, i, j)),
                   pl.BlockSpec((1, bm, LANES), lambda g, i, j: (g, i, 0))],
        out_shape=[jax.ShapeDtypeStruct((g_, m, n), F32),
                   jax.ShapeDtypeStruct((g_, m, LANES), F32)],
        scratch_shapes=[pltpu.VMEM((bm, d), BF16)],
        compiler_params=_cparams(("parallel", "parallel", "arbitrary")),
        name="inproj",
    )(x, gain, mod, mod, w_a, w_b, w_dt)


def _swap_halves(x):
    lane = lax.broadcasted_iota(jnp.int32, x.shape, 1)
    first = (lane % HEAD_DIM) < (HEAD_DIM // 2)
    return jnp.where(first, pltpu.roll(x, LANES - HEAD_DIM // 2, 1), pltpu.roll(x, HEAD_DIM // 2, 1))


def _qkv_kernel(x_ref, g_ref, sh_ref, sc_ref, w_ref, b_ref, cos_ref, sin_ref, o_ref, h_scr, *, rope_cols, bn):
    _prenorm_to(h_scr, x_ref, g_ref, sc_ref, sh_ref)
    h = h_scr[...]
    cos = cos_ref[...]
    sin = sin_ref[...]
    for c0 in range(0, w_ref.shape[1], bn):
        acc = jnp.dot(h, w_ref[:, c0:c0 + bn], preferred_element_type=F32) + b_ref[:, c0:c0 + bn]
        if c0 < rope_cols:
            for c in range(c0, c0 + bn, LANES):
                a = acc[:, c - c0:c - c0 + LANES]
                o_ref[0, :, c:c + LANES] = a * cos + _swap_halves(a) * sin
        else:
            o_ref[0, :, c0:c0 + bn] = acc


def _qkv(x, gain, mod, w, b, cos_t, sin_t, bm, bn):
    g_, m, d = x.shape
    n = w.shape[1]
    rope_cols = (ATT_HEADS + KV_HEADS) * HEAD_DIM
    rows_per_g = cos_t.shape[0] // bm
    return pl.pallas_call(
        functools.partial(_qkv_kernel, rope_cols=rope_cols, bn=bn),
        grid=(g_, m // bm),
        in_specs=[pl.BlockSpec((1, bm, d), lambda g, i: (g, i, 0)),
                  pl.BlockSpec((1, d), lambda g, i: (0, 0)),
                  _mod_spec(mod, bm, 0), _mod_spec(mod, bm, 1),
                  pl.BlockSpec((d, n), lambda g, i: (0, 0), pipeline_mode=pl.Buffered(1)),
                  pl.BlockSpec((1, n), lambda g, i: (0, 0)),
                  pl.BlockSpec((bm, LANES), lambda g, i: (i % rows_per_g, 0)),
                  pl.BlockSpec((bm, LANES), lambda g, i: (i % rows_per_g, 0))],
        out_specs=pl.BlockSpec((1, bm, n), lambda g, i: (g, i, 0)),
        out_shape=jax.ShapeDtypeStruct((g_, m, n), F32),
        scratch_shapes=[pltpu.VMEM((bm, d), BF16)],
        compiler_params=_cparams(("parallel", "parallel")),
        name="qkv",
    )(x, gain, mod, mod, w, b, cos_t, sin_t)


def _outproj_kernel(*refs, n_lhs):
    lhs = refs[:n_lhs]
    ws = refs[n_lhs:2 * n_lhs]
    x_ref, g_ref, gt_ref, o_ref = refs[2 * n_lhs:]
    acc = jnp.dot(lhs[0][0], ws[0][...], preferred_element_type=F32)
    for a, w in zip(lhs[1:], ws[1:]):
        acc = acc + jnp.dot(a[0], w[...], preferred_element_type=F32)
    o_ref[0] = x_ref[0] + gt_ref[0] * _rms(acc, g_ref[...])


def _outproj(lhs_list, w, x, gain, mod, bm):
    g_, m, d = x.shape
    n_lhs = len(lhs_list)
    kw = w.shape[0] // n_lhs
    lhs_specs = [pl.BlockSpec((1, bm, kw), lambda g, i: (g, i, 0)) for _ in lhs_list]
    w_specs = [pl.BlockSpec((kw, d), lambda g, i, k=k: (k, 0), pipeline_mode=pl.Buffered(1)) for k in range(n_lhs)]
    w_list = [w] * n_lhs
    if mod.shape[1] == 1:
        gt_spec = pl.BlockSpec((1, 1, d), lambda g, i: (g, 0, 2))
    else:
        gt_spec = pl.BlockSpec((1, bm, d), lambda g, i: (g, i, 2))
    return pl.pallas_call(
        functools.partial(_outproj_kernel, n_lhs=n_lhs),
        grid=(g_, m // bm),
        in_specs=lhs_specs + w_specs + [pl.BlockSpec((1, bm, d), lambda g, i: (g, i, 0)),
                                        pl.BlockSpec((1, d), lambda g, i: (0, 0)), gt_spec],
        out_specs=pl.BlockSpec((1, bm, d), lambda g, i: (g, i, 0)),
        out_shape=jax.ShapeDtypeStruct((g_, m, d), F32),
        compiler_params=_cparams(("parallel", "parallel")),
        name="outproj",
    )(*lhs_list, *w_list, x, gain, mod)


def _mlp_kernel(x_ref, g2_ref, sh_ref, sc_ref, gt_ref, wu_ref, wd_ref, g3_ref, o_ref, h_scr):
    f = pl.program_id(2)

    @pl.when(f == 0)
    def _():
        _prenorm_to(h_scr, x_ref, g2_ref, sc_ref, sh_ref)
        o_ref[0] = jnp.zeros(o_ref.shape[1:], F32)

    u = jnp.dot(h_scr[...], wu_ref[...], preferred_element_type=F32)
    u = jnp.square(jnp.maximum(u, 0.0)).astype(BF16)
    o_ref[0] += jnp.dot(u, wd_ref[...], preferred_element_type=F32)

    @pl.when(f == pl.num_programs(2) - 1)
    def _():
        o_ref[0] = x_ref[0] + gt_ref[0] * _rms(o_ref[0], g3_ref[...])


def _mlp(x, g2, g3, mod, w_up, w_down, layer, bm, bf):
    g_, m, d = x.shape
    dff = w_up.shape[2]
    if mod.shape[1] == 1:
        mspec = lambda k: pl.BlockSpec((1, 1, d), lambda g, i, f: (g, 0, k))
    else:
        mspec = lambda k: pl.BlockSpec((1, bm, d), lambda g, i, f: (g, i, k))
    return pl.pallas_call(
        _mlp_kernel,
        grid=(g_, m // bm, dff // bf),
        in_specs=[pl.BlockSpec((1, bm, d), lambda g, i, f: (g, i, 0), pipeline_mode=pl.Buffered(1)),
                  pl.BlockSpec((1, d), lambda g, i, f: (0, 0)),
                  mspec(3), mspec(4), mspec(5),
                  pl.BlockSpec((None, d, bf), lambda g, i, f: (layer, 0, f)),
                  pl.BlockSpec((None, bf, d), lambda g, i, f: (layer, f, 0)),
                  pl.BlockSpec((1, d), lambda g, i, f: (0, 0))],
        out_specs=pl.BlockSpec((1, bm, d), lambda g, i, f: (g, i, 0)),
        out_shape=jax.ShapeDtypeStruct((g_, m, d), F32),
        scratch_shapes=[pltpu.VMEM((bm, d), BF16)],
        compiler_params=_cparams(("parallel", "parallel", "arbitrary")),
        name="mlp",
    )(x, g2, mod, mod, mod, w_up, w_down, g3)


def _dwconv(raw, tail, w, b):
    t = raw.shape[0]
    assert w.shape[0] == 4
    ext = jnp.concatenate([tail, raw], axis=0)
    z1 = pltpu.roll(ext, 1, 0)
    near = ext * w[3:4] + z1 * w[2:3]
    far = ext * w[1:2] + z1 * w[0:1]
    y = near + pltpu.roll(far, 2, 0)
    return y[TAIL_ROWS:] + b, ext[t:t + TAIL_ROWS]


def _cumsum_rows(x):
    n = x.shape[0]
    row = lax.broadcasted_iota(jnp.int32, x.shape, 0)
    d = 1
    while d < n:
        x = x + jnp.where(row >= d, pltpu.roll(x, d, 0), 0.0)
        d *= 2
    return x


def _ssd_kernel(z_ref, xs_ref, bc_ref, dt_ref, buf_ref, h0_ref, cw_ref, cb_ref, dtb_ref, alog_ref,
                dsk_ref, ng_ref, exp_ref, y_ref, st_ref, tail_scr, *, q_valid):
    q = SSM_CHUNK
    c = pl.program_id(1)

    @pl.when(c == 0)
    def _():
        tail_scr[...] = buf_ref[0]
        st_ref[0] = h0_ref[0]

    def padded(v):
        if q_valid == q:
            return v
        return jnp.concatenate([v, jnp.zeros((q - q_valid, v.shape[1]), v.dtype)], axis=0)

    cw = cw_ref[...]
    cb = cb_ref[...]
    xs, tail_x = _dwconv(xs_ref[0], tail_scr[:, :SSM_INNER], cw[:, :SSM_INNER], cb[:, :SSM_INNER])
    bcm, tail_b = _dwconv(bc_ref[0], tail_scr[:, SSM_INNER:], cw[:, SSM_INNER:], cb[:, SSM_INNER:])
    tail_scr[:, :SSM_INNER] = tail_x
    tail_scr[:, SSM_INNER:] = tail_b
    xs = padded(_silu(xs))
    bcm = padded(_silu(bcm))
    z = padded(z_ref[0])

    dt = _softplus(dt_ref[0] + dtb_ref[...])
    dt = padded(dt)
    a = -jnp.exp(alog_ref[...])
    acs = _cumsum_rows(dt * a)
    acs_t = acs.T

    full = jnp.dot(jnp.concatenate(_split3(dt) + _split3(acs), axis=0), exp_ref[...], preferred_element_type=F32)
    dt_full = (full[0:q] + full[q:2 * q]) + full[2 * q:3 * q]
    acs_full = (full[3 * q:4 * q] + full[4 * q:5 * q]) + full[5 * q:6 * q]
    xdt = xs * dt_full
    dec_out = jnp.exp(acs_full)
    xw = (xdt * jnp.exp(acs_full[q - 1:q] - acs_full)).astype(BF16)
    xdt_b = xdt.astype(BF16)
    bc_b = bcm.astype(BF16)

    row = lax.broadcasted_iota(jnp.int32, (q, q), 0)
    col = lax.broadcasted_iota(jnp.int32, (q, q), 1)
    causal = row >= col
    lane = lax.broadcasted_iota(jnp.int32, (q, LANES), 1)
    lo_half = lane < SSM_HEAD_DIM
    nt = (((1,), (1,)), ((), ()))
    tn = (((0,), (0,)), ((), ()))
    n_bc = SSM_GROUPS * SSM_STATE

    for g in range(SSM_GROUPS):
        b_g = bc_b[:, g * SSM_STATE:(g + 1) * SSM_STATE]
        c_g = bc_b[:, n_bc + g * SSM_STATE:n_bc + (g + 1) * SSM_STATE]
        cbm = lax.dot_general(c_g, b_g, nt, preferred_element_type=F32)
        g0 = g * GROUP_WIDTH
        h_g = st_ref[0, g0:g0 + GROUP_WIDTH, :]
        y_off = lax.dot_general(c_g, h_g.astype(BF16), nt, preferred_element_type=F32)
        y_g = y_off * dec_out[:, g0:g0 + GROUP_WIDTH]
        pieces = []
        for pair in range(HEADS_PER_GROUP // 2):
            x_pair = xdt_b[:, g0 + pair * LANES:g0 + (pair + 1) * LANES]
            acc = None
            for e in range(2):
                h = g * HEADS_PER_GROUP + pair * 2 + e
                seg = jnp.exp(jnp.where(causal, acs[:, h:h + 1] - acs_t[h:h + 1, :], -jnp.inf))
                m = (cbm * seg).astype(BF16)
                x_h = jnp.where(lo_half if e == 0 else jnp.logical_not(lo_half), x_pair, jnp.zeros_like(x_pair))
                part = jnp.dot(m, x_h, preferred_element_type=F32)
                acc = part if acc is None else acc + part
            pieces.append(acc)
        y_g = y_g + jnp.concatenate(pieces, axis=1)
        y_g = y_g + xs[:, g0:g0 + GROUP_WIDTH] * dsk_ref[:, g0:g0 + GROUP_WIDTH]
        y_g = y_g * _silu(z[:, g0:g0 + GROUP_WIDTH])
        y_g = y_g * lax.rsqrt(jnp.mean(y_g * y_g, axis=-1, keepdims=True) + EPS)
        y_g = y_g * ng_ref[:, g0:g0 + GROUP_WIDTH]
        y_ref[0, :, g0:g0 + GROUP_WIDTH] = y_g[:q_valid].astype(y_ref.dtype)

        st_new = lax.dot_general(xw[:, g0:g0 + GROUP_WIDTH], b_g, tn, preferred_element_type=F32)
        for r in range(HEADS_PER_GROUP):
            h = g * HEADS_PER_GROUP + r
            decay = jnp.exp(acs_t[h:h + 1, q - 1:q])
            r0 = g0 + r * SSM_HEAD_DIM
            st_ref[0, r0:r0 + SSM_HEAD_DIM, :] = (
                h_g[r * SSM_HEAD_DIM:(r + 1) * SSM_HEAD_DIM] * decay
                + st_new[r * SSM_HEAD_DIM:(r + 1) * SSM_HEAD_DIM])


def _ssd(proj, dtp, buf8, h0, conv_w, conv_b, dt_bias, a_log, d_full, norm_g, expand, rows):
    b_, l, _ = proj.shape
    nc = l // rows
    st_rows = SSM_INNER
    return pl.pallas_call(
        functools.partial(_ssd_kernel, q_valid=rows),
        grid=(b_, nc),
        in_specs=[pl.BlockSpec((1, rows, SSM_INNER), lambda b, c: (b, c, 0)),
                  pl.BlockSpec((1, rows, SSM_INNER), lambda b, c: (b, c, 1)),
                  pl.BlockSpec((1, rows, SSM_INNER), lambda b, c: (b, c, 2)),
                  pl.BlockSpec((1, rows, LANES), lambda b, c: (b, c, 0)),
                  pl.BlockSpec((1, TAIL_ROWS, SSM_CONV_DIM), lambda b, c: (b, 0, 0)),
                  pl.BlockSpec((1, st_rows, SSM_STATE), lambda b, c: (b, 0, 0)),
                  pl.BlockSpec((SSM_CONV, SSM_CONV_DIM), lambda b, c: (0, 0)),
                  pl.BlockSpec((1, SSM_CONV_DIM), lambda b, c: (0, 0)),
                  pl.BlockSpec((1, LANES), lambda b, c: (0, 0)),
                  pl.BlockSpec((1, LANES), lambda b, c: (0, 0)),
                  pl.BlockSpec((1, SSM_INNER), lambda b, c: (0, 0)),
                  pl.BlockSpec((1, SSM_INNER), lambda b, c: (0, 0)),
                  pl.BlockSpec((LANES, SSM_INNER), lambda b, c: (0, 0))],
        out_specs=[pl.BlockSpec((1, rows, SSM_INNER), lambda b, c: (b, c, 0)),
                   pl.BlockSpec((1, st_rows, SSM_STATE), lambda b, c: (b, 0, 0))],
        out_shape=[jax.ShapeDtypeStruct((b_, l, SSM_INNER), BF16),
                   jax.ShapeDtypeStruct((b_, st_rows, SSM_STATE), F32)],
        scratch_shapes=[pltpu.VMEM((TAIL_ROWS, SSM_CONV_DIM), F32)],
        compiler_params=_cparams(("parallel", "arbitrary")),
        name="ssd",
    )(proj, proj, proj, dtp, buf8, h0, conv_w, conv_b, dt_bias, a_log, d_full, norm_g, expand)


def _lru_kernel(gate_ref, xr_ref, buf_ref, h0_ref, cw_ref, cb_ref, wa_ref, ba_ref, wx_ref, bx_ref,
                lam_ref, y_ref, hl_ref, tail_scr, a_scr, u_scr):
    t = xr_ref.shape[1]
    c = pl.program_id(1)

    @pl.when(c == 0)
    def _():
        tail_scr[...] = buf_ref[0]
        hl_ref[0] = h0_ref[0]

    xc, tail = _dwconv(xr_ref[0], tail_scr[...], cw_ref[...], cb_ref[...])
    tail_scr[...] = tail
    sp = _softplus(-lam_ref[...])
    for k in range(LRU_BLOCKS):
        k0 = k * LRU_BLOCK
        xk = xc[:, k0:k0 + LRU_BLOCK]
        xkb = xk.astype(BF16)
        gr = _sigmoid(jnp.dot(xkb, wa_ref[k], preferred_element_type=F32) + ba_ref[:, k0:k0 + LRU_BLOCK])
        gi = _sigmoid(jnp.dot(xkb, wx_ref[k], preferred_element_type=F32) + bx_ref[:, k0:k0 + LRU_BLOCK])
        log_a = -LRU_C * gr * sp[:, k0:k0 + LRU_BLOCK]
        a_scr[:, k0:k0 + LRU_BLOCK] = jnp.exp(log_a)
        u_scr[:, k0:k0 + LRU_BLOCK] = _sqrt_neg_expm1(2.0 * log_a) * gi * xk

    row = lax.broadcasted_iota(jnp.int32, (SUBLANES, LRU_WIDTH), 0)

    def tile(i, h_prev):
        r0 = pl.multiple_of(i * SUBLANES, SUBLANES)
        a = a_scr[pl.ds(r0, SUBLANES), :]
        u = u_scr[pl.ds(r0, SUBLANES), :]
        d = 1
        while d < SUBLANES:
            keep = row >= d
            u = u + a * jnp.where(keep, pltpu.roll(u, d, 0), 0.0)
            a = a * jnp.where(keep, pltpu.roll(a, d, 0), 1.0)
            d *= 2
        h = u + a * h_prev
        u_scr[pl.ds(r0, SUBLANES), :] = h
        return h[SUBLANES - 1:SUBLANES, :]

    h_last = lax.fori_loop(0, t // SUBLANES, tile, hl_ref[0])
    hl_ref[0] = h_last
    y_ref[0] = (u_scr[...] * jax.nn.gelu(gate_ref[0])).astype(y_ref.dtype)


def _lru(proj, buf8, h0, conv_w, conv_b, w_a, b_a, w_x, b_x, lam, rows, col0):
    b_, l, _ = proj.shape
    w = LRU_WIDTH
    full2 = lambda shape: pl.BlockSpec(shape, lambda b, c: (0,) * len(shape))
    return pl.pallas_call(
        _lru_kernel,
        grid=(b_, l // rows),
        in_specs=[pl.BlockSpec((1, rows, w), lambda b, c: (b, c, col0)),
                  pl.BlockSpec((1, rows, w), lambda b, c: (b, c, col0 + 1)),
                  pl.BlockSpec((1, TAIL_ROWS, w), lambda b, c: (b, 0, 0)),
                  pl.BlockSpec((1, 1, w), lambda b, c: (b, 0, 0)),
                  full2((SSM_CONV, w)), full2((1, w)),
                  full2((LRU_BLOCKS, LRU_BLOCK, LRU_BLOCK)), full2((1, w)),
                  full2((LRU_BLOCKS, LRU_BLOCK, LRU_BLOCK)), full2((1, w)),
                  full2((1, w))],
        out_specs=[pl.BlockSpec((1, rows, w), lambda b, c: (b, c, 0)),
                   pl.BlockSpec((1, 1, w), lambda b, c: (b, 0, 0))],
        out_shape=[jax.ShapeDtypeStruct((b_, l, w), BF16),
                   jax.ShapeDtypeStruct((b_, 1, w), F32)],
        scratch_shapes=[pltpu.VMEM((TAIL_ROWS, w), F32), pltpu.VMEM((rows, w), F32),
                        pltpu.VMEM((rows, w), F32)],
        compiler_params=_cparams(("parallel", "arbitrary")),
        name="lru",
    )(proj, proj, buf8, h0, conv_w, conv_b, w_a, b_a, w_x, b_x, lam)


def _head_variants(slab, half):
    lane = lax.broadcasted_iota(jnp.int32, slab.shape, 1)
    keep = (lane < HEAD_DIM) if half == 0 else (lane >= HEAD_DIM)
    own = jnp.where(keep, slab, 0.0)
    other = pltpu.roll(own, HEAD_DIM, 1)
    return (own, other) if half == 0 else (other, own)


def _slab_attention(q_ref, o_ref, k2, v2, valid, sink_ref):
    w = q_ref.shape[1]
    nk = k2.shape[0]
    nt = (((1,), (1,)), ((), ()))
    scale = HEAD_DIM ** -0.5
    heads = range(KV_HEADS)
    halves = [(e, c) for e in heads for c in range(2)]
    vvs, ss = [], []
    for e in heads:
        slab, half = divmod(e, 2)
        k_lo, k_hi = _head_variants(k2[:, slab * LANES:(slab + 1) * LANES], half)
        v_lo, v_hi = _head_variants(v2[:, slab * LANES:(slab + 1) * LANES], half)
        kk = jnp.concatenate([k_lo, k_hi], axis=0).astype(BF16)
        vvs.append(jnp.concatenate([v_lo, v_hi], axis=0).astype(BF16))
        c0 = 2 * e * LANES
        qq = jnp.concatenate([q_ref[0, :, c0:c0 + LANES], q_ref[0, :, c0 + LANES:c0 + 2 * LANES]],
                             axis=0).astype(BF16)
        ss.append(lax.dot_general(qq, kk, nt, preferred_element_type=F32) * scale)
    scs = [jnp.where(valid, ss[e][:, c * nk:(c + 1) * nk], -jnp.inf) for e, c in halves]
    sinks = [jnp.concatenate([jnp.full((w, 1), sink_ref[Q_PER_KV * e + c], F32),
                              jnp.full((w, 1), sink_ref[Q_PER_KV * e + 2 + c], F32)], axis=0) for e, c in halves]
    ms = [jnp.maximum(jnp.max(sc, axis=1, keepdims=True), sk) for sc, sk in zip(scs, sinks)]
    ps = [jnp.exp(sc - m) for sc, m in zip(scs, ms)]
    dens = [jnp.sum(p, axis=1, keepdims=True) + jnp.exp(sk - m) for p, sk, m in zip(ps, sinks, ms)]
    pn = [(p * (1.0 / den)).astype(BF16) for p, den in zip(ps, dens)]
    for e in heads:
        o = jnp.dot(jnp.concatenate(pn[2 * e:2 * e + 2], axis=1), vvs[e], preferred_element_type=F32)
        c0 = 2 * e * LANES
        o_ref[0, :, c0:c0 + LANES] = o[:w].astype(o_ref.dtype)
        o_ref[0, :, c0 + LANES:c0 + 2 * LANES] = o[w:].astype(o_ref.dtype)


def _attn_prompt_kernel(sink_ref, q_ref, kc_ref, kp_ref, vc_ref, vp_ref, o_ref):
    w = WINDOW
    has_prev = pl.program_id(1) > 0
    k2 = jnp.concatenate([jnp.where(has_prev, kp_ref[0], 0.0), kc_ref[0]], axis=0)
    v2 = jnp.concatenate([jnp.where(has_prev, vp_ref[0], 0.0), vc_ref[0]], axis=0)
    t = lax.broadcasted_iota(jnp.int32, (2 * w, 2 * w), 0) % w
    j = lax.broadcasted_iota(jnp.int32, (2 * w, 2 * w), 1)
    _slab_attention(q_ref, o_ref, k2, v2, (j > t) & (j <= t + w), sink_ref)


def _attn_prompt(qkv, sinks):
    b_, l, _ = qkv.shape
    w = WINDOW
    qw = ATT_HEADS * HEAD_DIM
    kvw = KV_HEADS * HEAD_DIM
    k_blk = qw // kvw
    prev = lambda n: jnp.maximum(n - 1, 0)
    return pl.pallas_call(
        _attn_prompt_kernel,
        grid=(b_, l // w),
        in_specs=[pl.BlockSpec(memory_space=pltpu.SMEM),
                  pl.BlockSpec((1, w, qw), lambda b, n: (b, n, 0)),
                  pl.BlockSpec((1, w, kvw), lambda b, n: (b, n, k_blk)),
                  pl.BlockSpec((1, w, kvw), lambda b, n: (b, prev(n), k_blk)),
                  pl.BlockSpec((1, w, kvw), lambda b, n: (b, n, k_blk + 1)),
                  pl.BlockSpec((1, w, kvw), lambda b, n: (b, prev(n), k_blk + 1))],
        out_specs=pl.BlockSpec((1, w, qw), lambda b, n: (b, n, 0)),
        out_shape=jax.ShapeDtypeStruct((b_, l, qw), BF16),
        compiler_params=_cparams(("parallel", "parallel")),
        name="attn_prompt",
    )(sinks, qkv, qkv, qkv, qkv, qkv)


def _attn_sample_kernel(sink_ref, qkv_ref, ck_ref, cv_ref, o_ref, nk_ref, nv_ref):
    l = qkv_ref.shape[1]
    n_keep = ck_ref.shape[1]
    kcol = ATT_HEADS * HEAD_DIM
    vcol = kcol + KV_HEADS * HEAD_DIM
    qkv = qkv_ref[0]
    k_all = jnp.concatenate([ck_ref[0], qkv[:, kcol:vcol]], axis=0)
    v_all = jnp.concatenate([cv_ref[0], qkv[:, vcol:]], axis=0)
    nk_ref[0] = k_all[l:]
    nv_ref[0] = v_all[l:]
    nk = -(-(n_keep + l) // LANES) * LANES
    pad = jnp.zeros((nk - n_keep - l, k_all.shape[1]), F32)
    t = lax.broadcasted_iota(jnp.int32, (2 * l, nk), 0) % l
    j = lax.broadcasted_iota(jnp.int32, (2 * l, nk), 1)
    rel = jnp.where(j < n_keep, t + n_keep - j, t - (j - n_keep))
    valid = (rel >= 0) & (rel < WINDOW) & (j < n_keep + l)
    _slab_attention(qkv_ref, o_ref, jnp.concatenate([k_all, pad], axis=0), jnp.concatenate([v_all, pad], axis=0),
                    valid, sink_ref)


def _attn_sample(qkv, cache_k, cache_v, sinks):
    b_, l, n = qkv.shape
    n_keep, kvw = cache_k.shape[1:]
    return pl.pallas_call(
        _attn_sample_kernel,
        grid=(b_,),
        in_specs=[pl.BlockSpec(memory_space=pltpu.SMEM),
                  pl.BlockSpec((1, l, n), lambda b: (b, 0, 0)),
                  pl.BlockSpec((1, n_keep, kvw), lambda b: (b, 0, 0)),
                  pl.BlockSpec((1, n_keep, kvw), lambda b: (b, 0, 0))],
        out_specs=[pl.BlockSpec((1, l, ATT_HEADS * HEAD_DIM), lambda b: (b, 0, 0)),
                   pl.BlockSpec((1, n_keep, kvw), lambda b: (b, 0, 0)),
                   pl.BlockSpec((1, n_keep, kvw), lambda b: (b, 0, 0))],
        out_shape=[jax.ShapeDtypeStruct((b_, l, ATT_HEADS * HEAD_DIM), BF16),
                   jax.ShapeDtypeStruct((b_, n_keep, kvw), F32),
                   jax.ShapeDtypeStruct((b_, n_keep, kvw), F32)],
        compiler_params=_cparams(("parallel",)),
        name="attn_sample",
    )(sinks, qkv, cache_k, cache_v)


def _rope_tables(pos):
    half = HEAD_DIM // 2
    inv_freq = ROPE_THETA ** (-jnp.arange(half, dtype=F32) / half)
    ang = pos.astype(F32)[:, None] * inv_freq[None, :]
    cos = jnp.cos(ang)
    sin = jnp.sin(ang)
    reps = LANES // HEAD_DIM
    return (jnp.tile(jnp.concatenate([cos, cos], axis=1), (1, reps)),
            jnp.tile(jnp.concatenate([-sin, sin], axis=1), (1, reps)))


def _tail_pad(buf):
    return jnp.pad(buf, ((0, 0), (TAIL_ROWS - buf.shape[1], 0), (0, 0)))


def _trunk(x, mods, pos, st, pw, seq_len):
    g_, m, d = x.shape
    nseq = g_ * m // seq_len
    blk = _block_rows(m)
    chunk = min(seq_len, SSM_CHUNK)
    lru_rows = min(seq_len, LRU_ROWS)

    proj, dtp = _inproj(x, pw['norms'][0, 0:1], mods[0], pw['w_in_a'], pw['w_in_b'], pw['w_in_dt'], blk['inproj'],
                        INPROJ_COLS)
    n_main = proj.shape[-1]
    proj_s = proj.reshape(nseq, seq_len, n_main)
    dtp_s = dtp.reshape(nseq, seq_len, LANES)
    y_ssm, ssm_h = _ssd(proj_s, dtp_s, _tail_pad(st['ssm_conv']), st['ssm'].reshape(nseq, SSM_INNER, SSM_STATE),
                        pw['ssm_conv_w'], pw['ssm_conv_b'], pw['ssm_dt_bias'], pw['ssm_a_log'], pw['ssm_d_full'],
                        pw['ssm_norm'], pw['expand'], chunk)
    y_lru, lru_h = _lru(proj_s, _tail_pad(st['lru_conv']), st['lru'].reshape(nseq, 1, LRU_WIDTH),
                        pw['lru_conv_w'], pw['lru_conv_b'], pw['lru_w_a'], pw['lru_b_a'], pw['lru_w_x'],
                        pw['lru_b_x'], pw['lru_lambda'], lru_rows, 3)
    keep = SSM_CONV - 1
    new_ssm_conv = proj_s[:, seq_len - keep:, SSM_INNER:SSM_INNER + SSM_CONV_DIM]
    new_lru_conv = proj_s[:, seq_len - keep:, n_main - LRU_WIDTH:]
    x = _outproj([y_ssm.reshape(g_, m, SSM_INNER), y_lru.reshape(g_, m, LRU_WIDTH)],
                 pw['w_out_hyb'], x, pw['norms'][0, 1:2], mods[0], blk['outproj'])
    x = _mlp(x, pw['norms'][0, 2:3], pw['norms'][0, 3:4], mods[0], pw['w_up'], pw['w_down'], 0, blk['mlp'],
             MLP_COLS)

    cos_t, sin_t = _rope_tables(pos)
    if seq_len < blk['qkv']:
        cos_t = jnp.tile(cos_t, (blk['qkv'] // seq_len, 1))
        sin_t = jnp.tile(sin_t, (blk['qkv'] // seq_len, 1))
    qkv = _qkv(x, pw['norms'][1, 0:1], mods[1], pw['w_qkv'], pw['b_qkv'], cos_t, sin_t, blk['qkv'], QKV_COLS)
    qkv_s = qkv.reshape(nseq, seq_len, qkv.shape[-1])
    kcol = ATT_HEADS * HEAD_DIM
    vcol = kcol + KV_HEADS * HEAD_DIM
    if st['k'] is None:
        o = _attn_prompt(qkv_s, pw['attn_sinks'])
        n_keep = min(WINDOW, seq_len)
        k_new = qkv_s[:, seq_len - n_keep:, kcol:vcol]
        v_new = qkv_s[:, seq_len - n_keep:, vcol:]
    else:
        n_keep = st['k'].shape[1]
        o, k_new, v_new = _attn_sample(qkv_s, st['k'].reshape(nseq, n_keep, KV_HEADS * HEAD_DIM),
                                       st['v'].reshape(nseq, n_keep, KV_HEADS * HEAD_DIM), pw['attn_sinks'])
    k_new = k_new.reshape(nseq, n_keep, KV_HEADS, HEAD_DIM)
    v_new = v_new.reshape(nseq, n_keep, KV_HEADS, HEAD_DIM)
    x = _outproj([o.reshape(g_, m, kcol)], pw['w_out_attn'], x, pw['norms'][1, 1:2], mods[1], blk['outproj'])
    x = _mlp(x, pw['norms'][1, 2:3], pw['norms'][1, 3:4], mods[1], pw['w_up'], pw['w_down'], 1, blk['mlp'],
             MLP_COLS)

    states = (new_ssm_conv[None], ssm_h.reshape(1, nseq, SSM_HEADS, SSM_HEAD_DIM, SSM_STATE),
              new_lru_conv[None], lru_h.reshape(1, nseq, LRU_WIDTH), k_new[None], v_new[None])
    return x, states


def kernel(x_prompt, x_sample, state_ssm_conv, state_ssm, state_lru_conv, state_lru, cache_k, cache_v, c_prompt, c_sample, w_mod, b_mod, norms, w_in_hyb, ssm_conv_w, ssm_conv_b, ssm_dt_bias, ssm_a_log, ssm_d, ssm_norm, lru_conv_w, lru_conv_b, lru_w_a, lru_b_a, lru_w_x, lru_b_x, lru_lambda, w_out_hyb, w_qkv, b_qkv, attn_sinks, w_out_attn, w_up, w_down):
    bp, lp, d = x_prompt.shape
    bs, ls, _ = x_sample.shape
    depth = w_mod.shape[0]

    dt0 = SSM_INNER + SSM_CONV_DIM
    w_in = w_in_hyb[0]
    pad_lanes = lambda v: jnp.pad(v.reshape(1, -1), ((0, 0), (0, LANES - v.size)))
    head_of_lane = jnp.arange(SSM_INNER) // SSM_HEAD_DIM
    pw = {
        'norms': norms,
        'w_in_a': w_in[:, :dt0].astype(BF16), 'w_in_b': w_in[:, dt0 + SSM_HEADS:].astype(BF16),
        'w_in_dt': jnp.pad(w_in[:, dt0:dt0 + SSM_HEADS], ((0, 0), (0, LANES - SSM_HEADS))).astype(BF16),
        'ssm_conv_w': ssm_conv_w[0], 'ssm_conv_b': ssm_conv_b[0].reshape(1, -1),
        'ssm_dt_bias': pad_lanes(ssm_dt_bias[0]), 'ssm_a_log': pad_lanes(ssm_a_log[0]),
        'ssm_d_full': jnp.repeat(ssm_d[0], SSM_HEAD_DIM).reshape(1, -1),
        'ssm_norm': ssm_norm[0].reshape(1, -1),
        'expand': (jnp.arange(LANES)[:, None] == head_of_lane[None, :]).astype(BF16),
        'lru_conv_w': lru_conv_w[0], 'lru_conv_b': lru_conv_b[0].reshape(1, -1),
        'lru_w_a': lru_w_a[0].astype(BF16), 'lru_b_a': lru_b_a[0].reshape(1, -1),
        'lru_w_x': lru_w_x[0].astype(BF16), 'lru_b_x': lru_b_x[0].reshape(1, -1),
        'lru_lambda': lru_lambda[0].reshape(1, -1),
        'w_out_hyb': w_out_hyb[0].astype(BF16),
        'w_qkv': w_qkv[0].astype(BF16), 'b_qkv': b_qkv[0].reshape(1, -1),
        'attn_sinks': attn_sinks[0],
        'w_out_attn': w_out_attn[0].astype(BF16),
        'w_up': w_up.astype(BF16), 'w_down': w_down.astype(BF16),
    }

    mod = _adaln(jnp.concatenate([c_prompt, c_sample], axis=0), w_mod, b_mod)
    mods_p = [mod[l, :bp][:, None, :] for l in range(depth)]
    mods_s = [jnp.repeat(mod[l, bp:], ls, axis=0)[None] for l in range(depth)]

    zeros = lambda *shape: jnp.zeros(shape, F32)
    st_p = {'ssm_conv': zeros(bp, SSM_CONV - 1, SSM_CONV_DIM), 'ssm': zeros(bp, SSM_INNER, SSM_STATE),
            'lru_conv': zeros(bp, SSM_CONV - 1, LRU_WIDTH), 'lru': zeros(bp, LRU_WIDTH), 'k': None, 'v': None}
    st_s = {'ssm_conv': state_ssm_conv[0], 'ssm': state_ssm[0], 'lru_conv': state_lru_conv[0],
            'lru': state_lru[0], 'k': cache_k[0], 'v': cache_v[0]}

    pos_p = jnp.arange(lp, dtype=jnp.int32)
    pos_s = PAST_LEN + jnp.arange(ls, dtype=jnp.int32)
    y_p, sp = _trunk(x_prompt, mods_p, pos_p, st_p, pw, lp)
    y_s, ss = _trunk(x_sample.reshape(1, bs * ls, d), mods_s, pos_s, st_s, pw, ls)
    return (y_p, y_s.reshape(bs, ls, d)) + sp + ss
```

```python
import functools
import math

import jax
import jax.numpy as jnp
from jax import lax
from jax.experimental import pallas as pl
from jax.experimental.pallas import tpu as pltpu

F32 = jnp.float32
BF16 = jnp.bfloat16

D_MODEL = 2048
PAST_LEN = 16384
SSM_HEADS = 32
SSM_HEAD_DIM = 64
SSM_INNER = SSM_HEADS * SSM_HEAD_DIM
SSM_GROUPS = 8
SSM_STATE = 128
SSM_CONV = 4
SSM_CHUNK = 128
SSM_CONV_DIM = SSM_INNER + 2 * SSM_GROUPS * SSM_STATE
HEADS_PER_GROUP = SSM_HEADS // SSM_GROUPS
GROUP_WIDTH = HEADS_PER_GROUP * SSM_HEAD_DIM
LRU_WIDTH = D_MODEL
LRU_BLOCKS = 8
LRU_BLOCK = LRU_WIDTH // LRU_BLOCKS
LRU_C = 8.0
ATT_HEADS = 32
KV_HEADS = 8
HEAD_DIM = 64
Q_PER_KV = ATT_HEADS // KV_HEADS
WINDOW = 128
ROPE_THETA = 10000.0
D_FF = 4 * D_MODEL
EPS = 1e-6

LANES = 128
SUBLANES = 8
TAIL_ROWS = SUBLANES
VMEM_LIMIT = 56 * 1024 * 1024


INPROJ_COLS = 1024
MLP_COLS = 512
QKV_COLS = 512
LRU_ROWS = 256


def _block_rows(m):
    return {'inproj': min(m, 1024), 'mlp': min(m, 1024), 'qkv': min(m, 512), 'outproj': min(m, 512)}


def _cparams(sem):
    return pltpu.CompilerParams(dimension_semantics=sem, vmem_limit_bytes=VMEM_LIMIT)


def _rms(x, g):
    return x * lax.rsqrt(jnp.mean(x * x, axis=-1, keepdims=True) + EPS) * g


ROW_CHUNK = 16
ROW_UNROLL = 4


def _for_row_chunks(n_rows, body):
    def step(i, carry):
        body(pl.multiple_of(i * ROW_CHUNK, ROW_CHUNK))
        return carry

    lax.fori_loop(0, n_rows // ROW_CHUNK, step, 0, unroll=ROW_UNROLL)


def _mod_rows(ref, r0):
    return ref[0] if ref.shape[1] == 1 else ref[0, pl.ds(r0, ROW_CHUNK), :]


def _prenorm_to(h_scr, x_ref, g_ref, sc_ref, sh_ref):
    def body(r0):
        x = x_ref[0, pl.ds(r0, ROW_CHUNK), :]
        h = _rms(x, g_ref[...]) * (1.0 + _mod_rows(sc_ref, r0)) + _mod_rows(sh_ref, r0)
        h_scr[pl.ds(r0, ROW_CHUNK), :] = h.astype(BF16)

    _for_row_chunks(x_ref.shape[1], body)


def _sigmoid(x):
    return 0.5 * (jnp.tanh(0.5 * x) + 1.0)


def _silu(x):
    return x * _sigmoid(x)


def _softplus(x):
    return jnp.maximum(x, 0.0) + jnp.log1p(jnp.exp(-jnp.abs(x)))


def _sqrt_neg_expm1(x):
    s = -jnp.tanh(0.5 * x)
    return jnp.sqrt(2.0 * s) * lax.rsqrt(1.0 + s)


def _split3(x):
    hi = x.astype(BF16)
    r = x - hi.astype(F32)
    mid = r.astype(BF16)
    lo = (r - mid.astype(F32)).astype(BF16)
    return [hi, mid, lo]


def _adaln_kernel(c_ref, w_ref, b_ref, o_ref):
    s = _silu(c_ref[...]).astype(BF16)
    o_ref[0] = jnp.dot(s, w_ref[0].astype(BF16), preferred_element_type=F32) + b_ref[0]


def _adaln(c_all, w_mod, b_mod):
    nb, d = c_all.shape
    depth, _, n = w_mod.shape
    bn = 1024
    return pl.pallas_call(
        _adaln_kernel,
        grid=(depth, n // bn),
        in_specs=[pl.BlockSpec((nb, d), lambda l, j: (0, 0)),
                  pl.BlockSpec((1, d, bn), lambda l, j: (l, 0, j)),
                  pl.BlockSpec((1, 1, bn), lambda l, j: (l, 0, j))],
        out_specs=pl.BlockSpec((1, nb, bn), lambda l, j: (l, 0, j)),
        out_shape=jax.ShapeDtypeStruct((depth, nb, n), F32),
        compiler_params=_cparams(("parallel", "parallel")),
        name="adaln",
    )(c_all, w_mod, b_mod.reshape(depth, 1, n))


def _mod_spec(mod, bm, k):
    if mod.shape[1] == 1:
        return pl.BlockSpec((1, 1, D_MODEL), lambda g, i, *_: (g, 0, k))
    return pl.BlockSpec((1, bm, D_MODEL), lambda g, i, *_: (g, i, k))


def _inproj_kernel(x_ref, g_ref, sh_ref, sc_ref, wa_ref, wb_ref, wdt_ref, o_ref, dt_ref, h_scr, *, na):
    j = pl.program_id(2)

    @pl.when(j == 0)
    def _():
        _prenorm_to(h_scr, x_ref, g_ref, sc_ref, sh_ref)
        dt_ref[0] = jnp.dot(h_scr[...], wdt_ref[...], preferred_element_type=F32)

    @pl.when(j < na)
    def _():
        o_ref[0] = jnp.dot(h_scr[...], wa_ref[...].astype(BF16), preferred_element_type=F32)

    @pl.when(j >= na)
    def _():
        o_ref[0] = jnp.dot(h_scr[...], wb_ref[...], preferred_element_type=F32)


def _inproj(x, gain, mod, w_a, n_a, w_b, w_dt, bm, bn):
    g_, m, d = x.shape
    na = n_a // bn
    n = n_a + w_b.shape[1]
    return pl.pallas_call(
        functools.partial(_inproj_kernel, na=na),
        grid=(g_, m // bm, n // bn),
        in_specs=[pl.BlockSpec((1, bm, d), lambda g, i, j: (g, i, 0), pipeline_mode=pl.Buffered(1)),
                  pl.BlockSpec((1, d), lambda g, i, j: (0, 0)),
                  _mod_spec(mod, bm, 0), _mod_spec(mod, bm, 1),
                  pl.BlockSpec((d, bn), lambda g, i, j: (0, jnp.minimum(j, na - 1))),
                  pl.BlockSpec((d, bn), lambda g, i, j: (0, jnp.maximum(j - na, 0))),
                  pl.BlockSpec((d, LANES), lambda g, i, j: (0, 0))],
        out_specs=[pl.BlockSpec((1, bm, bn), lambda g, i, j: (g, i, j)),
                   pl.BlockSpec((1, bm, LANES), lambda g, i, j: (g, i, 0))],
        out_shape=[jax.ShapeDtypeStruct((g_, m, n), F32),
                   jax.ShapeDtypeStruct((g_, m, LANES), F32)],
        scratch_shapes=[pltpu.VMEM((bm, d), BF16)],
        compiler_params=_cparams(("parallel", "parallel", "arbitrary")),
        name="inproj",
    )(x, gain, mod, mod, w_a, w_b, w_dt)


def _swap_halves(x):
    lane = lax.broadcasted_iota(jnp.int32, x.shape, 1)
    first = (lane % HEAD_DIM) < (HEAD_DIM // 2)
    return jnp.where(first, pltpu.roll(x, LANES - HEAD_DIM // 2, 1), pltpu.roll(x, HEAD_DIM // 2, 1))


def _qkv_kernel(x_ref, g_ref, sh_ref, sc_ref, w_ref, b_ref, cos_ref, sin_ref, o_ref, h_scr, *, rope_cols, bn):
    _prenorm_to(h_scr, x_ref, g_ref, sc_ref, sh_ref)
    h = h_scr[...]
    cos = cos_ref[...]
    sin = sin_ref[...]
    for c0 in range(0, w_ref.shape[1], bn):
        acc = jnp.dot(h, w_ref[:, c0:c0 + bn], preferred_element_type=F32) + b_ref[:, c0:c0 + bn]
        if c0 < rope_cols:
            for c in range(c0, c0 + bn, LANES):
                a = acc[:, c - c0:c - c0 + LANES]
                o_ref[0, :, c:c + LANES] = a * cos + _swap_halves(a) * sin
        else:
            o_ref[0, :, c0:c0 + bn] = acc


def _qkv(x, gain, mod, w, b, cos_t, sin_t, bm, bn):
    g_, m, d = x.shape
    n = w.shape[1]
    rope_cols = (ATT_HEADS + KV_HEADS) * HEAD_DIM
    rows_per_g = cos_t.shape[0] // bm
    return pl.pallas_call(
        functools.partial(_qkv_kernel, rope_cols=rope_cols, bn=bn),
        grid=(g_, m // bm),
        in_specs=[pl.BlockSpec((1, bm, d), lambda g, i: (g, i, 0)),
                  pl.BlockSpec((1, d), lambda g, i: (0, 0)),
                  _mod_spec(mod, bm, 0), _mod_spec(mod, bm, 1),
                  pl.BlockSpec((d, n), lambda g, i: (0, 0), pipeline_mode=pl.Buffered(1)),
                  pl.BlockSpec((1, n), lambda g, i: (0, 0)),
                  pl.BlockSpec((bm, LANES), lambda g, i: (i % rows_per_g, 0)),
                  pl.BlockSpec((bm, LANES), lambda g, i: (i % rows_per_g, 0))],
        out_specs=pl.BlockSpec((1, bm, n), lambda g, i: (g, i, 0)),
        out_shape=jax.ShapeDtypeStruct((g_, m, n), F32),
        scratch_shapes=[pltpu.VMEM((bm, d), BF16)],
        compiler_params=_cparams(("parallel", "parallel")),
        name="qkv",
    )(x, gain, mod, mod, w, b, cos_t, sin_t)


def _outproj_kernel(*refs, n_lhs):
    lhs = refs[:n_lhs]
    ws = refs[n_lhs:2 * n_lhs]
    x_ref, g_ref, gt_ref, o_ref = refs[2 * n_lhs:]
    acc = jnp.dot(lhs[0][0], ws[0][...], preferred_element_type=F32)
    for a, w in zip(lhs[1:], ws[1:]):
        acc = acc + jnp.dot(a[0], w[...], preferred_element_type=F32)
    o_ref[0] = x_ref[0] + gt_ref[0] * _rms(acc, g_ref[...])


def _outproj(lhs_list, w, x, gain, mod, bm):
    g_, m, d = x.shape
    n_lhs = len(lhs_list)
    kw = w.shape[0] // n_lhs
    lhs_specs = [pl.BlockSpec((1, bm, kw), lambda g, i: (g, i, 0)) for _ in lhs_list]
    w_specs = [pl.BlockSpec((kw, d), lambda g, i, k=k: (k, 0), pipeline_mode=pl.Buffered(1)) for k in range(n_lhs)]
    w_list = [w] * n_lhs
    if mod.shape[1] == 1:
        gt_spec = pl.BlockSpec((1, 1, d), lambda g, i: (g, 0, 2))
    else:
        gt_spec = pl.BlockSpec((1, bm, d), lambda g, i: (g, i, 2))
    return pl.pallas_call(
        functools.partial(_outproj_kernel, n_lhs=n_lhs),
        grid=(g_, m // bm),
        in_specs=lhs_specs + w_specs + [pl.BlockSpec((1, bm, d), lambda g, i: (g, i, 0)),
                                        pl.BlockSpec((1, d), lambda g, i: (0, 0)), gt_spec],
        out_specs=pl.BlockSpec((1, bm, d), lambda g, i: (g, i, 0)),
        out_shape=jax.ShapeDtypeStruct((g_, m, d), F32),
        compiler_params=_cparams(("parallel", "parallel")),
        name="outproj",
    )(*lhs_list, *w_list, x, gain, mod)


def _mlp_kernel(x_ref, g2_ref, sh_ref, sc_ref, gt_ref, wu_ref, wd_ref, g3_ref, o_ref, h_scr):
    f = pl.program_id(2)

    @pl.when(f == 0)
    def _():
        _prenorm_to(h_scr, x_ref, g2_ref, sc_ref, sh_ref)
        o_ref[0] = jnp.zeros(o_ref.shape[1:], F32)

    u = jnp.dot(h_scr[...], wu_ref[...].astype(BF16), preferred_element_type=F32)
    u = jnp.square(jnp.maximum(u, 0.0)).astype(BF16)
    o_ref[0] += jnp.dot(u, wd_ref[...].astype(BF16), preferred_element_type=F32)

    @pl.when(f == pl.num_programs(2) - 1)
    def _():
        o_ref[0] = x_ref[0] + gt_ref[0] * _rms(o_ref[0], g3_ref[...])


def _mlp(x, g2, g3, mod, w_up, w_down, layer, bm, bf):
    g_, m, d = x.shape
    dff = w_up.shape[2]
    if mod.shape[1] == 1:
        mspec = lambda k: pl.BlockSpec((1, 1, d), lambda g, i, f: (g, 0, k))
    else:
        mspec = lambda k: pl.BlockSpec((1, bm, d), lambda g, i, f: (g, i, k))
    return pl.pallas_call(
        _mlp_kernel,
        grid=(g_, m // bm, dff // bf),
        in_specs=[pl.BlockSpec((1, bm, d), lambda g, i, f: (g, i, 0), pipeline_mode=pl.Buffered(1)),
                  pl.BlockSpec((1, d), lambda g, i, f: (0, 0)),
                  mspec(3), mspec(4), mspec(5),
                  pl.BlockSpec((None, d, bf), lambda g, i, f: (layer, 0, f)),
                  pl.BlockSpec((None, bf, d), lambda g, i, f: (layer, f, 0)),
                  pl.BlockSpec((1, d), lambda g, i, f: (0, 0))],
        out_specs=pl.BlockSpec((1, bm, d), lambda g, i, f: (g, i, 0)),
        out_shape=jax.ShapeDtypeStruct((g_, m, d), F32),
        scratch_shapes=[pltpu.VMEM((bm, d), BF16)],
        compiler_params=_cparams(("parallel", "parallel", "arbitrary")),
        name="mlp",
    )(x, g2, mod, mod, mod, w_up, w_down, g3)


def _dwconv(raw, tail, w, b):
    t = raw.shape[0]
    assert w.shape[0] == 4
    ext = jnp.concatenate([tail, raw], axis=0)
    z1 = pltpu.roll(ext, 1, 0)
    near = ext * w[3:4] + z1 * w[2:3]
    far = ext * w[1:2] + z1 * w[0:1]
    y = near + pltpu.roll(far, 2, 0)
    return y[TAIL_ROWS:] + b, ext[t:t + TAIL_ROWS]


def _cumsum_rows(x):
    n = x.shape[0]
    row = lax.broadcasted_iota(jnp.int32, x.shape, 0)
    d = 1
    while d < n:
        x = x + jnp.where(row >= d, pltpu.roll(x, d, 0), 0.0)
        d *= 2
    return x


def _ssd_kernel(z_ref, xs_ref, bc_ref, dt_ref, buf_ref, h0_ref, cw_ref, cb_ref, dtb_ref, alog_ref,
                dsk_ref, ng_ref, exp_ref, y_ref, st_ref, tail_scr, *, q_valid):
    q = SSM_CHUNK
    c = pl.program_id(1)

    @pl.when(c == 0)
    def _():
        tail_scr[...] = buf_ref[0]
        st_ref[0] = h0_ref[0]

    def padded(v):
        if q_valid == q:
            return v
        return jnp.concatenate([v, jnp.zeros((q - q_valid, v.shape[1]), v.dtype)], axis=0)

    cw = cw_ref[...]
    cb = cb_ref[...]
    xs, tail_x = _dwconv(xs_ref[0], tail_scr[:, :SSM_INNER], cw[:, :SSM_INNER], cb[:, :SSM_INNER])
    bcm, tail_b = _dwconv(bc_ref[0], tail_scr[:, SSM_INNER:], cw[:, SSM_INNER:], cb[:, SSM_INNER:])
    tail_scr[:, :SSM_INNER] = tail_x
    tail_scr[:, SSM_INNER:] = tail_b
    xs = padded(_silu(xs))
    bcm = padded(_silu(bcm))
    z = padded(z_ref[0])

    dt = _softplus(dt_ref[0] + dtb_ref[...])
    dt = padded(dt)
    a = -jnp.exp(alog_ref[...])
    acs = _cumsum_rows(dt * a)
    acs_t = acs.T

    full = jnp.dot(jnp.concatenate(_split3(dt) + _split3(acs), axis=0), exp_ref[...], preferred_element_type=F32)
    dt_full = (full[0:q] + full[q:2 * q]) + full[2 * q:3 * q]
    acs_full = (full[3 * q:4 * q] + full[4 * q:5 * q]) + full[5 * q:6 * q]
    xdt = xs * dt_full
    dec_out = jnp.exp(acs_full)
    xw = (xdt * jnp.exp(acs_full[q - 1:q] - acs_full)).astype(BF16)
    xdt_b = xdt.astype(BF16)
    bc_b = bcm.astype(BF16)

    row = lax.broadcasted_iota(jnp.int32, (q, q), 0)
    col = lax.broadcasted_iota(jnp.int32, (q, q), 1)
    causal = row >= col
    lane = lax.broadcasted_iota(jnp.int32, (q, LANES), 1)
    lo_half = lane < SSM_HEAD_DIM
    nt = (((1,), (1,)), ((), ()))
    tn = (((0,), (0,)), ((), ()))
    n_bc = SSM_GROUPS * SSM_STATE

    for g in range(SSM_GROUPS):
        b_g = bc_b[:, g * SSM_STATE:(g + 1) * SSM_STATE]
        c_g = bc_b[:, n_bc + g * SSM_STATE:n_bc + (g + 1) * SSM_STATE]
        cbm = lax.dot_general(c_g, b_g, nt, preferred_element_type=F32)
        g0 = g * GROUP_WIDTH
        h_g = st_ref[0, g0:g0 + GROUP_WIDTH, :]
        y_off = lax.dot_general(c_g, h_g.astype(BF16), nt, preferred_element_type=F32)
        y_g = y_off * dec_out[:, g0:g0 + GROUP_WIDTH]
        pieces = []
        for pair in range(HEADS_PER_GROUP // 2):
            x_pair = xdt_b[:, g0 + pair * LANES:g0 + (pair + 1) * LANES]
            acc = None
            for e in range(2):
                h = g * HEADS_PER_GROUP + pair * 2 + e
                seg = jnp.exp(jnp.where(causal, acs[:, h:h + 1] - acs_t[h:h + 1, :], -jnp.inf))
                m = (cbm * seg).astype(BF16)
                x_h = jnp.where(lo_half if e == 0 else jnp.logical_not(lo_half), x_pair, jnp.zeros_like(x_pair))
                part = jnp.dot(m, x_h, preferred_element_type=F32)
                acc = part if acc is None else acc + part
            pieces.append(acc)
        y_g = y_g + jnp.concatenate(pieces, axis=1)
        y_g = y_g + xs[:, g0:g0 + GROUP_WIDTH] * dsk_ref[:, g0:g0 + GROUP_WIDTH]
        y_g = y_g * _silu(z[:, g0:g0 + GROUP_WIDTH])
        y_g = y_g * lax.rsqrt(jnp.mean(y_g * y_g, axis=-1, keepdims=True) + EPS)
        y_g = y_g * ng_ref[:, g0:g0 + GROUP_WIDTH]
        y_ref[0, :, g0:g0 + GROUP_WIDTH] = y_g[:q_valid].astype(y_ref.dtype)

        st_new = lax.dot_general(xw[:, g0:g0 + GROUP_WIDTH], b_g, tn, preferred_element_type=F32)
        for r in range(HEADS_PER_GROUP):
            h = g * HEADS_PER_GROUP + r
            decay = jnp.exp(acs_t[h:h + 1, q - 1:q])
            r0 = g0 + r * SSM_HEAD_DIM
            st_ref[0, r0:r0 + SSM_HEAD_DIM, :] = (
                h_g[r * SSM_HEAD_DIM:(r + 1) * SSM_HEAD_DIM] * decay
                + st_new[r * SSM_HEAD_DIM:(r + 1) * SSM_HEAD_DIM])


def _ssd(proj, dtp, buf8, h0, conv_w, conv_b, dt_bias, a_log, d_full, norm_g, expand, rows):
    b_, l, _ = proj.shape
    nc = l // rows
    st_rows = SSM_INNER
    return pl.pallas_call(
        functools.partial(_ssd_kernel, q_valid=rows),
        grid=(b_, nc),
        in_specs=[pl.BlockSpec((1, rows, SSM_INNER), lambda b, c: (b, c, 0)),
                  pl.BlockSpec((1, rows, SSM_INNER), lambda b, c: (b, c, 1)),
                  pl.BlockSpec((1, rows, SSM_INNER), lambda b, c: (b, c, 2)),
                  pl.BlockSpec((1, rows, LANES), lambda b, c: (b, c, 0)),
                  pl.BlockSpec((1, TAIL_ROWS, SSM_CONV_DIM), lambda b, c: (b, 0, 0)),
                  pl.BlockSpec((1, st_rows, SSM_STATE), lambda b, c: (b, 0, 0)),
                  pl.BlockSpec((SSM_CONV, SSM_CONV_DIM), lambda b, c: (0, 0)),
                  pl.BlockSpec((1, SSM_CONV_DIM), lambda b, c: (0, 0)),
                  pl.BlockSpec((1, LANES), lambda b, c: (0, 0)),
                  pl.BlockSpec((1, LANES), lambda b, c: (0, 0)),
                  pl.BlockSpec((1, SSM_INNER), lambda b, c: (0, 0)),
                  pl.BlockSpec((1, SSM_INNER), lambda b, c: (0, 0)),
                  pl.BlockSpec((LANES, SSM_INNER), lambda b, c: (0, 0))],
        out_specs=[pl.BlockSpec((1, rows, SSM_INNER), lambda b, c: (b, c, 0)),
                   pl.BlockSpec((1, st_rows, SSM_STATE), lambda b, c: (b, 0, 0))],
        out_shape=[jax.ShapeDtypeStruct((b_, l, SSM_INNER), BF16),
                   jax.ShapeDtypeStruct((b_, st_rows, SSM_STATE), F32)],
        scratch_shapes=[pltpu.VMEM((TAIL_ROWS, SSM_CONV_DIM), F32)],
        compiler_params=_cparams(("parallel", "arbitrary")),
        name="ssd",
    )(proj, proj, proj, dtp, buf8, h0, conv_w, conv_b, dt_bias, a_log, d_full, norm_g, expand)


def _lru_kernel(gate_ref, xr_ref, buf_ref, h0_ref, cw_ref, cb_ref, wa_ref, ba_ref, wx_ref, bx_ref,
                lam_ref, y_ref, hl_ref, tail_scr, a_scr, u_scr):
    t = xr_ref.shape[1]
    c = pl.program_id(1)

    @pl.when(c == 0)
    def _():
        tail_scr[...] = buf_ref[0]
        hl_ref[0] = h0_ref[0]

    xc, tail = _dwconv(xr_ref[0], tail_scr[...], cw_ref[...], cb_ref[...])
    tail_scr[...] = tail
    sp = _softplus(-lam_ref[...])
    for k in range(LRU_BLOCKS):
        k0 = k * LRU_BLOCK
        xk = xc[:, k0:k0 + LRU_BLOCK]
        xkb = xk.astype(BF16)
        gr = _sigmoid(jnp.dot(xkb, wa_ref[k], preferred_element_type=F32) + ba_ref[:, k0:k0 + LRU_BLOCK])
        gi = _sigmoid(jnp.dot(xkb, wx_ref[k], preferred_element_type=F32) + bx_ref[:, k0:k0 + LRU_BLOCK])
        log_a = -LRU_C * gr * sp[:, k0:k0 + LRU_BLOCK]
        a_scr[:, k0:k0 + LRU_BLOCK] = jnp.exp(log_a)
        u_scr[:, k0:k0 + LRU_BLOCK] = _sqrt_neg_expm1(2.0 * log_a) * gi * xk

    row = lax.broadcasted_iota(jnp.int32, (SUBLANES, LRU_WIDTH), 0)

    def tile(i, h_prev):
        r0 = pl.multiple_of(i * SUBLANES, SUBLANES)
        a = a_scr[pl.ds(r0, SUBLANES), :]
        u = u_scr[pl.ds(r0, SUBLANES), :]
        d = 1
        while d < SUBLANES:
            keep = row >= d
            u = u + a * jnp.where(keep, pltpu.roll(u, d, 0), 0.0)
            a = a * jnp.where(keep, pltpu.roll(a, d, 0), 1.0)
            d *= 2
        h = u + a * h_prev
        u_scr[pl.ds(r0, SUBLANES), :] = h
        return h[SUBLANES - 1:SUBLANES, :]

    h_last = lax.fori_loop(0, t // SUBLANES, tile, hl_ref[0])
    hl_ref[0] = h_last
    y_ref[0] = (u_scr[...] * jax.nn.gelu(gate_ref[0])).astype(y_ref.dtype)


def _lru(proj, buf8, h0, conv_w, conv_b, w_a, b_a, w_x, b_x, lam, rows, col0):
    b_, l, _ = proj.shape
    w = LRU_WIDTH
    full2 = lambda shape: pl.BlockSpec(shape, lambda b, c: (0,) * len(shape))
    return pl.pallas_call(
        _lru_kernel,
        grid=(b_, l // rows),
        in_specs=[pl.BlockSpec((1, rows, w), lambda b, c: (b, c, col0)),
                  pl.BlockSpec((1, rows, w), lambda b, c: (b, c, col0 + 1)),
                  pl.BlockSpec((1, TAIL_ROWS, w), lambda b, c: (b, 0, 0)),
                  pl.BlockSpec((1, 1, w), lambda b, c: (b, 0, 0)),
                  full2((SSM_CONV, w)), full2((1, w)),
                  full2((LRU_BLOCKS, LRU_BLOCK, LRU_BLOCK)), full2((1, w)),
                  full2((LRU_BLOCKS, LRU_BLOCK, LRU_BLOCK)), full2((1, w)),
                  full2((1, w))],
        out_specs=[pl.BlockSpec((1, rows, w), lambda b, c: (b, c, 0)),
                   pl.BlockSpec((1, 1, w), lambda b, c: (b, 0, 0))],
        out_shape=[jax.ShapeDtypeStruct((b_, l, w), BF16),
                   jax.ShapeDtypeStruct((b_, 1, w), F32)],
        scratch_shapes=[pltpu.VMEM((TAIL_ROWS, w), F32), pltpu.VMEM((rows, w), F32),
                        pltpu.VMEM((rows, w), F32)],
        compiler_params=_cparams(("parallel", "arbitrary")),
        name="lru",
    )(proj, proj, buf8, h0, conv_w, conv_b, w_a, b_a, w_x, b_x, lam)


def _head_variants(slab, half):
    lane = lax.broadcasted_iota(jnp.int32, slab.shape, 1)
    keep = (lane < HEAD_DIM) if half == 0 else (lane >= HEAD_DIM)
    own = jnp.where(keep, slab, 0.0)
    other = pltpu.roll(own, HEAD_DIM, 1)
    return (own, other) if half == 0 else (other, own)


def _slab_attention(q_ref, o_ref, k2, v2, valid, sink_ref):
    w = q_ref.shape[1]
    nk = k2.shape[0]
    nt = (((1,), (1,)), ((), ()))
    scale = HEAD_DIM ** -0.5
    heads = range(KV_HEADS)
    halves = [(e, c) for e in heads for c in range(2)]
    vvs, ss = [], []
    for e in heads:
        slab, half = divmod(e, 2)
        k_lo, k_hi = _head_variants(k2[:, slab * LANES:(slab + 1) * LANES], half)
        v_lo, v_hi = _head_variants(v2[:, slab * LANES:(slab + 1) * LANES], half)
        kk = jnp.concatenate([k_lo, k_hi], axis=0).astype(BF16)
        vvs.append(jnp.concatenate([v_lo, v_hi], axis=0).astype(BF16))
        c0 = 2 * e * LANES
        qq = jnp.concatenate([q_ref[0, :, c0:c0 + LANES], q_ref[0, :, c0 + LANES:c0 + 2 * LANES]],
                             axis=0).astype(BF16)
        ss.append(lax.dot_general(qq, kk, nt, preferred_element_type=F32) * scale)
    scs = [jnp.where(valid, ss[e][:, c * nk:(c + 1) * nk], -jnp.inf) for e, c in halves]
    sinks = [jnp.concatenate([jnp.full((w, 1), sink_ref[Q_PER_KV * e + c], F32),
                              jnp.full((w, 1), sink_ref[Q_PER_KV * e + 2 + c], F32)], axis=0) for e, c in halves]
    ms = [jnp.maximum(jnp.max(sc, axis=1, keepdims=True), sk) for sc, sk in zip(scs, sinks)]
    ps = [jnp.exp(sc - m) for sc, m in zip(scs, ms)]
    dens = [jnp.sum(p, axis=1, keepdims=True) + jnp.exp(sk - m) for p, sk, m in zip(ps, sinks, ms)]
    pn = [(p * (1.0 / den)).astype(BF16) for p, den in zip(ps, dens)]
    for e in heads:
        o = jnp.dot(jnp.concatenate(pn[2 * e:2 * e + 2], axis=1), vvs[e], preferred_element_type=F32)
        c0 = 2 * e * LANES
        o_ref[0, :, c0:c0 + LANES] = o[:w].astype(o_ref.dtype)
        o_ref[0, :, c0 + LANES:c0 + 2 * LANES] = o[w:].astype(o_ref.dtype)


def _attn_prompt_kernel(sink_ref, q_ref, kc_ref, kp_ref, vc_ref, vp_ref, o_ref):
    w = WINDOW
    has_prev = pl.program_id(1) > 0
    k2 = jnp.concatenate([jnp.where(has_prev, kp_ref[0], 0.0), kc_ref[0]], axis=0)
    v2 = jnp.concatenate([jnp.where(has_prev, vp_ref[0], 0.0), vc_ref[0]], axis=0)
    t = lax.broadcasted_iota(jnp.int32, (2 * w, 2 * w), 0) % w
    j = lax.broadcasted_iota(jnp.int32, (2 * w, 2 * w), 1)
    _slab_attention(q_ref, o_ref, k2, v2, (j > t) & (j <= t + w), sink_ref)


def _attn_prompt(qkv, sinks):
    b_, l, _ = qkv.shape
    w = WINDOW
    qw = ATT_HEADS * HEAD_DIM
    kvw = KV_HEADS * HEAD_DIM
    k_blk = qw // kvw
    prev = lambda n: jnp.maximum(n - 1, 0)
    return pl.pallas_call(
        _attn_prompt_kernel,
        grid=(b_, l // w),
        in_specs=[pl.BlockSpec(memory_space=pltpu.SMEM),
                  pl.BlockSpec((1, w, qw), lambda b, n: (b, n, 0)),
                  pl.BlockSpec((1, w, kvw), lambda b, n: (b, n, k_blk)),
                  pl.BlockSpec((1, w, kvw), lambda b, n: (b, prev(n), k_blk)),
                  pl.BlockSpec((1, w, kvw), lambda b, n: (b, n, k_blk + 1)),
                  pl.BlockSpec((1, w, kvw), lambda b, n: (b, prev(n), k_blk + 1))],
        out_specs=pl.BlockSpec((1, w, qw), lambda b, n: (b, n, 0)),
        out_shape=jax.ShapeDtypeStruct((b_, l, qw), BF16),
        compiler_params=_cparams(("parallel", "parallel")),
        name="attn_prompt",
    )(sinks, qkv, qkv, qkv, qkv, qkv)


def _attn_sample_kernel(sink_ref, qkv_ref, ck_ref, cv_ref, o_ref, nk_ref, nv_ref):
    l = qkv_ref.shape[1]
    n_keep = ck_ref.shape[1]
    kcol = ATT_HEADS * HEAD_DIM
    vcol = kcol + KV_HEADS * HEAD_DIM
    qkv = qkv_ref[0]
    k_all = jnp.concatenate([ck_ref[0], qkv[:, kcol:vcol]], axis=0)
    v_all = jnp.concatenate([cv_ref[0], qkv[:, vcol:]], axis=0)
    nk_ref[0] = k_all[l:]
    nv_ref[0] = v_all[l:]
    nk = -(-(n_keep + l) // LANES) * LANES
    pad = jnp.zeros((nk - n_keep - l, k_all.shape[1]), F32)
    t = lax.broadcasted_iota(jnp.int32, (2 * l, nk), 0) % l
    j = lax.broadcasted_iota(jnp.int32, (2 * l, nk), 1)
    rel = jnp.where(j < n_keep, t + n_keep - j, t - (j - n_keep))
    valid = (rel >= 0) & (rel < WINDOW) & (j < n_keep + l)
    _slab_attention(qkv_ref, o_ref, jnp.concatenate([k_all, pad], axis=0), jnp.concatenate([v_all, pad], axis=0),
                    valid, sink_ref)


def _attn_sample(qkv, cache_k, cache_v, sinks):
    b_, l, n = qkv.shape
    n_keep, kvw = cache_k.shape[1:]
    return pl.pallas_call(
        _attn_sample_kernel,
        grid=(b_,),
        in_specs=[pl.BlockSpec(memory_space=pltpu.SMEM),
                  pl.BlockSpec((1, l, n), lambda b: (b, 0, 0)),
                  pl.BlockSpec((1, n_keep, kvw), lambda b: (b, 0, 0)),
                  pl.BlockSpec((1, n_keep, kvw), lambda b: (b, 0, 0))],
        out_specs=[pl.BlockSpec((1, l, ATT_HEADS * HEAD_DIM), lambda b: (b, 0, 0)),
                   pl.BlockSpec((1, n_keep, kvw), lambda b: (b, 0, 0)),
                   pl.BlockSpec((1, n_keep, kvw), lambda b: (b, 0, 0))],
        out_shape=[jax.ShapeDtypeStruct((b_, l, ATT_HEADS * HEAD_DIM), BF16),
                   jax.ShapeDtypeStruct((b_, n_keep, kvw), F32),
                   jax.ShapeDtypeStruct((b_, n_keep, kvw), F32)],
        compiler_params=_cparams(("parallel",)),
        name="attn_sample",
    )(sinks, qkv, cache_k, cache_v)


def _rope_tables(pos):
    half = HEAD_DIM // 2
    inv_freq = ROPE_THETA ** (-jnp.arange(half, dtype=F32) / half)
    ang = pos.astype(F32)[:, None] * inv_freq[None, :]
    cos = jnp.cos(ang)
    sin = jnp.sin(ang)
    reps = LANES // HEAD_DIM
    return (jnp.tile(jnp.concatenate([cos, cos], axis=1), (1, reps)),
            jnp.tile(jnp.concatenate([-sin, sin], axis=1), (1, reps)))


def _tail_pad(buf):
    return jnp.pad(buf, ((0, 0), (TAIL_ROWS - buf.shape[1], 0), (0, 0)))


def _trunk(x, mods, pos, st, pw, seq_len):
    g_, m, d = x.shape
    nseq = g_ * m // seq_len
    blk = _block_rows(m)
    chunk = min(seq_len, SSM_CHUNK)
    lru_rows = min(seq_len, LRU_ROWS)

    proj, dtp = _inproj(x, pw['norms'][0, 0:1], mods[0], pw['w_in'], SSM_INNER + SSM_CONV_DIM, pw['w_in_b'],
                        pw['w_in_dt'], blk['inproj'], INPROJ_COLS)
    n_main = proj.shape[-1]
    proj_s = proj.reshape(nseq, seq_len, n_main)
    dtp_s = dtp.reshape(nseq, seq_len, LANES)
    y_ssm, ssm_h = _ssd(proj_s, dtp_s, _tail_pad(st['ssm_conv']), st['ssm'].reshape(nseq, SSM_INNER, SSM_STATE),
                        pw['ssm_conv_w'], pw['ssm_conv_b'], pw['ssm_dt_bias'], pw['ssm_a_log'], pw['ssm_d_full'],
                        pw['ssm_norm'], pw['expand'], chunk)
    y_lru, lru_h = _lru(proj_s, _tail_pad(st['lru_conv']), st['lru'].reshape(nseq, 1, LRU_WIDTH),
                        pw['lru_conv_w'], pw['lru_conv_b'], pw['lru_w_a'], pw['lru_b_a'], pw['lru_w_x'],
                        pw['lru_b_x'], pw['lru_lambda'], lru_rows, 3)
    keep = SSM_CONV - 1
    new_ssm_conv = proj_s[:, seq_len - keep:, SSM_INNER:SSM_INNER + SSM_CONV_DIM]
    new_lru_conv = proj_s[:, seq_len - keep:, n_main - LRU_WIDTH:]
    x = _outproj([y_ssm.reshape(g_, m, SSM_INNER), y_lru.reshape(g_, m, LRU_WIDTH)],
                 pw['w_out_hyb'], x, pw['norms'][0, 1:2], mods[0], blk['outproj'])
    x = _mlp(x, pw['norms'][0, 2:3], pw['norms'][0, 3:4], mods[0], pw['w_up'], pw['w_down'], 0, blk['mlp'],
             MLP_COLS)

    cos_t, sin_t = _rope_tables(pos)
    if seq_len < blk['qkv']:
        cos_t = jnp.tile(cos_t, (blk['qkv'] // seq_len, 1))
        sin_t = jnp.tile(sin_t, (blk['qkv'] // seq_len, 1))
    qkv = _qkv(x, pw['norms'][1, 0:1], mods[1], pw['w_qkv'], pw['b_qkv'], cos_t, sin_t, blk['qkv'], QKV_COLS)
    qkv_s = qkv.reshape(nseq, seq_len, qkv.shape[-1])
    kcol = ATT_HEADS * HEAD_DIM
    vcol = kcol + KV_HEADS * HEAD_DIM
    if st['k'] is None:
        o = _attn_prompt(qkv_s, pw['attn_sinks'])
        n_keep = min(WINDOW, seq_len)
        k_new = qkv_s[:, seq_len - n_keep:, kcol:vcol]
        v_new = qkv_s[:, seq_len - n_keep:, vcol:]
    else:
        n_keep = st['k'].shape[1]
        o, k_new, v_new = _attn_sample(qkv_s, st['k'].reshape(nseq, n_keep, KV_HEADS * HEAD_DIM),
                                       st['v'].reshape(nseq, n_keep, KV_HEADS * HEAD_DIM), pw['attn_sinks'])
    k_new = k_new.reshape(nseq, n_keep, KV_HEADS, HEAD_DIM)
    v_new = v_new.reshape(nseq, n_keep, KV_HEADS, HEAD_DIM)
    x = _outproj([o.reshape(g_, m, kcol)], pw['w_out_attn'], x, pw['norms'][1, 1:2], mods[1], blk['outproj'])
    x = _mlp(x, pw['norms'][1, 2:3], pw['norms'][1, 3:4], mods[1], pw['w_up'], pw['w_down'], 1, blk['mlp'],
             MLP_COLS)

    states = (new_ssm_conv[None], ssm_h.reshape(1, nseq, SSM_HEADS, SSM_HEAD_DIM, SSM_STATE),
              new_lru_conv[None], lru_h.reshape(1, nseq, LRU_WIDTH), k_new[None], v_new[None])
    return x, states


def kernel(x_prompt, x_sample, state_ssm_conv, state_ssm, state_lru_conv, state_lru, cache_k, cache_v, c_prompt, c_sample, w_mod, b_mod, norms, w_in_hyb, ssm_conv_w, ssm_conv_b, ssm_dt_bias, ssm_a_log, ssm_d, ssm_norm, lru_conv_w, lru_conv_b, lru_w_a, lru_b_a, lru_w_x, lru_b_x, lru_lambda, w_out_hyb, w_qkv, b_qkv, attn_sinks, w_out_attn, w_up, w_down):
    bp, lp, d = x_prompt.shape
    bs, ls, _ = x_sample.shape
    depth = w_mod.shape[0]

    dt0 = SSM_INNER + SSM_CONV_DIM
    w_in = w_in_hyb[0]
    pad_lanes = lambda v: jnp.pad(v.reshape(1, -1), ((0, 0), (0, LANES - v.size)))
    head_of_lane = jnp.arange(SSM_INNER) // SSM_HEAD_DIM
    pw = {
        'norms': norms,
        'w_in': w_in, 'w_in_b': w_in[:, dt0 + SSM_HEADS:].astype(BF16),
        'w_in_dt': jnp.pad(w_in[:, dt0:dt0 + SSM_HEADS], ((0, 0), (0, LANES - SSM_HEADS))).astype(BF16),
        'ssm_conv_w': ssm_conv_w[0], 'ssm_conv_b': ssm_conv_b[0].reshape(1, -1),
        'ssm_dt_bias': pad_lanes(ssm_dt_bias[0]), 'ssm_a_log': pad_lanes(ssm_a_log[0]),
        'ssm_d_full': jnp.repeat(ssm_d[0], SSM_HEAD_DIM).reshape(1, -1),
        'ssm_norm': ssm_norm[0].reshape(1, -1),
        'expand': (jnp.arange(LANES)[:, None] == head_of_lane[None, :]).astype(BF16),
        'lru_conv_w': lru_conv_w[0], 'lru_conv_b': lru_conv_b[0].reshape(1, -1),
        'lru_w_a': lru_w_a[0].astype(BF16), 'lru_b_a': lru_b_a[0].reshape(1, -1),
        'lru_w_x': lru_w_x[0].astype(BF16), 'lru_b_x': lru_b_x[0].reshape(1, -1),
        'lru_lambda': lru_lambda[0].reshape(1, -1),
        'w_out_hyb': w_out_hyb[0].astype(BF16),
        'w_qkv': w_qkv[0].astype(BF16), 'b_qkv': b_qkv[0].reshape(1, -1),
        'attn_sinks': attn_sinks[0],
        'w_out_attn': w_out_attn[0].astype(BF16),
        'w_up': w_up, 'w_down': w_down,
    }

    mod = _adaln(jnp.concatenate([c_prompt, c_sample], axis=0), w_mod, b_mod)
    mods_p = [mod[l, :bp][:, None, :] for l in range(depth)]
    mods_s = [jnp.repeat(mod[l, bp:], ls, axis=0)[None] for l in range(depth)]

    zeros = lambda *shape: jnp.zeros(shape, F32)
    st_p = {'ssm_conv': zeros(bp, SSM_CONV - 1, SSM_CONV_DIM), 'ssm': zeros(bp, SSM_INNER, SSM_STATE),
            'lru_conv': zeros(bp, SSM_CONV - 1, LRU_WIDTH), 'lru': zeros(bp, LRU_WIDTH), 'k': None, 'v': None}
    st_s = {'ssm_conv': state_ssm_conv[0], 'ssm': state_ssm[0], 'lru_conv': state_lru_conv[0],
            'lru': state_lru[0], 'k': cache_k[0], 'v': cache_v[0]}

    pos_p = jnp.arange(lp, dtype=jnp.int32)
    pos_s = PAST_LEN + jnp.arange(ls, dtype=jnp.int32)
    y_p, sp = _trunk(x_prompt, mods_p, pos_p, st_p, pw, lp)
    y_s, ss = _trunk(x_sample.reshape(1, bs * ls, d), mods_s, pos_s, st_s, pw, ls)
    return (y_p, y_s.reshape(bs, ls, d)) + sp + ss
```

```python
import functools
import math

import jax
import jax.numpy as jnp
from jax import lax
from jax.experimental import pallas as pl
from jax.experimental.pallas import tpu as pltpu

F32 = jnp.float32
BF16 = jnp.bfloat16

D_MODEL = 2048
PAST_LEN = 16384
SSM_HEADS = 32
SSM_HEAD_DIM = 64
SSM_INNER = SSM_HEADS * SSM_HEAD_DIM
SSM_GROUPS = 8
SSM_STATE = 128
SSM_CONV = 4
SSM_CHUNK = 128
SSM_CONV_DIM = SSM_INNER + 2 * SSM_GROUPS * SSM_STATE
HEADS_PER_GROUP = SSM_HEADS // SSM_GROUPS
GROUP_WIDTH = HEADS_PER_GROUP * SSM_HEAD_DIM
LRU_WIDTH = D_MODEL
LRU_BLOCKS = 8
LRU_BLOCK = LRU_WIDTH // LRU_BLOCKS
LRU_C = 8.0
ATT_HEADS = 32
KV_HEADS = 8
HEAD_DIM = 64
Q_PER_KV = ATT_HEADS // KV_HEADS
WINDOW = 128
ROPE_THETA = 10000.0
D_FF = 4 * D_MODEL
EPS = 1e-6

LANES = 128
SUBLANES = 8
BF16_SUBLANES = 16
TAIL_ROWS = SUBLANES
VMEM_LIMIT = 56 * 1024 * 1024


INPROJ_COLS = 1024
MLP_COLS = 512
QKV_COLS = 512
LRU_ROWS = 256


def _block_rows(m):
    return {'inproj': min(m, 1024), 'mlp': min(m, 1024), 'qkv': min(m, 512), 'outproj': min(m, 512)}


def _cparams(sem):
    return pltpu.CompilerParams(dimension_semantics=sem, vmem_limit_bytes=VMEM_LIMIT)


def _rms(x, g):
    return x * lax.rsqrt(jnp.mean(x * x, axis=-1, keepdims=True) + EPS) * g


ROW_CHUNK = 16
ROW_UNROLL = 4


def _for_row_chunks(n_rows, body):
    def step(i, carry):
        body(pl.multiple_of(i * ROW_CHUNK, ROW_CHUNK))
        return carry

    lax.fori_loop(0, n_rows // ROW_CHUNK, step, 0, unroll=ROW_UNROLL)


def _mod_rows(ref, r0):
    return ref[0] if ref.shape[1] == 1 else ref[0, pl.ds(r0, ROW_CHUNK), :]


def _prenorm_to(h_scr, x_ref, g_ref, sc_ref, sh_ref):
    def body(r0):
        x = x_ref[0, pl.ds(r0, ROW_CHUNK), :]
        h = _rms(x, g_ref[...]) * (1.0 + _mod_rows(sc_ref, r0)) + _mod_rows(sh_ref, r0)
        h_scr[pl.ds(r0, ROW_CHUNK), :] = h.astype(BF16)

    _for_row_chunks(x_ref.shape[1], body)


def _sigmoid(x):
    return 0.5 * (jnp.tanh(0.5 * x) + 1.0)


def _silu(x):
    return x * _sigmoid(x)


def _softplus(x):
    return jnp.maximum(x, 0.0) + jnp.log1p(jnp.exp(-jnp.abs(x)))


def _sqrt_neg_expm1(x):
    s = -jnp.tanh(0.5 * x)
    return jnp.sqrt(2.0 * s) * lax.rsqrt(1.0 + s)


def _split3(x):
    hi = x.astype(BF16)
    r = x - hi.astype(F32)
    mid = r.astype(BF16)
    lo = (r - mid.astype(F32)).astype(BF16)
    return [hi, mid, lo]


def _adaln_kernel(c_ref, w_ref, b_ref, o_ref):
    s = _silu(c_ref[...]).astype(BF16)
    o_ref[0] = jnp.dot(s, w_ref[0].astype(BF16), preferred_element_type=F32) + b_ref[0]


def _adaln(c_all, w_mod, b_mod):
    nb, d = c_all.shape
    depth, _, n = w_mod.shape
    bn = 1024
    return pl.pallas_call(
        _adaln_kernel,
        grid=(depth, n // bn),
        in_specs=[pl.BlockSpec((nb, d), lambda l, j: (0, 0)),
                  pl.BlockSpec((1, d, bn), lambda l, j: (l, 0, j)),
                  pl.BlockSpec((1, 1, bn), lambda l, j: (l, 0, j))],
        out_specs=pl.BlockSpec((1, nb, bn), lambda l, j: (l, 0, j)),
        out_shape=jax.ShapeDtypeStruct((depth, nb, n), F32),
        compiler_params=_cparams(("parallel", "parallel")),
        name="adaln",
    )(c_all, w_mod, b_mod.reshape(depth, 1, n))


def _mod_spec(mod, bm, k):
    if mod.shape[1] == 1:
        return pl.BlockSpec((1, 1, D_MODEL), lambda g, i, *_: (g, 0, k))
    return pl.BlockSpec((1, bm, D_MODEL), lambda g, i, *_: (g, i, k))


def _inproj_kernel(x_ref, g_ref, sh_ref, sc_ref, wa_ref, wb_ref, wdt_ref, o_ref, dt_ref, h_scr, *, na):
    j = pl.program_id(2)

    @pl.when(j == 0)
    def _():
        _prenorm_to(h_scr, x_ref, g_ref, sc_ref, sh_ref)
        dt_ref[0] = jnp.dot(h_scr[...], wdt_ref[...], preferred_element_type=F32)

    @pl.when(j < na)
    def _():
        o_ref[0] = jnp.dot(h_scr[...], wa_ref[...].astype(BF16), preferred_element_type=F32)

    @pl.when(j >= na)
    def _():
        o_ref[0] = jnp.dot(h_scr[...], wb_ref[...], preferred_element_type=F32)


def _inproj(x, gain, mod, w_a, n_a, w_b, w_dt, bm, bn):
    g_, m, d = x.shape
    na = n_a // bn
    n = n_a + w_b.shape[1]
    return pl.pallas_call(
        functools.partial(_inproj_kernel, na=na),
        grid=(g_, m // bm, n // bn),
        in_specs=[pl.BlockSpec((1, bm, d), lambda g, i, j: (g, i, 0), pipeline_mode=pl.Buffered(1)),
                  pl.BlockSpec((1, d), lambda g, i, j: (0, 0)),
                  _mod_spec(mod, bm, 0), _mod_spec(mod, bm, 1),
                  pl.BlockSpec((d, bn), lambda g, i, j: (0, jnp.minimum(j, na - 1))),
                  pl.BlockSpec((d, bn), lambda g, i, j: (0, jnp.maximum(j - na, 0))),
                  pl.BlockSpec((d, LANES), lambda g, i, j: (0, 0))],
        out_specs=[pl.BlockSpec((1, bm, bn), lambda g, i, j: (g, i, j)),
                   pl.BlockSpec((1, bm, LANES), lambda g, i, j: (g, i, 0))],
        out_shape=[jax.ShapeDtypeStruct((g_, m, n), F32),
                   jax.ShapeDtypeStruct((g_, m, LANES), F32)],
        scratch_shapes=[pltpu.VMEM((bm, d), BF16)],
        compiler_params=_cparams(("parallel", "parallel", "arbitrary")),
        name="inproj",
    )(x, gain, mod, mod, w_a, w_b, w_dt)


def _split_tail_kernel(w_ref, wb_ref, wdt_ref, *, c_dt, n_dt):
    wb_ref[...] = w_ref[:, c_dt + n_dt:].astype(BF16)
    lane = lax.broadcasted_iota(jnp.int32, wdt_ref.shape, 1)
    wdt_ref[...] = jnp.where(lane < n_dt, w_ref[:, c_dt:c_dt + LANES], 0.0).astype(BF16)


def _split_tail_columns(w, c_dt, n_dt):
    _, k, n = w.shape
    rows = LANES
    return pl.pallas_call(
        functools.partial(_split_tail_kernel, c_dt=c_dt, n_dt=n_dt),
        grid=(k // rows,),
        in_specs=[pl.BlockSpec((None, rows, n), lambda r: (0, r, 0))],
        out_specs=[pl.BlockSpec((rows, n - c_dt - n_dt), lambda r: (r, 0)),
                   pl.BlockSpec((rows, LANES), lambda r: (r, 0))],
        out_shape=[jax.ShapeDtypeStruct((k, n - c_dt - n_dt), BF16), jax.ShapeDtypeStruct((k, LANES), BF16)],
        compiler_params=_cparams(("parallel",)),
        name="split_tail",
    )(w)


def _swap_halves(x):
    lane = lax.broadcasted_iota(jnp.int32, x.shape, 1)
    first = (lane % HEAD_DIM) < (HEAD_DIM // 2)
    return jnp.where(first, pltpu.roll(x, LANES - HEAD_DIM // 2, 1), pltpu.roll(x, HEAD_DIM // 2, 1))


def _qkv_kernel(x_ref, g_ref, sh_ref, sc_ref, w_ref, b_ref, cos_ref, sin_ref, o_ref, h_scr, *, rope_cols, bn):
    _prenorm_to(h_scr, x_ref, g_ref, sc_ref, sh_ref)
    h = h_scr[...]
    cos = cos_ref[...]
    sin = sin_ref[...]
    for c0 in range(0, w_ref.shape[1], bn):
        acc = jnp.dot(h, w_ref[:, c0:c0 + bn], preferred_element_type=F32) + b_ref[:, c0:c0 + bn]
        if c0 < rope_cols:
            for c in range(c0, c0 + bn, LANES):
                a = acc[:, c - c0:c - c0 + LANES]
                o_ref[0, :, c:c + LANES] = a * cos + _swap_halves(a) * sin
        else:
            o_ref[0, :, c0:c0 + bn] = acc


def _qkv(x, gain, mod, w, b, cos_t, sin_t, bm, bn):
    g_, m, d = x.shape
    n = w.shape[1]
    rope_cols = (ATT_HEADS + KV_HEADS) * HEAD_DIM
    rows_per_g = cos_t.shape[0] // bm
    return pl.pallas_call(
        functools.partial(_qkv_kernel, rope_cols=rope_cols, bn=bn),
        grid=(g_, m // bm),
        in_specs=[pl.BlockSpec((1, bm, d), lambda g, i: (g, i, 0)),
                  pl.BlockSpec((1, d), lambda g, i: (0, 0)),
                  _mod_spec(mod, bm, 0), _mod_spec(mod, bm, 1),
                  pl.BlockSpec((d, n), lambda g, i: (0, 0), pipeline_mode=pl.Buffered(1)),
                  pl.BlockSpec((1, n), lambda g, i: (0, 0)),
                  pl.BlockSpec((bm, LANES), lambda g, i: (i % rows_per_g, 0)),
                  pl.BlockSpec((bm, LANES), lambda g, i: (i % rows_per_g, 0))],
        out_specs=pl.BlockSpec((1, bm, n), lambda g, i: (g, i, 0)),
        out_shape=jax.ShapeDtypeStruct((g_, m, n), F32),
        scratch_shapes=[pltpu.VMEM((bm, d), BF16)],
        compiler_params=_cparams(("parallel", "parallel")),
        name="qkv",
    )(x, gain, mod, mod, w, b, cos_t, sin_t)


def _outproj_kernel(*refs, n_lhs):
    lhs = refs[:n_lhs]
    ws = refs[n_lhs:2 * n_lhs]
    x_ref, g_ref, gt_ref, o_ref = refs[2 * n_lhs:]
    acc = jnp.dot(lhs[0][0], ws[0][...], preferred_element_type=F32)
    for a, w in zip(lhs[1:], ws[1:]):
        acc = acc + jnp.dot(a[0], w[...], preferred_element_type=F32)
    o_ref[0] = x_ref[0] + gt_ref[0] * _rms(acc, g_ref[...])


def _outproj(lhs_list, w, x, gain, mod, bm):
    g_, m, d = x.shape
    n_lhs = len(lhs_list)
    kw = w.shape[0] // n_lhs
    lhs_specs = [pl.BlockSpec((1, bm, kw), lambda g, i: (g, i, 0)) for _ in lhs_list]
    w_specs = [pl.BlockSpec((kw, d), lambda g, i, k=k: (k, 0), pipeline_mode=pl.Buffered(1)) for k in range(n_lhs)]
    w_list = [w] * n_lhs
    if mod.shape[1] == 1:
        gt_spec = pl.BlockSpec((1, 1, d), lambda g, i: (g, 0, 2))
    else:
        gt_spec = pl.BlockSpec((1, bm, d), lambda g, i: (g, i, 2))
    return pl.pallas_call(
        functools.partial(_outproj_kernel, n_lhs=n_lhs),
        grid=(g_, m // bm),
        in_specs=lhs_specs + w_specs + [pl.BlockSpec((1, bm, d), lambda g, i: (g, i, 0)),
                                        pl.BlockSpec((1, d), lambda g, i: (0, 0)), gt_spec],
        out_specs=pl.BlockSpec((1, bm, d), lambda g, i: (g, i, 0)),
        out_shape=jax.ShapeDtypeStruct((g_, m, d), F32),
        compiler_params=_cparams(("parallel", "parallel")),
        name="outproj",
    )(*lhs_list, *w_list, x, gain, mod)


def _mlp_kernel(x_ref, g2_ref, sh_ref, sc_ref, gt_ref, wu_ref, wd_ref, g3_ref, o_ref, h_scr):
    f = pl.program_id(2)

    @pl.when(f == 0)
    def _():
        _prenorm_to(h_scr, x_ref, g2_ref, sc_ref, sh_ref)
        o_ref[0] = jnp.zeros(o_ref.shape[1:], F32)

    u = jnp.dot(h_scr[...], wu_ref[...].astype(BF16), preferred_element_type=F32)
    u = jnp.square(jnp.maximum(u, 0.0)).astype(BF16)
    o_ref[0] += jnp.dot(u, wd_ref[...].astype(BF16), preferred_element_type=F32)

    @pl.when(f == pl.num_programs(2) - 1)
    def _():
        o_ref[0] = x_ref[0] + gt_ref[0] * _rms(o_ref[0], g3_ref[...])


def _mlp(x, g2, g3, mod, w_up, w_down, layer, bm, bf):
    g_, m, d = x.shape
    dff = w_up.shape[2]
    if mod.shape[1] == 1:
        mspec = lambda k: pl.BlockSpec((1, 1, d), lambda g, i, f: (g, 0, k))
    else:
        mspec = lambda k: pl.BlockSpec((1, bm, d), lambda g, i, f: (g, i, k))
    return pl.pallas_call(
        _mlp_kernel,
        grid=(g_, m // bm, dff // bf),
        in_specs=[pl.BlockSpec((1, bm, d), lambda g, i, f: (g, i, 0), pipeline_mode=pl.Buffered(1)),
                  pl.BlockSpec((1, d), lambda g, i, f: (0, 0)),
                  mspec(3), mspec(4), mspec(5),
                  pl.BlockSpec((None, d, bf), lambda g, i, f: (layer, 0, f)),
                  pl.BlockSpec((None, bf, d), lambda g, i, f: (layer, f, 0)),
                  pl.BlockSpec((1, d), lambda g, i, f: (0, 0))],
        out_specs=pl.BlockSpec((1, bm, d), lambda g, i, f: (g, i, 0)),
        out_shape=jax.ShapeDtypeStruct((g_, m, d), F32),
        scratch_shapes=[pltpu.VMEM((bm, d), BF16)],
        compiler_params=_cparams(("parallel", "parallel", "arbitrary")),
        name="mlp",
    )(x, g2, mod, mod, mod, w_up, w_down, g3)


def _dwconv(raw, tail, w, b):
    t = raw.shape[0]
    assert w.shape[0] == 4
    ext = jnp.concatenate([tail, raw], axis=0)
    z1 = pltpu.roll(ext, 1, 0)
    near = ext * w[3:4] + z1 * w[2:3]
    far = ext * w[1:2] + z1 * w[0:1]
    y = near + pltpu.roll(far, 2, 0)
    return y[TAIL_ROWS:] + b, ext[t:t + TAIL_ROWS]


def _cumsum_rows(x):
    n = x.shape[0]
    row = lax.broadcasted_iota(jnp.int32, x.shape, 0)
    d = 1
    while d < n:
        x = x + jnp.where(row >= d, pltpu.roll(x, d, 0), 0.0)
        d *= 2
    return x


def _ssd_kernel(z_ref, xs_ref, bc_ref, dt_ref, buf_ref, h0_ref, cw_ref, cb_ref, dtb_ref, alog_ref,
                dsk_ref, ng_ref, exp_ref, y_ref, st_ref, tail_scr, *, q_valid, q):
    c = pl.program_id(1)

    @pl.when(c == 0)
    def _():
        tail_scr[...] = buf_ref[0]
        st_ref[0] = h0_ref[0]

    def padded(v):
        if q_valid == q:
            return v
        return jnp.concatenate([v, jnp.zeros((q - q_valid, v.shape[1]), v.dtype)], axis=0)

    def transposed(v):
        if q == LANES:
            return v.T
        return jnp.concatenate([v, jnp.zeros((LANES - q, LANES), v.dtype)], axis=0).T[:, :q]

    cw = cw_ref[...]
    cb = cb_ref[...]
    xs, tail_x = _dwconv(xs_ref[0], tail_scr[:, :SSM_INNER], cw[:, :SSM_INNER], cb[:, :SSM_INNER])
    bcm, tail_b = _dwconv(bc_ref[0], tail_scr[:, SSM_INNER:], cw[:, SSM_INNER:], cb[:, SSM_INNER:])
    tail_scr[:, :SSM_INNER] = tail_x
    tail_scr[:, SSM_INNER:] = tail_b
    xs = padded(_silu(xs))
    bcm = padded(_silu(bcm))
    z = padded(z_ref[0])

    dt = _softplus(dt_ref[0] + dtb_ref[...])
    dt = padded(dt)
    a = -jnp.exp(alog_ref[...])
    acs = _cumsum_rows(dt * a)
    acs_t = transposed(acs)

    full = jnp.dot(jnp.concatenate(_split3(dt) + _split3(acs), axis=0), exp_ref[...], preferred_element_type=F32)
    dt_full = (full[0:q] + full[q:2 * q]) + full[2 * q:3 * q]
    acs_full = (full[3 * q:4 * q] + full[4 * q:5 * q]) + full[5 * q:6 * q]
    xdt = xs * dt_full
    dec_out = jnp.exp(acs_full)
    xw = (xdt * jnp.exp(acs_full[q - 1:q] - acs_full)).astype(BF16)
    xdt_b = xdt.astype(BF16)
    bc_b = bcm.astype(BF16)

    row = lax.broadcasted_iota(jnp.int32, (q, q), 0)
    col = lax.broadcasted_iota(jnp.int32, (q, q), 1)
    causal = row >= col
    lane = lax.broadcasted_iota(jnp.int32, (q, LANES), 1)
    lo_half = lane < SSM_HEAD_DIM
    nt = (((1,), (1,)), ((), ()))
    tn = (((0,), (0,)), ((), ()))
    n_bc = SSM_GROUPS * SSM_STATE

    for g in range(SSM_GROUPS):
        b_g = bc_b[:, g * SSM_STATE:(g + 1) * SSM_STATE]
        c_g = bc_b[:, n_bc + g * SSM_STATE:n_bc + (g + 1) * SSM_STATE]
        cbm = lax.dot_general(c_g, b_g, nt, preferred_element_type=F32)
        g0 = g * GROUP_WIDTH
        h_g = st_ref[0, g0:g0 + GROUP_WIDTH, :]
        y_off = lax.dot_general(c_g, h_g.astype(BF16), nt, preferred_element_type=F32)
        y_g = y_off * dec_out[:, g0:g0 + GROUP_WIDTH]
        pieces = []
        for pair in range(HEADS_PER_GROUP // 2):
            x_pair = xdt_b[:, g0 + pair * LANES:g0 + (pair + 1) * LANES]
            acc = None
            for e in range(2):
                h = g * HEADS_PER_GROUP + pair * 2 + e
                seg = jnp.exp(jnp.where(causal, acs[:, h:h + 1] - acs_t[h:h + 1, :], -jnp.inf))
                m = (cbm * seg).astype(BF16)
                x_h = jnp.where(lo_half if e == 0 else jnp.logical_not(lo_half), x_pair, jnp.zeros_like(x_pair))
                part = jnp.dot(m, x_h, preferred_element_type=F32)
                acc = part if acc is None else acc + part
            pieces.append(acc)
        y_g = y_g + jnp.concatenate(pieces, axis=1)
        y_g = y_g + xs[:, g0:g0 + GROUP_WIDTH] * dsk_ref[:, g0:g0 + GROUP_WIDTH]
        y_g = y_g * _silu(z[:, g0:g0 + GROUP_WIDTH])
        y_g = y_g * lax.rsqrt(jnp.mean(y_g * y_g, axis=-1, keepdims=True) + EPS)
        y_g = y_g * ng_ref[:, g0:g0 + GROUP_WIDTH]
        y_ref[0, :, g0:g0 + GROUP_WIDTH] = y_g[:q_valid].astype(y_ref.dtype)

        st_new = lax.dot_general(xw[:, g0:g0 + GROUP_WIDTH], b_g, tn, preferred_element_type=F32)
        for r in range(HEADS_PER_GROUP):
            h = g * HEADS_PER_GROUP + r
            decay = jnp.exp(acs_t[h:h + 1, q - 1:q])
            r0 = g0 + r * SSM_HEAD_DIM
            st_ref[0, r0:r0 + SSM_HEAD_DIM, :] = (
                h_g[r * SSM_HEAD_DIM:(r + 1) * SSM_HEAD_DIM] * decay
                + st_new[r * SSM_HEAD_DIM:(r + 1) * SSM_HEAD_DIM])


def _ssd(proj, dtp, buf8, h0, conv_w, conv_b, dt_bias, a_log, d_full, norm_g, expand, rows):
    b_, l, _ = proj.shape
    nc = l // rows
    st_rows = SSM_INNER
    return pl.pallas_call(
        functools.partial(_ssd_kernel, q_valid=rows, q=max(rows, BF16_SUBLANES)),
        grid=(b_, nc),
        in_specs=[pl.BlockSpec((1, rows, SSM_INNER), lambda b, c: (b, c, 0)),
                  pl.BlockSpec((1, rows, SSM_INNER), lambda b, c: (b, c, 1)),
                  pl.BlockSpec((1, rows, SSM_INNER), lambda b, c: (b, c, 2)),
                  pl.BlockSpec((1, rows, LANES), lambda b, c: (b, c, 0)),
                  pl.BlockSpec((1, TAIL_ROWS, SSM_CONV_DIM), lambda b, c: (b, 0, 0)),
                  pl.BlockSpec((1, st_rows, SSM_STATE), lambda b, c: (b, 0, 0)),
                  pl.BlockSpec((SSM_CONV, SSM_CONV_DIM), lambda b, c: (0, 0)),
                  pl.BlockSpec((1, SSM_CONV_DIM), lambda b, c: (0, 0)),
                  pl.BlockSpec((1, LANES), lambda b, c: (0, 0)),
                  pl.BlockSpec((1, LANES), lambda b, c: (0, 0)),
                  pl.BlockSpec((1, SSM_INNER), lambda b, c: (0, 0)),
                  pl.BlockSpec((1, SSM_INNER), lambda b, c: (0, 0)),
                  pl.BlockSpec((LANES, SSM_INNER), lambda b, c: (0, 0))],
        out_specs=[pl.BlockSpec((1, rows, SSM_INNER), lambda b, c: (b, c, 0)),
                   pl.BlockSpec((1, st_rows, SSM_STATE), lambda b, c: (b, 0, 0))],
        out_shape=[jax.ShapeDtypeStruct((b_, l, SSM_INNER), BF16),
                   jax.ShapeDtypeStruct((b_, st_rows, SSM_STATE), F32)],
        scratch_shapes=[pltpu.VMEM((TAIL_ROWS, SSM_CONV_DIM), F32)],
        compiler_params=_cparams(("parallel", "arbitrary")),
        name="ssd",
    )(proj, proj, proj, dtp, buf8, h0, conv_w, conv_b, dt_bias, a_log, d_full, norm_g, expand)


def _lru_kernel(gate_ref, xr_ref, buf_ref, h0_ref, cw_ref, cb_ref, wa_ref, ba_ref, wx_ref, bx_ref,
                lam_ref, y_ref, hl_ref, tail_scr, a_scr, u_scr):
    t = xr_ref.shape[1]
    c = pl.program_id(1)

    @pl.when(c == 0)
    def _():
        tail_scr[...] = buf_ref[0]
        hl_ref[0] = h0_ref[0]

    xc, tail = _dwconv(xr_ref[0], tail_scr[...], cw_ref[...], cb_ref[...])
    tail_scr[...] = tail
    sp = _softplus(-lam_ref[...])
    for k in range(LRU_BLOCKS):
        k0 = k * LRU_BLOCK
        xk = xc[:, k0:k0 + LRU_BLOCK]
        xkb = xk.astype(BF16)
        gr = _sigmoid(jnp.dot(xkb, wa_ref[k], preferred_element_type=F32) + ba_ref[:, k0:k0 + LRU_BLOCK])
        gi = _sigmoid(jnp.dot(xkb, wx_ref[k], preferred_element_type=F32) + bx_ref[:, k0:k0 + LRU_BLOCK])
        log_a = -LRU_C * gr * sp[:, k0:k0 + LRU_BLOCK]
        a_scr[:, k0:k0 + LRU_BLOCK] = jnp.exp(log_a)
        u_scr[:, k0:k0 + LRU_BLOCK] = _sqrt_neg_expm1(2.0 * log_a) * gi * xk

    row = lax.broadcasted_iota(jnp.int32, (SUBLANES, LRU_WIDTH), 0)

    def tile(i, h_prev):
        r0 = pl.multiple_of(i * SUBLANES, SUBLANES)
        a = a_scr[pl.ds(r0, SUBLANES), :]
        u = u_scr[pl.ds(r0, SUBLANES), :]
        d = 1
        while d < SUBLANES:
            keep = row >= d
            u = u + a * jnp.where(keep, pltpu.roll(u, d, 0), 0.0)
            a = a * jnp.where(keep, pltpu.roll(a, d, 0), 1.0)
            d *= 2
        h = u + a * h_prev
        u_scr[pl.ds(r0, SUBLANES), :] = h
        return h[SUBLANES - 1:SUBLANES, :]

    h_last = lax.fori_loop(0, t // SUBLANES, tile, hl_ref[0])
    hl_ref[0] = h_last
    y_ref[0] = (u_scr[...] * jax.nn.gelu(gate_ref[0])).astype(y_ref.dtype)


def _lru(proj, buf8, h0, conv_w, conv_b, w_a, b_a, w_x, b_x, lam, rows, col0):
    b_, l, _ = proj.shape
    w = LRU_WIDTH
    full2 = lambda shape: pl.BlockSpec(shape, lambda b, c: (0,) * len(shape))
    return pl.pallas_call(
        _lru_kernel,
        grid=(b_, l // rows),
        in_specs=[pl.BlockSpec((1, rows, w), lambda b, c: (b, c, col0)),
                  pl.BlockSpec((1, rows, w), lambda b, c: (b, c, col0 + 1)),
                  pl.BlockSpec((1, TAIL_ROWS, w), lambda b, c: (b, 0, 0)),
                  pl.BlockSpec((1, 1, w), lambda b, c: (b, 0, 0)),
                  full2((SSM_CONV, w)), full2((1, w)),
                  full2((LRU_BLOCKS, LRU_BLOCK, LRU_BLOCK)), full2((1, w)),
                  full2((LRU_BLOCKS, LRU_BLOCK, LRU_BLOCK)), full2((1, w)),
                  full2((1, w))],
        out_specs=[pl.BlockSpec((1, rows, w), lambda b, c: (b, c, 0)),
                   pl.BlockSpec((1, 1, w), lambda b, c: (b, 0, 0))],
        out_shape=[jax.ShapeDtypeStruct((b_, l, w), BF16),
                   jax.ShapeDtypeStruct((b_, 1, w), F32)],
        scratch_shapes=[pltpu.VMEM((TAIL_ROWS, w), F32), pltpu.VMEM((rows, w), F32),
                        pltpu.VMEM((rows, w), F32)],
        compiler_params=_cparams(("parallel", "arbitrary")),
        name="lru",
    )(proj, proj, buf8, h0, conv_w, conv_b, w_a, b_a, w_x, b_x, lam)


def _head_variants(slab, half):
    lane = lax.broadcasted_iota(jnp.int32, slab.shape, 1)
    keep = (lane < HEAD_DIM) if half == 0 else (lane >= HEAD_DIM)
    own = jnp.where(keep, slab, 0.0)
    other = pltpu.roll(own, HEAD_DIM, 1)
    return (own, other) if half == 0 else (other, own)


def _slab_attention(q_ref, o_ref, k2, v2, valid, sink_ref):
    w = q_ref.shape[1]
    nk = k2.shape[0]
    nt = (((1,), (1,)), ((), ()))
    scale = HEAD_DIM ** -0.5
    heads = range(KV_HEADS)
    halves = [(e, c) for e in heads for c in range(2)]
    vvs, ss = [], []
    for e in heads:
        slab, half = divmod(e, 2)
        k_lo, k_hi = _head_variants(k2[:, slab * LANES:(slab + 1) * LANES], half)
        v_lo, v_hi = _head_variants(v2[:, slab * LANES:(slab + 1) * LANES], half)
        kk = jnp.concatenate([k_lo, k_hi], axis=0).astype(BF16)
        vvs.append(jnp.concatenate([v_lo, v_hi], axis=0).astype(BF16))
        c0 = 2 * e * LANES
        qq = jnp.concatenate([q_ref[0, :, c0:c0 + LANES], q_ref[0, :, c0 + LANES:c0 + 2 * LANES]],
                             axis=0).astype(BF16)
        ss.append(lax.dot_general(qq, kk, nt, preferred_element_type=F32) * scale)
    scs = [jnp.where(valid, ss[e][:, c * nk:(c + 1) * nk], -jnp.inf) for e, c in halves]
    sinks = [jnp.concatenate([jnp.full((w, 1), sink_ref[Q_PER_KV * e + c], F32),
                              jnp.full((w, 1), sink_ref[Q_PER_KV * e + 2 + c], F32)], axis=0) for e, c in halves]
    ms = [jnp.maximum(jnp.max(sc, axis=1, keepdims=True), sk) for sc, sk in zip(scs, sinks)]
    ps = [jnp.exp(sc - m) for sc, m in zip(scs, ms)]
    dens = [jnp.sum(p, axis=1, keepdims=True) + jnp.exp(sk - m) for p, sk, m in zip(ps, sinks, ms)]
    pn = [(p * (1.0 / den)).astype(BF16) for p, den in zip(ps, dens)]
    for e in heads:
        o = jnp.dot(jnp.concatenate(pn[2 * e:2 * e + 2], axis=1), vvs[e], preferred_element_type=F32)
        c0 = 2 * e * LANES
        o_ref[0, :, c0:c0 + LANES] = o[:w].astype(o_ref.dtype)
        o_ref[0, :, c0 + LANES:c0 + 2 * LANES] = o[w:].astype(o_ref.dtype)


def _attn_prompt_kernel(sink_ref, q_ref, kc_ref, kp_ref, vc_ref, vp_ref, o_ref):
    w = WINDOW
    has_prev = pl.program_id(1) > 0
    k2 = jnp.concatenate([jnp.where(has_prev, kp_ref[0], 0.0), kc_ref[0]], axis=0)
    v2 = jnp.concatenate([jnp.where(has_prev, vp_ref[0], 0.0), vc_ref[0]], axis=0)
    t = lax.broadcasted_iota(jnp.int32, (2 * w, 2 * w), 0) % w
    j = lax.broadcasted_iota(jnp.int32, (2 * w, 2 * w), 1)
    _slab_attention(q_ref, o_ref, k2, v2, (j > t) & (j <= t + w), sink_ref)


def _attn_prompt(qkv, sinks):
    b_, l, _ = qkv.shape
    w = WINDOW
    qw = ATT_HEADS * HEAD_DIM
    kvw = KV_HEADS * HEAD_DIM
    k_blk = qw // kvw
    prev = lambda n: jnp.maximum(n - 1, 0)
    return pl.pallas_call(
        _attn_prompt_kernel,
        grid=(b_, l // w),
        in_specs=[pl.BlockSpec(memory_space=pltpu.SMEM),
                  pl.BlockSpec((1, w, qw), lambda b, n: (b, n, 0)),
                  pl.BlockSpec((1, w, kvw), lambda b, n: (b, n, k_blk)),
                  pl.BlockSpec((1, w, kvw), lambda b, n: (b, prev(n), k_blk)),
                  pl.BlockSpec((1, w, kvw), lambda b, n: (b, n, k_blk + 1)),
                  pl.BlockSpec((1, w, kvw), lambda b, n: (b, prev(n), k_blk + 1))],
        out_specs=pl.BlockSpec((1, w, qw), lambda b, n: (b, n, 0)),
        out_shape=jax.ShapeDtypeStruct((b_, l, qw), BF16),
        compiler_params=_cparams(("parallel", "parallel")),
        name="attn_prompt",
    )(sinks, qkv, qkv, qkv, qkv, qkv)


def _attn_sample_kernel(sink_ref, qkv_ref, ck_ref, cv_ref, o_ref, nk_ref, nv_ref):
    l = qkv_ref.shape[1]
    n_keep = ck_ref.shape[1]
    kcol = ATT_HEADS * HEAD_DIM
    vcol = kcol + KV_HEADS * HEAD_DIM
    qkv = qkv_ref[0]
    k_all = jnp.concatenate([ck_ref[0], qkv[:, kcol:vcol]], axis=0)
    v_all = jnp.concatenate([cv_ref[0], qkv[:, vcol:]], axis=0)
    nk_ref[0] = k_all[l:]
    nv_ref[0] = v_all[l:]
    nk = -(-(n_keep + l) // LANES) * LANES
    pad = jnp.zeros((nk - n_keep - l, k_all.shape[1]), F32)
    t = lax.broadcasted_iota(jnp.int32, (2 * l, nk), 0) % l
    j = lax.broadcasted_iota(jnp.int32, (2 * l, nk), 1)
    rel = jnp.where(j < n_keep, t + n_keep - j, t - (j - n_keep))
    valid = (rel >= 0) & (rel < WINDOW) & (j < n_keep + l)
    _slab_attention(qkv_ref, o_ref, jnp.concatenate([k_all, pad], axis=0), jnp.concatenate([v_all, pad], axis=0),
                    valid, sink_ref)


def _attn_sample(qkv, cache_k, cache_v, sinks):
    b_, l, n = qkv.shape
    n_keep, kvw = cache_k.shape[1:]
    return pl.pallas_call(
        _attn_sample_kernel,
        grid=(b_,),
        in_specs=[pl.BlockSpec(memory_space=pltpu.SMEM),
                  pl.BlockSpec((1, l, n), lambda b: (b, 0, 0)),
                  pl.BlockSpec((1, n_keep, kvw), lambda b: (b, 0, 0)),
                  pl.BlockSpec((1, n_keep, kvw), lambda b: (b, 0, 0))],
        out_specs=[pl.BlockSpec((1, l, ATT_HEADS * HEAD_DIM), lambda b: (b, 0, 0)),
                   pl.BlockSpec((1, n_keep, kvw), lambda b: (b, 0, 0)),
                   pl.BlockSpec((1, n_keep, kvw), lambda b: (b, 0, 0))],
        out_shape=[jax.ShapeDtypeStruct((b_, l, ATT_HEADS * HEAD_DIM), BF16),
                   jax.ShapeDtypeStruct((b_, n_keep, kvw), F32),
                   jax.ShapeDtypeStruct((b_, n_keep, kvw), F32)],
        compiler_params=_cparams(("parallel",)),
        name="attn_sample",
    )(sinks, qkv, cache_k, cache_v)


def _rope_tables(pos):
    half = HEAD_DIM // 2
    inv_freq = ROPE_THETA ** (-jnp.arange(half, dtype=F32) / half)
    ang = pos.astype(F32)[:, None] * inv_freq[None, :]
    cos = jnp.cos(ang)
    sin = jnp.sin(ang)
    reps = LANES // HEAD_DIM
    return (jnp.tile(jnp.concatenate([cos, cos], axis=1), (1, reps)),
            jnp.tile(jnp.concatenate([-sin, sin], axis=1), (1, reps)))


def _tail_pad(buf):
    return jnp.pad(buf, ((0, 0), (TAIL_ROWS - buf.shape[1], 0), (0, 0)))


def _trunk(x, mods, pos, st, pw, seq_len):
    g_, m, d = x.shape
    nseq = g_ * m // seq_len
    blk = _block_rows(m)
    chunk = min(seq_len, SSM_CHUNK)
    lru_rows = min(seq_len, LRU_ROWS)

    proj, dtp = _inproj(x, pw['norms'][0, 0:1], mods[0], pw['w_in'], SSM_INNER + SSM_CONV_DIM, pw['w_in_b'],
                        pw['w_in_dt'], blk['inproj'], INPROJ_COLS)
    n_main = proj.shape[-1]
    proj_s = proj.reshape(nseq, seq_len, n_main)
    dtp_s = dtp.reshape(nseq, seq_len, LANES)
    y_ssm, ssm_h = _ssd(proj_s, dtp_s, _tail_pad(st['ssm_conv']), st['ssm'].reshape(nseq, SSM_INNER, SSM_STATE),
                        pw['ssm_conv_w'], pw['ssm_conv_b'], pw['ssm_dt_bias'], pw['ssm_a_log'], pw['ssm_d_full'],
                        pw['ssm_norm'], pw['expand'], chunk)
    y_lru, lru_h = _lru(proj_s, _tail_pad(st['lru_conv']), st['lru'].reshape(nseq, 1, LRU_WIDTH),
                        pw['lru_conv_w'], pw['lru_conv_b'], pw['lru_w_a'], pw['lru_b_a'], pw['lru_w_x'],
                        pw['lru_b_x'], pw['lru_lambda'], lru_rows, 3)
    keep = SSM_CONV - 1
    new_ssm_conv = proj_s[:, seq_len - keep:, SSM_INNER:SSM_INNER + SSM_CONV_DIM]
    new_lru_conv = proj_s[:, seq_len - keep:, n_main - LRU_WIDTH:]
    x = _outproj([y_ssm.reshape(g_, m, SSM_INNER), y_lru.reshape(g_, m, LRU_WIDTH)],
                 pw['w_out_hyb'], x, pw['norms'][0, 1:2], mods[0], blk['outproj'])
    x = _mlp(x, pw['norms'][0, 2:3], pw['norms'][0, 3:4], mods[0], pw['w_up'], pw['w_down'], 0, blk['mlp'],
             MLP_COLS)

    cos_t, sin_t = _rope_tables(pos)
    if seq_len < blk['qkv']:
        cos_t = jnp.tile(cos_t, (blk['qkv'] // seq_len, 1))
        sin_t = jnp.tile(sin_t, (blk['qkv'] // seq_len, 1))
    qkv = _qkv(x, pw['norms'][1, 0:1], mods[1], pw['w_qkv'], pw['b_qkv'], cos_t, sin_t, blk['qkv'], QKV_COLS)
    qkv_s = qkv.reshape(nseq, seq_len, qkv.shape[-1])
    kcol = ATT_HEADS * HEAD_DIM
    vcol = kcol + KV_HEADS * HEAD_DIM
    if st['k'] is None:
        o = _attn_prompt(qkv_s, pw['attn_sinks'])
        n_keep = min(WINDOW, seq_len)
        k_new = qkv_s[:, seq_len - n_keep:, kcol:vcol]
        v_new = qkv_s[:, seq_len - n_keep:, vcol:]
    else:
        n_keep = st['k'].shape[1]
        o, k_new, v_new = _attn_sample(qkv_s, st['k'].reshape(nseq, n_keep, KV_HEADS * HEAD_DIM),
                                       st['v'].reshape(nseq, n_keep, KV_HEADS * HEAD_DIM), pw['attn_sinks'])
    k_new = k_new.reshape(nseq, n_keep, KV_HEADS, HEAD_DIM)
    v_new = v_new.reshape(nseq, n_keep, KV_HEADS, HEAD_DIM)
    x = _outproj([o.reshape(g_, m, kcol)], pw['w_out_attn'], x, pw['norms'][1, 1:2], mods[1], blk['outproj'])
    x = _mlp(x, pw['norms'][1, 2:3], pw['norms'][1, 3:4], mods[1], pw['w_up'], pw['w_down'], 1, blk['mlp'],
             MLP_COLS)

    states = (new_ssm_conv[None], ssm_h.reshape(1, nseq, SSM_HEADS, SSM_HEAD_DIM, SSM_STATE),
              new_lru_conv[None], lru_h.reshape(1, nseq, LRU_WIDTH), k_new[None], v_new[None])
    return x, states


def kernel(x_prompt, x_sample, state_ssm_conv, state_ssm, state_lru_conv, state_lru, cache_k, cache_v, c_prompt, c_sample, w_mod, b_mod, norms, w_in_hyb, ssm_conv_w, ssm_conv_b, ssm_dt_bias, ssm_a_log, ssm_d, ssm_norm, lru_conv_w, lru_conv_b, lru_w_a, lru_b_a, lru_w_x, lru_b_x, lru_lambda, w_out_hyb, w_qkv, b_qkv, attn_sinks, w_out_attn, w_up, w_down):
    bp, lp, d = x_prompt.shape
    bs, ls, _ = x_sample.shape
    depth = w_mod.shape[0]

    dt0 = SSM_INNER + SSM_CONV_DIM
    w_in_b, w_in_dt = _split_tail_columns(w_in_hyb, dt0, SSM_HEADS)
    pad_lanes = lambda v: jnp.pad(v.reshape(1, -1), ((0, 0), (0, LANES - v.size)))
    head_of_lane = jnp.arange(SSM_INNER) // SSM_HEAD_DIM
    pw = {
        'norms': norms,
        'w_in': w_in_hyb[0], 'w_in_b': w_in_b, 'w_in_dt': w_in_dt,
        'ssm_conv_w': ssm_conv_w[0], 'ssm_conv_b': ssm_conv_b[0].reshape(1, -1),
        'ssm_dt_bias': pad_lanes(ssm_dt_bias[0]), 'ssm_a_log': pad_lanes(ssm_a_log[0]),
        'ssm_d_full': jnp.repeat(ssm_d[0], SSM_HEAD_DIM).reshape(1, -1),
        'ssm_norm': ssm_norm[0].reshape(1, -1),
        'expand': (jnp.arange(LANES)[:, None] == head_of_lane[None, :]).astype(BF16),
        'lru_conv_w': lru_conv_w[0], 'lru_conv_b': lru_conv_b[0].reshape(1, -1),
        'lru_w_a': lru_w_a[0].astype(BF16), 'lru_b_a': lru_b_a[0].reshape(1, -1),
        'lru_w_x': lru_w_x[0].astype(BF16), 'lru_b_x': lru_b_x[0].reshape(1, -1),
        'lru_lambda': lru_lambda[0].reshape(1, -1),
        'w_out_hyb': w_out_hyb[0].astype(BF16),
        'w_qkv': w_qkv[0].astype(BF16), 'b_qkv': b_qkv[0].reshape(1, -1),
        'attn_sinks': attn_sinks[0],
        'w_out_attn': w_out_attn[0].astype(BF16),
        'w_up': w_up, 'w_down': w_down,
    }

    mod = _adaln(jnp.concatenate([c_prompt, c_sample], axis=0), w_mod, b_mod)
    mods_p = [mod[l, :bp][:, None, :] for l in range(depth)]
    mods_s = [jnp.repeat(mod[l, bp:], ls, axis=0)[None] for l in range(depth)]

    zeros = lambda *shape: jnp.zeros(shape, F32)
    st_p = {'ssm_conv': zeros(bp, SSM_CONV - 1, SSM_CONV_DIM), 'ssm': zeros(bp, SSM_INNER, SSM_STATE),
            'lru_conv': zeros(bp, SSM_CONV - 1, LRU_WIDTH), 'lru': zeros(bp, LRU_WIDTH), 'k': None, 'v': None}
    st_s = {'ssm_conv': state_ssm_conv[0], 'ssm': state_ssm[0], 'lru_conv': state_lru_conv[0],
            'lru': state_lru[0], 'k': cache_k[0], 'v': cache_v[0]}

    pos_p = jnp.arange(lp, dtype=jnp.int32)
    pos_s = PAST_LEN + jnp.arange(ls, dtype=jnp.int32)
    y_p, sp = _trunk(x_prompt, mods_p, pos_p, st_p, pw, lp)
    y_s, ss = _trunk(x_sample.reshape(1, bs * ls, d), mods_s, pos_s, st_s, pw, ls)
    return (y_p, y_s.reshape(bs, ls, d)) + sp + ss
```

```python
import functools
import math

import jax
import jax.numpy as jnp
from jax import lax
from jax.experimental import pallas as pl
from jax.experimental.pallas import tpu as pltpu

F32 = jnp.float32
BF16 = jnp.bfloat16

D_MODEL = 2048
PAST_LEN = 16384
SSM_HEADS = 32
SSM_HEAD_DIM = 64
SSM_INNER = SSM_HEADS * SSM_HEAD_DIM
SSM_GROUPS = 8
SSM_STATE = 128
SSM_CONV = 4
SSM_CHUNK = 128
SSM_CONV_DIM = SSM_INNER + 2 * SSM_GROUPS * SSM_STATE
HEADS_PER_GROUP = SSM_HEADS // SSM_GROUPS
GROUP_WIDTH = HEADS_PER_GROUP * SSM_HEAD_DIM
LRU_WIDTH = D_MODEL
LRU_BLOCKS = 8
LRU_BLOCK = LRU_WIDTH // LRU_BLOCKS
LRU_C = 8.0
ATT_HEADS = 32
KV_HEADS = 8
HEAD_DIM = 64
Q_PER_KV = ATT_HEADS // KV_HEADS
WINDOW = 128
ROPE_THETA = 10000.0
D_FF = 4 * D_MODEL
EPS = 1e-6

LANES = 128
SUBLANES = 8
BF16_SUBLANES = 16
TAIL_ROWS = SUBLANES
VMEM_LIMIT = 56 * 1024 * 1024


INPROJ_COLS = 1024
MLP_COLS = 512
QKV_COLS = 512
LRU_ROWS = 256


def _block_rows(m):
    return {'inproj': min(m, 1024), 'mlp': min(m, 1024), 'qkv': min(m, 512), 'outproj': min(m, 512)}


def _cparams(sem):
    return pltpu.CompilerParams(dimension_semantics=sem, vmem_limit_bytes=VMEM_LIMIT)


def _rms(x, g):
    return x * lax.rsqrt(jnp.mean(x * x, axis=-1, keepdims=True) + EPS) * g


ROW_CHUNK = 16
ROW_UNROLL = 4


def _for_row_chunks(n_rows, body):
    def step(i, carry):
        body(pl.multiple_of(i * ROW_CHUNK, ROW_CHUNK))
        return carry

    lax.fori_loop(0, n_rows // ROW_CHUNK, step, 0, unroll=ROW_UNROLL)


def _mod_rows(ref, r0):
    return ref[0] if ref.shape[1] == 1 else ref[0, pl.ds(r0, ROW_CHUNK), :]


def _prenorm_to(h_scr, x_ref, g_ref, sc_ref, sh_ref):
    def body(r0):
        x = x_ref[0, pl.ds(r0, ROW_CHUNK), :]
        h = _rms(x, g_ref[...]) * (1.0 + _mod_rows(sc_ref, r0)) + _mod_rows(sh_ref, r0)
        h_scr[pl.ds(r0, ROW_CHUNK), :] = h.astype(BF16)

    _for_row_chunks(x_ref.shape[1], body)


def _sigmoid(x):
    return 0.5 * (jnp.tanh(0.5 * x) + 1.0)


def _silu(x):
    return x * _sigmoid(x)


def _softplus(x):
    return jnp.maximum(x, 0.0) + jnp.log1p(jnp.exp(-jnp.abs(x)))


def _sqrt_neg_expm1(x):
    s = -jnp.tanh(0.5 * x)
    return jnp.sqrt(2.0 * s) * lax.rsqrt(1.0 + s)


def _split3(x):
    hi = x.astype(BF16)
    r = x - hi.astype(F32)
    mid = r.astype(BF16)
    lo = (r - mid.astype(F32)).astype(BF16)
    return [hi, mid, lo]


def _adaln_kernel(c_ref, w_ref, b_ref, o_ref):
    s = _silu(c_ref[...]).astype(BF16)
    o_ref[0] = jnp.dot(s, w_ref[0].astype(BF16), preferred_element_type=F32) + b_ref[0]


def _adaln(c_all, w_mod, b_mod):
    nb, d = c_all.shape
    depth, _, n = w_mod.shape
    bn = 1024
    return pl.pallas_call(
        _adaln_kernel,
        grid=(depth, n // bn),
        in_specs=[pl.BlockSpec((nb, d), lambda l, j: (0, 0)),
                  pl.BlockSpec((1, d, bn), lambda l, j: (l, 0, j)),
                  pl.BlockSpec((1, 1, bn), lambda l, j: (l, 0, j))],
        out_specs=pl.BlockSpec((1, nb, bn), lambda l, j: (l, 0, j)),
        out_shape=jax.ShapeDtypeStruct((depth, nb, n), F32),
        compiler_params=_cparams(("parallel", "parallel")),
        name="adaln",
    )(c_all, w_mod, b_mod.reshape(depth, 1, n))


def _mod_spec(mod, bm, k):
    if mod.shape[1] == 1:
        return pl.BlockSpec((1, 1, D_MODEL), lambda g, i, *_: (g, 0, k))
    return pl.BlockSpec((1, bm, D_MODEL), lambda g, i, *_: (g, i, k))


def _inproj_kernel(x_ref, g_ref, sh_ref, sc_ref, wt_ref, wdt_ref, o_ref, dt_ref, h_scr, *, n_dt):
    nt = (((1,), (1,)), ((), ()))

    @pl.when(pl.program_id(2) == 0)
    def _():
        _prenorm_to(h_scr, x_ref, g_ref, sc_ref, sh_ref)
        dt = lax.dot_general(h_scr[...], wdt_ref[...].astype(BF16), nt, preferred_element_type=F32)
        lane = lax.broadcasted_iota(jnp.int32, dt.shape, 1)
        dt_ref[0] = jnp.where(lane < n_dt, dt, 0.0)

    o_ref[0] = lax.dot_general(h_scr[...], wt_ref[...].astype(BF16), nt, preferred_element_type=F32)


def _inproj(x, gain, mod, w_t, c_dt, n_dt, bm, bn):
    g_, m, d = x.shape
    n = w_t.shape[1] - n_dt
    row0 = lambda j: pl.multiple_of(jnp.where(j * bn < c_dt, j * bn, j * bn + n_dt), SUBLANES)
    return pl.pallas_call(
        functools.partial(_inproj_kernel, n_dt=n_dt),
        grid=(g_, m // bm, n // bn),
        in_specs=[pl.BlockSpec((1, bm, d), lambda g, i, j: (g, i, 0), pipeline_mode=pl.Buffered(1)),
                  pl.BlockSpec((1, d), lambda g, i, j: (0, 0)),
                  _mod_spec(mod, bm, 0), _mod_spec(mod, bm, 1),
                  pl.BlockSpec((None, pl.Element(bn), pl.Element(d)), lambda g, i, j: (0, row0(j), 0)),
                  pl.BlockSpec((None, pl.Element(LANES), pl.Element(d)), lambda g, i, j: (0, c_dt, 0))],
        out_specs=[pl.BlockSpec((1, bm, bn), lambda g, i, j: (g, i, j)),
                   pl.BlockSpec((1, bm, LANES), lambda g, i, j: (g, i, 0))],
        out_shape=[jax.ShapeDtypeStruct((g_, m, n), F32),
                   jax.ShapeDtypeStruct((g_, m, LANES), F32)],
        scratch_shapes=[pltpu.VMEM((bm, d), BF16)],
        compiler_params=_cparams(("parallel", "parallel", "arbitrary")),
        name="inproj",
    )(x, gain, mod, mod, w_t, w_t)


def _swap_halves(x):
    lane = lax.broadcasted_iota(jnp.int32, x.shape, 1)
    first = (lane % HEAD_DIM) < (HEAD_DIM // 2)
    return jnp.where(first, pltpu.roll(x, LANES - HEAD_DIM // 2, 1), pltpu.roll(x, HEAD_DIM // 2, 1))


def _qkv_kernel(x_ref, g_ref, sh_ref, sc_ref, w_ref, b_ref, cos_ref, sin_ref, o_ref, h_scr, *, rope_cols, bn):
    _prenorm_to(h_scr, x_ref, g_ref, sc_ref, sh_ref)
    h = h_scr[...]
    cos = cos_ref[...]
    sin = sin_ref[...]
    for c0 in range(0, w_ref.shape[1], bn):
        acc = jnp.dot(h, w_ref[:, c0:c0 + bn], preferred_element_type=F32) + b_ref[:, c0:c0 + bn]
        if c0 < rope_cols:
            for c in range(c0, c0 + bn, LANES):
                a = acc[:, c - c0:c - c0 + LANES]
                o_ref[0, :, c:c + LANES] = a * cos + _swap_halves(a) * sin
        else:
            o_ref[0, :, c0:c0 + bn] = acc


def _qkv(x, gain, mod, w, b, cos_t, sin_t, bm, bn):
    g_, m, d = x.shape
    n = w.shape[1]
    rope_cols = (ATT_HEADS + KV_HEADS) * HEAD_DIM
    rows_per_g = cos_t.shape[0] // bm
    return pl.pallas_call(
        functools.partial(_qkv_kernel, rope_cols=rope_cols, bn=bn),
        grid=(g_, m // bm),
        in_specs=[pl.BlockSpec((1, bm, d), lambda g, i: (g, i, 0)),
                  pl.BlockSpec((1, d), lambda g, i: (0, 0)),
                  _mod_spec(mod, bm, 0), _mod_spec(mod, bm, 1),
                  pl.BlockSpec((d, n), lambda g, i: (0, 0), pipeline_mode=pl.Buffered(1)),
                  pl.BlockSpec((1, n), lambda g, i: (0, 0)),
                  pl.BlockSpec((bm, LANES), lambda g, i: (i % rows_per_g, 0)),
                  pl.BlockSpec((bm, LANES), lambda g, i: (i % rows_per_g, 0))],
        out_specs=pl.BlockSpec((1, bm, n), lambda g, i: (g, i, 0)),
        out_shape=jax.ShapeDtypeStruct((g_, m, n), F32),
        scratch_shapes=[pltpu.VMEM((bm, d), BF16)],
        compiler_params=_cparams(("parallel", "parallel")),
        name="qkv",
    )(x, gain, mod, mod, w, b, cos_t, sin_t)


def _outproj_kernel(*refs, n_lhs):
    lhs = refs[:n_lhs]
    ws = refs[n_lhs:2 * n_lhs]
    x_ref, g_ref, gt_ref, o_ref = refs[2 * n_lhs:]
    acc = jnp.dot(lhs[0][0], ws[0][...], preferred_element_type=F32)
    for a, w in zip(lhs[1:], ws[1:]):
        acc = acc + jnp.dot(a[0], w[...], preferred_element_type=F32)
    o_ref[0] = x_ref[0] + gt_ref[0] * _rms(acc, g_ref[...])


def _outproj(lhs_list, w, x, gain, mod, bm):
    g_, m, d = x.shape
    n_lhs = len(lhs_list)
    kw = w.shape[0] // n_lhs
    lhs_specs = [pl.BlockSpec((1, bm, kw), lambda g, i: (g, i, 0)) for _ in lhs_list]
    w_specs = [pl.BlockSpec((kw, d), lambda g, i, k=k: (k, 0), pipeline_mode=pl.Buffered(1)) for k in range(n_lhs)]
    w_list = [w] * n_lhs
    if mod.shape[1] == 1:
        gt_spec = pl.BlockSpec((1, 1, d), lambda g, i: (g, 0, 2))
    else:
        gt_spec = pl.BlockSpec((1, bm, d), lambda g, i: (g, i, 2))
    return pl.pallas_call(
        functools.partial(_outproj_kernel, n_lhs=n_lhs),
        grid=(g_, m // bm),
        in_specs=lhs_specs + w_specs + [pl.BlockSpec((1, bm, d), lambda g, i: (g, i, 0)),
                                        pl.BlockSpec((1, d), lambda g, i: (0, 0)), gt_spec],
        out_specs=pl.BlockSpec((1, bm, d), lambda g, i: (g, i, 0)),
        out_shape=jax.ShapeDtypeStruct((g_, m, d), F32),
        compiler_params=_cparams(("parallel", "parallel")),
        name="outproj",
    )(*lhs_list, *w_list, x, gain, mod)


def _mlp_kernel(x_ref, g2_ref, sh_ref, sc_ref, gt_ref, wu_ref, wd_ref, g3_ref, o_ref, h_scr):
    f = pl.program_id(2)

    @pl.when(f == 0)
    def _():
        _prenorm_to(h_scr, x_ref, g2_ref, sc_ref, sh_ref)
        o_ref[0] = jnp.zeros(o_ref.shape[1:], F32)

    u = jnp.dot(h_scr[...], wu_ref[...].astype(BF16), preferred_element_type=F32)
    u = jnp.square(jnp.maximum(u, 0.0)).astype(BF16)
    o_ref[0] += jnp.dot(u, wd_ref[...].astype(BF16), preferred_element_type=F32)

    @pl.when(f == pl.num_programs(2) - 1)
    def _():
        o_ref[0] = x_ref[0] + gt_ref[0] * _rms(o_ref[0], g3_ref[...])


def _mlp(x, g2, g3, mod, w_up, w_down, layer, bm, bf):
    g_, m, d = x.shape
    dff = w_up.shape[2]
    if mod.shape[1] == 1:
        mspec = lambda k: pl.BlockSpec((1, 1, d), lambda g, i, f: (g, 0, k))
    else:
        mspec = lambda k: pl.BlockSpec((1, bm, d), lambda g, i, f: (g, i, k))
    return pl.pallas_call(
        _mlp_kernel,
        grid=(g_, m // bm, dff // bf),
        in_specs=[pl.BlockSpec((1, bm, d), lambda g, i, f: (g, i, 0), pipeline_mode=pl.Buffered(1)),
                  pl.BlockSpec((1, d), lambda g, i, f: (0, 0)),
                  mspec(3), mspec(4), mspec(5),
                  pl.BlockSpec((None, d, bf), lambda g, i, f: (layer, 0, f)),
                  pl.BlockSpec((None, bf, d), lambda g, i, f: (layer, f, 0)),
                  pl.BlockSpec((1, d), lambda g, i, f: (0, 0))],
        out_specs=pl.BlockSpec((1, bm, d), lambda g, i, f: (g, i, 0)),
        out_shape=jax.ShapeDtypeStruct((g_, m, d), F32),
        scratch_shapes=[pltpu.VMEM((bm, d), BF16)],
        compiler_params=_cparams(("parallel", "parallel", "arbitrary")),
        name="mlp",
    )(x, g2, mod, mod, mod, w_up, w_down, g3)


def _dwconv(raw, tail, w, b):
    t = raw.shape[0]
    assert w.shape[0] == 4
    ext = jnp.concatenate([tail, raw], axis=0)
    z1 = pltpu.roll(ext, 1, 0)
    near = ext * w[3:4] + z1 * w[2:3]
    far = ext * w[1:2] + z1 * w[0:1]
    y = near + pltpu.roll(far, 2, 0)
    return y[TAIL_ROWS:] + b, ext[t:t + TAIL_ROWS]


def _cumsum_rows(x):
    n = x.shape[0]
    row = lax.broadcasted_iota(jnp.int32, x.shape, 0)
    d = 1
    while d < n:
        x = x + jnp.where(row >= d, pltpu.roll(x, d, 0), 0.0)
        d *= 2
    return x


def _ssd_kernel(z_ref, xs_ref, bc_ref, dt_ref, buf_ref, h0_ref, cw_ref, cb_ref, dtb_ref, alog_ref,
                dsk_ref, ng_ref, exp_ref, y_ref, st_ref, tail_scr, *, q_valid, q):
    c = pl.program_id(1)

    @pl.when(c == 0)
    def _():
        tail_scr[...] = buf_ref[0]
        st_ref[0] = h0_ref[0]

    def padded(v):
        if q_valid == q:
            return v
        return jnp.concatenate([v, jnp.zeros((q - q_valid, v.shape[1]), v.dtype)], axis=0)

    def transposed(v):
        if q == LANES:
            return v.T
        return jnp.concatenate([v, jnp.zeros((LANES - q, LANES), v.dtype)], axis=0).T[:, :q]

    cw = cw_ref[...]
    cb = cb_ref[...]
    xs, tail_x = _dwconv(xs_ref[0], tail_scr[:, :SSM_INNER], cw[:, :SSM_INNER], cb[:, :SSM_INNER])
    bcm, tail_b = _dwconv(bc_ref[0], tail_scr[:, SSM_INNER:], cw[:, SSM_INNER:], cb[:, SSM_INNER:])
    tail_scr[:, :SSM_INNER] = tail_x
    tail_scr[:, SSM_INNER:] = tail_b
    xs = padded(_silu(xs))
    bcm = padded(_silu(bcm))
    z = padded(z_ref[0])

    dt = _softplus(dt_ref[0] + dtb_ref[...])
    dt = padded(dt)
    a = -jnp.exp(alog_ref[...])
    acs = _cumsum_rows(dt * a)
    acs_t = transposed(acs)

    full = jnp.dot(jnp.concatenate(_split3(dt) + _split3(acs), axis=0), exp_ref[...], preferred_element_type=F32)
    dt_full = (full[0:q] + full[q:2 * q]) + full[2 * q:3 * q]
    acs_full = (full[3 * q:4 * q] + full[4 * q:5 * q]) + full[5 * q:6 * q]
    xdt = xs * dt_full
    dec_out = jnp.exp(acs_full)
    xw = (xdt * jnp.exp(acs_full[q - 1:q] - acs_full)).astype(BF16)
    xdt_b = xdt.astype(BF16)
    bc_b = bcm.astype(BF16)

    row = lax.broadcasted_iota(jnp.int32, (q, q), 0)
    col = lax.broadcasted_iota(jnp.int32, (q, q), 1)
    causal = row >= col
    lane = lax.broadcasted_iota(jnp.int32, (q, LANES), 1)
    lo_half = lane < SSM_HEAD_DIM
    nt = (((1,), (1,)), ((), ()))
    tn = (((0,), (0,)), ((), ()))
    n_bc = SSM_GROUPS * SSM_STATE

    for g in range(SSM_GROUPS):
        b_g = bc_b[:, g * SSM_STATE:(g + 1) * SSM_STATE]
        c_g = bc_b[:, n_bc + g * SSM_STATE:n_bc + (g + 1) * SSM_STATE]
        cbm = lax.dot_general(c_g, b_g, nt, preferred_element_type=F32)
        g0 = g * GROUP_WIDTH
        h_g = st_ref[0, g0:g0 + GROUP_WIDTH, :]
        y_off = lax.dot_general(c_g, h_g.astype(BF16), nt, preferred_element_type=F32)
        y_g = y_off * dec_out[:, g0:g0 + GROUP_WIDTH]
        pieces = []
        for pair in range(HEADS_PER_GROUP // 2):
            x_pair = xdt_b[:, g0 + pair * LANES:g0 + (pair + 1) * LANES]
            acc = None
            for e in range(2):
                h = g * HEADS_PER_GROUP + pair * 2 + e
                seg = jnp.exp(jnp.where(causal, acs[:, h:h + 1] - acs_t[h:h + 1, :], -jnp.inf))
                m = (cbm * seg).astype(BF16)
                x_h = jnp.where(lo_half if e == 0 else jnp.logical_not(lo_half), x_pair, jnp.zeros_like(x_pair))
                part = jnp.dot(m, x_h, preferred_element_type=F32)
                acc = part if acc is None else acc + part
            pieces.append(acc)
        y_g = y_g + jnp.concatenate(pieces, axis=1)
        y_g = y_g + xs[:, g0:g0 + GROUP_WIDTH] * dsk_ref[:, g0:g0 + GROUP_WIDTH]
        y_g = y_g * _silu(z[:, g0:g0 + GROUP_WIDTH])
        y_g = y_g * lax.rsqrt(jnp.mean(y_g * y_g, axis=-1, keepdims=True) + EPS)
        y_g = y_g * ng_ref[:, g0:g0 + GROUP_WIDTH]
        y_ref[0, :, g0:g0 + GROUP_WIDTH] = y_g[:q_valid].astype(y_ref.dtype)

        st_new = lax.dot_general(xw[:, g0:g0 + GROUP_WIDTH], b_g, tn, preferred_element_type=F32)
        for r in range(HEADS_PER_GROUP):
            h = g * HEADS_PER_GROUP + r
            decay = jnp.exp(acs_t[h:h + 1, q - 1:q])
            r0 = g0 + r * SSM_HEAD_DIM
            st_ref[0, r0:r0 + SSM_HEAD_DIM, :] = (
                h_g[r * SSM_HEAD_DIM:(r + 1) * SSM_HEAD_DIM] * decay
                + st_new[r * SSM_HEAD_DIM:(r + 1) * SSM_HEAD_DIM])


def _ssd(proj, dtp, buf8, h0, conv_w, conv_b, dt_bias, a_log, d_full, norm_g, expand, rows):
    b_, l, _ = proj.shape
    nc = l // rows
    st_rows = SSM_INNER
    return pl.pallas_call(
        functools.partial(_ssd_kernel, q_valid=rows, q=max(rows, BF16_SUBLANES)),
        grid=(b_, nc),
        in_specs=[pl.BlockSpec((1, rows, SSM_INNER), lambda b, c: (b, c, 0)),
                  pl.BlockSpec((1, rows, SSM_INNER), lambda b, c: (b, c, 1)),
                  pl.BlockSpec((1, rows, SSM_INNER), lambda b, c: (b, c, 2)),
                  pl.BlockSpec((1, rows, LANES), lambda b, c: (b, c, 0)),
                  pl.BlockSpec((1, TAIL_ROWS, SSM_CONV_DIM), lambda b, c: (b, 0, 0)),
                  pl.BlockSpec((1, st_rows, SSM_STATE), lambda b, c: (b, 0, 0)),
                  pl.BlockSpec((SSM_CONV, SSM_CONV_DIM), lambda b, c: (0, 0)),
                  pl.BlockSpec((1, SSM_CONV_DIM), lambda b, c: (0, 0)),
                  pl.BlockSpec((1, LANES), lambda b, c: (0, 0)),
                  pl.BlockSpec((1, LANES), lambda b, c: (0, 0)),
                  pl.BlockSpec((1, SSM_INNER), lambda b, c: (0, 0)),
                  pl.BlockSpec((1, SSM_INNER), lambda b, c: (0, 0)),
                  pl.BlockSpec((LANES, SSM_INNER), lambda b, c: (0, 0))],
        out_specs=[pl.BlockSpec((1, rows, SSM_INNER), lambda b, c: (b, c, 0)),
                   pl.BlockSpec((1, st_rows, SSM_STATE), lambda b, c: (b, 0, 0))],
        out_shape=[jax.ShapeDtypeStruct((b_, l, SSM_INNER), BF16),
                   jax.ShapeDtypeStruct((b_, st_rows, SSM_STATE), F32)],
        scratch_shapes=[pltpu.VMEM((TAIL_ROWS, SSM_CONV_DIM), F32)],
        compiler_params=_cparams(("parallel", "arbitrary")),
        name="ssd",
    )(proj, proj, proj, dtp, buf8, h0, conv_w, conv_b, dt_bias, a_log, d_full, norm_g, expand)


def _lru_kernel(gate_ref, xr_ref, buf_ref, h0_ref, cw_ref, cb_ref, wa_ref, ba_ref, wx_ref, bx_ref,
                lam_ref, y_ref, hl_ref, tail_scr, a_scr, u_scr):
    t = xr_ref.shape[1]
    c = pl.program_id(1)

    @pl.when(c == 0)
    def _():
        tail_scr[...] = buf_ref[0]
        hl_ref[0] = h0_ref[0]

    xc, tail = _dwconv(xr_ref[0], tail_scr[...], cw_ref[...], cb_ref[...])
    tail_scr[...] = tail
    sp = _softplus(-lam_ref[...])
    for k in range(LRU_BLOCKS):
        k0 = k * LRU_BLOCK
        xk = xc[:, k0:k0 + LRU_BLOCK]
        xkb = xk.astype(BF16)
        gr = _sigmoid(jnp.dot(xkb, wa_ref[k], preferred_element_type=F32) + ba_ref[:, k0:k0 + LRU_BLOCK])
        gi = _sigmoid(jnp.dot(xkb, wx_ref[k], preferred_element_type=F32) + bx_ref[:, k0:k0 + LRU_BLOCK])
        log_a = -LRU_C * gr * sp[:, k0:k0 + LRU_BLOCK]
        a_scr[:, k0:k0 + LRU_BLOCK] = jnp.exp(log_a)
        u_scr[:, k0:k0 + LRU_BLOCK] = _sqrt_neg_expm1(2.0 * log_a) * gi * xk

    row = lax.broadcasted_iota(jnp.int32, (SUBLANES, LRU_WIDTH), 0)

    def tile(i, h_prev):
        r0 = pl.multiple_of(i * SUBLANES, SUBLANES)
        a = a_scr[pl.ds(r0, SUBLANES), :]
        u = u_scr[pl.ds(r0, SUBLANES), :]
        d = 1
        while d < SUBLANES:
            keep = row >= d
            u = u + a * jnp.where(keep, pltpu.roll(u, d, 0), 0.0)
            a = a * jnp.where(keep, pltpu.roll(a, d, 0), 1.0)
            d *= 2
        h = u + a * h_prev
        u_scr[pl.ds(r0, SUBLANES), :] = h
        return h[SUBLANES - 1:SUBLANES, :]

    h_last = lax.fori_loop(0, t // SUBLANES, tile, hl_ref[0])
    hl_ref[0] = h_last
    y_ref[0] = (u_scr[...] * jax.nn.gelu(gate_ref[0])).astype(y_ref.dtype)


def _lru(proj, buf8, h0, conv_w, conv_b, w_a, b_a, w_x, b_x, lam, rows, col0):
    b_, l, _ = proj.shape
    w = LRU_WIDTH
    full2 = lambda shape: pl.BlockSpec(shape, lambda b, c: (0,) * len(shape))
    return pl.pallas_call(
        _lru_kernel,
        grid=(b_, l // rows),
        in_specs=[pl.BlockSpec((1, rows, w), lambda b, c: (b, c, col0)),
                  pl.BlockSpec((1, rows, w), lambda b, c: (b, c, col0 + 1)),
                  pl.BlockSpec((1, TAIL_ROWS, w), lambda b, c: (b, 0, 0)),
                  pl.BlockSpec((1, 1, w), lambda b, c: (b, 0, 0)),
                  full2((SSM_CONV, w)), full2((1, w)),
                  full2((LRU_BLOCKS, LRU_BLOCK, LRU_BLOCK)), full2((1, w)),
                  full2((LRU_BLOCKS, LRU_BLOCK, LRU_BLOCK)), full2((1, w)),
                  full2((1, w))],
        out_specs=[pl.BlockSpec((1, rows, w), lambda b, c: (b, c, 0)),
                   pl.BlockSpec((1, 1, w), lambda b, c: (b, 0, 0))],
        out_shape=[jax.ShapeDtypeStruct((b_, l, w), BF16),
                   jax.ShapeDtypeStruct((b_, 1, w), F32)],
        scratch_shapes=[pltpu.VMEM((TAIL_ROWS, w), F32), pltpu.VMEM((rows, w), F32),
                        pltpu.VMEM((rows, w), F32)],
        compiler_params=_cparams(("parallel", "arbitrary")),
        name="lru",
    )(proj, proj, buf8, h0, conv_w, conv_b, w_a, b_a, w_x, b_x, lam)


def _head_variants(slab, half):
    lane = lax.broadcasted_iota(jnp.int32, slab.shape, 1)
    keep = (lane < HEAD_DIM) if half == 0 else (lane >= HEAD_DIM)
    own = jnp.where(keep, slab, 0.0)
    other = pltpu.roll(own, HEAD_DIM, 1)
    return (own, other) if half == 0 else (other, own)


def _slab_attention(q_ref, o_ref, k2, v2, valid, sink_ref):
    w = q_ref.shape[1]
    nk = k2.shape[0]
    nt = (((1,), (1,)), ((), ()))
    scale = HEAD_DIM ** -0.5
    ones = jnp.ones((nk, LANES), BF16)
    slabs = range(nk // LANES)
    heads = range(KV_HEADS)
    halves = [(e, c) for e in heads for c in range(2)]
    vs, ss = [], []
    for e in heads:
        slab, half = divmod(e, 2)
        k_lo, k_hi = _head_variants(k2[:, slab * LANES:(slab + 1) * LANES], half)
        vs.extend(v.astype(BF16) for v in _head_variants(v2[:, slab * LANES:(slab + 1) * LANES], half))
        kk = jnp.concatenate([k_lo, k_hi], axis=0).astype(BF16)
        c0 = 2 * e * LANES
        qq = jnp.concatenate([q_ref[0, :, c0:c0 + LANES], q_ref[0, :, c0 + LANES:c0 + 2 * LANES]], axis=0)
        ss.append(lax.dot_general((qq * scale).astype(BF16), kk, nt, preferred_element_type=F32))
    scs = [jnp.where(valid, ss[e][:, c * nk:(c + 1) * nk], -jnp.inf) for e, c in halves]
    sinks = [jnp.concatenate([jnp.full((w, LANES), sink_ref[Q_PER_KV * e + c], F32),
                              jnp.full((w, LANES), sink_ref[Q_PER_KV * e + 2 + c], F32)], axis=0) for e, c in halves]
    ms = [jnp.maximum(jnp.broadcast_to(jnp.max(sc, axis=1, keepdims=True), sk.shape), sk)
          for sc, sk in zip(scs, sinks)]
    ps = [jnp.concatenate([jnp.exp(sc[:, i * LANES:(i + 1) * LANES] - m) for i in slabs], axis=1).astype(BF16)
          for sc, m in zip(scs, ms)]
    dens = [jnp.dot(p, ones, preferred_element_type=F32) + jnp.exp(sk - m) for p, sk, m in zip(ps, sinks, ms)]
    outs = [jnp.dot(p, v, preferred_element_type=F32) * (1.0 / den) for p, v, den in zip(ps, vs, dens)]
    for e in heads:
        o = outs[2 * e] + outs[2 * e + 1]
        c0 = 2 * e * LANES
        o_ref[0, :, c0:c0 + LANES] = o[:w].astype(o_ref.dtype)
        o_ref[0, :, c0 + LANES:c0 + 2 * LANES] = o[w:].astype(o_ref.dtype)


def _attn_prompt_kernel(sink_ref, q_ref, kc_ref, kp_ref, vc_ref, vp_ref, o_ref):
    w = WINDOW
    has_prev = pl.program_id(1) > 0
    k2 = jnp.concatenate([jnp.where(has_prev, kp_ref[0], 0.0), kc_ref[0]], axis=0)
    v2 = jnp.concatenate([jnp.where(has_prev, vp_ref[0], 0.0), vc_ref[0]], axis=0)
    t = lax.broadcasted_iota(jnp.int32, (2 * w, 2 * w), 0) % w
    j = lax.broadcasted_iota(jnp.int32, (2 * w, 2 * w), 1)
    _slab_attention(q_ref, o_ref, k2, v2, (j > t) & (j <= t + w), sink_ref)


def _attn_prompt(qkv, sinks):
    b_, l, _ = qkv.shape
    w = WINDOW
    qw = ATT_HEADS * HEAD_DIM
    kvw = KV_HEADS * HEAD_DIM
    k_blk = qw // kvw
    prev = lambda n: jnp.maximum(n - 1, 0)
    return pl.pallas_call(
        _attn_prompt_kernel,
        grid=(b_, l // w),
        in_specs=[pl.BlockSpec(memory_space=pltpu.SMEM),
                  pl.BlockSpec((1, w, qw), lambda b, n: (b, n, 0)),
                  pl.BlockSpec((1, w, kvw), lambda b, n: (b, n, k_blk)),
                  pl.BlockSpec((1, w, kvw), lambda b, n: (b, prev(n), k_blk)),
                  pl.BlockSpec((1, w, kvw), lambda b, n: (b, n, k_blk + 1)),
                  pl.BlockSpec((1, w, kvw), lambda b, n: (b, prev(n), k_blk + 1))],
        out_specs=pl.BlockSpec((1, w, qw), lambda b, n: (b, n, 0)),
        out_shape=jax.ShapeDtypeStruct((b_, l, qw), BF16),
        compiler_params=_cparams(("parallel", "parallel")),
        name="attn_prompt",
    )(sinks, qkv, qkv, qkv, qkv, qkv)


def _attn_sample_kernel(sink_ref, qkv_ref, ck_ref, cv_ref, o_ref, nk_ref, nv_ref):
    l = qkv_ref.shape[1]
    n_keep = ck_ref.shape[1]
    kcol = ATT_HEADS * HEAD_DIM
    vcol = kcol + KV_HEADS * HEAD_DIM
    qkv = qkv_ref[0]
    k_all = jnp.concatenate([ck_ref[0], qkv[:, kcol:vcol]], axis=0)
    v_all = jnp.concatenate([cv_ref[0], qkv[:, vcol:]], axis=0)
    nk_ref[0] = k_all[l:]
    nv_ref[0] = v_all[l:]
    nk = -(-(n_keep + l) // LANES) * LANES
    pad = jnp.zeros((nk - n_keep - l, k_all.shape[1]), F32)
    t = lax.broadcasted_iota(jnp.int32, (2 * l, nk), 0) % l
    j = lax.broadcasted_iota(jnp.int32, (2 * l, nk), 1)
    rel = jnp.where(j < n_keep, t + n_keep - j, t - (j - n_keep))
    valid = (rel >= 0) & (rel < WINDOW) & (j < n_keep + l)
    _slab_attention(qkv_ref, o_ref, jnp.concatenate([k_all, pad], axis=0), jnp.concatenate([v_all, pad], axis=0),
                    valid, sink_ref)


def _attn_sample(qkv, cache_k, cache_v, sinks):
    b_, l, n = qkv.shape
    n_keep, kvw = cache_k.shape[1:]
    return pl.pallas_call(
        _attn_sample_kernel,
        grid=(b_,),
        in_specs=[pl.BlockSpec(memory_space=pltpu.SMEM),
                  pl.BlockSpec((1, l, n), lambda b: (b, 0, 0)),
                  pl.BlockSpec((1, n_keep, kvw), lambda b: (b, 0, 0)),
                  pl.BlockSpec((1, n_keep, kvw), lambda b: (b, 0, 0))],
        out_specs=[pl.BlockSpec((1, l, ATT_HEADS * HEAD_DIM), lambda b: (b, 0, 0)),
                   pl.BlockSpec((1, n_keep, kvw), lambda b: (b, 0, 0)),
                   pl.BlockSpec((1, n_keep, kvw), lambda b: (b, 0, 0))],
        out_shape=[jax.ShapeDtypeStruct((b_, l, ATT_HEADS * HEAD_DIM), BF16),
                   jax.ShapeDtypeStruct((b_, n_keep, kvw), F32),
                   jax.ShapeDtypeStruct((b_, n_keep, kvw), F32)],
        compiler_params=_cparams(("parallel",)),
        name="attn_sample",
    )(sinks, qkv, cache_k, cache_v)


def _rope_tables(pos):
    half = HEAD_DIM // 2
    inv_freq = ROPE_THETA ** (-jnp.arange(half, dtype=F32) / half)
    ang = pos.astype(F32)[:, None] * inv_freq[None, :]
    cos = jnp.cos(ang)
    sin = jnp.sin(ang)
    reps = LANES // HEAD_DIM
    return (jnp.tile(jnp.concatenate([cos, cos], axis=1), (1, reps)),
            jnp.tile(jnp.concatenate([-sin, sin], axis=1), (1, reps)))


def _tail_pad(buf):
    return jnp.pad(buf, ((0, 0), (TAIL_ROWS - buf.shape[1], 0), (0, 0)))


def _trunk(x, mods, pos, st, pw, seq_len):
    g_, m, d = x.shape
    nseq = g_ * m // seq_len
    blk = _block_rows(m)
    chunk = min(seq_len, SSM_CHUNK)
    lru_rows = min(seq_len, LRU_ROWS)

    proj, dtp = _inproj(x, pw['norms'][0, 0:1], mods[0], pw['w_in_t'], SSM_INNER + SSM_CONV_DIM, SSM_HEADS,
                        blk['inproj'], INPROJ_COLS)
    n_main = proj.shape[-1]
    proj_s = proj.reshape(nseq, seq_len, n_main)
    dtp_s = dtp.reshape(nseq, seq_len, LANES)
    y_ssm, ssm_h = _ssd(proj_s, dtp_s, _tail_pad(st['ssm_conv']), st['ssm'].reshape(nseq, SSM_INNER, SSM_STATE),
                        pw['ssm_conv_w'], pw['ssm_conv_b'], pw['ssm_dt_bias'], pw['ssm_a_log'], pw['ssm_d_full'],
                        pw['ssm_norm'], pw['expand'], chunk)
    y_lru, lru_h = _lru(proj_s, _tail_pad(st['lru_conv']), st['lru'].reshape(nseq, 1, LRU_WIDTH),
                        pw['lru_conv_w'], pw['lru_conv_b'], pw['lru_w_a'], pw['lru_b_a'], pw['lru_w_x'],
                        pw['lru_b_x'], pw['lru_lambda'], lru_rows, 3)
    keep = SSM_CONV - 1
    new_ssm_conv = proj_s[:, seq_len - keep:, SSM_INNER:SSM_INNER + SSM_CONV_DIM]
    new_lru_conv = proj_s[:, seq_len - keep:, n_main - LRU_WIDTH:]
    x = _outproj([y_ssm.reshape(g_, m, SSM_INNER), y_lru.reshape(g_, m, LRU_WIDTH)],
                 pw['w_out_hyb'], x, pw['norms'][0, 1:2], mods[0], blk['outproj'])
    x = _mlp(x, pw['norms'][0, 2:3], pw['norms'][0, 3:4], mods[0], pw['w_up'], pw['w_down'], 0, blk['mlp'],
             MLP_COLS)

    cos_t, sin_t = _rope_tables(pos)
    if seq_len < blk['qkv']:
        cos_t = jnp.tile(cos_t, (blk['qkv'] // seq_len, 1))
        sin_t = jnp.tile(sin_t, (blk['qkv'] // seq_len, 1))
    qkv = _qkv(x, pw['norms'][1, 0:1], mods[1], pw['w_qkv'], pw['b_qkv'], cos_t, sin_t, blk['qkv'], QKV_COLS)
    qkv_s = qkv.reshape(nseq, seq_len, qkv.shape[-1])
    kcol = ATT_HEADS * HEAD_DIM
    vcol = kcol + KV_HEADS * HEAD_DIM
    if st['k'] is None:
        o = _attn_prompt(qkv_s, pw['attn_sinks'])
        n_keep = min(WINDOW, seq_len)
        k_new = qkv_s[:, seq_len - n_keep:, kcol:vcol]
        v_new = qkv_s[:, seq_len - n_keep:, vcol:]
    else:
        n_keep = st['k'].shape[1]
        o, k_new, v_new = _attn_sample(qkv_s, st['k'].reshape(nseq, n_keep, KV_HEADS * HEAD_DIM),
                                       st['v'].reshape(nseq, n_keep, KV_HEADS * HEAD_DIM), pw['attn_sinks'])
    k_new = k_new.reshape(nseq, n_keep, KV_HEADS, HEAD_DIM)
    v_new = v_new.reshape(nseq, n_keep, KV_HEADS, HEAD_DIM)
    x = _outproj([o.reshape(g_, m, kcol)], pw['w_out_attn'], x, pw['norms'][1, 1:2], mods[1], blk['outproj'])
    x = _mlp(x, pw['norms'][1, 2:3], pw['norms'][1, 3:4], mods[1], pw['w_up'], pw['w_down'], 1, blk['mlp'],
             MLP_COLS)

    states = (new_ssm_conv[None], ssm_h.reshape(1, nseq, SSM_HEADS, SSM_HEAD_DIM, SSM_STATE),
              new_lru_conv[None], lru_h.reshape(1, nseq, LRU_WIDTH), k_new[None], v_new[None])
    return x, states


def kernel(x_prompt, x_sample, state_ssm_conv, state_ssm, state_lru_conv, state_lru, cache_k, cache_v, c_prompt, c_sample, w_mod, b_mod, norms, w_in_hyb, ssm_conv_w, ssm_conv_b, ssm_dt_bias, ssm_a_log, ssm_d, ssm_norm, lru_conv_w, lru_conv_b, lru_w_a, lru_b_a, lru_w_x, lru_b_x, lru_lambda, w_out_hyb, w_qkv, b_qkv, attn_sinks, w_out_attn, w_up, w_down):
    bp, lp, d = x_prompt.shape
    bs, ls, _ = x_sample.shape
    depth = w_mod.shape[0]

    pad_lanes = lambda v: jnp.pad(v.reshape(1, -1), ((0, 0), (0, LANES - v.size)))
    head_of_lane = jnp.arange(SSM_INNER) // SSM_HEAD_DIM
    pw = {
        'norms': norms,
        'w_in_t': jnp.swapaxes(w_in_hyb, 1, 2),
        'ssm_conv_w': ssm_conv_w[0], 'ssm_conv_b': ssm_conv_b[0].reshape(1, -1),
        'ssm_dt_bias': pad_lanes(ssm_dt_bias[0]), 'ssm_a_log': pad_lanes(ssm_a_log[0]),
        'ssm_d_full': jnp.repeat(ssm_d[0], SSM_HEAD_DIM).reshape(1, -1),
        'ssm_norm': ssm_norm[0].reshape(1, -1),
        'expand': (jnp.arange(LANES)[:, None] == head_of_lane[None, :]).astype(BF16),
        'lru_conv_w': lru_conv_w[0], 'lru_conv_b': lru_conv_b[0].reshape(1, -1),
        'lru_w_a': lru_w_a[0].astype(BF16), 'lru_b_a': lru_b_a[0].reshape(1, -1),
        'lru_w_x': lru_w_x[0].astype(BF16), 'lru_b_x': lru_b_x[0].reshape(1, -1),
        'lru_lambda': lru_lambda[0].reshape(1, -1),
        'w_out_hyb': w_out_hyb[0].astype(BF16),
        'w_qkv': w_qkv[0].astype(BF16), 'b_qkv': b_qkv[0].reshape(1, -1),
        'attn_sinks': attn_sinks[0],
        'w_out_attn': w_out_attn[0].astype(BF16),
        'w_up': w_up, 'w_down': w_down,
    }

    mod = _adaln(jnp.concatenate([c_prompt, c_sample], axis=0), w_mod, b_mod)
    mods_p = [mod[l, :bp][:, None, :] for l in range(depth)]
    mods_s = [jnp.repeat(mod[l, bp:], ls, axis=0)[None] for l in range(depth)]

    zeros = lambda *shape: jnp.zeros(shape, F32)
    st_p = {'ssm_conv': zeros(bp, SSM_CONV - 1, SSM_CONV_DIM), 'ssm': zeros(bp, SSM_INNER, SSM_STATE),
            'lru_conv': zeros(bp, SSM_CONV - 1, LRU_WIDTH), 'lru': zeros(bp, LRU_WIDTH), 'k': None, 'v': None}
    st_s = {'ssm_conv': state_ssm_conv[0], 'ssm': state_ssm[0], 'lru_conv': state_lru_conv[0],
            'lru': state_lru[0], 'k': cache_k[0], 'v': cache_v[0]}

    pos_p = jnp.arange(lp, dtype=jnp.int32)
    pos_s = PAST_LEN + jnp.arange(ls, dtype=jnp.int32)
    y_p, sp = _trunk(x_prompt, mods_p, pos_p, st_p, pw, lp)
    y_s, ss = _trunk(x_sample.reshape(1, bs * ls, d), mods_s, pos_s, st_s, pw, ls)
    return (y_p, y_s.reshape(bs, ls, d)) + sp + ss
```

```python
import functools
import math

import jax
import jax.numpy as jnp
from jax import lax
from jax.experimental import pallas as pl
from jax.experimental.pallas import tpu as pltpu

F32 = jnp.float32
BF16 = jnp.bfloat16

D_MODEL = 2048
PAST_LEN = 16384
SSM_HEADS = 32
SSM_HEAD_DIM = 64
SSM_INNER = SSM_HEADS * SSM_HEAD_DIM
SSM_GROUPS = 8
SSM_STATE = 128
SSM_CONV = 4
SSM_CHUNK = 128
SSM_CONV_DIM = SSM_INNER + 2 * SSM_GROUPS * SSM_STATE
HEADS_PER_GROUP = SSM_HEADS // SSM_GROUPS
GROUP_WIDTH = HEADS_PER_GROUP * SSM_HEAD_DIM
LRU_WIDTH = D_MODEL
LRU_BLOCKS = 8
LRU_BLOCK = LRU_WIDTH // LRU_BLOCKS
LRU_C = 8.0
ATT_HEADS = 32
KV_HEADS = 8
HEAD_DIM = 64
Q_PER_KV = ATT_HEADS // KV_HEADS
WINDOW = 128
ROPE_THETA = 10000.0
D_FF = 4 * D_MODEL
EPS = 1e-6

LANES = 128
SUBLANES = 8
BF16_SUBLANES = 16
TAIL_ROWS = SUBLANES
VMEM_LIMIT = 56 * 1024 * 1024


INPROJ_COLS = 1024
MLP_COLS = 512
QKV_COLS = 512
LRU_ROWS = 256


def _block_rows(m):
    return {'inproj': min(m, 1024), 'mlp': min(m, 1024), 'qkv': min(m, 512), 'outproj': min(m, 512)}


def _cparams(sem):
    return pltpu.CompilerParams(dimension_semantics=sem, vmem_limit_bytes=VMEM_LIMIT)


def _rms(x, g):
    return x * lax.rsqrt(jnp.mean(x * x, axis=-1, keepdims=True) + EPS) * g


ROW_CHUNK = 16
ROW_UNROLL = 4


def _for_row_chunks(n_rows, body):
    def step(i, carry):
        body(pl.multiple_of(i * ROW_CHUNK, ROW_CHUNK))
        return carry

    lax.fori_loop(0, n_rows // ROW_CHUNK, step, 0, unroll=ROW_UNROLL)


def _mod_rows(ref, r0):
    return ref[0] if ref.shape[1] == 1 else ref[0, pl.ds(r0, ROW_CHUNK), :]


def _prenorm_to(h_scr, x_ref, g_ref, sc_ref, sh_ref):
    def body(r0):
        x = x_ref[0, pl.ds(r0, ROW_CHUNK), :]
        h = _rms(x, g_ref[...]) * (1.0 + _mod_rows(sc_ref, r0)) + _mod_rows(sh_ref, r0)
        h_scr[pl.ds(r0, ROW_CHUNK), :] = h.astype(BF16)

    _for_row_chunks(x_ref.shape[1], body)


def _sigmoid(x):
    return 0.5 * (jnp.tanh(0.5 * x) + 1.0)


def _silu(x):
    return x * _sigmoid(x)


def _softplus(x):
    return jnp.maximum(x, 0.0) + jnp.log1p(jnp.exp(-jnp.abs(x)))


def _sqrt_neg_expm1(x):
    s = jnp.tanh(-0.5 * x)
    t = (2.0 * s) * (1.0 + s)
    return jnp.where(t > 0.0, (2.0 * s) * lax.rsqrt(t), 0.0)


def _split3(x):
    hi = x.astype(BF16)
    r = x - hi.astype(F32)
    mid = r.astype(BF16)
    lo = (r - mid.astype(F32)).astype(BF16)
    return [hi, mid, lo]


def _adaln_kernel(c_ref, w_ref, b_ref, o_ref):
    s = _silu(c_ref[...]).astype(BF16)
    o_ref[0] = jnp.dot(s, w_ref[0].astype(BF16), preferred_element_type=F32) + b_ref[0]


def _adaln(c_all, w_mod, b_mod):
    nb, d = c_all.shape
    depth, _, n = w_mod.shape
    bn = 1024
    return pl.pallas_call(
        _adaln_kernel,
        grid=(depth, n // bn),
        in_specs=[pl.BlockSpec((nb, d), lambda l, j: (0, 0)),
                  pl.BlockSpec((1, d, bn), lambda l, j: (l, 0, j)),
                  pl.BlockSpec((1, 1, bn), lambda l, j: (l, 0, j))],
        out_specs=pl.BlockSpec((1, nb, bn), lambda l, j: (l, 0, j)),
        out_shape=jax.ShapeDtypeStruct((depth, nb, n), F32),
        compiler_params=_cparams(("parallel", "parallel")),
        name="adaln",
    )(c_all, w_mod, b_mod.reshape(depth, 1, n))


def _mod_spec(mod, bm, k):
    if mod.shape[1] == 1:
        return pl.BlockSpec((1, 1, D_MODEL), lambda g, i, *_: (g, 0, k))
    return pl.BlockSpec((1, bm, D_MODEL), lambda g, i, *_: (g, i, k))


def _inproj_kernel(x_ref, g_ref, sh_ref, sc_ref, wt_ref, wdt_ref, o_ref, dt_ref, h_scr, *, n_dt):
    nt = (((1,), (1,)), ((), ()))

    @pl.when(pl.program_id(2) == 0)
    def _():
        _prenorm_to(h_scr, x_ref, g_ref, sc_ref, sh_ref)
        dt = lax.dot_general(h_scr[...], wdt_ref[...].astype(BF16), nt, preferred_element_type=F32)
        lane = lax.broadcasted_iota(jnp.int32, dt.shape, 1)
        dt_ref[0] = jnp.where(lane < n_dt, dt, 0.0)

    o_ref[0] = lax.dot_general(h_scr[...], wt_ref[...].astype(BF16), nt, preferred_element_type=F32)


def _inproj(x, gain, mod, w_t, c_dt, n_dt, bm, bn):
    g_, m, d = x.shape
    n = w_t.shape[1] - n_dt
    row0 = lambda j: pl.multiple_of(jnp.where(j * bn < c_dt, j * bn, j * bn + n_dt), SUBLANES)
    return pl.pallas_call(
        functools.partial(_inproj_kernel, n_dt=n_dt),
        grid=(g_, m // bm, n // bn),
        in_specs=[pl.BlockSpec((1, bm, d), lambda g, i, j: (g, i, 0), pipeline_mode=pl.Buffered(1)),
                  pl.BlockSpec((1, d), lambda g, i, j: (0, 0)),
                  _mod_spec(mod, bm, 0), _mod_spec(mod, bm, 1),
                  pl.BlockSpec((None, pl.Element(bn), pl.Element(d)), lambda g, i, j: (0, row0(j), 0)),
                  pl.BlockSpec((None, pl.Element(LANES), pl.Element(d)), lambda g, i, j: (0, c_dt, 0))],
        out_specs=[pl.BlockSpec((1, bm, bn), lambda g, i, j: (g, i, j)),
                   pl.BlockSpec((1, bm, LANES), lambda g, i, j: (g, i, 0))],
        out_shape=[jax.ShapeDtypeStruct((g_, m, n), F32),
                   jax.ShapeDtypeStruct((g_, m, LANES), F32)],
        scratch_shapes=[pltpu.VMEM((bm, d), BF16)],
        compiler_params=_cparams(("parallel", "parallel", "arbitrary")),
        name="inproj",
    )(x, gain, mod, mod, w_t, w_t)


def _swap_halves(x):
    lane = lax.broadcasted_iota(jnp.int32, x.shape, 1)
    first = (lane % HEAD_DIM) < (HEAD_DIM // 2)
    return jnp.where(first, pltpu.roll(x, LANES - HEAD_DIM // 2, 1), pltpu.roll(x, HEAD_DIM // 2, 1))


def _qkv_kernel(x_ref, g_ref, sh_ref, sc_ref, w_ref, b_ref, cos_ref, sin_ref, o_ref, h_scr, *, rope_cols, bn):
    _prenorm_to(h_scr, x_ref, g_ref, sc_ref, sh_ref)
    h = h_scr[...]
    cos = cos_ref[...]
    sin = sin_ref[...]
    for c0 in range(0, w_ref.shape[1], bn):
        acc = jnp.dot(h, w_ref[:, c0:c0 + bn], preferred_element_type=F32) + b_ref[:, c0:c0 + bn]
        if c0 < rope_cols:
            for c in range(c0, c0 + bn, LANES):
                a = acc[:, c - c0:c - c0 + LANES]
                o_ref[0, :, c:c + LANES] = a * cos + _swap_halves(a) * sin
        else:
            o_ref[0, :, c0:c0 + bn] = acc


def _qkv(x, gain, mod, w, b, cos_t, sin_t, bm, bn):
    g_, m, d = x.shape
    n = w.shape[1]
    rope_cols = (ATT_HEADS + KV_HEADS) * HEAD_DIM
    rows_per_g = cos_t.shape[0] // bm
    return pl.pallas_call(
        functools.partial(_qkv_kernel, rope_cols=rope_cols, bn=bn),
        grid=(g_, m // bm),
        in_specs=[pl.BlockSpec((1, bm, d), lambda g, i: (g, i, 0)),
                  pl.BlockSpec((1, d), lambda g, i: (0, 0)),
                  _mod_spec(mod, bm, 0), _mod_spec(mod, bm, 1),
                  pl.BlockSpec((d, n), lambda g, i: (0, 0), pipeline_mode=pl.Buffered(1)),
                  pl.BlockSpec((1, n), lambda g, i: (0, 0)),
                  pl.BlockSpec((bm, LANES), lambda g, i: (i % rows_per_g, 0)),
                  pl.BlockSpec((bm, LANES), lambda g, i: (i % rows_per_g, 0))],
        out_specs=pl.BlockSpec((1, bm, n), lambda g, i: (g, i, 0)),
        out_shape=jax.ShapeDtypeStruct((g_, m, n), F32),
        scratch_shapes=[pltpu.VMEM((bm, d), BF16)],
        compiler_params=_cparams(("parallel", "parallel")),
        name="qkv",
    )(x, gain, mod, mod, w, b, cos_t, sin_t)


def _outproj_kernel(*refs, n_lhs):
    lhs = refs[:n_lhs]
    ws = refs[n_lhs:2 * n_lhs]
    x_ref, g_ref, gt_ref, o_ref = refs[2 * n_lhs:]
    acc = jnp.dot(lhs[0][0], ws[0][...], preferred_element_type=F32)
    for a, w in zip(lhs[1:], ws[1:]):
        acc = acc + jnp.dot(a[0], w[...], preferred_element_type=F32)
    o_ref[0] = x_ref[0] + gt_ref[0] * _rms(acc, g_ref[...])


def _outproj(lhs_list, w, x, gain, mod, bm):
    g_, m, d = x.shape
    n_lhs = len(lhs_list)
    kw = w.shape[0] // n_lhs
    lhs_specs = [pl.BlockSpec((1, bm, kw), lambda g, i: (g, i, 0)) for _ in lhs_list]
    w_specs = [pl.BlockSpec((kw, d), lambda g, i, k=k: (k, 0), pipeline_mode=pl.Buffered(1)) for k in range(n_lhs)]
    w_list = [w] * n_lhs
    if mod.shape[1] == 1:
        gt_spec = pl.BlockSpec((1, 1, d), lambda g, i: (g, 0, 2))
    else:
        gt_spec = pl.BlockSpec((1, bm, d), lambda g, i: (g, i, 2))
    return pl.pallas_call(
        functools.partial(_outproj_kernel, n_lhs=n_lhs),
        grid=(g_, m // bm),
        in_specs=lhs_specs + w_specs + [pl.BlockSpec((1, bm, d), lambda g, i: (g, i, 0)),
                                        pl.BlockSpec((1, d), lambda g, i: (0, 0)), gt_spec],
        out_specs=pl.BlockSpec((1, bm, d), lambda g, i: (g, i, 0)),
        out_shape=jax.ShapeDtypeStruct((g_, m, d), F32),
        compiler_params=_cparams(("parallel", "parallel")),
        name="outproj",
    )(*lhs_list, *w_list, x, gain, mod)


def _mlp_kernel(x_ref, g2_ref, sh_ref, sc_ref, gt_ref, wu_ref, wd_ref, g3_ref, o_ref, h_scr):
    f = pl.program_id(2)

    @pl.when(f == 0)
    def _():
        _prenorm_to(h_scr, x_ref, g2_ref, sc_ref, sh_ref)
        o_ref[0] = jnp.zeros(o_ref.shape[1:], F32)

    u = jnp.dot(h_scr[...], wu_ref[...].astype(BF16), preferred_element_type=F32)
    u = jnp.square(jnp.maximum(u, 0.0)).astype(BF16)
    o_ref[0] += jnp.dot(u, wd_ref[...].astype(BF16), preferred_element_type=F32)

    @pl.when(f == pl.num_programs(2) - 1)
    def _():
        o_ref[0] = x_ref[0] + gt_ref[0] * _rms(o_ref[0], g3_ref[...])


def _mlp(x, g2, g3, mod, w_up, w_down, layer, bm, bf):
    g_, m, d = x.shape
    dff = w_up.shape[2]
    if mod.shape[1] == 1:
        mspec = lambda k: pl.BlockSpec((1, 1, d), lambda g, i, f: (g, 0, k))
    else:
        mspec = lambda k: pl.BlockSpec((1, bm, d), lambda g, i, f: (g, i, k))
    return pl.pallas_call(
        _mlp_kernel,
        grid=(g_, m // bm, dff // bf),
        in_specs=[pl.BlockSpec((1, bm, d), lambda g, i, f: (g, i, 0), pipeline_mode=pl.Buffered(1)),
                  pl.BlockSpec((1, d), lambda g, i, f: (0, 0)),
                  mspec(3), mspec(4), mspec(5),
                  pl.BlockSpec((None, d, bf), lambda g, i, f: (layer, 0, f)),
                  pl.BlockSpec((None, bf, d), lambda g, i, f: (layer, f, 0)),
                  pl.BlockSpec((1, d), lambda g, i, f: (0, 0))],
        out_specs=pl.BlockSpec((1, bm, d), lambda g, i, f: (g, i, 0)),
        out_shape=jax.ShapeDtypeStruct((g_, m, d), F32),
        scratch_shapes=[pltpu.VMEM((bm, d), BF16)],
        compiler_params=_cparams(("parallel", "parallel", "arbitrary")),
        name="mlp",
    )(x, g2, mod, mod, mod, w_up, w_down, g3)


def _dwconv(raw, tail, w, b):
    t = raw.shape[0]
    assert w.shape[0] == 4
    ext = jnp.concatenate([tail, raw], axis=0)
    z1 = pltpu.roll(ext, 1, 0)
    near = ext * w[3:4] + z1 * w[2:3]
    far = ext * w[1:2] + z1 * w[0:1]
    y = near + pltpu.roll(far, 2, 0)
    return y[TAIL_ROWS:] + b, ext[t:t + TAIL_ROWS]


def _cumsum_rows(x):
    n = x.shape[0]
    row = lax.broadcasted_iota(jnp.int32, x.shape, 0)
    d = 1
    while d < n:
        x = x + jnp.where(row >= d, pltpu.roll(x, d, 0), 0.0)
        d *= 2
    return x


def _ssd_kernel(z_ref, xs_ref, bc_ref, dt_ref, buf_ref, h0_ref, cw_ref, cb_ref, dtb_ref, alog_ref,
                dsk_ref, ng_ref, exp_ref, y_ref, st_ref, tail_scr, *, q_valid, q):
    c = pl.program_id(1)

    @pl.when(c == 0)
    def _():
        tail_scr[...] = buf_ref[0]
        st_ref[0] = h0_ref[0]

    def padded(v):
        if q_valid == q:
            return v
        return jnp.concatenate([v, jnp.zeros((q - q_valid, v.shape[1]), v.dtype)], axis=0)

    def transposed(v):
        if q == LANES:
            return v.T
        return jnp.concatenate([v, jnp.zeros((LANES - q, LANES), v.dtype)], axis=0).T[:, :q]

    cw = cw_ref[...]
    cb = cb_ref[...]
    xs, tail_x = _dwconv(xs_ref[0], tail_scr[:, :SSM_INNER], cw[:, :SSM_INNER], cb[:, :SSM_INNER])
    bcm, tail_b = _dwconv(bc_ref[0], tail_scr[:, SSM_INNER:], cw[:, SSM_INNER:], cb[:, SSM_INNER:])
    tail_scr[:, :SSM_INNER] = tail_x
    tail_scr[:, SSM_INNER:] = tail_b
    xs = padded(_silu(xs))
    bcm = padded(_silu(bcm))
    z = padded(z_ref[0])

    dt = _softplus(dt_ref[0] + dtb_ref[...])
    dt = padded(dt)
    a = -jnp.exp(alog_ref[...])
    acs = _cumsum_rows(dt * a)
    acs_t = transposed(acs)

    full = jnp.dot(jnp.concatenate(_split3(dt) + _split3(acs), axis=0), exp_ref[...], preferred_element_type=F32)
    dt_full = (full[0:q] + full[q:2 * q]) + full[2 * q:3 * q]
    acs_full = (full[3 * q:4 * q] + full[4 * q:5 * q]) + full[5 * q:6 * q]
    xdt = xs * dt_full
    dec_out = jnp.exp(acs_full)
    xw = (xdt * jnp.exp(acs_full[q - 1:q] - acs_full)).astype(BF16)
    xdt_b = xdt.astype(BF16)
    bc_b = bcm.astype(BF16)

    row = lax.broadcasted_iota(jnp.int32, (q, q), 0)
    col = lax.broadcasted_iota(jnp.int32, (q, q), 1)
    causal = row >= col
    lane = lax.broadcasted_iota(jnp.int32, (q, LANES), 1)
    lo_half = lane < SSM_HEAD_DIM
    nt = (((1,), (1,)), ((), ()))
    tn = (((0,), (0,)), ((), ()))
    n_bc = SSM_GROUPS * SSM_STATE

    for g in range(SSM_GROUPS):
        b_g = bc_b[:, g * SSM_STATE:(g + 1) * SSM_STATE]
        c_g = bc_b[:, n_bc + g * SSM_STATE:n_bc + (g + 1) * SSM_STATE]
        cbm = lax.dot_general(c_g, b_g, nt, preferred_element_type=F32)
        g0 = g * GROUP_WIDTH
        h_g = st_ref[0, g0:g0 + GROUP_WIDTH, :]
        y_off = lax.dot_general(c_g, h_g.astype(BF16), nt, preferred_element_type=F32)
        y_g = y_off * dec_out[:, g0:g0 + GROUP_WIDTH]
        pieces = []
        for pair in range(HEADS_PER_GROUP // 2):
            x_pair = xdt_b[:, g0 + pair * LANES:g0 + (pair + 1) * LANES]
            acc = None
            for e in range(2):
                h = g * HEADS_PER_GROUP + pair * 2 + e
                seg = jnp.exp(jnp.where(causal, acs[:, h:h + 1] - acs_t[h:h + 1, :], -jnp.inf))
                m = (cbm * seg).astype(BF16)
                x_h = jnp.where(lo_half if e == 0 else jnp.logical_not(lo_half), x_pair, jnp.zeros_like(x_pair))
                part = jnp.dot(m, x_h, preferred_element_type=F32)
                acc = part if acc is None else acc + part
            pieces.append(acc)
        y_g = y_g + jnp.concatenate(pieces, axis=1)
        y_g = y_g + xs[:, g0:g0 + GROUP_WIDTH] * dsk_ref[:, g0:g0 + GROUP_WIDTH]
        y_g = y_g * _silu(z[:, g0:g0 + GROUP_WIDTH])
        y_g = y_g * lax.rsqrt(jnp.mean(y_g * y_g, axis=-1, keepdims=True) + EPS)
        y_g = y_g * ng_ref[:, g0:g0 + GROUP_WIDTH]
        y_ref[0, :, g0:g0 + GROUP_WIDTH] = y_g[:q_valid].astype(y_ref.dtype)

        st_new = lax.dot_general(xw[:, g0:g0 + GROUP_WIDTH], b_g, tn, preferred_element_type=F32)
        for r in range(HEADS_PER_GROUP):
            h = g * HEADS_PER_GROUP + r
            decay = jnp.exp(acs_t[h:h + 1, q - 1:q])
            r0 = g0 + r * SSM_HEAD_DIM
            st_ref[0, r0:r0 + SSM_HEAD_DIM, :] = (
                h_g[r * SSM_HEAD_DIM:(r + 1) * SSM_HEAD_DIM] * decay
                + st_new[r * SSM_HEAD_DIM:(r + 1) * SSM_HEAD_DIM])


def _ssd(proj, dtp, buf8, h0, conv_w, conv_b, dt_bias, a_log, d_full, norm_g, expand, rows):
    b_, l, _ = proj.shape
    nc = l // rows
    st_rows = SSM_INNER
    return pl.pallas_call(
        functools.partial(_ssd_kernel, q_valid=rows, q=max(rows, BF16_SUBLANES)),
        grid=(b_, nc),
        in_specs=[pl.BlockSpec((1, rows, SSM_INNER), lambda b, c: (b, c, 0)),
                  pl.BlockSpec((1, rows, SSM_INNER), lambda b, c: (b, c, 1)),
                  pl.BlockSpec((1, rows, SSM_INNER), lambda b, c: (b, c, 2)),
                  pl.BlockSpec((1, rows, LANES), lambda b, c: (b, c, 0)),
                  pl.BlockSpec((1, TAIL_ROWS, SSM_CONV_DIM), lambda b, c: (b, 0, 0)),
                  pl.BlockSpec((1, st_rows, SSM_STATE), lambda b, c: (b, 0, 0)),
                  pl.BlockSpec((SSM_CONV, SSM_CONV_DIM), lambda b, c: (0, 0)),
                  pl.BlockSpec((1, SSM_CONV_DIM), lambda b, c: (0, 0)),
                  pl.BlockSpec((1, LANES), lambda b, c: (0, 0)),
                  pl.BlockSpec((1, LANES), lambda b, c: (0, 0)),
                  pl.BlockSpec((1, SSM_INNER), lambda b, c: (0, 0)),
                  pl.BlockSpec((1, SSM_INNER), lambda b, c: (0, 0)),
                  pl.BlockSpec((LANES, SSM_INNER), lambda b, c: (0, 0))],
        out_specs=[pl.BlockSpec((1, rows, SSM_INNER), lambda b, c: (b, c, 0)),
                   pl.BlockSpec((1, st_rows, SSM_STATE), lambda b, c: (b, 0, 0))],
        out_shape=[jax.ShapeDtypeStruct((b_, l, SSM_INNER), BF16),
                   jax.ShapeDtypeStruct((b_, st_rows, SSM_STATE), F32)],
        scratch_shapes=[pltpu.VMEM((TAIL_ROWS, SSM_CONV_DIM), F32)],
        compiler_params=_cparams(("parallel", "arbitrary")),
        name="ssd",
    )(proj, proj, proj, dtp, buf8, h0, conv_w, conv_b, dt_bias, a_log, d_full, norm_g, expand)


SCAN_SEG = 4
SCAN_ROWS = SUBLANES * SCAN_SEG


def _shift_rows_fill(x, d, fill, row):
    return jnp.where(row >= d, pltpu.roll(x, d, 0), fill)


def _sublane_scan(e, pe, row):
    d = 1
    while d < SUBLANES:
        e = e + pe * _shift_rows_fill(e, d, 0.0, row)
        pe = pe * _shift_rows_fill(pe, d, 1.0, row)
        d *= 2
    return e, pe


def _linear_scan(a_scr, u_scr, h0):
    n_slab, t, _ = a_scr.shape
    seg = SCAN_SEG if t % SCAN_ROWS == 0 else 1
    row = lax.broadcasted_iota(jnp.int32, (SUBLANES, LANES), 0)

    def block(i, h_prev):
        base = pl.multiple_of(i * (SUBLANES * seg), SUBLANES)
        out = []
        for c in range(n_slab):
            rows = [pl.ds(base + j, SUBLANES, stride=seg) if seg > 1 else pl.ds(base, SUBLANES) for j in range(seg)]
            hs, ps = [], []
            for j in range(seg):
                a, u = a_scr[c, rows[j], :], u_scr[c, rows[j], :]
                hs.append(u if j == 0 else a * hs[-1] + u)
                ps.append(a if j == 0 else a * ps[-1])
            e, pe = _sublane_scan(hs[-1], ps[-1], row)
            g = e + pe * h_prev[c]
            carry = jnp.where(row == 0, h_prev[c], pltpu.roll(g, 1, 0))
            for j in range(seg):
                u_scr[c, rows[j], :] = hs[j] + ps[j] * carry
            out.append(jnp.broadcast_to(g[SUBLANES - 1:], g.shape))
        return tuple(out)

    h_init = tuple(jnp.broadcast_to(h0[:, c * LANES:(c + 1) * LANES], (SUBLANES, LANES)) for c in range(n_slab))
    h_fin = lax.fori_loop(0, t // (SUBLANES * seg), block, h_init)
    return jnp.concatenate([h[:1] for h in h_fin], axis=1)


def _gelu_tanh(x):
    c0 = math.sqrt(2.0 / math.pi)
    hx = 0.5 * x
    return hx + hx * jnp.tanh(x * (c0 + (c0 * 0.044715) * (x * x)))


def _lru_kernel(gate_ref, xr_ref, buf_ref, h0_ref, cw_ref, cb_ref, wa_ref, ba_ref, wx_ref, bx_ref,
                lam_ref, y_ref, hl_ref, tail_scr, a_scr, u_scr):
    c = pl.program_id(1)

    @pl.when(c == 0)
    def _():
        tail_scr[...] = buf_ref[0]
        hl_ref[0] = h0_ref[0]

    xc, tail = _dwconv(xr_ref[0], tail_scr[...], cw_ref[...], cb_ref[...])
    tail_scr[...] = tail
    sp = _softplus(-lam_ref[...])
    slabs_per_block = LRU_BLOCK // LANES
    for k in range(LRU_BLOCKS):
        k0 = k * LRU_BLOCK
        xk = xc[:, k0:k0 + LRU_BLOCK]
        xkb = xk.astype(BF16)
        gr = _sigmoid(jnp.dot(xkb, wa_ref[k], preferred_element_type=F32) + ba_ref[:, k0:k0 + LRU_BLOCK])
        gi = _sigmoid(jnp.dot(xkb, wx_ref[k], preferred_element_type=F32) + bx_ref[:, k0:k0 + LRU_BLOCK])
        log_a = -LRU_C * gr * sp[:, k0:k0 + LRU_BLOCK]
        a = jnp.exp(log_a)
        u = _sqrt_neg_expm1(2.0 * log_a) * gi * xk
        for s in range(slabs_per_block):
            a_scr[k * slabs_per_block + s] = a[:, s * LANES:(s + 1) * LANES]
            u_scr[k * slabs_per_block + s] = u[:, s * LANES:(s + 1) * LANES]

    hl_ref[0] = _linear_scan(a_scr, u_scr, hl_ref[0])
    for s in range(a_scr.shape[0]):
        gate = gate_ref[0, :, s * LANES:(s + 1) * LANES]
        y_ref[0, :, s * LANES:(s + 1) * LANES] = (u_scr[s] * _gelu_tanh(gate)).astype(y_ref.dtype)


def _lru(proj, buf8, h0, conv_w, conv_b, w_a, b_a, w_x, b_x, lam, rows, col0):
    b_, l, _ = proj.shape
    w = LRU_WIDTH
    full2 = lambda shape: pl.BlockSpec(shape, lambda b, c: (0,) * len(shape))
    return pl.pallas_call(
        _lru_kernel,
        grid=(b_, l // rows),
        in_specs=[pl.BlockSpec((1, rows, w), lambda b, c: (b, c, col0)),
                  pl.BlockSpec((1, rows, w), lambda b, c: (b, c, col0 + 1)),
                  pl.BlockSpec((1, TAIL_ROWS, w), lambda b, c: (b, 0, 0)),
                  pl.BlockSpec((1, 1, w), lambda b, c: (b, 0, 0)),
                  full2((SSM_CONV, w)), full2((1, w)),
                  full2((LRU_BLOCKS, LRU_BLOCK, LRU_BLOCK)), full2((1, w)),
                  full2((LRU_BLOCKS, LRU_BLOCK, LRU_BLOCK)), full2((1, w)),
                  full2((1, w))],
        out_specs=[pl.BlockSpec((1, rows, w), lambda b, c: (b, c, 0)),
                   pl.BlockSpec((1, 1, w), lambda b, c: (b, 0, 0))],
        out_shape=[jax.ShapeDtypeStruct((b_, l, w), BF16),
                   jax.ShapeDtypeStruct((b_, 1, w), F32)],
        scratch_shapes=[pltpu.VMEM((TAIL_ROWS, w), F32), pltpu.VMEM((w // LANES, rows, LANES), F32),
                        pltpu.VMEM((w // LANES, rows, LANES), F32)],
        compiler_params=_cparams(("parallel", "arbitrary")),
        name="lru",
    )(proj, proj, buf8, h0, conv_w, conv_b, w_a, b_a, w_x, b_x, lam)


def _head_variants(slab, half):
    lane = lax.broadcasted_iota(jnp.int32, slab.shape, 1)
    keep = (lane < HEAD_DIM) if half == 0 else (lane >= HEAD_DIM)
    own = jnp.where(keep, slab, 0.0)
    other = pltpu.roll(own, HEAD_DIM, 1)
    return (own, other) if half == 0 else (other, own)


def _slab_attention(q_ref, o_ref, k2, v2, valid, sink_ref):
    w = q_ref.shape[1]
    nk = k2.shape[0]
    nt = (((1,), (1,)), ((), ()))
    scale = HEAD_DIM ** -0.5
    ones = jnp.ones((nk, LANES), BF16)
    slabs = range(nk // LANES)
    heads = range(KV_HEADS)
    halves = [(e, c) for e in heads for c in range(2)]
    vs, ss = [], []
    for e in heads:
        slab, half = divmod(e, 2)
        k_lo, k_hi = _head_variants(k2[:, slab * LANES:(slab + 1) * LANES], half)
        vs.extend(v.astype(BF16) for v in _head_variants(v2[:, slab * LANES:(slab + 1) * LANES], half))
        kk = jnp.concatenate([k_lo, k_hi], axis=0).astype(BF16)
        c0 = 2 * e * LANES
        qq = jnp.concatenate([q_ref[0, :, c0:c0 + LANES], q_ref[0, :, c0 + LANES:c0 + 2 * LANES]], axis=0)
        ss.append(lax.dot_general((qq * scale).astype(BF16), kk, nt, preferred_element_type=F32))
    scs = [jnp.where(valid, ss[e][:, c * nk:(c + 1) * nk], -jnp.inf) for e, c in halves]
    sinks = [jnp.concatenate([jnp.full((w, LANES), sink_ref[Q_PER_KV * e + c], F32),
                              jnp.full((w, LANES), sink_ref[Q_PER_KV * e + 2 + c], F32)], axis=0) for e, c in halves]
    ms = [jnp.maximum(jnp.broadcast_to(jnp.max(sc, axis=1, keepdims=True), sk.shape), sk)
          for sc, sk in zip(scs, sinks)]
    ps = [jnp.concatenate([jnp.exp(sc[:, i * LANES:(i + 1) * LANES] - m) for i in slabs], axis=1).astype(BF16)
          for sc, m in zip(scs, ms)]
    dens = [jnp.dot(p, ones, preferred_element_type=F32) + jnp.exp(sk - m) for p, sk, m in zip(ps, sinks, ms)]
    outs = [jnp.dot(p, v, preferred_element_type=F32) * (1.0 / den) for p, v, den in zip(ps, vs, dens)]
    for e in heads:
        o = outs[2 * e] + outs[2 * e + 1]
        c0 = 2 * e * LANES
        o_ref[0, :, c0:c0 + LANES] = o[:w].astype(o_ref.dtype)
        o_ref[0, :, c0 + LANES:c0 + 2 * LANES] = o[w:].astype(o_ref.dtype)


def _attn_prompt_kernel(sink_ref, q_ref, kc_ref, kp_ref, vc_ref, vp_ref, o_ref):
    w = WINDOW
    has_prev = pl.program_id(1) > 0
    k2 = jnp.concatenate([jnp.where(has_prev, kp_ref[0], 0.0), kc_ref[0]], axis=0)
    v2 = jnp.concatenate([jnp.where(has_prev, vp_ref[0], 0.0), vc_ref[0]], axis=0)
    t = lax.broadcasted_iota(jnp.int32, (2 * w, 2 * w), 0) % w
    j = lax.broadcasted_iota(jnp.int32, (2 * w, 2 * w), 1)
    _slab_attention(q_ref, o_ref, k2, v2, (j > t) & (j <= t + w), sink_ref)


def _attn_prompt(qkv, sinks):
    b_, l, _ = qkv.shape
    w = WINDOW
    qw = ATT_HEADS * HEAD_DIM
    kvw = KV_HEADS * HEAD_DIM
    k_blk = qw // kvw
    prev = lambda n: jnp.maximum(n - 1, 0)
    return pl.pallas_call(
        _attn_prompt_kernel,
        grid=(b_, l // w),
        in_specs=[pl.BlockSpec(memory_space=pltpu.SMEM),
                  pl.BlockSpec((1, w, qw), lambda b, n: (b, n, 0)),
                  pl.BlockSpec((1, w, kvw), lambda b, n: (b, n, k_blk)),
                  pl.BlockSpec((1, w, kvw), lambda b, n: (b, prev(n), k_blk)),
                  pl.BlockSpec((1, w, kvw), lambda b, n: (b, n, k_blk + 1)),
                  pl.BlockSpec((1, w, kvw), lambda b, n: (b, prev(n), k_blk + 1))],
        out_specs=pl.BlockSpec((1, w, qw), lambda b, n: (b, n, 0)),
        out_shape=jax.ShapeDtypeStruct((b_, l, qw), BF16),
        compiler_params=_cparams(("parallel", "parallel")),
        name="attn_prompt",
    )(sinks, qkv, qkv, qkv, qkv, qkv)


def _attn_sample_kernel(sink_ref, qkv_ref, ck_ref, cv_ref, o_ref, nk_ref, nv_ref):
    l = qkv_ref.shape[1]
    n_keep = ck_ref.shape[1]
    kcol = ATT_HEADS * HEAD_DIM
    vcol = kcol + KV_HEADS * HEAD_DIM
    qkv = qkv_ref[0]
    k_all = jnp.concatenate([ck_ref[0], qkv[:, kcol:vcol]], axis=0)
    v_all = jnp.concatenate([cv_ref[0], qkv[:, vcol:]], axis=0)
    nk_ref[0] = k_all[l:]
    nv_ref[0] = v_all[l:]
    nk = -(-(n_keep + l) // LANES) * LANES
    pad = jnp.zeros((nk - n_keep - l, k_all.shape[1]), F32)
    t = lax.broadcasted_iota(jnp.int32, (2 * l, nk), 0) % l
    j = lax.broadcasted_iota(jnp.int32, (2 * l, nk), 1)
    rel = jnp.where(j < n_keep, t + n_keep - j, t - (j - n_keep))
    valid = (rel >= 0) & (rel < WINDOW) & (j < n_keep + l)
    _slab_attention(qkv_ref, o_ref, jnp.concatenate([k_all, pad], axis=0), jnp.concatenate([v_all, pad], axis=0),
                    valid, sink_ref)


def _attn_sample(qkv, cache_k, cache_v, sinks):
    b_, l, n = qkv.shape
    n_keep, kvw = cache_k.shape[1:]
    return pl.pallas_call(
        _attn_sample_kernel,
        grid=(b_,),
        in_specs=[pl.BlockSpec(memory_space=pltpu.SMEM),
                  pl.BlockSpec((1, l, n), lambda b: (b, 0, 0)),
                  pl.BlockSpec((1, n_keep, kvw), lambda b: (b, 0, 0)),
                  pl.BlockSpec((1, n_keep, kvw), lambda b: (b, 0, 0))],
        out_specs=[pl.BlockSpec((1, l, ATT_HEADS * HEAD_DIM), lambda b: (b, 0, 0)),
                   pl.BlockSpec((1, n_keep, kvw), lambda b: (b, 0, 0)),
                   pl.BlockSpec((1, n_keep, kvw), lambda b: (b, 0, 0))],
        out_shape=[jax.ShapeDtypeStruct((b_, l, ATT_HEADS * HEAD_DIM), BF16),
                   jax.ShapeDtypeStruct((b_, n_keep, kvw), F32),
                   jax.ShapeDtypeStruct((b_, n_keep, kvw), F32)],
        compiler_params=_cparams(("parallel",)),
        name="attn_sample",
    )(sinks, qkv, cache_k, cache_v)


def _rope_tables(pos):
    half = HEAD_DIM // 2
    inv_freq = ROPE_THETA ** (-jnp.arange(half, dtype=F32) / half)
    ang = pos.astype(F32)[:, None] * inv_freq[None, :]
    cos = jnp.cos(ang)
    sin = jnp.sin(ang)
    reps = LANES // HEAD_DIM
    return (jnp.tile(jnp.concatenate([cos, cos], axis=1), (1, reps)),
            jnp.tile(jnp.concatenate([-sin, sin], axis=1), (1, reps)))


def _tail_pad(buf):
    return jnp.pad(buf, ((0, 0), (TAIL_ROWS - buf.shape[1], 0), (0, 0)))


def _trunk(x, mods, pos, st, pw, seq_len):
    g_, m, d = x.shape
    nseq = g_ * m // seq_len
    blk = _block_rows(m)
    chunk = min(seq_len, SSM_CHUNK)
    lru_rows = min(seq_len, LRU_ROWS)

    proj, dtp = _inproj(x, pw['norms'][0, 0:1], mods[0], pw['w_in_t'], SSM_INNER + SSM_CONV_DIM, SSM_HEADS,
                        blk['inproj'], INPROJ_COLS)
    n_main = proj.shape[-1]
    proj_s = proj.reshape(nseq, seq_len, n_main)
    dtp_s = dtp.reshape(nseq, seq_len, LANES)
    y_ssm, ssm_h = _ssd(proj_s, dtp_s, _tail_pad(st['ssm_conv']), st['ssm'].reshape(nseq, SSM_INNER, SSM_STATE),
                        pw['ssm_conv_w'], pw['ssm_conv_b'], pw['ssm_dt_bias'], pw['ssm_a_log'], pw['ssm_d_full'],
                        pw['ssm_norm'], pw['expand'], chunk)
    y_lru, lru_h = _lru(proj_s, _tail_pad(st['lru_conv']), st['lru'].reshape(nseq, 1, LRU_WIDTH),
                        pw['lru_conv_w'], pw['lru_conv_b'], pw['lru_w_a'], pw['lru_b_a'], pw['lru_w_x'],
                        pw['lru_b_x'], pw['lru_lambda'], lru_rows, 3)
    keep = SSM_CONV - 1
    new_ssm_conv = proj_s[:, seq_len - keep:, SSM_INNER:SSM_INNER + SSM_CONV_DIM]
    new_lru_conv = proj_s[:, seq_len - keep:, n_main - LRU_WIDTH:]
    x = _outproj([y_ssm.reshape(g_, m, SSM_INNER), y_lru.reshape(g_, m, LRU_WIDTH)],
                 pw['w_out_hyb'], x, pw['norms'][0, 1:2], mods[0], blk['outproj'])
    x = _mlp(x, pw['norms'][0, 2:3], pw['norms'][0, 3:4], mods[0], pw['w_up'], pw['w_down'], 0, blk['mlp'],
             MLP_COLS)

    cos_t, sin_t = _rope_tables(pos)
    if seq_len < blk['qkv']:
        cos_t = jnp.tile(cos_t, (blk['qkv'] // seq_len, 1))
        sin_t = jnp.tile(sin_t, (blk['qkv'] // seq_len, 1))
    qkv = _qkv(x, pw['norms'][1, 0:1], mods[1], pw['w_qkv'], pw['b_qkv'], cos_t, sin_t, blk['qkv'], QKV_COLS)
    qkv_s = qkv.reshape(nseq, seq_len, qkv.shape[-1])
    kcol = ATT_HEADS * HEAD_DIM
    vcol = kcol + KV_HEADS * HEAD_DIM
    if st['k'] is None:
        o = _attn_prompt(qkv_s, pw['attn_sinks'])
        n_keep = min(WINDOW, seq_len)
        k_new = qkv_s[:, seq_len - n_keep:, kcol:vcol]
        v_new = qkv_s[:, seq_len - n_keep:, vcol:]
    else:
        n_keep = st['k'].shape[1]
        o, k_new, v_new = _attn_sample(qkv_s, st['k'].reshape(nseq, n_keep, KV_HEADS * HEAD_DIM),
                                       st['v'].reshape(nseq, n_keep, KV_HEADS * HEAD_DIM), pw['attn_sinks'])
    k_new = k_new.reshape(nseq, n_keep, KV_HEADS, HEAD_DIM)
    v_new = v_new.reshape(nseq, n_keep, KV_HEADS, HEAD_DIM)
    x = _outproj([o.reshape(g_, m, kcol)], pw['w_out_attn'], x, pw['norms'][1, 1:2], mods[1], blk['outproj'])
    x = _mlp(x, pw['norms'][1, 2:3], pw['norms'][1, 3:4], mods[1], pw['w_up'], pw['w_down'], 1, blk['mlp'],
             MLP_COLS)

    states = (new_ssm_conv[None], ssm_h.reshape(1, nseq, SSM_HEADS, SSM_HEAD_DIM, SSM_STATE),
              new_lru_conv[None], lru_h.reshape(1, nseq, LRU_WIDTH), k_new[None], v_new[None])
    return x, states


def kernel(x_prompt, x_sample, state_ssm_conv, state_ssm, state_lru_conv, state_lru, cache_k, cache_v, c_prompt, c_sample, w_mod, b_mod, norms, w_in_hyb, ssm_conv_w, ssm_conv_b, ssm_dt_bias, ssm_a_log, ssm_d, ssm_norm, lru_conv_w, lru_conv_b, lru_w_a, lru_b_a, lru_w_x, lru_b_x, lru_lambda, w_out_hyb, w_qkv, b_qkv, attn_sinks, w_out_attn, w_up, w_down):
    bp, lp, d = x_prompt.shape
    bs, ls, _ = x_sample.shape
    depth = w_mod.shape[0]

    pad_lanes = lambda v: jnp.pad(v.reshape(1, -1), ((0, 0), (0, LANES - v.size)))
    head_of_lane = jnp.arange(SSM_INNER) // SSM_HEAD_DIM
    pw = {
        'norms': norms,
        'w_in_t': jnp.swapaxes(w_in_hyb, 1, 2),
        'ssm_conv_w': ssm_conv_w[0], 'ssm_conv_b': ssm_conv_b[0].reshape(1, -1),
        'ssm_dt_bias': pad_lanes(ssm_dt_bias[0]), 'ssm_a_log': pad_lanes(ssm_a_log[0]),
        'ssm_d_full': jnp.repeat(ssm_d[0], SSM_HEAD_DIM).reshape(1, -1),
        'ssm_norm': ssm_norm[0].reshape(1, -1),
        'expand': (jnp.arange(LANES)[:, None] == head_of_lane[None, :]).astype(BF16),
        'lru_conv_w': lru_conv_w[0], 'lru_conv_b': lru_conv_b[0].reshape(1, -1),
        'lru_w_a': lru_w_a[0].astype(BF16), 'lru_b_a': lru_b_a[0].reshape(1, -1),
        'lru_w_x': lru_w_x[0].astype(BF16), 'lru_b_x': lru_b_x[0].reshape(1, -1),
        'lru_lambda': lru_lambda[0].reshape(1, -1),
        'w_out_hyb': w_out_hyb[0].astype(BF16),
        'w_qkv': w_qkv[0].astype(BF16), 'b_qkv': b_qkv[0].reshape(1, -1),
        'attn_sinks': attn_sinks[0],
        'w_out_attn': w_out_attn[0].astype(BF16),
        'w_up': w_up, 'w_down': w_down,
    }

    mod = _adaln(jnp.concatenate([c_prompt, c_sample], axis=0), w_mod, b_mod)
    mods_p = [mod[l, :bp][:, None, :] for l in range(depth)]
    mods_s = [jnp.repeat(mod[l, bp:], ls, axis=0)[None] for l in range(depth)]

    zeros = lambda *shape: jnp.zeros(shape, F32)
    st_p = {'ssm_conv': zeros(bp, SSM_CONV - 1, SSM_CONV_DIM), 'ssm': zeros(bp, SSM_INNER, SSM_STATE),
            'lru_conv': zeros(bp, SSM_CONV - 1, LRU_WIDTH), 'lru': zeros(bp, LRU_WIDTH), 'k': None, 'v': None}
    st_s = {'ssm_conv': state_ssm_conv[0], 'ssm': state_ssm[0], 'lru_conv': state_lru_conv[0],
            'lru': state_lru[0], 'k': cache_k[0], 'v': cache_v[0]}

    pos_p = jnp.arange(lp, dtype=jnp.int32)
    pos_s = PAST_LEN + jnp.arange(ls, dtype=jnp.int32)
    y_p, sp = _trunk(x_prompt, mods_p, pos_p, st_p, pw, lp)
    y_s, ss = _trunk(x_sample.reshape(1, bs * ls, d), mods_s, pos_s, st_s, pw, ls)
    return (y_p, y_s.reshape(bs, ls, d)) + sp + ss
```

```python
import functools
import math

import jax
import jax.numpy as jnp
from jax import lax
from jax.experimental import pallas as pl
from jax.experimental.pallas import tpu as pltpu

F32 = jnp.float32
BF16 = jnp.bfloat16

D_MODEL = 2048
PAST_LEN = 16384
SSM_HEADS = 32
SSM_HEAD_DIM = 64
SSM_INNER = SSM_HEADS * SSM_HEAD_DIM
SSM_GROUPS = 8
SSM_STATE = 128
SSM_CONV = 4
SSM_CHUNK = 128
SSM_CONV_DIM = SSM_INNER + 2 * SSM_GROUPS * SSM_STATE
HEADS_PER_GROUP = SSM_HEADS // SSM_GROUPS
GROUP_WIDTH = HEADS_PER_GROUP * SSM_HEAD_DIM
LRU_WIDTH = D_MODEL
LRU_BLOCKS = 8
LRU_BLOCK = LRU_WIDTH // LRU_BLOCKS
LRU_C = 8.0
ATT_HEADS = 32
KV_HEADS = 8
HEAD_DIM = 64
Q_PER_KV = ATT_HEADS // KV_HEADS
WINDOW = 128
ROPE_THETA = 10000.0
D_FF = 4 * D_MODEL
EPS = 1e-6

LANES = 128
SUBLANES = 8
BF16_SUBLANES = 16
TAIL_ROWS = SUBLANES
VMEM_LIMIT = 56 * 1024 * 1024


INPROJ_COLS = 1024
MLP_COLS = 512
QKV_COLS = 512
LRU_ROWS = 512


def _block_rows(m):
    return {'inproj': min(m, 1024), 'mlp': min(m, 1024), 'qkv': min(m, 512), 'outproj': min(m, 512)}


def _cparams(sem):
    return pltpu.CompilerParams(dimension_semantics=sem, vmem_limit_bytes=VMEM_LIMIT)


def _rms(x, g):
    return x * lax.rsqrt(jnp.mean(x * x, axis=-1, keepdims=True) + EPS) * g


ROW_CHUNK = 16
ROW_UNROLL = 4


def _for_row_chunks(n_rows, body):
    def step(i, carry):
        body(pl.multiple_of(i * ROW_CHUNK, ROW_CHUNK))
        return carry

    lax.fori_loop(0, n_rows // ROW_CHUNK, step, 0, unroll=ROW_UNROLL)


def _mod_rows(ref, r0):
    return ref[0] if ref.shape[1] == 1 else ref[0, pl.ds(r0, ROW_CHUNK), :]


def _prenorm_to(h_scr, x_ref, g_ref, sc_ref, sh_ref):
    def body(r0):
        x = x_ref[0, pl.ds(r0, ROW_CHUNK), :]
        h = _rms(x, g_ref[...]) * (1.0 + _mod_rows(sc_ref, r0)) + _mod_rows(sh_ref, r0)
        h_scr[pl.ds(r0, ROW_CHUNK), :] = h.astype(BF16)

    _for_row_chunks(x_ref.shape[1], body)


def _sigmoid(x):
    return 0.5 * (jnp.tanh(0.5 * x) + 1.0)


def _silu(x):
    return x * _sigmoid(x)


def _softplus(x):
    return jnp.maximum(x, 0.0) + jnp.log1p(jnp.exp(-jnp.abs(x)))


def _sqrt_neg_expm1(x):
    s = jnp.tanh(-0.5 * x)
    t = (2.0 * s) * (1.0 + s)
    return jnp.where(t > 0.0, (2.0 * s) * lax.rsqrt(t), 0.0)


def _split3(x):
    hi = x.astype(BF16)
    r = x - hi.astype(F32)
    mid = r.astype(BF16)
    lo = (r - mid.astype(F32)).astype(BF16)
    return [hi, mid, lo]


def _adaln_kernel(c_ref, w_ref, b_ref, o_ref):
    s = _silu(c_ref[...]).astype(BF16)
    o_ref[0] = jnp.dot(s, w_ref[0].astype(BF16), preferred_element_type=F32) + b_ref[0]


def _adaln(c_all, w_mod, b_mod):
    nb, d = c_all.shape
    depth, _, n = w_mod.shape
    bn = 1024
    return pl.pallas_call(
        _adaln_kernel,
        grid=(depth, n // bn),
        in_specs=[pl.BlockSpec((nb, d), lambda l, j: (0, 0)),
                  pl.BlockSpec((1, d, bn), lambda l, j: (l, 0, j)),
                  pl.BlockSpec((1, 1, bn), lambda l, j: (l, 0, j))],
        out_specs=pl.BlockSpec((1, nb, bn), lambda l, j: (l, 0, j)),
        out_shape=jax.ShapeDtypeStruct((depth, nb, n), F32),
        compiler_params=_cparams(("parallel", "parallel")),
        name="adaln",
    )(c_all, w_mod, b_mod.reshape(depth, 1, n))


def _mod_spec(mod, bm, k):
    if mod.shape[1] == 1:
        return pl.BlockSpec((1, 1, D_MODEL), lambda g, i, *_: (g, 0, k))
    return pl.BlockSpec((1, bm, D_MODEL), lambda g, i, *_: (g, i, k))


def _inproj_kernel(x_ref, g_ref, sh_ref, sc_ref, wt_ref, wdt_ref, o_ref, dt_ref, h_scr, *, n_dt):
    nt = (((1,), (1,)), ((), ()))

    @pl.when(pl.program_id(2) == 0)
    def _():
        _prenorm_to(h_scr, x_ref, g_ref, sc_ref, sh_ref)
        dt = lax.dot_general(h_scr[...], wdt_ref[...].astype(BF16), nt, preferred_element_type=F32)
        lane = lax.broadcasted_iota(jnp.int32, dt.shape, 1)
        dt_ref[0] = jnp.where(lane < n_dt, dt, 0.0)

    o_ref[0] = lax.dot_general(h_scr[...], wt_ref[...].astype(BF16), nt, preferred_element_type=F32)


def _inproj(x, gain, mod, w_t, c_dt, n_dt, bm, bn):
    g_, m, d = x.shape
    n = w_t.shape[1] - n_dt
    row0 = lambda j: pl.multiple_of(jnp.where(j * bn < c_dt, j * bn, j * bn + n_dt), SUBLANES)
    return pl.pallas_call(
        functools.partial(_inproj_kernel, n_dt=n_dt),
        grid=(g_, m // bm, n // bn),
        in_specs=[pl.BlockSpec((1, bm, d), lambda g, i, j: (g, i, 0), pipeline_mode=pl.Buffered(1)),
                  pl.BlockSpec((1, d), lambda g, i, j: (0, 0)),
                  _mod_spec(mod, bm, 0), _mod_spec(mod, bm, 1),
                  pl.BlockSpec((None, pl.Element(bn), pl.Element(d)), lambda g, i, j: (0, row0(j), 0)),
                  pl.BlockSpec((None, pl.Element(LANES), pl.Element(d)), lambda g, i, j: (0, c_dt, 0))],
        out_specs=[pl.BlockSpec((1, bm, bn), lambda g, i, j: (g, i, j)),
                   pl.BlockSpec((1, bm, LANES), lambda g, i, j: (g, i, 0))],
        out_shape=[jax.ShapeDtypeStruct((g_, m, n), F32),
                   jax.ShapeDtypeStruct((g_, m, LANES), F32)],
        scratch_shapes=[pltpu.VMEM((bm, d), BF16)],
        compiler_params=_cparams(("parallel", "parallel", "arbitrary")),
        name="inproj",
    )(x, gain, mod, mod, w_t, w_t)


def _swap_halves(x):
    lane = lax.broadcasted_iota(jnp.int32, x.shape, 1)
    first = (lane % HEAD_DIM) < (HEAD_DIM // 2)
    return jnp.where(first, pltpu.roll(x, LANES - HEAD_DIM // 2, 1), pltpu.roll(x, HEAD_DIM // 2, 1))


def _qkv_kernel(x_ref, g_ref, sh_ref, sc_ref, w_ref, b_ref, cos_ref, sin_ref, o_ref, h_scr, *, rope_cols, bn):
    _prenorm_to(h_scr, x_ref, g_ref, sc_ref, sh_ref)
    h = h_scr[...]
    cos = cos_ref[...]
    sin = sin_ref[...]
    for c0 in range(0, w_ref.shape[1], bn):
        acc = jnp.dot(h, w_ref[:, c0:c0 + bn].astype(BF16), preferred_element_type=F32) + b_ref[:, c0:c0 + bn]
        if c0 < rope_cols:
            for c in range(c0, c0 + bn, LANES):
                a = acc[:, c - c0:c - c0 + LANES]
                o_ref[0, :, c:c + LANES] = a * cos + _swap_halves(a) * sin
        else:
            o_ref[0, :, c0:c0 + bn] = acc


def _qkv(x, gain, mod, w, b, cos_t, sin_t, bm, bn):
    g_, m, d = x.shape
    n = w.shape[1]
    rope_cols = (ATT_HEADS + KV_HEADS) * HEAD_DIM
    rows_per_g = cos_t.shape[0] // bm
    return pl.pallas_call(
        functools.partial(_qkv_kernel, rope_cols=rope_cols, bn=bn),
        grid=(g_, m // bm),
        in_specs=[pl.BlockSpec((1, bm, d), lambda g, i: (g, i, 0)),
                  pl.BlockSpec((1, d), lambda g, i: (0, 0)),
                  _mod_spec(mod, bm, 0), _mod_spec(mod, bm, 1),
                  pl.BlockSpec((d, n), lambda g, i: (0, 0), pipeline_mode=pl.Buffered(1)),
                  pl.BlockSpec((1, n), lambda g, i: (0, 0)),
                  pl.BlockSpec((bm, LANES), lambda g, i: (i % rows_per_g, 0)),
                  pl.BlockSpec((bm, LANES), lambda g, i: (i % rows_per_g, 0))],
        out_specs=pl.BlockSpec((1, bm, n), lambda g, i: (g, i, 0)),
        out_shape=jax.ShapeDtypeStruct((g_, m, n), F32),
        scratch_shapes=[pltpu.VMEM((bm, d), BF16)],
        compiler_params=_cparams(("parallel", "parallel")),
        name="qkv",
    )(x, gain, mod, mod, w, b, cos_t, sin_t)


def _outproj_kernel(*refs, n_lhs):
    lhs = refs[:n_lhs]
    ws = refs[n_lhs:2 * n_lhs]
    x_ref, g_ref, gt_ref, o_ref = refs[2 * n_lhs:]
    acc = jnp.dot(lhs[0][0], ws[0][...].astype(BF16), preferred_element_type=F32)
    for a, w in zip(lhs[1:], ws[1:]):
        acc = acc + jnp.dot(a[0], w[...].astype(BF16), preferred_element_type=F32)
    o_ref[0] = x_ref[0] + gt_ref[0] * _rms(acc, g_ref[...])


def _outproj(lhs_list, w, x, gain, mod, bm):
    g_, m, d = x.shape
    n_lhs = len(lhs_list)
    kw = w.shape[0] // n_lhs
    lhs_specs = [pl.BlockSpec((1, bm, kw), lambda g, i: (g, i, 0)) for _ in lhs_list]
    w_specs = [pl.BlockSpec((kw, d), lambda g, i, k=k: (k, 0), pipeline_mode=pl.Buffered(1)) for k in range(n_lhs)]
    w_list = [w] * n_lhs
    if mod.shape[1] == 1:
        gt_spec = pl.BlockSpec((1, 1, d), lambda g, i: (g, 0, 2))
    else:
        gt_spec = pl.BlockSpec((1, bm, d), lambda g, i: (g, i, 2))
    return pl.pallas_call(
        functools.partial(_outproj_kernel, n_lhs=n_lhs),
        grid=(g_, m // bm),
        in_specs=lhs_specs + w_specs + [pl.BlockSpec((1, bm, d), lambda g, i: (g, i, 0)),
                                        pl.BlockSpec((1, d), lambda g, i: (0, 0)), gt_spec],
        out_specs=pl.BlockSpec((1, bm, d), lambda g, i: (g, i, 0)),
        out_shape=jax.ShapeDtypeStruct((g_, m, d), F32),
        compiler_params=_cparams(("parallel", "parallel")),
        name="outproj",
    )(*lhs_list, *w_list, x, gain, mod)


def _mlp_kernel(x_ref, g2_ref, sh_ref, sc_ref, gt_ref, wu_ref, wd_ref, g3_ref, o_ref, h_scr):
    f = pl.program_id(2)

    @pl.when(f == 0)
    def _():
        _prenorm_to(h_scr, x_ref, g2_ref, sc_ref, sh_ref)
        o_ref[0] = jnp.zeros(o_ref.shape[1:], F32)

    u = jnp.dot(h_scr[...], wu_ref[...].astype(BF16), preferred_element_type=F32)
    u = jnp.square(jnp.maximum(u, 0.0)).astype(BF16)
    o_ref[0] += jnp.dot(u, wd_ref[...].astype(BF16), preferred_element_type=F32)

    @pl.when(f == pl.num_programs(2) - 1)
    def _():
        o_ref[0] = x_ref[0] + gt_ref[0] * _rms(o_ref[0], g3_ref[...])


def _mlp(x, g2, g3, mod, w_up, w_down, layer, bm, bf):
    g_, m, d = x.shape
    dff = w_up.shape[2]
    if mod.shape[1] == 1:
        mspec = lambda k: pl.BlockSpec((1, 1, d), lambda g, i, f: (g, 0, k))
    else:
        mspec = lambda k: pl.BlockSpec((1, bm, d), lambda g, i, f: (g, i, k))
    return pl.pallas_call(
        _mlp_kernel,
        grid=(g_, m // bm, dff // bf),
        in_specs=[pl.BlockSpec((1, bm, d), lambda g, i, f: (g, i, 0), pipeline_mode=pl.Buffered(1)),
                  pl.BlockSpec((1, d), lambda g, i, f: (0, 0)),
                  mspec(3), mspec(4), mspec(5),
                  pl.BlockSpec((None, d, bf), lambda g, i, f: (layer, 0, f)),
                  pl.BlockSpec((None, bf, d), lambda g, i, f: (layer, f, 0)),
                  pl.BlockSpec((1, d), lambda g, i, f: (0, 0))],
        out_specs=pl.BlockSpec((1, bm, d), lambda g, i, f: (g, i, 0)),
        out_shape=jax.ShapeDtypeStruct((g_, m, d), F32),
        scratch_shapes=[pltpu.VMEM((bm, d), BF16)],
        compiler_params=_cparams(("parallel", "parallel", "arbitrary")),
        name="mlp",
    )(x, g2, mod, mod, mod, w_up, w_down, g3)


def _dwconv(raw, tail, w, b):
    t = raw.shape[0]
    assert w.shape[0] == 4
    ext = jnp.concatenate([tail, raw], axis=0)
    z1 = pltpu.roll(ext, 1, 0)
    near = ext * w[3:4] + z1 * w[2:3]
    far = ext * w[1:2] + z1 * w[0:1]
    y = near + pltpu.roll(far, 2, 0)
    return y[TAIL_ROWS:] + b, ext[t:t + TAIL_ROWS]


def _cumsum_rows(x):
    n = x.shape[0]
    row = lax.broadcasted_iota(jnp.int32, x.shape, 0)
    d = 1
    while d < n:
        x = x + jnp.where(row >= d, pltpu.roll(x, d, 0), 0.0)
        d *= 2
    return x


def _ssd_kernel(z_ref, xs_ref, bc_ref, dt_ref, buf_ref, h0_ref, cw_ref, cb_ref, dtb_ref, alog_ref,
                dsk_ref, ng_ref, exp_ref, y_ref, st_ref, tail_scr, *, q_valid, q):
    c = pl.program_id(1)

    @pl.when(c == 0)
    def _():
        tail_scr[...] = buf_ref[0]
        st_ref[0] = h0_ref[0]

    def padded(v):
        if q_valid == q:
            return v
        return jnp.concatenate([v, jnp.zeros((q - q_valid, v.shape[1]), v.dtype)], axis=0)

    def transposed(v):
        if q == LANES:
            return v.T
        return jnp.concatenate([v, jnp.zeros((LANES - q, LANES), v.dtype)], axis=0).T[:, :q]

    cw = cw_ref[...]
    cb = cb_ref[...]
    xs, tail_x = _dwconv(xs_ref[0], tail_scr[:, :SSM_INNER], cw[:, :SSM_INNER], cb[:, :SSM_INNER])
    bcm, tail_b = _dwconv(bc_ref[0], tail_scr[:, SSM_INNER:], cw[:, SSM_INNER:], cb[:, SSM_INNER:])
    tail_scr[:, :SSM_INNER] = tail_x
    tail_scr[:, SSM_INNER:] = tail_b
    xs = padded(_silu(xs))
    bcm = padded(_silu(bcm))
    z = padded(z_ref[0])

    dt = _softplus(dt_ref[0] + dtb_ref[...])
    dt = padded(dt)
    a = -jnp.exp(alog_ref[...])
    acs = _cumsum_rows(dt * a)
    acs_t = transposed(acs)

    full = jnp.dot(jnp.concatenate(_split3(dt) + _split3(acs), axis=0), exp_ref[...], preferred_element_type=F32)
    dt_full = (full[0:q] + full[q:2 * q]) + full[2 * q:3 * q]
    acs_full = (full[3 * q:4 * q] + full[4 * q:5 * q]) + full[5 * q:6 * q]
    xdt = xs * dt_full
    dec_out = jnp.exp(acs_full)
    xw = (xdt * jnp.exp(acs_full[q - 1:q] - acs_full)).astype(BF16)
    xdt_b = xdt.astype(BF16)
    bc_b = bcm.astype(BF16)

    row = lax.broadcasted_iota(jnp.int32, (q, q), 0)
    col = lax.broadcasted_iota(jnp.int32, (q, q), 1)
    causal = row >= col
    lane = lax.broadcasted_iota(jnp.int32, (q, LANES), 1)
    lo_half = lane < SSM_HEAD_DIM
    nt = (((1,), (1,)), ((), ()))
    tn = (((0,), (0,)), ((), ()))
    n_bc = SSM_GROUPS * SSM_STATE

    groups = range(SSM_GROUPS)
    cols = [slice(g * GROUP_WIDTH, (g + 1) * GROUP_WIDTH) for g in groups]
    b_gs = [bc_b[:, g * SSM_STATE:(g + 1) * SSM_STATE] for g in groups]
    c_gs = [bc_b[:, n_bc + g * SSM_STATE:n_bc + (g + 1) * SSM_STATE] for g in groups]
    h_gs = [st_ref[0, cols[g], :] for g in groups]
    cbms = [lax.dot_general(c_gs[g], b_gs[g], nt, preferred_element_type=F32) for g in groups]
    y_offs = [lax.dot_general(c_gs[g], h_gs[g].astype(BF16), nt, preferred_element_type=F32) for g in groups]
    st_news = [lax.dot_general(xw[:, cols[g]], b_gs[g], tn, preferred_element_type=F32) for g in groups]
    ms = [(cbms[h // HEADS_PER_GROUP]
           * jnp.exp(jnp.where(causal, acs[:, h:h + 1] - acs_t[h:h + 1, :], -jnp.inf))).astype(BF16)
          for h in range(SSM_HEADS)]
    y_diags = []
    for pair in range(SSM_HEADS // 2):
        x_pair = xdt_b[:, pair * LANES:(pair + 1) * LANES]
        zero = jnp.zeros_like(x_pair)
        y_diags.append(jnp.dot(ms[2 * pair], jnp.where(lo_half, x_pair, zero), preferred_element_type=F32)
                       + jnp.dot(ms[2 * pair + 1], jnp.where(lo_half, zero, x_pair), preferred_element_type=F32))
    pairs_per_group = HEADS_PER_GROUP // 2
    for g in groups:
        y_g = y_offs[g] * dec_out[:, cols[g]]
        y_g = y_g + jnp.concatenate(y_diags[g * pairs_per_group:(g + 1) * pairs_per_group], axis=1)
        y_g = y_g + xs[:, cols[g]] * dsk_ref[:, cols[g]]
        y_g = y_g * _silu(z[:, cols[g]])
        y_g = y_g * lax.rsqrt(jnp.mean(y_g * y_g, axis=-1, keepdims=True) + EPS)
        y_g = y_g * ng_ref[:, cols[g]]
        y_ref[0, :, cols[g]] = y_g[:q_valid].astype(y_ref.dtype)
    for h in range(SSM_HEADS):
        g, r = divmod(h, HEADS_PER_GROUP)
        rows = slice(r * SSM_HEAD_DIM, (r + 1) * SSM_HEAD_DIM)
        decay = jnp.exp(acs_t[h:h + 1, q - 1:q])
        st_ref[0, h * SSM_HEAD_DIM:(h + 1) * SSM_HEAD_DIM, :] = h_gs[g][rows] * decay + st_news[g][rows]


def _ssd(proj, dtp, buf8, h0, conv_w, conv_b, dt_bias, a_log, d_full, norm_g, expand, rows):
    b_, l, _ = proj.shape
    nc = l // rows
    st_rows = SSM_INNER
    return pl.pallas_call(
        functools.partial(_ssd_kernel, q_valid=rows, q=max(rows, BF16_SUBLANES)),
        grid=(b_, nc),
        in_specs=[pl.BlockSpec((1, rows, SSM_INNER), lambda b, c: (b, c, 0)),
                  pl.BlockSpec((1, rows, SSM_INNER), lambda b, c: (b, c, 1)),
                  pl.BlockSpec((1, rows, SSM_INNER), lambda b, c: (b, c, 2)),
                  pl.BlockSpec((1, rows, LANES), lambda b, c: (b, c, 0)),
                  pl.BlockSpec((1, TAIL_ROWS, SSM_CONV_DIM), lambda b, c: (b, 0, 0)),
                  pl.BlockSpec((1, st_rows, SSM_STATE), lambda b, c: (b, 0, 0)),
                  pl.BlockSpec((SSM_CONV, SSM_CONV_DIM), lambda b, c: (0, 0)),
                  pl.BlockSpec((1, SSM_CONV_DIM), lambda b, c: (0, 0)),
                  pl.BlockSpec((1, LANES), lambda b, c: (0, 0)),
                  pl.BlockSpec((1, LANES), lambda b, c: (0, 0)),
                  pl.BlockSpec((1, SSM_INNER), lambda b, c: (0, 0)),
                  pl.BlockSpec((1, SSM_INNER), lambda b, c: (0, 0)),
                  pl.BlockSpec((LANES, SSM_INNER), lambda b, c: (0, 0))],
        out_specs=[pl.BlockSpec((1, rows, SSM_INNER), lambda b, c: (b, c, 0)),
                   pl.BlockSpec((1, st_rows, SSM_STATE), lambda b, c: (b, 0, 0))],
        out_shape=[jax.ShapeDtypeStruct((b_, l, SSM_INNER), BF16),
                   jax.ShapeDtypeStruct((b_, st_rows, SSM_STATE), F32)],
        scratch_shapes=[pltpu.VMEM((TAIL_ROWS, SSM_CONV_DIM), F32)],
        compiler_params=_cparams(("parallel", "arbitrary")),
        name="ssd",
    )(proj, proj, proj, dtp, buf8, h0, conv_w, conv_b, dt_bias, a_log, d_full, norm_g, expand)


SCAN_SEG = 4
SCAN_ROWS = SUBLANES * SCAN_SEG


def _shift_rows_fill(x, d, fill, row):
    return jnp.where(row >= d, pltpu.roll(x, d, 0), fill)


def _sublane_scan(e, pe, row):
    d = 1
    while d < SUBLANES:
        e = e + pe * _shift_rows_fill(e, d, 0.0, row)
        pe = pe * _shift_rows_fill(pe, d, 1.0, row)
        d *= 2
    return e, pe


def _linear_scan(a_scr, u_scr, h0):
    n_slab, t, _ = a_scr.shape
    seg = SCAN_SEG if t % SCAN_ROWS == 0 else 1
    row = lax.broadcasted_iota(jnp.int32, (SUBLANES, LANES), 0)

    def block(i, h_prev):
        base = pl.multiple_of(i * (SUBLANES * seg), SUBLANES)
        out = []
        for c in range(n_slab):
            rows = [pl.ds(base + j, SUBLANES, stride=seg) if seg > 1 else pl.ds(base, SUBLANES) for j in range(seg)]
            hs, ps = [], []
            for j in range(seg):
                a, u = a_scr[c, rows[j], :], u_scr[c, rows[j], :]
                hs.append(u if j == 0 else a * hs[-1] + u)
                ps.append(a if j == 0 else a * ps[-1])
            e, pe = _sublane_scan(hs[-1], ps[-1], row)
            g = e + pe * h_prev[c]
            carry = jnp.where(row == 0, h_prev[c], pltpu.roll(g, 1, 0))
            for j in range(seg):
                u_scr[c, rows[j], :] = hs[j] + ps[j] * carry
            out.append(jnp.broadcast_to(g[SUBLANES - 1:], g.shape))
        return tuple(out)

    h_init = tuple(jnp.broadcast_to(h0[:, c * LANES:(c + 1) * LANES], (SUBLANES, LANES)) for c in range(n_slab))
    h_fin = lax.fori_loop(0, t // (SUBLANES * seg), block, h_init)
    return jnp.concatenate([h[:1] for h in h_fin], axis=1)


def _gelu_tanh(x):
    c0 = math.sqrt(2.0 / math.pi)
    hx = 0.5 * x
    return hx + hx * jnp.tanh(x * (c0 + (c0 * 0.044715) * (x * x)))


def _lru_kernel(gate_ref, xr_ref, buf_ref, h0_ref, cw_ref, cb_ref, wa_ref, ba_ref, wx_ref, bx_ref,
                lam_ref, y_ref, hl_ref, tail_scr, a_scr, u_scr):
    c = pl.program_id(1)

    @pl.when(c == 0)
    def _():
        tail_scr[...] = buf_ref[0]
        hl_ref[0] = h0_ref[0]

    xc, tail = _dwconv(xr_ref[0], tail_scr[...], cw_ref[...], cb_ref[...])
    tail_scr[...] = tail
    sp = _softplus(-lam_ref[...])
    slabs_per_block = LRU_BLOCK // LANES
    for k in range(LRU_BLOCKS):
        k0 = k * LRU_BLOCK
        xk = xc[:, k0:k0 + LRU_BLOCK]
        xkb = xk.astype(BF16)
        gr = _sigmoid(jnp.dot(xkb, wa_ref[k], preferred_element_type=F32) + ba_ref[:, k0:k0 + LRU_BLOCK])
        gi = _sigmoid(jnp.dot(xkb, wx_ref[k], preferred_element_type=F32) + bx_ref[:, k0:k0 + LRU_BLOCK])
        log_a = -LRU_C * gr * sp[:, k0:k0 + LRU_BLOCK]
        a = jnp.exp(log_a)
        u = _sqrt_neg_expm1(2.0 * log_a) * gi * xk
        for s in range(slabs_per_block):
            a_scr[k * slabs_per_block + s] = a[:, s * LANES:(s + 1) * LANES]
            u_scr[k * slabs_per_block + s] = u[:, s * LANES:(s + 1) * LANES]

    hl_ref[0] = _linear_scan(a_scr, u_scr, hl_ref[0])
    for s in range(a_scr.shape[0]):
        gate = gate_ref[0, :, s * LANES:(s + 1) * LANES]
        y_ref[0, :, s * LANES:(s + 1) * LANES] = (u_scr[s] * _gelu_tanh(gate)).astype(y_ref.dtype)


def _lru(proj, buf8, h0, conv_w, conv_b, w_a, b_a, w_x, b_x, lam, rows, col0):
    b_, l, _ = proj.shape
    w = LRU_WIDTH
    full2 = lambda shape: pl.BlockSpec(shape, lambda b, c: (0,) * len(shape))
    return pl.pallas_call(
        _lru_kernel,
        grid=(b_, l // rows),
        in_specs=[pl.BlockSpec((1, rows, w), lambda b, c: (b, c, col0)),
                  pl.BlockSpec((1, rows, w), lambda b, c: (b, c, col0 + 1)),
                  pl.BlockSpec((1, TAIL_ROWS, w), lambda b, c: (b, 0, 0)),
                  pl.BlockSpec((1, 1, w), lambda b, c: (b, 0, 0)),
                  full2((SSM_CONV, w)), full2((1, w)),
                  full2((LRU_BLOCKS, LRU_BLOCK, LRU_BLOCK)), full2((1, w)),
                  full2((LRU_BLOCKS, LRU_BLOCK, LRU_BLOCK)), full2((1, w)),
                  full2((1, w))],
        out_specs=[pl.BlockSpec((1, rows, w), lambda b, c: (b, c, 0)),
                   pl.BlockSpec((1, 1, w), lambda b, c: (b, 0, 0))],
        out_shape=[jax.ShapeDtypeStruct((b_, l, w), BF16),
                   jax.ShapeDtypeStruct((b_, 1, w), F32)],
        scratch_shapes=[pltpu.VMEM((TAIL_ROWS, w), F32), pltpu.VMEM((w // LANES, rows, LANES), F32),
                        pltpu.VMEM((w // LANES, rows, LANES), F32)],
        compiler_params=_cparams(("parallel", "arbitrary")),
        name="lru",
    )(proj, proj, buf8, h0, conv_w, conv_b, w_a, b_a, w_x, b_x, lam)


def _head_variants(slab, half):
    lane = lax.broadcasted_iota(jnp.int32, slab.shape, 1)
    keep = (lane < HEAD_DIM) if half == 0 else (lane >= HEAD_DIM)
    own = jnp.where(keep, slab, 0.0)
    other = pltpu.roll(own, HEAD_DIM, 1)
    return (own, other) if half == 0 else (other, own)


def _slab_attention(q_ref, o_ref, k2, v2, valid, sink_ref):
    w = q_ref.shape[1]
    nk = k2.shape[0]
    nt = (((1,), (1,)), ((), ()))
    scale = HEAD_DIM ** -0.5
    ones = jnp.ones((nk, LANES), BF16)
    slabs = range(nk // LANES)
    heads = range(KV_HEADS)
    halves = [(e, c) for e in heads for c in range(2)]
    vs, ss = [], []
    for e in heads:
        slab, half = divmod(e, 2)
        k_lo, k_hi = _head_variants(k2[:, slab * LANES:(slab + 1) * LANES], half)
        vs.extend(v.astype(BF16) for v in _head_variants(v2[:, slab * LANES:(slab + 1) * LANES], half))
        kk = jnp.concatenate([k_lo, k_hi], axis=0).astype(BF16)
        c0 = 2 * e * LANES
        qq = jnp.concatenate([q_ref[0, :, c0:c0 + LANES], q_ref[0, :, c0 + LANES:c0 + 2 * LANES]], axis=0)
        ss.append(lax.dot_general((qq * scale).astype(BF16), kk, nt, preferred_element_type=F32))
    scs = [jnp.where(valid, ss[e][:, c * nk:(c + 1) * nk], -jnp.inf) for e, c in halves]
    sinks = [jnp.concatenate([jnp.full((w, LANES), sink_ref[Q_PER_KV * e + c], F32),
                              jnp.full((w, LANES), sink_ref[Q_PER_KV * e + 2 + c], F32)], axis=0) for e, c in halves]
    ms = [jnp.maximum(jnp.broadcast_to(jnp.max(sc, axis=1, keepdims=True), sk.shape), sk)
          for sc, sk in zip(scs, sinks)]
    ps = [jnp.concatenate([jnp.exp(sc[:, i * LANES:(i + 1) * LANES] - m) for i in slabs], axis=1).astype(BF16)
          for sc, m in zip(scs, ms)]
    dens = [jnp.dot(p, ones, preferred_element_type=F32) + jnp.exp(sk - m) for p, sk, m in zip(ps, sinks, ms)]
    outs = [jnp.dot(p, v, preferred_element_type=F32) * (1.0 / den) for p, v, den in zip(ps, vs, dens)]
    for e in heads:
        o = outs[2 * e] + outs[2 * e + 1]
        c0 = 2 * e * LANES
        o_ref[0, :, c0:c0 + LANES] = o[:w].astype(o_ref.dtype)
        o_ref[0, :, c0 + LANES:c0 + 2 * LANES] = o[w:].astype(o_ref.dtype)


def _attn_prompt_kernel(sink_ref, q_ref, kc_ref, kp_ref, vc_ref, vp_ref, o_ref):
    w = WINDOW
    has_prev = pl.program_id(1) > 0
    k2 = jnp.concatenate([jnp.where(has_prev, kp_ref[0], 0.0), kc_ref[0]], axis=0)
    v2 = jnp.concatenate([jnp.where(has_prev, vp_ref[0], 0.0), vc_ref[0]], axis=0)
    t = lax.broadcasted_iota(jnp.int32, (2 * w, 2 * w), 0) % w
    j = lax.broadcasted_iota(jnp.int32, (2 * w, 2 * w), 1)
    _slab_attention(q_ref, o_ref, k2, v2, (j > t) & (j <= t + w), sink_ref)


def _attn_prompt(qkv, sinks):
    b_, l, _ = qkv.shape
    w = WINDOW
    qw = ATT_HEADS * HEAD_DIM
    kvw = KV_HEADS * HEAD_DIM
    k_blk = qw // kvw
    prev = lambda n: jnp.maximum(n - 1, 0)
    return pl.pallas_call(
        _attn_prompt_kernel,
        grid=(b_, l // w),
        in_specs=[pl.BlockSpec(memory_space=pltpu.SMEM),
                  pl.BlockSpec((1, w, qw), lambda b, n: (b, n, 0)),
                  pl.BlockSpec((1, w, kvw), lambda b, n: (b, n, k_blk)),
                  pl.BlockSpec((1, w, kvw), lambda b, n: (b, prev(n), k_blk)),
                  pl.BlockSpec((1, w, kvw), lambda b, n: (b, n, k_blk + 1)),
                  pl.BlockSpec((1, w, kvw), lambda b, n: (b, prev(n), k_blk + 1))],
        out_specs=pl.BlockSpec((1, w, qw), lambda b, n: (b, n, 0)),
        out_shape=jax.ShapeDtypeStruct((b_, l, qw), BF16),
        compiler_params=_cparams(("parallel", "parallel")),
        name="attn_prompt",
    )(sinks, qkv, qkv, qkv, qkv, qkv)


def _attn_sample_kernel(sink_ref, qkv_ref, ck_ref, cv_ref, o_ref, nk_ref, nv_ref):
    l = qkv_ref.shape[1]
    n_keep = ck_ref.shape[1]
    kcol = ATT_HEADS * HEAD_DIM
    vcol = kcol + KV_HEADS * HEAD_DIM
    qkv = qkv_ref[0]
    k_all = jnp.concatenate([ck_ref[0], qkv[:, kcol:vcol]], axis=0)
    v_all = jnp.concatenate([cv_ref[0], qkv[:, vcol:]], axis=0)
    nk_ref[0] = k_all[l:]
    nv_ref[0] = v_all[l:]
    nk = -(-(n_keep + l) // LANES) * LANES
    pad = jnp.zeros((nk - n_keep - l, k_all.shape[1]), F32)
    t = lax.broadcasted_iota(jnp.int32, (2 * l, nk), 0) % l
    j = lax.broadcasted_iota(jnp.int32, (2 * l, nk), 1)
    rel = jnp.where(j < n_keep, t + n_keep - j, t - (j - n_keep))
    valid = (rel >= 0) & (rel < WINDOW) & (j < n_keep + l)
    _slab_attention(qkv_ref, o_ref, jnp.concatenate([k_all, pad], axis=0), jnp.concatenate([v_all, pad], axis=0),
                    valid, sink_ref)


def _attn_sample(qkv, cache_k, cache_v, sinks):
    b_, l, n = qkv.shape
    n_keep, kvw = cache_k.shape[1:]
    return pl.pallas_call(
        _attn_sample_kernel,
        grid=(b_,),
        in_specs=[pl.BlockSpec(memory_space=pltpu.SMEM),
                  pl.BlockSpec((1, l, n), lambda b: (b, 0, 0)),
                  pl.BlockSpec((1, n_keep, kvw), lambda b: (b, 0, 0)),
                  pl.BlockSpec((1, n_keep, kvw), lambda b: (b, 0, 0))],
        out_specs=[pl.BlockSpec((1, l, ATT_HEADS * HEAD_DIM), lambda b: (b, 0, 0)),
                   pl.BlockSpec((1, n_keep, kvw), lambda b: (b, 0, 0)),
                   pl.BlockSpec((1, n_keep, kvw), lambda b: (b, 0, 0))],
        out_shape=[jax.ShapeDtypeStruct((b_, l, ATT_HEADS * HEAD_DIM), BF16),
                   jax.ShapeDtypeStruct((b_, n_keep, kvw), F32),
                   jax.ShapeDtypeStruct((b_, n_keep, kvw), F32)],
        compiler_params=_cparams(("parallel",)),
        name="attn_sample",
    )(sinks, qkv, cache_k, cache_v)


def _rope_tables(pos):
    half = HEAD_DIM // 2
    inv_freq = ROPE_THETA ** (-jnp.arange(half, dtype=F32) / half)
    ang = pos.astype(F32)[:, None] * inv_freq[None, :]
    cos = jnp.cos(ang)
    sin = jnp.sin(ang)
    reps = LANES // HEAD_DIM
    return (jnp.tile(jnp.concatenate([cos, cos], axis=1), (1, reps)),
            jnp.tile(jnp.concatenate([-sin, sin], axis=1), (1, reps)))


def _tail_pad(buf):
    return jnp.pad(buf, ((0, 0), (TAIL_ROWS - buf.shape[1], 0), (0, 0)))


def _trunk(x, mods, pos, st, pw, seq_len):
    g_, m, d = x.shape
    nseq = g_ * m // seq_len
    blk = _block_rows(m)
    chunk = min(seq_len, SSM_CHUNK)
    lru_rows = min(seq_len, LRU_ROWS)

    proj, dtp = _inproj(x, pw['norms'][0, 0:1], mods[0], pw['w_in_t'], SSM_INNER + SSM_CONV_DIM, SSM_HEADS,
                        blk['inproj'], INPROJ_COLS)
    n_main = proj.shape[-1]
    proj_s = proj.reshape(nseq, seq_len, n_main)
    dtp_s = dtp.reshape(nseq, seq_len, LANES)
    y_ssm, ssm_h = _ssd(proj_s, dtp_s, _tail_pad(st['ssm_conv']), st['ssm'].reshape(nseq, SSM_INNER, SSM_STATE),
                        pw['ssm_conv_w'], pw['ssm_conv_b'], pw['ssm_dt_bias'], pw['ssm_a_log'], pw['ssm_d_full'],
                        pw['ssm_norm'], pw['expand'], chunk)
    y_lru, lru_h = _lru(proj_s, _tail_pad(st['lru_conv']), st['lru'].reshape(nseq, 1, LRU_WIDTH),
                        pw['lru_conv_w'], pw['lru_conv_b'], pw['lru_w_a'], pw['lru_b_a'], pw['lru_w_x'],
                        pw['lru_b_x'], pw['lru_lambda'], lru_rows, 3)
    keep = SSM_CONV - 1
    new_ssm_conv = proj_s[:, seq_len - keep:, SSM_INNER:SSM_INNER + SSM_CONV_DIM]
    new_lru_conv = proj_s[:, seq_len - keep:, n_main - LRU_WIDTH:]
    x = _outproj([y_ssm.reshape(g_, m, SSM_INNER), y_lru.reshape(g_, m, LRU_WIDTH)],
                 pw['w_out_hyb'], x, pw['norms'][0, 1:2], mods[0], blk['outproj'])
    x = _mlp(x, pw['norms'][0, 2:3], pw['norms'][0, 3:4], mods[0], pw['w_up'], pw['w_down'], 0, blk['mlp'],
             MLP_COLS)

    cos_t, sin_t = _rope_tables(pos)
    if seq_len < blk['qkv']:
        cos_t = jnp.tile(cos_t, (blk['qkv'] // seq_len, 1))
        sin_t = jnp.tile(sin_t, (blk['qkv'] // seq_len, 1))
    qkv = _qkv(x, pw['norms'][1, 0:1], mods[1], pw['w_qkv'], pw['b_qkv'], cos_t, sin_t, blk['qkv'], QKV_COLS)
    qkv_s = qkv.reshape(nseq, seq_len, qkv.shape[-1])
    kcol = ATT_HEADS * HEAD_DIM
    vcol = kcol + KV_HEADS * HEAD_DIM
    if st['k'] is None:
        o = _attn_prompt(qkv_s, pw['attn_sinks'])
        n_keep = min(WINDOW, seq_len)
        k_new = qkv_s[:, seq_len - n_keep:, kcol:vcol]
        v_new = qkv_s[:, seq_len - n_keep:, vcol:]
    else:
        n_keep = st['k'].shape[1]
        o, k_new, v_new = _attn_sample(qkv_s, st['k'].reshape(nseq, n_keep, KV_HEADS * HEAD_DIM),
                                       st['v'].reshape(nseq, n_keep, KV_HEADS * HEAD_DIM), pw['attn_sinks'])
    k_new = k_new.reshape(nseq, n_keep, KV_HEADS, HEAD_DIM)
    v_new = v_new.reshape(nseq, n_keep, KV_HEADS, HEAD_DIM)
    x = _outproj([o.reshape(g_, m, kcol)], pw['w_out_attn'], x, pw['norms'][1, 1:2], mods[1], blk['outproj'])
    x = _mlp(x, pw['norms'][1, 2:3], pw['norms'][1, 3:4], mods[1], pw['w_up'], pw['w_down'], 1, blk['mlp'],
             MLP_COLS)

    states = (new_ssm_conv[None], ssm_h.reshape(1, nseq, SSM_HEADS, SSM_HEAD_DIM, SSM_STATE),
              new_lru_conv[None], lru_h.reshape(1, nseq, LRU_WIDTH), k_new[None], v_new[None])
    return x, states


def kernel(x_prompt, x_sample, state_ssm_conv, state_ssm, state_lru_conv, state_lru, cache_k, cache_v, c_prompt, c_sample, w_mod, b_mod, norms, w_in_hyb, ssm_conv_w, ssm_conv_b, ssm_dt_bias, ssm_a_log, ssm_d, ssm_norm, lru_conv_w, lru_conv_b, lru_w_a, lru_b_a, lru_w_x, lru_b_x, lru_lambda, w_out_hyb, w_qkv, b_qkv, attn_sinks, w_out_attn, w_up, w_down):
    bp, lp, d = x_prompt.shape
    bs, ls, _ = x_sample.shape
    depth = w_mod.shape[0]

    pad_lanes = lambda v: jnp.pad(v.reshape(1, -1), ((0, 0), (0, LANES - v.size)))
    head_of_lane = jnp.arange(SSM_INNER) // SSM_HEAD_DIM
    pw = {
        'norms': norms,
        'w_in_t': jnp.swapaxes(w_in_hyb, 1, 2),
        'ssm_conv_w': ssm_conv_w[0], 'ssm_conv_b': ssm_conv_b[0].reshape(1, -1),
        'ssm_dt_bias': pad_lanes(ssm_dt_bias[0]), 'ssm_a_log': pad_lanes(ssm_a_log[0]),
        'ssm_d_full': jnp.repeat(ssm_d[0], SSM_HEAD_DIM).reshape(1, -1),
        'ssm_norm': ssm_norm[0].reshape(1, -1),
        'expand': (jnp.arange(LANES)[:, None] == head_of_lane[None, :]).astype(BF16),
        'lru_conv_w': lru_conv_w[0], 'lru_conv_b': lru_conv_b[0].reshape(1, -1),
        'lru_w_a': lru_w_a[0].astype(BF16), 'lru_b_a': lru_b_a[0].reshape(1, -1),
        'lru_w_x': lru_w_x[0].astype(BF16), 'lru_b_x': lru_b_x[0].reshape(1, -1),
        'lru_lambda': lru_lambda[0].reshape(1, -1),
        'w_out_hyb': w_out_hyb[0].astype(BF16),
        'w_qkv': w_qkv[0], 'b_qkv': b_qkv[0].reshape(1, -1),
        'attn_sinks': attn_sinks[0],
        'w_out_attn': w_out_attn[0],
        'w_up': w_up, 'w_down': w_down,
    }

    mod = _adaln(jnp.concatenate([c_prompt, c_sample], axis=0), w_mod, b_mod)
    mods_p = [mod[l, :bp][:, None, :] for l in range(depth)]
    mods_s = [jnp.repeat(mod[l, bp:], ls, axis=0)[None] for l in range(depth)]

    zeros = lambda *shape: jnp.zeros(shape, F32)
    st_p = {'ssm_conv': zeros(bp, SSM_CONV - 1, SSM_CONV_DIM), 'ssm': zeros(bp, SSM_INNER, SSM_STATE),
            'lru_conv': zeros(bp, SSM_CONV - 1, LRU_WIDTH), 'lru': zeros(bp, LRU_WIDTH), 'k': None, 'v': None}
    st_s = {'ssm_conv': state_ssm_conv[0], 'ssm': state_ssm[0], 'lru_conv': state_lru_conv[0],
            'lru': state_lru[0], 'k': cache_k[0], 'v': cache_v[0]}

    pos_p = jnp.arange(lp, dtype=jnp.int32)
    pos_s = PAST_LEN + jnp.arange(ls, dtype=jnp.int32)
    y_p, sp = _trunk(x_prompt, mods_p, pos_p, st_p, pw, lp)
    y_s, ss = _trunk(x_sample.reshape(1, bs * ls, d), mods_s, pos_s, st_s, pw, ls)
    return (y_p, y_s.reshape(bs, ls, d)) + sp + ss
```

```python
import functools
import math

import jax
import jax.numpy as jnp
from jax import lax
from jax.experimental import pallas as pl
from jax.experimental.pallas import tpu as pltpu

F32 = jnp.float32
BF16 = jnp.bfloat16

D_MODEL = 2048
PAST_LEN = 16384
SSM_HEADS = 32
SSM_HEAD_DIM = 64
SSM_INNER = SSM_HEADS * SSM_HEAD_DIM
SSM_GROUPS = 8
SSM_STATE = 128
SSM_CONV = 4
SSM_CHUNK = 128
SSM_CONV_DIM = SSM_INNER + 2 * SSM_GROUPS * SSM_STATE
HEADS_PER_GROUP = SSM_HEADS // SSM_GROUPS
GROUP_WIDTH = HEADS_PER_GROUP * SSM_HEAD_DIM
LRU_WIDTH = D_MODEL
LRU_BLOCKS = 8
LRU_BLOCK = LRU_WIDTH // LRU_BLOCKS
LRU_C = 8.0
ATT_HEADS = 32
KV_HEADS = 8
HEAD_DIM = 64
Q_PER_KV = ATT_HEADS // KV_HEADS
WINDOW = 128
ROPE_THETA = 10000.0
D_FF = 4 * D_MODEL
EPS = 1e-6

LANES = 128
SUBLANES = 8
BF16_SUBLANES = 16
TAIL_ROWS = SUBLANES
VMEM_LIMIT = 56 * 1024 * 1024


INPROJ_COLS = 1024
MLP_COLS = 512
QKV_COLS = 512
LRU_ROWS = 512


def _block_rows(m):
    return {'inproj': min(m, 1024), 'mlp': min(m, 1024), 'qkv': min(m, 512), 'outproj': min(m, 512)}


SEQS_PER_STEP = 4


def _seqs_per_step(n_seq, n_chunks):
    return SEQS_PER_STEP if n_chunks == 1 and n_seq % SEQS_PER_STEP == 0 and n_seq > SEQS_PER_STEP else 1


def _per_sequence(body, nb, per_seq):
    if nb == 1:
        return body

    def kernel(*refs):
        def one(n, carry):
            body(*[r.at[pl.ds(n, 1)] if flag else r for r, flag in zip(refs, per_seq)])
            return carry

        lax.fori_loop(0, nb, one, 0)

    return kernel


def _cparams(sem):
    return pltpu.CompilerParams(dimension_semantics=sem, vmem_limit_bytes=VMEM_LIMIT)


def _rms(x, g):
    return x * lax.rsqrt(jnp.mean(x * x, axis=-1, keepdims=True) + EPS) * g


ROW_CHUNK = 16
ROW_UNROLL = 4


def _for_row_chunks(n_rows, body):
    def step(i, carry):
        body(pl.multiple_of(i * ROW_CHUNK, ROW_CHUNK))
        return carry

    lax.fori_loop(0, n_rows // ROW_CHUNK, step, 0, unroll=ROW_UNROLL)


def _mod_rows(ref, r0):
    return ref[0] if ref.shape[1] == 1 else ref[0, pl.ds(r0, ROW_CHUNK), :]


def _prenorm_to(h_scr, x_ref, g_ref, sc_ref, sh_ref):
    one_seq = sc_ref.shape[1] == 1
    gain = g_ref[...] * (1.0 + sc_ref[0]) if one_seq else g_ref[...]

    def body(r0):
        x = x_ref[0, pl.ds(r0, ROW_CHUNK), :]
        h = _rms(x, gain) if one_seq else _rms(x, gain) * (1.0 + _mod_rows(sc_ref, r0))
        h_scr[pl.ds(r0, ROW_CHUNK), :] = (h + _mod_rows(sh_ref, r0)).astype(BF16)

    _for_row_chunks(x_ref.shape[1], body)


def _sigmoid(x):
    return 0.5 * (jnp.tanh(0.5 * x) + 1.0)


def _silu(x):
    return x * _sigmoid(x)


def _softplus(x):
    return jnp.maximum(x, 0.0) + jnp.log1p(jnp.exp(-jnp.abs(x)))


def _sqrt_neg_expm1(x):
    s = jnp.tanh(-0.5 * x)
    t = (2.0 * s) * (1.0 + s)
    return jnp.where(t > 0.0, (2.0 * s) * lax.rsqrt(t), 0.0)


def _split3(x):
    hi = x.astype(BF16)
    r = x - hi.astype(F32)
    mid = r.astype(BF16)
    lo = (r - mid.astype(F32)).astype(BF16)
    return [hi, mid, lo]


def _adaln_kernel(c_ref, w_ref, b_ref, o_ref):
    s = _silu(c_ref[...]).astype(BF16)
    o_ref[0] = jnp.dot(s, w_ref[0].astype(BF16), preferred_element_type=F32) + b_ref[0]


def _adaln(c_all, w_mod, b_mod):
    nb, d = c_all.shape
    depth, _, n = w_mod.shape
    bn = 1024
    return pl.pallas_call(
        _adaln_kernel,
        grid=(depth, n // bn),
        in_specs=[pl.BlockSpec((nb, d), lambda l, j: (0, 0)),
                  pl.BlockSpec((1, d, bn), lambda l, j: (l, 0, j)),
                  pl.BlockSpec((1, 1, bn), lambda l, j: (l, 0, j))],
        out_specs=pl.BlockSpec((1, nb, bn), lambda l, j: (l, 0, j)),
        out_shape=jax.ShapeDtypeStruct((depth, nb, n), F32),
        compiler_params=_cparams(("parallel", "parallel")),
        name="adaln",
    )(c_all, w_mod, b_mod.reshape(depth, 1, n))


def _mod_spec(mod, bm, k):
    if mod.shape[1] == 1:
        return pl.BlockSpec((1, 1, D_MODEL), lambda g, i, *_: (g, 0, k))
    return pl.BlockSpec((1, bm, D_MODEL), lambda g, i, *_: (g, i, k))


def _inproj_kernel(x_ref, g_ref, sh_ref, sc_ref, wt_ref, wdt_ref, o_ref, dt_ref, h_scr, *, n_dt):
    nt = (((1,), (1,)), ((), ()))

    @pl.when(pl.program_id(2) == 0)
    def _():
        _prenorm_to(h_scr, x_ref, g_ref, sc_ref, sh_ref)
        dt = lax.dot_general(h_scr[...], wdt_ref[...].astype(BF16), nt, preferred_element_type=F32)
        lane = lax.broadcasted_iota(jnp.int32, dt.shape, 1)
        dt_ref[0] = jnp.where(lane < n_dt, dt, 0.0)

    o_ref[0] = lax.dot_general(h_scr[...], wt_ref[...].astype(BF16), nt, preferred_element_type=F32)


def _inproj(x, gain, mod, w_t, c_dt, n_dt, bm, bn):
    g_, m, d = x.shape
    n = w_t.shape[1] - n_dt
    row0 = lambda j: pl.multiple_of(jnp.where(j * bn < c_dt, j * bn, j * bn + n_dt), SUBLANES)
    return pl.pallas_call(
        functools.partial(_inproj_kernel, n_dt=n_dt),
        grid=(g_, m // bm, n // bn),
        in_specs=[pl.BlockSpec((1, bm, d), lambda g, i, j: (g, i, 0), pipeline_mode=pl.Buffered(1)),
                  pl.BlockSpec((1, d), lambda g, i, j: (0, 0)),
                  _mod_spec(mod, bm, 0), _mod_spec(mod, bm, 1),
                  pl.BlockSpec((None, pl.Element(bn), pl.Element(d)), lambda g, i, j: (0, row0(j), 0)),
                  pl.BlockSpec((None, pl.Element(LANES), pl.Element(d)), lambda g, i, j: (0, c_dt, 0))],
        out_specs=[pl.BlockSpec((1, bm, bn), lambda g, i, j: (g, i, j)),
                   pl.BlockSpec((1, bm, LANES), lambda g, i, j: (g, i, 0))],
        out_shape=[jax.ShapeDtypeStruct((g_, m, n), F32),
                   jax.ShapeDtypeStruct((g_, m, LANES), F32)],
        scratch_shapes=[pltpu.VMEM((bm, d), BF16)],
        compiler_params=_cparams(("parallel", "parallel", "arbitrary")),
        name="inproj",
    )(x, gain, mod, mod, w_t, w_t)


def _swap_halves(x):
    lane = lax.broadcasted_iota(jnp.int32, x.shape, 1)
    first = (lane % HEAD_DIM) < (HEAD_DIM // 2)
    return jnp.where(first, pltpu.roll(x, LANES - HEAD_DIM // 2, 1), pltpu.roll(x, HEAD_DIM // 2, 1))


def _qkv_kernel(x_ref, g_ref, sh_ref, sc_ref, w_ref, b_ref, cos_ref, sin_ref, o_ref, h_scr, *, rope_cols, bn):
    _prenorm_to(h_scr, x_ref, g_ref, sc_ref, sh_ref)
    h = h_scr[...]
    cos = cos_ref[...]
    sin = sin_ref[...]
    for c0 in range(0, w_ref.shape[1], bn):
        acc = jnp.dot(h, w_ref[:, c0:c0 + bn].astype(BF16), preferred_element_type=F32) + b_ref[:, c0:c0 + bn]
        if c0 < rope_cols:
            for c in range(c0, c0 + bn, LANES):
                a = acc[:, c - c0:c - c0 + LANES]
                o_ref[0, :, c:c + LANES] = a * cos + _swap_halves(a) * sin
        else:
            o_ref[0, :, c0:c0 + bn] = acc


def _qkv(x, gain, mod, w, b, cos_t, sin_t, bm, bn):
    g_, m, d = x.shape
    n = w.shape[1]
    rope_cols = (ATT_HEADS + KV_HEADS) * HEAD_DIM
    rows_per_g = cos_t.shape[0] // bm
    return pl.pallas_call(
        functools.partial(_qkv_kernel, rope_cols=rope_cols, bn=bn),
        grid=(g_, m // bm),
        in_specs=[pl.BlockSpec((1, bm, d), lambda g, i: (g, i, 0)),
                  pl.BlockSpec((1, d), lambda g, i: (0, 0)),
                  _mod_spec(mod, bm, 0), _mod_spec(mod, bm, 1),
                  pl.BlockSpec((d, n), lambda g, i: (0, 0), pipeline_mode=pl.Buffered(1)),
                  pl.BlockSpec((1, n), lambda g, i: (0, 0)),
                  pl.BlockSpec((bm, LANES), lambda g, i: (i % rows_per_g, 0)),
                  pl.BlockSpec((bm, LANES), lambda g, i: (i % rows_per_g, 0))],
        out_specs=pl.BlockSpec((1, bm, n), lambda g, i: (g, i, 0)),
        out_shape=jax.ShapeDtypeStruct((g_, m, n), F32),
        scratch_shapes=[pltpu.VMEM((bm, d), BF16)],
        compiler_params=_cparams(("parallel", "parallel")),
        name="qkv",
    )(x, gain, mod, mod, w, b, cos_t, sin_t)


def _outproj_kernel(*refs, n_lhs):
    lhs = refs[:n_lhs]
    ws = refs[n_lhs:2 * n_lhs]
    x_ref, g_ref, gt_ref, o_ref = refs[2 * n_lhs:]
    acc = jnp.dot(lhs[0][0], ws[0][...].astype(BF16), preferred_element_type=F32)
    for a, w in zip(lhs[1:], ws[1:]):
        acc = acc + jnp.dot(a[0], w[...].astype(BF16), preferred_element_type=F32)
    o_ref[0] = x_ref[0] + gt_ref[0] * _rms(acc, g_ref[...])


def _outproj(lhs_list, w, x, gain, mod, bm):
    g_, m, d = x.shape
    n_lhs = len(lhs_list)
    kw = w.shape[0] // n_lhs
    lhs_specs = [pl.BlockSpec((1, bm, kw), lambda g, i: (g, i, 0)) for _ in lhs_list]
    w_specs = [pl.BlockSpec((kw, d), lambda g, i, k=k: (k, 0), pipeline_mode=pl.Buffered(1)) for k in range(n_lhs)]
    w_list = [w] * n_lhs
    if mod.shape[1] == 1:
        gt_spec = pl.BlockSpec((1, 1, d), lambda g, i: (g, 0, 2))
    else:
        gt_spec = pl.BlockSpec((1, bm, d), lambda g, i: (g, i, 2))
    return pl.pallas_call(
        functools.partial(_outproj_kernel, n_lhs=n_lhs),
        grid=(g_, m // bm),
        in_specs=lhs_specs + w_specs + [pl.BlockSpec((1, bm, d), lambda g, i: (g, i, 0)),
                                        pl.BlockSpec((1, d), lambda g, i: (0, 0)), gt_spec],
        out_specs=pl.BlockSpec((1, bm, d), lambda g, i: (g, i, 0)),
        out_shape=jax.ShapeDtypeStruct((g_, m, d), F32),
        compiler_params=_cparams(("parallel", "parallel")),
        name="outproj",
    )(*lhs_list, *w_list, x, gain, mod)


def _mlp_kernel(x_ref, g2_ref, sh_ref, sc_ref, gt_ref, wu_ref, wd_ref, g3_ref, o_ref, h_scr):
    f = pl.program_id(2)

    @pl.when(f == 0)
    def _():
        _prenorm_to(h_scr, x_ref, g2_ref, sc_ref, sh_ref)
        o_ref[0] = jnp.zeros(o_ref.shape[1:], F32)

    u = jnp.dot(h_scr[...], wu_ref[...].astype(BF16), preferred_element_type=F32)
    u = jnp.square(jnp.maximum(u, 0.0)).astype(BF16)
    o_ref[0] += jnp.dot(u, wd_ref[...].astype(BF16), preferred_element_type=F32)

    @pl.when(f == pl.num_programs(2) - 1)
    def _():
        o_ref[0] = x_ref[0] + gt_ref[0] * _rms(o_ref[0], g3_ref[...])


def _mlp(x, g2, g3, mod, w_up, w_down, layer, bm, bf):
    g_, m, d = x.shape
    dff = w_up.shape[2]
    if mod.shape[1] == 1:
        mspec = lambda k: pl.BlockSpec((1, 1, d), lambda g, i, f: (g, 0, k))
    else:
        mspec = lambda k: pl.BlockSpec((1, bm, d), lambda g, i, f: (g, i, k))
    return pl.pallas_call(
        _mlp_kernel,
        grid=(g_, m // bm, dff // bf),
        in_specs=[pl.BlockSpec((1, bm, d), lambda g, i, f: (g, i, 0), pipeline_mode=pl.Buffered(1)),
                  pl.BlockSpec((1, d), lambda g, i, f: (0, 0)),
                  mspec(3), mspec(4), mspec(5),
                  pl.BlockSpec((None, d, bf), lambda g, i, f: (layer, 0, f)),
                  pl.BlockSpec((None, bf, d), lambda g, i, f: (layer, f, 0)),
                  pl.BlockSpec((1, d), lambda g, i, f: (0, 0))],
        out_specs=pl.BlockSpec((1, bm, d), lambda g, i, f: (g, i, 0)),
        out_shape=jax.ShapeDtypeStruct((g_, m, d), F32),
        scratch_shapes=[pltpu.VMEM((bm, d), BF16)],
        compiler_params=_cparams(("parallel", "parallel", "arbitrary")),
        name="mlp",
    )(x, g2, mod, mod, mod, w_up, w_down, g3)


def _dwconv(raw, tail, w, b):
    t = raw.shape[0]
    assert w.shape[0] == 4
    ext = jnp.concatenate([tail, raw], axis=0)
    z1 = pltpu.roll(ext, 1, 0)
    near = ext * w[3:4] + z1 * w[2:3]
    far = ext * w[1:2] + z1 * w[0:1]
    y = near + pltpu.roll(far, 2, 0)
    return y[TAIL_ROWS:] + b, ext[t:t + TAIL_ROWS]


def _cumsum_rows(x):
    n = x.shape[0]
    row = lax.broadcasted_iota(jnp.int32, x.shape, 0)
    d = 1
    while d < n:
        x = x + jnp.where(row >= d, pltpu.roll(x, d, 0), 0.0)
        d *= 2
    return x


def _ssd_kernel(z_ref, xs_ref, bc_ref, dt_ref, buf_ref, h0_ref, cw_ref, cb_ref, dtb_ref, alog_ref,
                dsk_ref, ng_ref, exp_ref, y_ref, st_ref, tail_scr, *, q_valid, q):
    c = pl.program_id(1)

    @pl.when(c == 0)
    def _():
        tail_scr[...] = buf_ref[0]
        st_ref[0] = h0_ref[0]

    def padded(v):
        if q_valid == q:
            return v
        return jnp.concatenate([v, jnp.zeros((q - q_valid, v.shape[1]), v.dtype)], axis=0)

    def transposed(v):
        if q == LANES:
            return v.T
        return jnp.concatenate([v, jnp.zeros((LANES - q, LANES), v.dtype)], axis=0).T[:, :q]

    cw = cw_ref[...]
    cb = cb_ref[...]
    xs, tail_x = _dwconv(xs_ref[0], tail_scr[:, :SSM_INNER], cw[:, :SSM_INNER], cb[:, :SSM_INNER])
    bcm, tail_b = _dwconv(bc_ref[0], tail_scr[:, SSM_INNER:], cw[:, SSM_INNER:], cb[:, SSM_INNER:])
    tail_scr[:, :SSM_INNER] = tail_x
    tail_scr[:, SSM_INNER:] = tail_b
    xs = padded(_silu(xs))
    bcm = padded(_silu(bcm))
    z = padded(z_ref[0])

    dt = _softplus(dt_ref[0] + dtb_ref[...])
    dt = padded(dt)
    a = -jnp.exp(alog_ref[...])
    acs = _cumsum_rows(dt * a)
    acs_t = transposed(acs)

    full = jnp.dot(jnp.concatenate(_split3(dt) + _split3(acs), axis=0), exp_ref[...], preferred_element_type=F32)
    dt_full = (full[0:q] + full[q:2 * q]) + full[2 * q:3 * q]
    acs_full = (full[3 * q:4 * q] + full[4 * q:5 * q]) + full[5 * q:6 * q]
    xdt = xs * dt_full
    dec_out = jnp.exp(acs_full)
    xw = (xdt * jnp.exp(acs_full[q - 1:q] - acs_full)).astype(BF16)
    xdt_b = xdt.astype(BF16)
    bc_b = bcm.astype(BF16)

    row = lax.broadcasted_iota(jnp.int32, (q, q), 0)
    col = lax.broadcasted_iota(jnp.int32, (q, q), 1)
    causal = row >= col
    lane = lax.broadcasted_iota(jnp.int32, (q, LANES), 1)
    lo_half = lane < SSM_HEAD_DIM
    nt = (((1,), (1,)), ((), ()))
    tn = (((0,), (0,)), ((), ()))
    n_bc = SSM_GROUPS * SSM_STATE

    groups = range(SSM_GROUPS)
    cols = [slice(g * GROUP_WIDTH, (g + 1) * GROUP_WIDTH) for g in groups]
    b_gs = [bc_b[:, g * SSM_STATE:(g + 1) * SSM_STATE] for g in groups]
    c_gs = [bc_b[:, n_bc + g * SSM_STATE:n_bc + (g + 1) * SSM_STATE] for g in groups]
    h_gs = [st_ref[0, cols[g], :] for g in groups]
    cbms = [lax.dot_general(c_gs[g], b_gs[g], nt, preferred_element_type=F32) for g in groups]
    y_offs = [lax.dot_general(c_gs[g], h_gs[g].astype(BF16), nt, preferred_element_type=F32) for g in groups]
    st_news = [lax.dot_general(xw[:, cols[g]], b_gs[g], tn, preferred_element_type=F32) for g in groups]
    ms = [(cbms[h // HEADS_PER_GROUP]
           * jnp.exp(jnp.where(causal, acs[:, h:h + 1] - acs_t[h:h + 1, :], -jnp.inf))).astype(BF16)
          for h in range(SSM_HEADS)]
    y_diags = []
    for pair in range(SSM_HEADS // 2):
        x_pair = xdt_b[:, pair * LANES:(pair + 1) * LANES]
        zero = jnp.zeros_like(x_pair)
        y_diags.append(jnp.dot(ms[2 * pair], jnp.where(lo_half, x_pair, zero), preferred_element_type=F32)
                       + jnp.dot(ms[2 * pair + 1], jnp.where(lo_half, zero, x_pair), preferred_element_type=F32))
    pairs_per_group = HEADS_PER_GROUP // 2
    for g in groups:
        y_g = y_offs[g] * dec_out[:, cols[g]]
        y_g = y_g + jnp.concatenate(y_diags[g * pairs_per_group:(g + 1) * pairs_per_group], axis=1)
        y_g = y_g + xs[:, cols[g]] * dsk_ref[:, cols[g]]
        y_g = y_g * _silu(z[:, cols[g]])
        y_g = y_g * lax.rsqrt(jnp.mean(y_g * y_g, axis=-1, keepdims=True) + EPS)
        y_g = y_g * ng_ref[:, cols[g]]
        y_ref[0, :, cols[g]] = y_g[:q_valid].astype(y_ref.dtype)
    for h in range(SSM_HEADS):
        g, r = divmod(h, HEADS_PER_GROUP)
        rows = slice(r * SSM_HEAD_DIM, (r + 1) * SSM_HEAD_DIM)
        decay = jnp.exp(acs_t[h:h + 1, q - 1:q])
        st_ref[0, h * SSM_HEAD_DIM:(h + 1) * SSM_HEAD_DIM, :] = h_gs[g][rows] * decay + st_news[g][rows]


def _ssd(proj, dtp, buf8, h0, conv_w, conv_b, dt_bias, a_log, d_full, norm_g, expand, rows):
    b_, l, _ = proj.shape
    nc = l // rows
    nb = _seqs_per_step(b_, nc)
    st_rows = SSM_INNER
    kernel_fn = functools.partial(_ssd_kernel, q_valid=rows, q=max(rows, BF16_SUBLANES))
    return pl.pallas_call(
        _per_sequence(kernel_fn, nb, [True] * 6 + [False] * 7 + [True] * 2 + [False]),
        grid=(b_ // nb, nc),
        in_specs=[pl.BlockSpec((nb, rows, SSM_INNER), lambda b, c: (b, c, 0)),
                  pl.BlockSpec((nb, rows, SSM_INNER), lambda b, c: (b, c, 1)),
                  pl.BlockSpec((nb, rows, SSM_INNER), lambda b, c: (b, c, 2)),
                  pl.BlockSpec((nb, rows, LANES), lambda b, c: (b, c, 0)),
                  pl.BlockSpec((nb, TAIL_ROWS, SSM_CONV_DIM), lambda b, c: (b, 0, 0)),
                  pl.BlockSpec((nb, st_rows, SSM_STATE), lambda b, c: (b, 0, 0)),
                  pl.BlockSpec((SSM_CONV, SSM_CONV_DIM), lambda b, c: (0, 0)),
                  pl.BlockSpec((1, SSM_CONV_DIM), lambda b, c: (0, 0)),
                  pl.BlockSpec((1, LANES), lambda b, c: (0, 0)),
                  pl.BlockSpec((1, LANES), lambda b, c: (0, 0)),
                  pl.BlockSpec((1, SSM_INNER), lambda b, c: (0, 0)),
                  pl.BlockSpec((1, SSM_INNER), lambda b, c: (0, 0)),
                  pl.BlockSpec((LANES, SSM_INNER), lambda b, c: (0, 0))],
        out_specs=[pl.BlockSpec((nb, rows, SSM_INNER), lambda b, c: (b, c, 0)),
                   pl.BlockSpec((nb, st_rows, SSM_STATE), lambda b, c: (b, 0, 0))],
        out_shape=[jax.ShapeDtypeStruct((b_, l, SSM_INNER), BF16),
                   jax.ShapeDtypeStruct((b_, st_rows, SSM_STATE), F32)],
        scratch_shapes=[pltpu.VMEM((TAIL_ROWS, SSM_CONV_DIM), F32)],
        compiler_params=_cparams(("parallel", "arbitrary")),
        name="ssd",
    )(proj, proj, proj, dtp, buf8, h0, conv_w, conv_b, dt_bias, a_log, d_full, norm_g, expand)


SCAN_SEG = 4
SCAN_ROWS = SUBLANES * SCAN_SEG


def _shift_rows_fill(x, d, fill, row):
    return jnp.where(row >= d, pltpu.roll(x, d, 0), fill)


def _sublane_scan(e, pe, row):
    d = 1
    while d < SUBLANES:
        e = e + pe * _shift_rows_fill(e, d, 0.0, row)
        pe = pe * _shift_rows_fill(pe, d, 1.0, row)
        d *= 2
    return e, pe


def _linear_scan(a_scr, u_scr, h0):
    n_slab, t, _ = a_scr.shape
    seg = SCAN_SEG if t % SCAN_ROWS == 0 else 1
    row = lax.broadcasted_iota(jnp.int32, (SUBLANES, LANES), 0)

    def block(i, h_prev):
        base = pl.multiple_of(i * (SUBLANES * seg), SUBLANES)
        out = []
        for c in range(n_slab):
            rows = [pl.ds(base + j, SUBLANES, stride=seg) if seg > 1 else pl.ds(base, SUBLANES) for j in range(seg)]
            hs, ps = [], []
            for j in range(seg):
                a, u = a_scr[c, rows[j], :], u_scr[c, rows[j], :]
                hs.append(u if j == 0 else a * hs[-1] + u)
                ps.append(a if j == 0 else a * ps[-1])
            e, pe = _sublane_scan(hs[-1], ps[-1], row)
            g = e + pe * h_prev[c]
            carry = jnp.where(row == 0, h_prev[c], pltpu.roll(g, 1, 0))
            for j in range(seg):
                u_scr[c, rows[j], :] = hs[j] + ps[j] * carry
            out.append(jnp.broadcast_to(g[SUBLANES - 1:], g.shape))
        return tuple(out)

    h_init = tuple(jnp.broadcast_to(h0[:, c * LANES:(c + 1) * LANES], (SUBLANES, LANES)) for c in range(n_slab))
    h_fin = lax.fori_loop(0, t // (SUBLANES * seg), block, h_init)
    return jnp.concatenate([h[:1] for h in h_fin], axis=1)


def _gelu_tanh(x):
    c0 = math.sqrt(2.0 / math.pi)
    hx = 0.5 * x
    return hx + hx * jnp.tanh(x * (c0 + (c0 * 0.044715) * (x * x)))


def _lru_kernel(gate_ref, xr_ref, buf_ref, h0_ref, cw_ref, cb_ref, wa_ref, ba_ref, wx_ref, bx_ref,
                lam_ref, y_ref, hl_ref, tail_scr, a_scr, u_scr):
    c = pl.program_id(1)

    @pl.when(c == 0)
    def _():
        tail_scr[...] = buf_ref[0]
        hl_ref[0] = h0_ref[0]

    xc, tail = _dwconv(xr_ref[0], tail_scr[...], cw_ref[...], cb_ref[...])
    tail_scr[...] = tail
    sp = _softplus(-lam_ref[...])
    slabs_per_block = LRU_BLOCK // LANES
    for k in range(LRU_BLOCKS):
        k0 = k * LRU_BLOCK
        xk = xc[:, k0:k0 + LRU_BLOCK]
        xkb = xk.astype(BF16)
        gr = _sigmoid(jnp.dot(xkb, wa_ref[k], preferred_element_type=F32) + ba_ref[:, k0:k0 + LRU_BLOCK])
        gi = _sigmoid(jnp.dot(xkb, wx_ref[k], preferred_element_type=F32) + bx_ref[:, k0:k0 + LRU_BLOCK])
        log_a = -LRU_C * gr * sp[:, k0:k0 + LRU_BLOCK]
        a = jnp.exp(log_a)
        u = _sqrt_neg_expm1(2.0 * log_a) * gi * xk
        for s in range(slabs_per_block):
            a_scr[k * slabs_per_block + s] = a[:, s * LANES:(s + 1) * LANES]
            u_scr[k * slabs_per_block + s] = u[:, s * LANES:(s + 1) * LANES]

    hl_ref[0] = _linear_scan(a_scr, u_scr, hl_ref[0])
    for s in range(a_scr.shape[0]):
        gate = gate_ref[0, :, s * LANES:(s + 1) * LANES]
        y_ref[0, :, s * LANES:(s + 1) * LANES] = (u_scr[s] * _gelu_tanh(gate)).astype(y_ref.dtype)


def _lru(proj, buf8, h0, conv_w, conv_b, w_a, b_a, w_x, b_x, lam, rows, col0):
    b_, l, _ = proj.shape
    w = LRU_WIDTH
    nb = _seqs_per_step(b_, l // rows)
    full2 = lambda shape: pl.BlockSpec(shape, lambda b, c: (0,) * len(shape))
    return pl.pallas_call(
        _per_sequence(_lru_kernel, nb, [True] * 4 + [False] * 7 + [True] * 2 + [False] * 3),
        grid=(b_ // nb, l // rows),
        in_specs=[pl.BlockSpec((nb, rows, w), lambda b, c: (b, c, col0)),
                  pl.BlockSpec((nb, rows, w), lambda b, c: (b, c, col0 + 1)),
                  pl.BlockSpec((nb, TAIL_ROWS, w), lambda b, c: (b, 0, 0)),
                  pl.BlockSpec((nb, 1, w), lambda b, c: (b, 0, 0)),
                  full2((SSM_CONV, w)), full2((1, w)),
                  full2((LRU_BLOCKS, LRU_BLOCK, LRU_BLOCK)), full2((1, w)),
                  full2((LRU_BLOCKS, LRU_BLOCK, LRU_BLOCK)), full2((1, w)),
                  full2((1, w))],
        out_specs=[pl.BlockSpec((nb, rows, w), lambda b, c: (b, c, 0)),
                   pl.BlockSpec((nb, 1, w), lambda b, c: (b, 0, 0))],
        out_shape=[jax.ShapeDtypeStruct((b_, l, w), BF16),
                   jax.ShapeDtypeStruct((b_, 1, w), F32)],
        scratch_shapes=[pltpu.VMEM((TAIL_ROWS, w), F32), pltpu.VMEM((w // LANES, rows, LANES), F32),
                        pltpu.VMEM((w // LANES, rows, LANES), F32)],
        compiler_params=_cparams(("parallel", "arbitrary")),
        name="lru",
    )(proj, proj, buf8, h0, conv_w, conv_b, w_a, b_a, w_x, b_x, lam)


def _head_variants(slab, half):
    lane = lax.broadcasted_iota(jnp.int32, slab.shape, 1)
    keep = (lane < HEAD_DIM) if half == 0 else (lane >= HEAD_DIM)
    own = jnp.where(keep, slab, 0.0)
    other = pltpu.roll(own, HEAD_DIM, 1)
    return (own, other) if half == 0 else (other, own)


def _slab_attention(q_ref, o_ref, k2, v2, valid, sink_ref):
    w = q_ref.shape[1]
    nk = k2.shape[0]
    nt = (((1,), (1,)), ((), ()))
    scale = HEAD_DIM ** -0.5
    ones = jnp.ones((nk, LANES), BF16)
    slabs = range(nk // LANES)
    heads = range(KV_HEADS)
    halves = [(e, c) for e in heads for c in range(2)]
    vs, ss = [], []
    for e in heads:
        slab, half = divmod(e, 2)
        k_lo, k_hi = _head_variants(k2[:, slab * LANES:(slab + 1) * LANES], half)
        vs.extend(v.astype(BF16) for v in _head_variants(v2[:, slab * LANES:(slab + 1) * LANES], half))
        kk = jnp.concatenate([k_lo, k_hi], axis=0).astype(BF16)
        c0 = 2 * e * LANES
        qq = jnp.concatenate([q_ref[0, :, c0:c0 + LANES], q_ref[0, :, c0 + LANES:c0 + 2 * LANES]], axis=0)
        ss.append(lax.dot_general((qq * scale).astype(BF16), kk, nt, preferred_element_type=F32))
    scs = [jnp.where(valid, ss[e][:, c * nk:(c + 1) * nk], -jnp.inf) for e, c in halves]
    sinks = [jnp.concatenate([jnp.full((w, LANES), sink_ref[Q_PER_KV * e + c], F32),
                              jnp.full((w, LANES), sink_ref[Q_PER_KV * e + 2 + c], F32)], axis=0) for e, c in halves]
    ms = [jnp.maximum(jnp.broadcast_to(jnp.max(sc, axis=1, keepdims=True), sk.shape), sk)
          for sc, sk in zip(scs, sinks)]
    ps = [jnp.concatenate([jnp.exp(sc[:, i * LANES:(i + 1) * LANES] - m) for i in slabs], axis=1).astype(BF16)
          for sc, m in zip(scs, ms)]
    dens = [jnp.dot(p, ones, preferred_element_type=F32) + jnp.exp(sk - m) for p, sk, m in zip(ps, sinks, ms)]
    outs = [jnp.dot(p, v, preferred_element_type=F32) * (1.0 / den) for p, v, den in zip(ps, vs, dens)]
    for e in heads:
        o = outs[2 * e] + outs[2 * e + 1]
        c0 = 2 * e * LANES
        o_ref[0, :, c0:c0 + LANES] = o[:w].astype(o_ref.dtype)
        o_ref[0, :, c0 + LANES:c0 + 2 * LANES] = o[w:].astype(o_ref.dtype)


def _attn_prompt_kernel(sink_ref, q_ref, kc_ref, kp_ref, vc_ref, vp_ref, o_ref):
    w = WINDOW
    has_prev = pl.program_id(1) > 0
    k2 = jnp.concatenate([jnp.where(has_prev, kp_ref[0], 0.0), kc_ref[0]], axis=0)
    v2 = jnp.concatenate([jnp.where(has_prev, vp_ref[0], 0.0), vc_ref[0]], axis=0)
    t = lax.broadcasted_iota(jnp.int32, (2 * w, 2 * w), 0) % w
    j = lax.broadcasted_iota(jnp.int32, (2 * w, 2 * w), 1)
    _slab_attention(q_ref, o_ref, k2, v2, (j > t) & (j <= t + w), sink_ref)


def _attn_prompt(qkv, sinks):
    b_, l, _ = qkv.shape
    w = WINDOW
    qw = ATT_HEADS * HEAD_DIM
    kvw = KV_HEADS * HEAD_DIM
    k_blk = qw // kvw
    prev = lambda n: jnp.maximum(n - 1, 0)
    return pl.pallas_call(
        _attn_prompt_kernel,
        grid=(b_, l // w),
        in_specs=[pl.BlockSpec(memory_space=pltpu.SMEM),
                  pl.BlockSpec((1, w, qw), lambda b, n: (b, n, 0)),
                  pl.BlockSpec((1, w, kvw), lambda b, n: (b, n, k_blk)),
                  pl.BlockSpec((1, w, kvw), lambda b, n: (b, prev(n), k_blk)),
                  pl.BlockSpec((1, w, kvw), lambda b, n: (b, n, k_blk + 1)),
                  pl.BlockSpec((1, w, kvw), lambda b, n: (b, prev(n), k_blk + 1))],
        out_specs=pl.BlockSpec((1, w, qw), lambda b, n: (b, n, 0)),
        out_shape=jax.ShapeDtypeStruct((b_, l, qw), BF16),
        compiler_params=_cparams(("parallel", "parallel")),
        name="attn_prompt",
    )(sinks, qkv, qkv, qkv, qkv, qkv)


def _attn_sample_kernel(sink_ref, qkv_ref, ck_ref, cv_ref, o_ref, nk_ref, nv_ref):
    l = qkv_ref.shape[1]
    n_keep = ck_ref.shape[1]
    kcol = ATT_HEADS * HEAD_DIM
    vcol = kcol + KV_HEADS * HEAD_DIM
    qkv = qkv_ref[0]
    k_all = jnp.concatenate([ck_ref[0], qkv[:, kcol:vcol]], axis=0)
    v_all = jnp.concatenate([cv_ref[0], qkv[:, vcol:]], axis=0)
    nk_ref[0] = k_all[l:]
    nv_ref[0] = v_all[l:]
    nk = -(-(n_keep + l) // LANES) * LANES
    pad = jnp.zeros((nk - n_keep - l, k_all.shape[1]), F32)
    t = lax.broadcasted_iota(jnp.int32, (2 * l, nk), 0) % l
    j = lax.broadcasted_iota(jnp.int32, (2 * l, nk), 1)
    rel = jnp.where(j < n_keep, t + n_keep - j, t - (j - n_keep))
    valid = (rel >= 0) & (rel < WINDOW) & (j < n_keep + l)
    _slab_attention(qkv_ref, o_ref, jnp.concatenate([k_all, pad], axis=0), jnp.concatenate([v_all, pad], axis=0),
                    valid, sink_ref)


def _attn_sample(qkv, cache_k, cache_v, sinks):
    b_, l, n = qkv.shape
    n_keep, kvw = cache_k.shape[1:]
    nb = _seqs_per_step(b_, 1)
    return pl.pallas_call(
        _per_sequence(_attn_sample_kernel, nb, [False] + [True] * 6),
        grid=(b_ // nb,),
        in_specs=[pl.BlockSpec(memory_space=pltpu.SMEM),
                  pl.BlockSpec((nb, l, n), lambda b: (b, 0, 0)),
                  pl.BlockSpec((nb, n_keep, kvw), lambda b: (b, 0, 0)),
                  pl.BlockSpec((nb, n_keep, kvw), lambda b: (b, 0, 0))],
        out_specs=[pl.BlockSpec((nb, l, ATT_HEADS * HEAD_DIM), lambda b: (b, 0, 0)),
                   pl.BlockSpec((nb, n_keep, kvw), lambda b: (b, 0, 0)),
                   pl.BlockSpec((nb, n_keep, kvw), lambda b: (b, 0, 0))],
        out_shape=[jax.ShapeDtypeStruct((b_, l, ATT_HEADS * HEAD_DIM), BF16),
                   jax.ShapeDtypeStruct((b_, n_keep, kvw), F32),
                   jax.ShapeDtypeStruct((b_, n_keep, kvw), F32)],
        compiler_params=_cparams(("parallel",)),
        name="attn_sample",
    )(sinks, qkv, cache_k, cache_v)


def _rope_tables(pos):
    half = HEAD_DIM // 2
    inv_freq = ROPE_THETA ** (-jnp.arange(half, dtype=F32) / half)
    ang = pos.astype(F32)[:, None] * inv_freq[None, :]
    cos = jnp.cos(ang)
    sin = jnp.sin(ang)
    reps = LANES // HEAD_DIM
    return (jnp.tile(jnp.concatenate([cos, cos], axis=1), (1, reps)),
            jnp.tile(jnp.concatenate([-sin, sin], axis=1), (1, reps)))


def _tail_pad(buf):
    return jnp.pad(buf, ((0, 0), (TAIL_ROWS - buf.shape[1], 0), (0, 0)))


def _trunk(x, mods, pos, st, pw, seq_len):
    g_, m, d = x.shape
    nseq = g_ * m // seq_len
    blk = _block_rows(m)
    chunk = min(seq_len, SSM_CHUNK)
    lru_rows = min(seq_len, LRU_ROWS)

    proj, dtp = _inproj(x, pw['norms'][0, 0:1], mods[0], pw['w_in_t'], SSM_INNER + SSM_CONV_DIM, SSM_HEADS,
                        blk['inproj'], INPROJ_COLS)
    n_main = proj.shape[-1]
    proj_s = proj.reshape(nseq, seq_len, n_main)
    dtp_s = dtp.reshape(nseq, seq_len, LANES)
    y_ssm, ssm_h = _ssd(proj_s, dtp_s, _tail_pad(st['ssm_conv']), st['ssm'].reshape(nseq, SSM_INNER, SSM_STATE),
                        pw['ssm_conv_w'], pw['ssm_conv_b'], pw['ssm_dt_bias'], pw['ssm_a_log'], pw['ssm_d_full'],
                        pw['ssm_norm'], pw['expand'], chunk)
    y_lru, lru_h = _lru(proj_s, _tail_pad(st['lru_conv']), st['lru'].reshape(nseq, 1, LRU_WIDTH),
                        pw['lru_conv_w'], pw['lru_conv_b'], pw['lru_w_a'], pw['lru_b_a'], pw['lru_w_x'],
                        pw['lru_b_x'], pw['lru_lambda'], lru_rows, 3)
    keep = SSM_CONV - 1
    new_ssm_conv = proj_s[:, seq_len - keep:, SSM_INNER:SSM_INNER + SSM_CONV_DIM]
    new_lru_conv = proj_s[:, seq_len - keep:, n_main - LRU_WIDTH:]
    x = _outproj([y_ssm.reshape(g_, m, SSM_INNER), y_lru.reshape(g_, m, LRU_WIDTH)],
                 pw['w_out_hyb'], x, pw['norms'][0, 1:2], mods[0], blk['outproj'])
    x = _mlp(x, pw['norms'][0, 2:3], pw['norms'][0, 3:4], mods[0], pw['w_up'], pw['w_down'], 0, blk['mlp'],
             MLP_COLS)

    cos_t, sin_t = _rope_tables(pos)
    if seq_len < blk['qkv']:
        cos_t = jnp.tile(cos_t, (blk['qkv'] // seq_len, 1))
        sin_t = jnp.tile(sin_t, (blk['qkv'] // seq_len, 1))
    qkv = _qkv(x, pw['norms'][1, 0:1], mods[1], pw['w_qkv'], pw['b_qkv'], cos_t, sin_t, blk['qkv'], QKV_COLS)
    qkv_s = qkv.reshape(nseq, seq_len, qkv.shape[-1])
    kcol = ATT_HEADS * HEAD_DIM
    vcol = kcol + KV_HEADS * HEAD_DIM
    if st['k'] is None:
        o = _attn_prompt(qkv_s, pw['attn_sinks'])
        n_keep = min(WINDOW, seq_len)
        k_new = qkv_s[:, seq_len - n_keep:, kcol:vcol]
        v_new = qkv_s[:, seq_len - n_keep:, vcol:]
    else:
        n_keep = st['k'].shape[1]
        o, k_new, v_new = _attn_sample(qkv_s, st['k'].reshape(nseq, n_keep, KV_HEADS * HEAD_DIM),
                                       st['v'].reshape(nseq, n_keep, KV_HEADS * HEAD_DIM), pw['attn_sinks'])
    k_new = k_new.reshape(nseq, n_keep, KV_HEADS, HEAD_DIM)
    v_new = v_new.reshape(nseq, n_keep, KV_HEADS, HEAD_DIM)
    x = _outproj([o.reshape(g_, m, kcol)], pw['w_out_attn'], x, pw['norms'][1, 1:2], mods[1], blk['outproj'])
    x = _mlp(x, pw['norms'][1, 2:3], pw['norms'][1, 3:4], mods[1], pw['w_up'], pw['w_down'], 1, blk['mlp'],
             MLP_COLS)

    states = (new_ssm_conv[None], ssm_h.reshape(1, nseq, SSM_HEADS, SSM_HEAD_DIM, SSM_STATE),
              new_lru_conv[None], lru_h.reshape(1, nseq, LRU_WIDTH), k_new[None], v_new[None])
    return x, states


def kernel(x_prompt, x_sample, state_ssm_conv, state_ssm, state_lru_conv, state_lru, cache_k, cache_v, c_prompt, c_sample, w_mod, b_mod, norms, w_in_hyb, ssm_conv_w, ssm_conv_b, ssm_dt_bias, ssm_a_log, ssm_d, ssm_norm, lru_conv_w, lru_conv_b, lru_w_a, lru_b_a, lru_w_x, lru_b_x, lru_lambda, w_out_hyb, w_qkv, b_qkv, attn_sinks, w_out_attn, w_up, w_down):
    bp, lp, d = x_prompt.shape
    bs, ls, _ = x_sample.shape
    depth = w_mod.shape[0]

    pad_lanes = lambda v: jnp.pad(v.reshape(1, -1), ((0, 0), (0, LANES - v.size)))
    head_of_lane = jnp.arange(SSM_INNER) // SSM_HEAD_DIM
    pw = {
        'norms': norms,
        'w_in_t': jnp.swapaxes(w_in_hyb, 1, 2),
        'ssm_conv_w': ssm_conv_w[0], 'ssm_conv_b': ssm_conv_b[0].reshape(1, -1),
        'ssm_dt_bias': pad_lanes(ssm_dt_bias[0]), 'ssm_a_log': pad_lanes(ssm_a_log[0]),
        'ssm_d_full': jnp.repeat(ssm_d[0], SSM_HEAD_DIM).reshape(1, -1),
        'ssm_norm': ssm_norm[0].reshape(1, -1),
        'expand': (jnp.arange(LANES)[:, None] == head_of_lane[None, :]).astype(BF16),
        'lru_conv_w': lru_conv_w[0], 'lru_conv_b': lru_conv_b[0].reshape(1, -1),
        'lru_w_a': lru_w_a[0].astype(BF16), 'lru_b_a': lru_b_a[0].reshape(1, -1),
        'lru_w_x': lru_w_x[0].astype(BF16), 'lru_b_x': lru_b_x[0].reshape(1, -1),
        'lru_lambda': lru_lambda[0].reshape(1, -1),
        'w_out_hyb': w_out_hyb[0].astype(BF16),
        'w_qkv': w_qkv[0], 'b_qkv': b_qkv[0].reshape(1, -1),
        'attn_sinks': attn_sinks[0],
        'w_out_attn': w_out_attn[0],
        'w_up': w_up, 'w_down': w_down,
    }

    mod = _adaln(jnp.concatenate([c_prompt, c_sample], axis=0), w_mod, b_mod)
    mods_p = [mod[l, :bp][:, None, :] for l in range(depth)]
    mods_s = [jnp.repeat(mod[l, bp:], ls, axis=0)[None] for l in range(depth)]

    zeros = lambda *shape: jnp.zeros(shape, F32)
    st_p = {'ssm_conv': zeros(bp, SSM_CONV - 1, SSM_CONV_DIM), 'ssm': zeros(bp, SSM_INNER, SSM_STATE),
            'lru_conv': zeros(bp, SSM_CONV - 1, LRU_WIDTH), 'lru': zeros(bp, LRU_WIDTH), 'k': None, 'v': None}
    st_s = {'ssm_conv': state_ssm_conv[0], 'ssm': state_ssm[0], 'lru_conv': state_lru_conv[0],
            'lru': state_lru[0], 'k': cache_k[0], 'v': cache_v[0]}

    pos_p = jnp.arange(lp, dtype=jnp.int32)
    pos_s = PAST_LEN + jnp.arange(ls, dtype=jnp.int32)
    y_p, sp = _trunk(x_prompt, mods_p, pos_p, st_p, pw, lp)
    y_s, ss = _trunk(x_sample.reshape(1, bs * ls, d), mods_s, pos_s, st_s, pw, ls)
    return (y_p, y_s.reshape(bs, ls, d)) + sp + ss
```

```python
import functools
import math

import jax
import jax.numpy as jnp
from jax import lax
from jax.experimental import pallas as pl
from jax.experimental.pallas import tpu as pltpu

F32 = jnp.float32
BF16 = jnp.bfloat16

D_MODEL = 2048
PAST_LEN = 16384
SSM_HEADS = 32
SSM_HEAD_DIM = 64
SSM_INNER = SSM_HEADS * SSM_HEAD_DIM
SSM_GROUPS = 8
SSM_STATE = 128
SSM_CONV = 4
SSM_CHUNK = 128
SSM_CONV_DIM = SSM_INNER + 2 * SSM_GROUPS * SSM_STATE
HEADS_PER_GROUP = SSM_HEADS // SSM_GROUPS
GROUP_WIDTH = HEADS_PER_GROUP * SSM_HEAD_DIM
LRU_WIDTH = D_MODEL
LRU_BLOCKS = 8
LRU_BLOCK = LRU_WIDTH // LRU_BLOCKS
LRU_C = 8.0
ATT_HEADS = 32
KV_HEADS = 8
HEAD_DIM = 64
Q_PER_KV = ATT_HEADS // KV_HEADS
WINDOW = 128
ROPE_THETA = 10000.0
D_FF = 4 * D_MODEL
EPS = 1e-6

LANES = 128
SUBLANES = 8
BF16_SUBLANES = 16
TAIL_ROWS = SUBLANES
VMEM_LIMIT = 56 * 1024 * 1024


ADALN_COLS = 2048
INPROJ_COLS = 1024
MLP_COLS = 512
QKV_COLS = 512
LRU_ROWS = 512


def _block_rows(m):
    return {'inproj': min(m, 1024), 'mlp': min(m, 1024), 'qkv': min(m, 512), 'outproj': min(m, 512)}


SEQS_PER_STEP = 4


def _seqs_per_step(n_seq, n_chunks):
    return SEQS_PER_STEP if n_chunks == 1 and n_seq % SEQS_PER_STEP == 0 and n_seq > SEQS_PER_STEP else 1


def _per_sequence(body, nb, per_seq):
    if nb == 1:
        return body

    def kernel(*refs):
        def one(n, carry):
            body(*[r.at[pl.ds(n, 1)] if flag else r for r, flag in zip(refs, per_seq)])
            return carry

        lax.fori_loop(0, nb, one, 0)

    return kernel


def _cparams(sem):
    return pltpu.CompilerParams(dimension_semantics=sem, vmem_limit_bytes=VMEM_LIMIT)


def _rms(x, g):
    return x * lax.rsqrt(jnp.mean(x * x, axis=-1, keepdims=True) + EPS) * g


def _gated_postnorm(x, acc, g_ref, gt_ref):
    if gt_ref.shape[1] == 1:
        return x + _rms(acc, g_ref[...] * gt_ref[0])
    return x + gt_ref[0] * _rms(acc, g_ref[...])


ROW_CHUNK = 16
ROW_UNROLL = 4


def _for_row_chunks(n_rows, body):
    def step(i, carry):
        body(pl.multiple_of(i * ROW_CHUNK, ROW_CHUNK))
        return carry

    lax.fori_loop(0, n_rows // ROW_CHUNK, step, 0, unroll=ROW_UNROLL)


def _mod_rows(ref, r0):
    return ref[0] if ref.shape[1] == 1 else ref[0, pl.ds(r0, ROW_CHUNK), :]


def _prenorm_to(h_scr, x_ref, g_ref, sc_ref, sh_ref):
    one_seq = sc_ref.shape[1] == 1
    gain = g_ref[...] * (1.0 + sc_ref[0]) if one_seq else g_ref[...]

    def body(r0):
        x = x_ref[0, pl.ds(r0, ROW_CHUNK), :]
        h = _rms(x, gain) if one_seq else _rms(x, gain) * (1.0 + _mod_rows(sc_ref, r0))
        h_scr[pl.ds(r0, ROW_CHUNK), :] = (h + _mod_rows(sh_ref, r0)).astype(BF16)

    _for_row_chunks(x_ref.shape[1], body)


def _sigmoid(x):
    return 0.5 * (jnp.tanh(0.5 * x) + 1.0)


def _silu(x):
    return x * _sigmoid(x)


def _softplus(x):
    return jnp.maximum(x, 0.0) + jnp.log1p(jnp.exp(-jnp.abs(x)))


def _sqrt_neg_expm1(x):
    s = jnp.tanh(-0.5 * x)
    t = (2.0 * s) * (1.0 + s)
    return jnp.where(t > 0.0, (2.0 * s) * lax.rsqrt(t), 0.0)


def _split3(x):
    hi = x.astype(BF16)
    r = x - hi.astype(F32)
    mid = r.astype(BF16)
    lo = (r - mid.astype(F32)).astype(BF16)
    return [hi, mid, lo]


def _adaln_kernel(c_ref, w_ref, b_ref, o_ref):
    s = _silu(c_ref[...]).astype(BF16)
    o_ref[0] = jnp.dot(s, w_ref[0].astype(BF16), preferred_element_type=F32) + b_ref[0]


def _adaln(c_all, w_mod, b_mod):
    nb, d = c_all.shape
    depth, _, n = w_mod.shape
    bn = ADALN_COLS
    return pl.pallas_call(
        _adaln_kernel,
        grid=(depth, n // bn),
        in_specs=[pl.BlockSpec((nb, d), lambda l, j: (0, 0)),
                  pl.BlockSpec((1, d, bn), lambda l, j: (l, 0, j)),
                  pl.BlockSpec((1, 1, bn), lambda l, j: (l, 0, j))],
        out_specs=pl.BlockSpec((1, nb, bn), lambda l, j: (l, 0, j)),
        out_shape=jax.ShapeDtypeStruct((depth, nb, n), F32),
        compiler_params=_cparams(("parallel", "parallel")),
        name="adaln",
    )(c_all, w_mod, b_mod.reshape(depth, 1, n))


def _mod_spec(mod, bm, k):
    if mod.shape[1] == 1:
        return pl.BlockSpec((1, 1, D_MODEL), lambda g, i, *_: (g, 0, k))
    return pl.BlockSpec((1, bm, D_MODEL), lambda g, i, *_: (g, i, k))


def _inproj_kernel(x_ref, g_ref, sh_ref, sc_ref, wt_ref, wdt_ref, o_ref, dt_ref, h_scr, *, n_dt):
    nt = (((1,), (1,)), ((), ()))

    @pl.when(pl.program_id(2) == 0)
    def _():
        _prenorm_to(h_scr, x_ref, g_ref, sc_ref, sh_ref)
        dt = lax.dot_general(h_scr[...], wdt_ref[...].astype(BF16), nt, preferred_element_type=F32)
        lane = lax.broadcasted_iota(jnp.int32, dt.shape, 1)
        dt_ref[0] = jnp.where(lane < n_dt, dt, 0.0)

    o_ref[0] = lax.dot_general(h_scr[...], wt_ref[...].astype(BF16), nt, preferred_element_type=F32)


def _inproj(x, gain, mod, w_t, c_dt, n_dt, bm, bn):
    g_, m, d = x.shape
    n = w_t.shape[1] - n_dt
    row0 = lambda j: pl.multiple_of(jnp.where(j * bn < c_dt, j * bn, j * bn + n_dt), SUBLANES)
    return pl.pallas_call(
        functools.partial(_inproj_kernel, n_dt=n_dt),
        grid=(g_, m // bm, n // bn),
        in_specs=[pl.BlockSpec((1, bm, d), lambda g, i, j: (g, i, 0)),
                  pl.BlockSpec((1, d), lambda g, i, j: (0, 0)),
                  _mod_spec(mod, bm, 0), _mod_spec(mod, bm, 1),
                  pl.BlockSpec((None, pl.Element(bn), pl.Element(d)), lambda g, i, j: (0, row0(j), 0)),
                  pl.BlockSpec((None, pl.Element(LANES), pl.Element(d)), lambda g, i, j: (0, c_dt, 0))],
        out_specs=[pl.BlockSpec((1, bm, bn), lambda g, i, j: (g, i, j)),
                   pl.BlockSpec((1, bm, LANES), lambda g, i, j: (g, i, 0))],
        out_shape=[jax.ShapeDtypeStruct((g_, m, n), F32),
                   jax.ShapeDtypeStruct((g_, m, LANES), F32)],
        scratch_shapes=[pltpu.VMEM((bm, d), BF16)],
        compiler_params=_cparams(("parallel", "parallel", "arbitrary")),
        name="inproj",
    )(x, gain, mod, mod, w_t, w_t)


def _swap_halves(x):
    lane = lax.broadcasted_iota(jnp.int32, x.shape, 1)
    first = (lane % HEAD_DIM) < (HEAD_DIM // 2)
    return jnp.where(first, pltpu.roll(x, LANES - HEAD_DIM // 2, 1), pltpu.roll(x, HEAD_DIM // 2, 1))


def _qkv_kernel(x_ref, g_ref, sh_ref, sc_ref, w_ref, b_ref, cos_ref, sin_ref, o_ref, h_scr, *, rope_cols, bn):
    _prenorm_to(h_scr, x_ref, g_ref, sc_ref, sh_ref)
    h = h_scr[...]
    cos = cos_ref[...]
    sin = sin_ref[...]
    for c0 in range(0, w_ref.shape[1], bn):
        acc = jnp.dot(h, w_ref[:, c0:c0 + bn].astype(BF16), preferred_element_type=F32) + b_ref[:, c0:c0 + bn]
        if c0 < rope_cols:
            for c in range(c0, c0 + bn, LANES):
                a = acc[:, c - c0:c - c0 + LANES]
                o_ref[0, :, c:c + LANES] = a * cos + _swap_halves(a) * sin
        else:
            o_ref[0, :, c0:c0 + bn] = acc


def _qkv(x, gain, mod, w, b, cos_t, sin_t, bm, bn):
    g_, m, d = x.shape
    n = w.shape[1]
    rope_cols = (ATT_HEADS + KV_HEADS) * HEAD_DIM
    rows_per_g = cos_t.shape[0] // bm
    return pl.pallas_call(
        functools.partial(_qkv_kernel, rope_cols=rope_cols, bn=bn),
        grid=(g_, m // bm),
        in_specs=[pl.BlockSpec((1, bm, d), lambda g, i: (g, i, 0)),
                  pl.BlockSpec((1, d), lambda g, i: (0, 0)),
                  _mod_spec(mod, bm, 0), _mod_spec(mod, bm, 1),
                  pl.BlockSpec((d, n), lambda g, i: (0, 0), pipeline_mode=pl.Buffered(1)),
                  pl.BlockSpec((1, n), lambda g, i: (0, 0)),
                  pl.BlockSpec((bm, LANES), lambda g, i: (i % rows_per_g, 0)),
                  pl.BlockSpec((bm, LANES), lambda g, i: (i % rows_per_g, 0))],
        out_specs=pl.BlockSpec((1, bm, n), lambda g, i: (g, i, 0)),
        out_shape=jax.ShapeDtypeStruct((g_, m, n), F32),
        scratch_shapes=[pltpu.VMEM((bm, d), BF16)],
        compiler_params=_cparams(("parallel", "parallel")),
        name="qkv",
    )(x, gain, mod, mod, w, b, cos_t, sin_t)


def _outproj_kernel(*refs, n_lhs):
    lhs = refs[:n_lhs]
    ws = refs[n_lhs:2 * n_lhs]
    x_ref, g_ref, gt_ref, o_ref = refs[2 * n_lhs:]
    acc = jnp.dot(lhs[0][0], ws[0][...].astype(BF16), preferred_element_type=F32)
    for a, w in zip(lhs[1:], ws[1:]):
        acc = acc + jnp.dot(a[0], w[...].astype(BF16), preferred_element_type=F32)
    o_ref[0] = _gated_postnorm(x_ref[0], acc, g_ref, gt_ref)


def _outproj(lhs_list, w, x, gain, mod, bm):
    g_, m, d = x.shape
    n_lhs = len(lhs_list)
    kw = w.shape[0] // n_lhs
    lhs_specs = [pl.BlockSpec((1, bm, kw), lambda g, i: (g, i, 0)) for _ in lhs_list]
    w_specs = [pl.BlockSpec((kw, d), lambda g, i, k=k: (k, 0), pipeline_mode=pl.Buffered(1)) for k in range(n_lhs)]
    w_list = [w] * n_lhs
    if mod.shape[1] == 1:
        gt_spec = pl.BlockSpec((1, 1, d), lambda g, i: (g, 0, 2))
    else:
        gt_spec = pl.BlockSpec((1, bm, d), lambda g, i: (g, i, 2))
    return pl.pallas_call(
        functools.partial(_outproj_kernel, n_lhs=n_lhs),
        grid=(g_, m // bm),
        in_specs=lhs_specs + w_specs + [pl.BlockSpec((1, bm, d), lambda g, i: (g, i, 0)),
                                        pl.BlockSpec((1, d), lambda g, i: (0, 0)), gt_spec],
        out_specs=pl.BlockSpec((1, bm, d), lambda g, i: (g, i, 0)),
        out_shape=jax.ShapeDtypeStruct((g_, m, d), F32),
        compiler_params=_cparams(("parallel", "parallel")),
        name="outproj",
    )(*lhs_list, *w_list, x, gain, mod)


def _mlp_kernel(x_ref, g2_ref, sh_ref, sc_ref, gt_ref, wu_ref, wd_ref, g3_ref, o_ref, h_scr):
    f = pl.program_id(2)

    @pl.when(f == 0)
    def _():
        _prenorm_to(h_scr, x_ref, g2_ref, sc_ref, sh_ref)
        o_ref[0] = jnp.zeros(o_ref.shape[1:], F32)

    u = jnp.dot(h_scr[...], wu_ref[...].astype(BF16), preferred_element_type=F32)
    u = jnp.square(jnp.maximum(u, 0.0)).astype(BF16)
    o_ref[0] += jnp.dot(u, wd_ref[...].astype(BF16), preferred_element_type=F32)

    @pl.when(f == pl.num_programs(2) - 1)
    def _():
        o_ref[0] = _gated_postnorm(x_ref[0], o_ref[0], g3_ref, gt_ref)


def _mlp(x, g2, g3, mod, w_up, w_down, layer, bm, bf):
    g_, m, d = x.shape
    dff = w_up.shape[2]
    if mod.shape[1] == 1:
        mspec = lambda k: pl.BlockSpec((1, 1, d), lambda g, i, f: (g, 0, k))
    else:
        mspec = lambda k: pl.BlockSpec((1, bm, d), lambda g, i, f: (g, i, k))
    return pl.pallas_call(
        _mlp_kernel,
        grid=(g_, m // bm, dff // bf),
        in_specs=[pl.BlockSpec((1, bm, d), lambda g, i, f: (g, i, 0), pipeline_mode=pl.Buffered(1)),
                  pl.BlockSpec((1, d), lambda g, i, f: (0, 0)),
                  mspec(3), mspec(4), mspec(5),
                  pl.BlockSpec((None, d, bf), lambda g, i, f: (layer, 0, f)),
                  pl.BlockSpec((None, bf, d), lambda g, i, f: (layer, f, 0)),
                  pl.BlockSpec((1, d), lambda g, i, f: (0, 0))],
        out_specs=pl.BlockSpec((1, bm, d), lambda g, i, f: (g, i, 0)),
        out_shape=jax.ShapeDtypeStruct((g_, m, d), F32),
        scratch_shapes=[pltpu.VMEM((bm, d), BF16)],
        compiler_params=_cparams(("parallel", "parallel", "arbitrary")),
        name="mlp",
    )(x, g2, mod, mod, mod, w_up, w_down, g3)


def _dwconv(raw, tail, w, b):
    t = raw.shape[0]
    assert w.shape[0] == 4
    ext = jnp.concatenate([tail, raw], axis=0)
    z1 = pltpu.roll(ext, 1, 0)
    near = ext * w[3:4] + z1 * w[2:3]
    far = ext * w[1:2] + z1 * w[0:1]
    y = near + pltpu.roll(far, 2, 0)
    return y[TAIL_ROWS:] + b, ext[t:t + TAIL_ROWS]


def _cumsum_rows(x):
    n = x.shape[0]
    row = lax.broadcasted_iota(jnp.int32, x.shape, 0)
    d = 1
    while d < n:
        x = x + jnp.where(row >= d, pltpu.roll(x, d, 0), 0.0)
        d *= 2
    return x


def _ssd_kernel(z_ref, xs_ref, bc_ref, dt_ref, buf_ref, h0_ref, cw_ref, cb_ref, dtb_ref, alog_ref,
                dsk_ref, ng_ref, exp_ref, y_ref, st_ref, tail_scr, *, q_valid, q):
    c = pl.program_id(1)

    @pl.when(c == 0)
    def _():
        tail_scr[...] = buf_ref[0]
        st_ref[0] = h0_ref[0]

    def padded(v):
        if q_valid == q:
            return v
        return jnp.concatenate([v, jnp.zeros((q - q_valid, v.shape[1]), v.dtype)], axis=0)

    def transposed(v):
        if q == LANES:
            return v.T
        return jnp.concatenate([v, jnp.zeros((LANES - q, LANES), v.dtype)], axis=0).T[:, :q]

    cw = cw_ref[...]
    cb = cb_ref[...]
    xs, tail_x = _dwconv(xs_ref[0], tail_scr[:, :SSM_INNER], cw[:, :SSM_INNER], cb[:, :SSM_INNER])
    bcm, tail_b = _dwconv(bc_ref[0], tail_scr[:, SSM_INNER:], cw[:, SSM_INNER:], cb[:, SSM_INNER:])
    tail_scr[:, :SSM_INNER] = tail_x
    tail_scr[:, SSM_INNER:] = tail_b
    xs = padded(_silu(xs))
    bcm = padded(_silu(bcm))
    z = padded(z_ref[0])

    dt = _softplus(dt_ref[0] + dtb_ref[...])
    dt = padded(dt)
    a = -jnp.exp(alog_ref[...])
    acs = _cumsum_rows(dt * a)
    acs_t = transposed(acs)

    full = jnp.dot(jnp.concatenate(_split3(dt) + _split3(acs), axis=0), exp_ref[...], preferred_element_type=F32)
    dt_full = (full[0:q] + full[q:2 * q]) + full[2 * q:3 * q]
    acs_full = (full[3 * q:4 * q] + full[4 * q:5 * q]) + full[5 * q:6 * q]
    xdt = xs * dt_full
    dec_out = jnp.exp(acs_full)
    xw = (xdt * jnp.exp(acs_full[q - 1:q] - acs_full)).astype(BF16)
    xdt_b = xdt.astype(BF16)
    bc_b = bcm.astype(BF16)

    row = lax.broadcasted_iota(jnp.int32, (q, q), 0)
    col = lax.broadcasted_iota(jnp.int32, (q, q), 1)
    causal = row >= col
    lane = lax.broadcasted_iota(jnp.int32, (q, LANES), 1)
    lo_half = lane < SSM_HEAD_DIM
    nt = (((1,), (1,)), ((), ()))
    tn = (((0,), (0,)), ((), ()))
    n_bc = SSM_GROUPS * SSM_STATE

    groups = range(SSM_GROUPS)
    cols = [slice(g * GROUP_WIDTH, (g + 1) * GROUP_WIDTH) for g in groups]
    b_gs = [bc_b[:, g * SSM_STATE:(g + 1) * SSM_STATE] for g in groups]
    c_gs = [bc_b[:, n_bc + g * SSM_STATE:n_bc + (g + 1) * SSM_STATE] for g in groups]
    h_gs = [st_ref[0, cols[g], :] for g in groups]
    cbms = [lax.dot_general(c_gs[g], b_gs[g], nt, preferred_element_type=F32) for g in groups]
    y_offs = [lax.dot_general(c_gs[g], h_gs[g].astype(BF16), nt, preferred_element_type=F32) for g in groups]
    st_news = [lax.dot_general(xw[:, cols[g]], b_gs[g], tn, preferred_element_type=F32) for g in groups]
    ms = [(cbms[h // HEADS_PER_GROUP]
           * jnp.exp(jnp.where(causal, acs[:, h:h + 1] - acs_t[h:h + 1, :], -jnp.inf))).astype(BF16)
          for h in range(SSM_HEADS)]
    y_diags = []
    for pair in range(SSM_HEADS // 2):
        x_pair = xdt_b[:, pair * LANES:(pair + 1) * LANES]
        zero = jnp.zeros_like(x_pair)
        y_diags.append(jnp.dot(ms[2 * pair], jnp.where(lo_half, x_pair, zero), preferred_element_type=F32)
                       + jnp.dot(ms[2 * pair + 1], jnp.where(lo_half, zero, x_pair), preferred_element_type=F32))
    pairs_per_group = HEADS_PER_GROUP // 2
    for g in groups:
        y_g = y_offs[g] * dec_out[:, cols[g]]
        y_g = y_g + jnp.concatenate(y_diags[g * pairs_per_group:(g + 1) * pairs_per_group], axis=1)
        y_g = y_g + xs[:, cols[g]] * dsk_ref[:, cols[g]]
        y_g = y_g * _silu(z[:, cols[g]])
        y_g = y_g * lax.rsqrt(jnp.mean(y_g * y_g, axis=-1, keepdims=True) + EPS)
        y_g = y_g * ng_ref[:, cols[g]]
        y_ref[0, :, cols[g]] = y_g[:q_valid].astype(y_ref.dtype)
    for h in range(SSM_HEADS):
        g, r = divmod(h, HEADS_PER_GROUP)
        rows = slice(r * SSM_HEAD_DIM, (r + 1) * SSM_HEAD_DIM)
        decay = jnp.exp(acs_t[h:h + 1, q - 1:q])
        st_ref[0, h * SSM_HEAD_DIM:(h + 1) * SSM_HEAD_DIM, :] = h_gs[g][rows] * decay + st_news[g][rows]


def _ssd(proj, dtp, buf8, h0, conv_w, conv_b, dt_bias, a_log, d_full, norm_g, expand, rows):
    b_, l, _ = proj.shape
    nc = l // rows
    nb = _seqs_per_step(b_, nc)
    st_rows = SSM_INNER
    kernel_fn = functools.partial(_ssd_kernel, q_valid=rows, q=max(rows, BF16_SUBLANES))
    return pl.pallas_call(
        _per_sequence(kernel_fn, nb, [True] * 6 + [False] * 7 + [True] * 2 + [False]),
        grid=(b_ // nb, nc),
        in_specs=[pl.BlockSpec((nb, rows, SSM_INNER), lambda b, c: (b, c, 0)),
                  pl.BlockSpec((nb, rows, SSM_INNER), lambda b, c: (b, c, 1)),
                  pl.BlockSpec((nb, rows, SSM_INNER), lambda b, c: (b, c, 2)),
                  pl.BlockSpec((nb, rows, LANES), lambda b, c: (b, c, 0)),
                  pl.BlockSpec((nb, TAIL_ROWS, SSM_CONV_DIM), lambda b, c: (b, 0, 0)),
                  pl.BlockSpec((nb, st_rows, SSM_STATE), lambda b, c: (b, 0, 0)),
                  pl.BlockSpec((SSM_CONV, SSM_CONV_DIM), lambda b, c: (0, 0)),
                  pl.BlockSpec((1, SSM_CONV_DIM), lambda b, c: (0, 0)),
                  pl.BlockSpec((1, LANES), lambda b, c: (0, 0)),
                  pl.BlockSpec((1, LANES), lambda b, c: (0, 0)),
                  pl.BlockSpec((1, SSM_INNER), lambda b, c: (0, 0)),
                  pl.BlockSpec((1, SSM_INNER), lambda b, c: (0, 0)),
                  pl.BlockSpec((LANES, SSM_INNER), lambda b, c: (0, 0))],
        out_specs=[pl.BlockSpec((nb, rows, SSM_INNER), lambda b, c: (b, c, 0)),
                   pl.BlockSpec((nb, st_rows, SSM_STATE), lambda b, c: (b, 0, 0))],
        out_shape=[jax.ShapeDtypeStruct((b_, l, SSM_INNER), BF16),
                   jax.ShapeDtypeStruct((b_, st_rows, SSM_STATE), F32)],
        scratch_shapes=[pltpu.VMEM((TAIL_ROWS, SSM_CONV_DIM), F32)],
        compiler_params=_cparams(("parallel", "arbitrary")),
        name="ssd",
    )(proj, proj, proj, dtp, buf8, h0, conv_w, conv_b, dt_bias, a_log, d_full, norm_g, expand)


SCAN_SEG = 4
SCAN_ROWS = SUBLANES * SCAN_SEG


def _shift_rows_fill(x, d, fill, row):
    return jnp.where(row >= d, pltpu.roll(x, d, 0), fill)


def _sublane_scan(e, pe, row):
    d = 1
    while d < SUBLANES:
        e = e + pe * _shift_rows_fill(e, d, 0.0, row)
        pe = pe * _shift_rows_fill(pe, d, 1.0, row)
        d *= 2
    return e, pe


def _linear_scan(a_scr, u_scr, h0):
    n_slab, t, _ = a_scr.shape
    seg = SCAN_SEG if t % SCAN_ROWS == 0 else 1
    row = lax.broadcasted_iota(jnp.int32, (SUBLANES, LANES), 0)

    def block(i, h_prev):
        base = pl.multiple_of(i * (SUBLANES * seg), SUBLANES)
        out = []
        for c in range(n_slab):
            rows = [pl.ds(base + j, SUBLANES, stride=seg) if seg > 1 else pl.ds(base, SUBLANES) for j in range(seg)]
            hs, ps = [], []
            for j in range(seg):
                a, u = a_scr[c, rows[j], :], u_scr[c, rows[j], :]
                hs.append(u if j == 0 else a * hs[-1] + u)
                ps.append(a if j == 0 else a * ps[-1])
            e, pe = _sublane_scan(hs[-1], ps[-1], row)
            g = e + pe * h_prev[c]
            carry = jnp.where(row == 0, h_prev[c], pltpu.roll(g, 1, 0))
            for j in range(seg):
                u_scr[c, rows[j], :] = hs[j] + ps[j] * carry
            out.append(jnp.broadcast_to(g[SUBLANES - 1:], g.shape))
        return tuple(out)

    h_init = tuple(jnp.broadcast_to(h0[:, c * LANES:(c + 1) * LANES], (SUBLANES, LANES)) for c in range(n_slab))
    h_fin = lax.fori_loop(0, t // (SUBLANES * seg), block, h_init)
    return jnp.concatenate([h[:1] for h in h_fin], axis=1)


def _gelu_tanh(x):
    c0 = math.sqrt(2.0 / math.pi)
    hx = 0.5 * x
    return hx + hx * jnp.tanh(x * (c0 + (c0 * 0.044715) * (x * x)))


def _lru_kernel(gate_ref, xr_ref, buf_ref, h0_ref, cw_ref, cb_ref, wa_ref, ba_ref, wx_ref, bx_ref,
                lam_ref, y_ref, hl_ref, tail_scr, a_scr, u_scr):
    c = pl.program_id(1)

    @pl.when(c == 0)
    def _():
        tail_scr[...] = buf_ref[0]
        hl_ref[0] = h0_ref[0]

    xc, tail = _dwconv(xr_ref[0], tail_scr[...], cw_ref[...], cb_ref[...])
    tail_scr[...] = tail
    sp = _softplus(-lam_ref[...])
    slabs_per_block = LRU_BLOCK // LANES
    for k in range(LRU_BLOCKS):
        k0 = k * LRU_BLOCK
        xk = xc[:, k0:k0 + LRU_BLOCK]
        xkb = xk.astype(BF16)
        gr = _sigmoid(jnp.dot(xkb, wa_ref[k], preferred_element_type=F32) + ba_ref[:, k0:k0 + LRU_BLOCK])
        gi = _sigmoid(jnp.dot(xkb, wx_ref[k], preferred_element_type=F32) + bx_ref[:, k0:k0 + LRU_BLOCK])
        log_a = -LRU_C * gr * sp[:, k0:k0 + LRU_BLOCK]
        a = jnp.exp(log_a)
        u = _sqrt_neg_expm1(2.0 * log_a) * gi * xk
        for s in range(slabs_per_block):
            a_scr[k * slabs_per_block + s] = a[:, s * LANES:(s + 1) * LANES]
            u_scr[k * slabs_per_block + s] = u[:, s * LANES:(s + 1) * LANES]

    hl_ref[0] = _linear_scan(a_scr, u_scr, hl_ref[0])
    for s in range(a_scr.shape[0]):
        gate = gate_ref[0, :, s * LANES:(s + 1) * LANES]
        y_ref[0, :, s * LANES:(s + 1) * LANES] = (u_scr[s] * _gelu_tanh(gate)).astype(y_ref.dtype)


def _lru(proj, buf8, h0, conv_w, conv_b, w_a, b_a, w_x, b_x, lam, rows, col0):
    b_, l, _ = proj.shape
    w = LRU_WIDTH
    nb = _seqs_per_step(b_, l // rows)
    full2 = lambda shape: pl.BlockSpec(shape, lambda b, c: (0,) * len(shape))
    return pl.pallas_call(
        _per_sequence(_lru_kernel, nb, [True] * 4 + [False] * 7 + [True] * 2 + [False] * 3),
        grid=(b_ // nb, l // rows),
        in_specs=[pl.BlockSpec((nb, rows, w), lambda b, c: (b, c, col0)),
                  pl.BlockSpec((nb, rows, w), lambda b, c: (b, c, col0 + 1)),
                  pl.BlockSpec((nb, TAIL_ROWS, w), lambda b, c: (b, 0, 0)),
                  pl.BlockSpec((nb, 1, w), lambda b, c: (b, 0, 0)),
                  full2((SSM_CONV, w)), full2((1, w)),
                  full2((LRU_BLOCKS, LRU_BLOCK, LRU_BLOCK)), full2((1, w)),
                  full2((LRU_BLOCKS, LRU_BLOCK, LRU_BLOCK)), full2((1, w)),
                  full2((1, w))],
        out_specs=[pl.BlockSpec((nb, rows, w), lambda b, c: (b, c, 0)),
                   pl.BlockSpec((nb, 1, w), lambda b, c: (b, 0, 0))],
        out_shape=[jax.ShapeDtypeStruct((b_, l, w), BF16),
                   jax.ShapeDtypeStruct((b_, 1, w), F32)],
        scratch_shapes=[pltpu.VMEM((TAIL_ROWS, w), F32), pltpu.VMEM((w // LANES, rows, LANES), F32),
                        pltpu.VMEM((w // LANES, rows, LANES), F32)],
        compiler_params=_cparams(("parallel", "arbitrary")),
        name="lru",
    )(proj, proj, buf8, h0, conv_w, conv_b, w_a, b_a, w_x, b_x, lam)


def _head_variants(slab, half):
    lane = lax.broadcasted_iota(jnp.int32, slab.shape, 1)
    keep = (lane < HEAD_DIM) if half == 0 else (lane >= HEAD_DIM)
    own = jnp.where(keep, slab, 0.0)
    other = pltpu.roll(own, HEAD_DIM, 1)
    return (own, other) if half == 0 else (other, own)


def _slab_attention(q_ref, o_ref, k2, v2, valid, sink_ref):
    w = q_ref.shape[1]
    nk = k2.shape[0]
    nt = (((1,), (1,)), ((), ()))
    scale = HEAD_DIM ** -0.5
    ones = jnp.ones((nk, LANES), BF16)
    slabs = range(nk // LANES)
    heads = range(KV_HEADS)
    halves = [(e, c) for e in heads for c in range(2)]
    vs, ss = [], []
    for e in heads:
        slab, half = divmod(e, 2)
        k_lo, k_hi = _head_variants(k2[:, slab * LANES:(slab + 1) * LANES], half)
        vs.extend(v.astype(BF16) for v in _head_variants(v2[:, slab * LANES:(slab + 1) * LANES], half))
        kk = jnp.concatenate([k_lo, k_hi], axis=0).astype(BF16)
        c0 = 2 * e * LANES
        qq = jnp.concatenate([q_ref[0, :, c0:c0 + LANES], q_ref[0, :, c0 + LANES:c0 + 2 * LANES]], axis=0)
        ss.append(lax.dot_general((qq * scale).astype(BF16), kk, nt, preferred_element_type=F32))
    scs = [jnp.where(valid, ss[e][:, c * nk:(c + 1) * nk], -jnp.inf) for e, c in halves]
    sinks = [jnp.concatenate([jnp.full((w, LANES), sink_ref[Q_PER_KV * e + c], F32),
                              jnp.full((w, LANES), sink_ref[Q_PER_KV * e + 2 + c], F32)], axis=0) for e, c in halves]
    ms = [jnp.maximum(jnp.broadcast_to(jnp.max(sc, axis=1, keepdims=True), sk.shape), sk)
          for sc, sk in zip(scs, sinks)]
    ps = [jnp.concatenate([jnp.exp(sc[:, i * LANES:(i + 1) * LANES] - m) for i in slabs], axis=1).astype(BF16)
          for sc, m in zip(scs, ms)]
    dens = [jnp.dot(p, ones, preferred_element_type=F32) + jnp.exp(sk - m) for p, sk, m in zip(ps, sinks, ms)]
    outs = [jnp.dot(p, v, preferred_element_type=F32) * (1.0 / den) for p, v, den in zip(ps, vs, dens)]
    for e in heads:
        o = outs[2 * e] + outs[2 * e + 1]
        c0 = 2 * e * LANES
        o_ref[0, :, c0:c0 + LANES] = o[:w].astype(o_ref.dtype)
        o_ref[0, :, c0 + LANES:c0 + 2 * LANES] = o[w:].astype(o_ref.dtype)


def _attn_prompt_kernel(sink_ref, q_ref, kc_ref, kp_ref, vc_ref, vp_ref, o_ref):
    w = WINDOW
    has_prev = pl.program_id(1) > 0
    k2 = jnp.concatenate([jnp.where(has_prev, kp_ref[0], 0.0), kc_ref[0]], axis=0)
    v2 = jnp.concatenate([jnp.where(has_prev, vp_ref[0], 0.0), vc_ref[0]], axis=0)
    t = lax.broadcasted_iota(jnp.int32, (2 * w, 2 * w), 0) % w
    j = lax.broadcasted_iota(jnp.int32, (2 * w, 2 * w), 1)
    _slab_attention(q_ref, o_ref, k2, v2, (j > t) & (j <= t + w), sink_ref)


def _attn_prompt(qkv, sinks):
    b_, l, _ = qkv.shape
    w = WINDOW
    qw = ATT_HEADS * HEAD_DIM
    kvw = KV_HEADS * HEAD_DIM
    k_blk = qw // kvw
    prev = lambda n: jnp.maximum(n - 1, 0)
    return pl.pallas_call(
        _attn_prompt_kernel,
        grid=(b_, l // w),
        in_specs=[pl.BlockSpec(memory_space=pltpu.SMEM),
                  pl.BlockSpec((1, w, qw), lambda b, n: (b, n, 0)),
                  pl.BlockSpec((1, w, kvw), lambda b, n: (b, n, k_blk)),
                  pl.BlockSpec((1, w, kvw), lambda b, n: (b, prev(n), k_blk)),
                  pl.BlockSpec((1, w, kvw), lambda b, n: (b, n, k_blk + 1)),
                  pl.BlockSpec((1, w, kvw), lambda b, n: (b, prev(n), k_blk + 1))],
        out_specs=pl.BlockSpec((1, w, qw), lambda b, n: (b, n, 0)),
        out_shape=jax.ShapeDtypeStruct((b_, l, qw), BF16),
        compiler_params=_cparams(("parallel", "parallel")),
        name="attn_prompt",
    )(sinks, qkv, qkv, qkv, qkv, qkv)


def _attn_sample_kernel(sink_ref, qkv_ref, ck_ref, cv_ref, o_ref, nk_ref, nv_ref):
    l = qkv_ref.shape[1]
    n_keep = ck_ref.shape[1]
    kcol = ATT_HEADS * HEAD_DIM
    vcol = kcol + KV_HEADS * HEAD_DIM
    qkv = qkv_ref[0]
    k_all = jnp.concatenate([ck_ref[0], qkv[:, kcol:vcol]], axis=0)
    v_all = jnp.concatenate([cv_ref[0], qkv[:, vcol:]], axis=0)
    nk_ref[0] = k_all[l:]
    nv_ref[0] = v_all[l:]
    nk = -(-(n_keep + l) // LANES) * LANES
    pad = jnp.zeros((nk - n_keep - l, k_all.shape[1]), F32)
    t = lax.broadcasted_iota(jnp.int32, (2 * l, nk), 0) % l
    j = lax.broadcasted_iota(jnp.int32, (2 * l, nk), 1)
    rel = jnp.where(j < n_keep, t + n_keep - j, t - (j - n_keep))
    valid = (rel >= 0) & (rel < WINDOW) & (j < n_keep + l)
    _slab_attention(qkv_ref, o_ref, jnp.concatenate([k_all, pad], axis=0), jnp.concatenate([v_all, pad], axis=0),
                    valid, sink_ref)


def _attn_sample(qkv, cache_k, cache_v, sinks):
    b_, l, n = qkv.shape
    n_keep, kvw = cache_k.shape[1:]
    nb = _seqs_per_step(b_, 1)
    return pl.pallas_call(
        _per_sequence(_attn_sample_kernel, nb, [False] + [True] * 6),
        grid=(b_ // nb,),
        in_specs=[pl.BlockSpec(memory_space=pltpu.SMEM),
                  pl.BlockSpec((nb, l, n), lambda b: (b, 0, 0)),
                  pl.BlockSpec((nb, n_keep, kvw), lambda b: (b, 0, 0)),
                  pl.BlockSpec((nb, n_keep, kvw), lambda b: (b, 0, 0))],
        out_specs=[pl.BlockSpec((nb, l, ATT_HEADS * HEAD_DIM), lambda b: (b, 0, 0)),
                   pl.BlockSpec((nb, n_keep, kvw), lambda b: (b, 0, 0)),
                   pl.BlockSpec((nb, n_keep, kvw), lambda b: (b, 0, 0))],
        out_shape=[jax.ShapeDtypeStruct((b_, l, ATT_HEADS * HEAD_DIM), BF16),
                   jax.ShapeDtypeStruct((b_, n_keep, kvw), F32),
                   jax.ShapeDtypeStruct((b_, n_keep, kvw), F32)],
        compiler_params=_cparams(("parallel",)),
        name="attn_sample",
    )(sinks, qkv, cache_k, cache_v)


def _rope_tables(pos):
    half = HEAD_DIM // 2
    inv_freq = ROPE_THETA ** (-jnp.arange(half, dtype=F32) / half)
    ang = pos.astype(F32)[:, None] * inv_freq[None, :]
    cos = jnp.cos(ang)
    sin = jnp.sin(ang)
    reps = LANES // HEAD_DIM
    return (jnp.tile(jnp.concatenate([cos, cos], axis=1), (1, reps)),
            jnp.tile(jnp.concatenate([-sin, sin], axis=1), (1, reps)))


def _tail_pad(buf):
    return jnp.pad(buf, ((0, 0), (TAIL_ROWS - buf.shape[1], 0), (0, 0)))


def _trunk(x, mods, pos, st, pw, seq_len):
    g_, m, d = x.shape
    nseq = g_ * m // seq_len
    blk = _block_rows(m)
    chunk = min(seq_len, SSM_CHUNK)
    lru_rows = min(seq_len, LRU_ROWS)

    proj, dtp = _inproj(x, pw['norms'][0, 0:1], mods[0], pw['w_in_t'], SSM_INNER + SSM_CONV_DIM, SSM_HEADS,
                        blk['inproj'], INPROJ_COLS)
    n_main = proj.shape[-1]
    proj_s = proj.reshape(nseq, seq_len, n_main)
    dtp_s = dtp.reshape(nseq, seq_len, LANES)
    y_ssm, ssm_h = _ssd(proj_s, dtp_s, _tail_pad(st['ssm_conv']), st['ssm'].reshape(nseq, SSM_INNER, SSM_STATE),
                        pw['ssm_conv_w'], pw['ssm_conv_b'], pw['ssm_dt_bias'], pw['ssm_a_log'], pw['ssm_d_full'],
                        pw['ssm_norm'], pw['expand'], chunk)
    y_lru, lru_h = _lru(proj_s, _tail_pad(st['lru_conv']), st['lru'].reshape(nseq, 1, LRU_WIDTH),
                        pw['lru_conv_w'], pw['lru_conv_b'], pw['lru_w_a'], pw['lru_b_a'], pw['lru_w_x'],
                        pw['lru_b_x'], pw['lru_lambda'], lru_rows, 3)
    keep = SSM_CONV - 1
    new_ssm_conv = proj_s[:, seq_len - keep:, SSM_INNER:SSM_INNER + SSM_CONV_DIM]
    new_lru_conv = proj_s[:, seq_len - keep:, n_main - LRU_WIDTH:]
    x = _outproj([y_ssm.reshape(g_, m, SSM_INNER), y_lru.reshape(g_, m, LRU_WIDTH)],
                 pw['w_out_hyb'], x, pw['norms'][0, 1:2], mods[0], blk['outproj'])
    x = _mlp(x, pw['norms'][0, 2:3], pw['norms'][0, 3:4], mods[0], pw['w_up'], pw['w_down'], 0, blk['mlp'],
             MLP_COLS)

    cos_t, sin_t = _rope_tables(pos)
    if seq_len < blk['qkv']:
        cos_t = jnp.tile(cos_t, (blk['qkv'] // seq_len, 1))
        sin_t = jnp.tile(sin_t, (blk['qkv'] // seq_len, 1))
    qkv = _qkv(x, pw['norms'][1, 0:1], mods[1], pw['w_qkv'], pw['b_qkv'], cos_t, sin_t, blk['qkv'], QKV_COLS)
    qkv_s = qkv.reshape(nseq, seq_len, qkv.shape[-1])
    kcol = ATT_HEADS * HEAD_DIM
    vcol = kcol + KV_HEADS * HEAD_DIM
    if st['k'] is None:
        o = _attn_prompt(qkv_s, pw['attn_sinks'])
        n_keep = min(WINDOW, seq_len)
        k_new = qkv_s[:, seq_len - n_keep:, kcol:vcol]
        v_new = qkv_s[:, seq_len - n_keep:, vcol:]
    else:
        n_keep = st['k'].shape[1]
        o, k_new, v_new = _attn_sample(qkv_s, st['k'].reshape(nseq, n_keep, KV_HEADS * HEAD_DIM),
                                       st['v'].reshape(nseq, n_keep, KV_HEADS * HEAD_DIM), pw['attn_sinks'])
    k_new = k_new.reshape(nseq, n_keep, KV_HEADS, HEAD_DIM)
    v_new = v_new.reshape(nseq, n_keep, KV_HEADS, HEAD_DIM)
    x = _outproj([o.reshape(g_, m, kcol)], pw['w_out_attn'], x, pw['norms'][1, 1:2], mods[1], blk['outproj'])
    x = _mlp(x, pw['norms'][1, 2:3], pw['norms'][1, 3:4], mods[1], pw['w_up'], pw['w_down'], 1, blk['mlp'],
             MLP_COLS)

    states = (new_ssm_conv[None], ssm_h.reshape(1, nseq, SSM_HEADS, SSM_HEAD_DIM, SSM_STATE),
              new_lru_conv[None], lru_h.reshape(1, nseq, LRU_WIDTH), k_new[None], v_new[None])
    return x, states


def kernel(x_prompt, x_sample, state_ssm_conv, state_ssm, state_lru_conv, state_lru, cache_k, cache_v, c_prompt, c_sample, w_mod, b_mod, norms, w_in_hyb, ssm_conv_w, ssm_conv_b, ssm_dt_bias, ssm_a_log, ssm_d, ssm_norm, lru_conv_w, lru_conv_b, lru_w_a, lru_b_a, lru_w_x, lru_b_x, lru_lambda, w_out_hyb, w_qkv, b_qkv, attn_sinks, w_out_attn, w_up, w_down):
    bp, lp, d = x_prompt.shape
    bs, ls, _ = x_sample.shape
    depth = w_mod.shape[0]

    pad_lanes = lambda v: jnp.pad(v.reshape(1, -1), ((0, 0), (0, LANES - v.size)))
    head_of_lane = jnp.arange(SSM_INNER) // SSM_HEAD_DIM
    pw = {
        'norms': norms,
        'w_in_t': jnp.swapaxes(w_in_hyb, 1, 2),
        'ssm_conv_w': ssm_conv_w[0], 'ssm_conv_b': ssm_conv_b[0].reshape(1, -1),
        'ssm_dt_bias': pad_lanes(ssm_dt_bias[0]), 'ssm_a_log': pad_lanes(ssm_a_log[0]),
        'ssm_d_full': jnp.repeat(ssm_d[0], SSM_HEAD_DIM).reshape(1, -1),
        'ssm_norm': ssm_norm[0].reshape(1, -1),
        'expand': (jnp.arange(LANES)[:, None] == head_of_lane[None, :]).astype(BF16),
        'lru_conv_w': lru_conv_w[0], 'lru_conv_b': lru_conv_b[0].reshape(1, -1),
        'lru_w_a': lru_w_a[0].astype(BF16), 'lru_b_a': lru_b_a[0].reshape(1, -1),
        'lru_w_x': lru_w_x[0].astype(BF16), 'lru_b_x': lru_b_x[0].reshape(1, -1),
        'lru_lambda': lru_lambda[0].reshape(1, -1),
        'w_out_hyb': w_out_hyb[0].astype(BF16),
        'w_qkv': w_qkv[0], 'b_qkv': b_qkv[0].reshape(1, -1),
        'attn_sinks': attn_sinks[0],
        'w_out_attn': w_out_attn[0],
        'w_up': w_up, 'w_down': w_down,
    }

    mod = _adaln(jnp.concatenate([c_prompt, c_sample], axis=0), w_mod, b_mod)
    mods_p = [mod[l, :bp][:, None, :] for l in range(depth)]
    mods_s = [jnp.repeat(mod[l, bp:], ls, axis=0)[None] for l in range(depth)]

    zeros = lambda *shape: jnp.zeros(shape, F32)
    st_p = {'ssm_conv': zeros(bp, SSM_CONV - 1, SSM_CONV_DIM), 'ssm': zeros(bp, SSM_INNER, SSM_STATE),
            'lru_conv': zeros(bp, SSM_CONV - 1, LRU_WIDTH), 'lru': zeros(bp, LRU_WIDTH), 'k': None, 'v': None}
    st_s = {'ssm_conv': state_ssm_conv[0], 'ssm': state_ssm[0], 'lru_conv': state_lru_conv[0],
            'lru': state_lru[0], 'k': cache_k[0], 'v': cache_v[0]}

    pos_p = jnp.arange(lp, dtype=jnp.int32)
    pos_s = PAST_LEN + jnp.arange(ls, dtype=jnp.int32)
    y_p, sp = _trunk(x_prompt, mods_p, pos_p, st_p, pw, lp)
    y_s, ss = _trunk(x_sample.reshape(1, bs * ls, d), mods_s, pos_s, st_s, pw, ls)
    return (y_p, y_s.reshape(bs, ls, d)) + sp + ss
```

```python
import functools
import math

import jax
import jax.numpy as jnp
from jax import lax
from jax.experimental import pallas as pl
from jax.experimental.pallas import tpu as pltpu

F32 = jnp.float32
BF16 = jnp.bfloat16

D_MODEL = 2048
PAST_LEN = 16384
SSM_HEADS = 32
SSM_HEAD_DIM = 64
SSM_INNER = SSM_HEADS * SSM_HEAD_DIM
SSM_GROUPS = 8
SSM_STATE = 128
SSM_CONV = 4
SSM_CHUNK = 128
SSM_CONV_DIM = SSM_INNER + 2 * SSM_GROUPS * SSM_STATE
HEADS_PER_GROUP = SSM_HEADS // SSM_GROUPS
GROUP_WIDTH = HEADS_PER_GROUP * SSM_HEAD_DIM
LRU_WIDTH = D_MODEL
LRU_BLOCKS = 8
LRU_BLOCK = LRU_WIDTH // LRU_BLOCKS
LRU_C = 8.0
ATT_HEADS = 32
KV_HEADS = 8
HEAD_DIM = 64
Q_PER_KV = ATT_HEADS // KV_HEADS
WINDOW = 128
ROPE_THETA = 10000.0
D_FF = 4 * D_MODEL
EPS = 1e-6

LANES = 128
SUBLANES = 8
BF16_SUBLANES = 16
TAIL_ROWS = SUBLANES
VMEM_LIMIT = 56 * 1024 * 1024


ADALN_COLS = 1024
INPROJ_COLS = 1024
MLP_COLS = 512
QKV_COLS = 512
LRU_ROWS = 512


def _block_rows(m):
    return {'inproj': min(m, 1024), 'mlp': min(m, 1024), 'qkv': min(m, 512), 'outproj': min(m, 512)}


SEQS_PER_STEP = 4


def _seqs_per_step(n_seq, n_chunks):
    return SEQS_PER_STEP if n_chunks == 1 and n_seq % SEQS_PER_STEP == 0 and n_seq > SEQS_PER_STEP else 1


def _per_sequence(body, nb, per_seq):
    if nb == 1:
        return body

    def kernel(*refs):
        def one(n, carry):
            body(*[r.at[pl.ds(n, 1)] if flag else r for r, flag in zip(refs, per_seq)])
            return carry

        lax.fori_loop(0, nb, one, 0)

    return kernel


def _cparams(sem):
    return pltpu.CompilerParams(dimension_semantics=sem, vmem_limit_bytes=VMEM_LIMIT)


def _rms(x, g):
    return x * lax.rsqrt(jnp.mean(x * x, axis=-1, keepdims=True) + EPS) * g


def _gated_postnorm(x, acc, g_ref, gt_ref):
    if gt_ref.shape[1] == 1:
        return x + _rms(acc, g_ref[...] * gt_ref[0])
    return x + gt_ref[0] * _rms(acc, g_ref[...])


ROW_CHUNK = 16
ROW_UNROLL = 4


def _for_row_chunks(n_rows, body):
    def step(i, carry):
        body(pl.multiple_of(i * ROW_CHUNK, ROW_CHUNK))
        return carry

    lax.fori_loop(0, n_rows // ROW_CHUNK, step, 0, unroll=ROW_UNROLL)


def _mod_rows(ref, r0):
    return ref[0] if ref.shape[1] == 1 else ref[0, pl.ds(r0, ROW_CHUNK), :]


def _prenorm_to(h_scr, x_ref, g_ref, sc_ref, sh_ref):
    one_seq = sc_ref.shape[1] == 1
    gain = g_ref[...] * (1.0 + sc_ref[0]) if one_seq else g_ref[...]

    def body(r0):
        x = x_ref[0, pl.ds(r0, ROW_CHUNK), :]
        h = _rms(x, gain) if one_seq else _rms(x, gain) * (1.0 + _mod_rows(sc_ref, r0))
        h_scr[pl.ds(r0, ROW_CHUNK), :] = (h + _mod_rows(sh_ref, r0)).astype(BF16)

    _for_row_chunks(x_ref.shape[1], body)


def _sigmoid(x):
    return 0.5 * (jnp.tanh(0.5 * x) + 1.0)


def _silu(x):
    return x * _sigmoid(x)


def _softplus(x):
    return jnp.maximum(x, 0.0) + jnp.log1p(jnp.exp(-jnp.abs(x)))


def _sqrt_neg_expm1(x):
    s = jnp.tanh(-0.5 * x)
    t = (2.0 * s) * (1.0 + s)
    return jnp.where(t > 0.0, (2.0 * s) * lax.rsqrt(t), 0.0)


def _split3(x):
    hi = x.astype(BF16)
    r = x - hi.astype(F32)
    mid = r.astype(BF16)
    lo = (r - mid.astype(F32)).astype(BF16)
    return [hi, mid, lo]


def _adaln_kernel(c_ref, w_ref, b_ref, o_ref):
    s = _silu(c_ref[...]).astype(BF16)
    o_ref[0] = jnp.dot(s, w_ref[0].astype(BF16), preferred_element_type=F32) + b_ref[0]


def _adaln(c_all, w_mod, b_mod):
    nb, d = c_all.shape
    depth, _, n = w_mod.shape
    bn = ADALN_COLS
    return pl.pallas_call(
        _adaln_kernel,
        grid=(depth, n // bn),
        in_specs=[pl.BlockSpec((nb, d), lambda l, j: (0, 0)),
                  pl.BlockSpec((1, d, bn), lambda l, j: (l, 0, j)),
                  pl.BlockSpec((1, 1, bn), lambda l, j: (l, 0, j))],
        out_specs=pl.BlockSpec((1, nb, bn), lambda l, j: (l, 0, j)),
        out_shape=jax.ShapeDtypeStruct((depth, nb, n), F32),
        compiler_params=_cparams(("parallel", "parallel")),
        name="adaln",
    )(c_all, w_mod, b_mod.reshape(depth, 1, n))


def _mod_spec(mod, bm, k):
    if mod.shape[1] == 1:
        return pl.BlockSpec((1, 1, D_MODEL), lambda g, i, *_: (g, 0, k))
    return pl.BlockSpec((1, bm, D_MODEL), lambda g, i, *_: (g, i, k))


def _inproj_kernel(x_ref, g_ref, sh_ref, sc_ref, wt_ref, wdt_ref, o_ref, dt_ref, h_scr, *, n_dt):
    nt = (((1,), (1,)), ((), ()))

    @pl.when(pl.program_id(2) == 0)
    def _():
        _prenorm_to(h_scr, x_ref, g_ref, sc_ref, sh_ref)
        dt = lax.dot_general(h_scr[...], wdt_ref[...].astype(BF16), nt, preferred_element_type=F32)
        lane = lax.broadcasted_iota(jnp.int32, dt.shape, 1)
        dt_ref[0] = jnp.where(lane < n_dt, dt, 0.0)

    o_ref[0] = lax.dot_general(h_scr[...], wt_ref[...].astype(BF16), nt, preferred_element_type=F32)


def _inproj(x, gain, mod, w_t, c_dt, n_dt, bm, bn):
    g_, m, d = x.shape
    n = w_t.shape[1] - n_dt
    row0 = lambda j: pl.multiple_of(jnp.where(j * bn < c_dt, j * bn, j * bn + n_dt), SUBLANES)
    return pl.pallas_call(
        functools.partial(_inproj_kernel, n_dt=n_dt),
        grid=(g_, m // bm, n // bn),
        in_specs=[pl.BlockSpec((1, bm, d), lambda g, i, j: (g, i, 0)),
                  pl.BlockSpec((1, d), lambda g, i, j: (0, 0)),
                  _mod_spec(mod, bm, 0), _mod_spec(mod, bm, 1),
                  pl.BlockSpec((None, pl.Element(bn), pl.Element(d)), lambda g, i, j: (0, row0(j), 0)),
                  pl.BlockSpec((None, pl.Element(LANES), pl.Element(d)), lambda g, i, j: (0, c_dt, 0))],
        out_specs=[pl.BlockSpec((1, bm, bn), lambda g, i, j: (g, i, j)),
                   pl.BlockSpec((1, bm, LANES), lambda g, i, j: (g, i, 0))],
        out_shape=[jax.ShapeDtypeStruct((g_, m, n), F32),
                   jax.ShapeDtypeStruct((g_, m, LANES), F32)],
        scratch_shapes=[pltpu.VMEM((bm, d), BF16)],
        compiler_params=_cparams(("parallel", "parallel", "arbitrary")),
        name="inproj",
    )(x, gain, mod, mod, w_t, w_t)


def _swap_halves(x):
    lane = lax.broadcasted_iota(jnp.int32, x.shape, 1)
    first = (lane % HEAD_DIM) < (HEAD_DIM // 2)
    return jnp.where(first, pltpu.roll(x, LANES - HEAD_DIM // 2, 1), pltpu.roll(x, HEAD_DIM // 2, 1))


def _qkv_kernel(x_ref, g_ref, sh_ref, sc_ref, w_ref, b_ref, cos_ref, sin_ref, o_ref, h_scr, *, rope_cols, bn):
    _prenorm_to(h_scr, x_ref, g_ref, sc_ref, sh_ref)
    h = h_scr[...]
    cos = cos_ref[...]
    sin = sin_ref[...]
    for c0 in range(0, w_ref.shape[1], bn):
        acc = jnp.dot(h, w_ref[:, c0:c0 + bn].astype(BF16), preferred_element_type=F32) + b_ref[:, c0:c0 + bn]
        if c0 < rope_cols:
            for c in range(c0, c0 + bn, LANES):
                a = acc[:, c - c0:c - c0 + LANES]
                o_ref[0, :, c:c + LANES] = a * cos + _swap_halves(a) * sin
        else:
            o_ref[0, :, c0:c0 + bn] = acc


def _qkv(x, gain, mod, w, b, cos_t, sin_t, bm, bn):
    g_, m, d = x.shape
    n = w.shape[1]
    rope_cols = (ATT_HEADS + KV_HEADS) * HEAD_DIM
    rows_per_g = cos_t.shape[0] // bm
    return pl.pallas_call(
        functools.partial(_qkv_kernel, rope_cols=rope_cols, bn=bn),
        grid=(g_, m // bm),
        in_specs=[pl.BlockSpec((1, bm, d), lambda g, i: (g, i, 0)),
                  pl.BlockSpec((1, d), lambda g, i: (0, 0)),
                  _mod_spec(mod, bm, 0), _mod_spec(mod, bm, 1),
                  pl.BlockSpec((d, n), lambda g, i: (0, 0), pipeline_mode=pl.Buffered(1)),
                  pl.BlockSpec((1, n), lambda g, i: (0, 0)),
                  pl.BlockSpec((bm, LANES), lambda g, i: (i % rows_per_g, 0)),
                  pl.BlockSpec((bm, LANES), lambda g, i: (i % rows_per_g, 0))],
        out_specs=pl.BlockSpec((1, bm, n), lambda g, i: (g, i, 0)),
        out_shape=jax.ShapeDtypeStruct((g_, m, n), F32),
        scratch_shapes=[pltpu.VMEM((bm, d), BF16)],
        compiler_params=_cparams(("parallel", "parallel")),
        name="qkv",
    )(x, gain, mod, mod, w, b, cos_t, sin_t)


def _outproj_kernel(*refs, n_lhs):
    lhs = refs[:n_lhs]
    ws = refs[n_lhs:2 * n_lhs]
    x_ref, g_ref, gt_ref, o_ref = refs[2 * n_lhs:]
    acc = jnp.dot(lhs[0][0], ws[0][...].astype(BF16), preferred_element_type=F32)
    for a, w in zip(lhs[1:], ws[1:]):
        acc = acc + jnp.dot(a[0], w[...].astype(BF16), preferred_element_type=F32)
    o_ref[0] = _gated_postnorm(x_ref[0], acc, g_ref, gt_ref)


def _outproj(lhs_list, w, x, gain, mod, bm):
    g_, m, d = x.shape
    n_lhs = len(lhs_list)
    kw = w.shape[0] // n_lhs
    lhs_specs = [pl.BlockSpec((1, bm, kw), lambda g, i: (g, i, 0)) for _ in lhs_list]
    w_specs = [pl.BlockSpec((kw, d), lambda g, i, k=k: (k, 0), pipeline_mode=pl.Buffered(1)) for k in range(n_lhs)]
    w_list = [w] * n_lhs
    if mod.shape[1] == 1:
        gt_spec = pl.BlockSpec((1, 1, d), lambda g, i: (g, 0, 2))
    else:
        gt_spec = pl.BlockSpec((1, bm, d), lambda g, i: (g, i, 2))
    return pl.pallas_call(
        functools.partial(_outproj_kernel, n_lhs=n_lhs),
        grid=(g_, m // bm),
        in_specs=lhs_specs + w_specs + [pl.BlockSpec((1, bm, d), lambda g, i: (g, i, 0)),
                                        pl.BlockSpec((1, d), lambda g, i: (0, 0)), gt_spec],
        out_specs=pl.BlockSpec((1, bm, d), lambda g, i: (g, i, 0)),
        out_shape=jax.ShapeDtypeStruct((g_, m, d), F32),
        compiler_params=_cparams(("parallel", "parallel")),
        name="outproj",
    )(*lhs_list, *w_list, x, gain, mod)


def _mlp_kernel(x_ref, g2_ref, sh_ref, sc_ref, gt_ref, wu_ref, wd_ref, g3_ref, o_ref, h_scr):
    f = pl.program_id(2)

    @pl.when(f == 0)
    def _():
        _prenorm_to(h_scr, x_ref, g2_ref, sc_ref, sh_ref)
        o_ref[0] = jnp.zeros(o_ref.shape[1:], F32)

    u = jnp.dot(h_scr[...], wu_ref[...].astype(BF16), preferred_element_type=F32)
    u = jnp.square(jnp.maximum(u, 0.0)).astype(BF16)
    o_ref[0] += jnp.dot(u, wd_ref[...].astype(BF16), preferred_element_type=F32)

    @pl.when(f == pl.num_programs(2) - 1)
    def _():
        o_ref[0] = _gated_postnorm(x_ref[0], o_ref[0], g3_ref, gt_ref)


def _mlp(x, g2, g3, mod, w_up, w_down, layer, bm, bf):
    g_, m, d = x.shape
    dff = w_up.shape[2]
    if mod.shape[1] == 1:
        mspec = lambda k: pl.BlockSpec((1, 1, d), lambda g, i, f: (g, 0, k))
    else:
        mspec = lambda k: pl.BlockSpec((1, bm, d), lambda g, i, f: (g, i, k))
    return pl.pallas_call(
        _mlp_kernel,
        grid=(g_, m // bm, dff // bf),
        in_specs=[pl.BlockSpec((1, bm, d), lambda g, i, f: (g, i, 0), pipeline_mode=pl.Buffered(1)),
                  pl.BlockSpec((1, d), lambda g, i, f: (0, 0)),
                  mspec(3), mspec(4), mspec(5),
                  pl.BlockSpec((None, d, bf), lambda g, i, f: (layer, 0, f)),
                  pl.BlockSpec((None, bf, d), lambda g, i, f: (layer, f, 0)),
                  pl.BlockSpec((1, d), lambda g, i, f: (0, 0))],
        out_specs=pl.BlockSpec((1, bm, d), lambda g, i, f: (g, i, 0)),
        out_shape=jax.ShapeDtypeStruct((g_, m, d), F32),
        scratch_shapes=[pltpu.VMEM((bm, d), BF16)],
        compiler_params=_cparams(("parallel", "parallel", "arbitrary")),
        name="mlp",
    )(x, g2, mod, mod, mod, w_up, w_down, g3)


def _dwconv(raw, tail, w, b):
    t = raw.shape[0]
    assert w.shape[0] == 4
    ext = jnp.concatenate([tail, raw], axis=0)
    z1 = pltpu.roll(ext, 1, 0)
    near = ext * w[3:4] + z1 * w[2:3]
    far = ext * w[1:2] + z1 * w[0:1]
    y = near + pltpu.roll(far, 2, 0)
    return y[TAIL_ROWS:] + b, ext[t:t + TAIL_ROWS]


def _cumsum_rows(x):
    n = x.shape[0]
    row = lax.broadcasted_iota(jnp.int32, x.shape, 0)
    d = 1
    while d < n:
        x = x + jnp.where(row >= d, pltpu.roll(x, d, 0), 0.0)
        d *= 2
    return x


def _ssd_kernel(z_ref, xs_ref, bc_ref, dt_ref, buf_ref, h0_ref, cw_ref, cb_ref, dtb_ref, alog_ref,
                dsk_ref, ng_ref, exp_ref, y_ref, st_ref, tail_scr, *, q_valid, q):
    c = pl.program_id(1)

    @pl.when(c == 0)
    def _():
        tail_scr[...] = buf_ref[0]
        st_ref[0] = h0_ref[0]

    def padded(v):
        if q_valid == q:
            return v
        return jnp.concatenate([v, jnp.zeros((q - q_valid, v.shape[1]), v.dtype)], axis=0)

    def transposed(v):
        if q == LANES:
            return v.T
        return jnp.concatenate([v, jnp.zeros((LANES - q, LANES), v.dtype)], axis=0).T[:, :q]

    cw = cw_ref[...]
    cb = cb_ref[...]
    xs, tail_x = _dwconv(xs_ref[0], tail_scr[:, :SSM_INNER], cw[:, :SSM_INNER], cb[:, :SSM_INNER])
    bcm, tail_b = _dwconv(bc_ref[0], tail_scr[:, SSM_INNER:], cw[:, SSM_INNER:], cb[:, SSM_INNER:])
    tail_scr[:, :SSM_INNER] = tail_x
    tail_scr[:, SSM_INNER:] = tail_b
    xs = padded(_silu(xs))
    bcm = padded(_silu(bcm))
    z = padded(z_ref[0])

    dt = _softplus(dt_ref[0] + dtb_ref[...])
    dt = padded(dt)
    a = -jnp.exp(alog_ref[...])
    acs = _cumsum_rows(dt * a)
    acs_t = transposed(acs)

    full = jnp.dot(jnp.concatenate(_split3(dt) + _split3(acs), axis=0), exp_ref[...], preferred_element_type=F32)
    dt_full = (full[0:q] + full[q:2 * q]) + full[2 * q:3 * q]
    acs_full = (full[3 * q:4 * q] + full[4 * q:5 * q]) + full[5 * q:6 * q]
    xdt = xs * dt_full
    dec_out = jnp.exp(acs_full)
    xw = (xdt * jnp.exp(acs_full[q - 1:q] - acs_full)).astype(BF16)
    xdt_b = xdt.astype(BF16)
    bc_b = bcm.astype(BF16)

    row = lax.broadcasted_iota(jnp.int32, (q, q), 0)
    col = lax.broadcasted_iota(jnp.int32, (q, q), 1)
    causal = row >= col
    lane = lax.broadcasted_iota(jnp.int32, (q, LANES), 1)
    lo_half = lane < SSM_HEAD_DIM
    nt = (((1,), (1,)), ((), ()))
    tn = (((0,), (0,)), ((), ()))
    n_bc = SSM_GROUPS * SSM_STATE

    groups = range(SSM_GROUPS)
    cols = [slice(g * GROUP_WIDTH, (g + 1) * GROUP_WIDTH) for g in groups]
    b_gs = [bc_b[:, g * SSM_STATE:(g + 1) * SSM_STATE] for g in groups]
    c_gs = [bc_b[:, n_bc + g * SSM_STATE:n_bc + (g + 1) * SSM_STATE] for g in groups]
    h_gs = [st_ref[0, cols[g], :] for g in groups]
    cbms = [lax.dot_general(c_gs[g], b_gs[g], nt, preferred_element_type=F32) for g in groups]
    y_offs = [lax.dot_general(c_gs[g], h_gs[g].astype(BF16), nt, preferred_element_type=F32) for g in groups]
    st_news = [lax.dot_general(xw[:, cols[g]], b_gs[g], tn, preferred_element_type=F32) for g in groups]
    ms = [(cbms[h // HEADS_PER_GROUP]
           * jnp.exp(jnp.where(causal, acs[:, h:h + 1] - acs_t[h:h + 1, :], -jnp.inf))).astype(BF16)
          for h in range(SSM_HEADS)]
    y_diags = []
    for pair in range(SSM_HEADS // 2):
        x_pair = xdt_b[:, pair * LANES:(pair + 1) * LANES]
        zero = jnp.zeros_like(x_pair)
        y_diags.append(jnp.dot(ms[2 * pair], jnp.where(lo_half, x_pair, zero), preferred_element_type=F32)
                       + jnp.dot(ms[2 * pair + 1], jnp.where(lo_half, zero, x_pair), preferred_element_type=F32))
    pairs_per_group = HEADS_PER_GROUP // 2
    for g in groups:
        y_g = y_offs[g] * dec_out[:, cols[g]]
        y_g = y_g + jnp.concatenate(y_diags[g * pairs_per_group:(g + 1) * pairs_per_group], axis=1)
        y_g = y_g + xs[:, cols[g]] * dsk_ref[:, cols[g]]
        y_g = y_g * _silu(z[:, cols[g]])
        y_g = y_g * lax.rsqrt(jnp.mean(y_g * y_g, axis=-1, keepdims=True) + EPS)
        y_g = y_g * ng_ref[:, cols[g]]
        y_ref[0, :, cols[g]] = y_g[:q_valid].astype(y_ref.dtype)
    for h in range(SSM_HEADS):
        g, r = divmod(h, HEADS_PER_GROUP)
        rows = slice(r * SSM_HEAD_DIM, (r + 1) * SSM_HEAD_DIM)
        decay = jnp.exp(acs_t[h:h + 1, q - 1:q])
        st_ref[0, h * SSM_HEAD_DIM:(h + 1) * SSM_HEAD_DIM, :] = h_gs[g][rows] * decay + st_news[g][rows]


def _ssd(proj, dtp, buf8, h0, conv_w, conv_b, dt_bias, a_log, d_full, norm_g, expand, rows):
    b_, l, _ = proj.shape
    nc = l // rows
    nb = _seqs_per_step(b_, nc)
    st_rows = SSM_INNER
    kernel_fn = functools.partial(_ssd_kernel, q_valid=rows, q=max(rows, BF16_SUBLANES))
    return pl.pallas_call(
        _per_sequence(kernel_fn, nb, [True] * 6 + [False] * 7 + [True] * 2 + [False]),
        grid=(b_ // nb, nc),
        in_specs=[pl.BlockSpec((nb, rows, SSM_INNER), lambda b, c: (b, c, 0)),
                  pl.BlockSpec((nb, rows, SSM_INNER), lambda b, c: (b, c, 1)),
                  pl.BlockSpec((nb, rows, SSM_INNER), lambda b, c: (b, c, 2)),
                  pl.BlockSpec((nb, rows, LANES), lambda b, c: (b, c, 0)),
                  pl.BlockSpec((nb, TAIL_ROWS, SSM_CONV_DIM), lambda b, c: (b, 0, 0)),
                  pl.BlockSpec((nb, st_rows, SSM_STATE), lambda b, c: (b, 0, 0)),
                  pl.BlockSpec((SSM_CONV, SSM_CONV_DIM), lambda b, c: (0, 0)),
                  pl.BlockSpec((1, SSM_CONV_DIM), lambda b, c: (0, 0)),
                  pl.BlockSpec((1, LANES), lambda b, c: (0, 0)),
                  pl.BlockSpec((1, LANES), lambda b, c: (0, 0)),
                  pl.BlockSpec((1, SSM_INNER), lambda b, c: (0, 0)),
                  pl.BlockSpec((1, SSM_INNER), lambda b, c: (0, 0)),
                  pl.BlockSpec((LANES, SSM_INNER), lambda b, c: (0, 0))],
        out_specs=[pl.BlockSpec((nb, rows, SSM_INNER), lambda b, c: (b, c, 0)),
                   pl.BlockSpec((nb, st_rows, SSM_STATE), lambda b, c: (b, 0, 0))],
        out_shape=[jax.ShapeDtypeStruct((b_, l, SSM_INNER), BF16),
                   jax.ShapeDtypeStruct((b_, st_rows, SSM_STATE), F32)],
        scratch_shapes=[pltpu.VMEM((TAIL_ROWS, SSM_CONV_DIM), F32)],
        compiler_params=_cparams(("parallel", "arbitrary")),
        name="ssd",
    )(proj, proj, proj, dtp, buf8, h0, conv_w, conv_b, dt_bias, a_log, d_full, norm_g, expand)


SCAN_SEG = 4
SCAN_ROWS = SUBLANES * SCAN_SEG


def _shift_rows_fill(x, d, fill, row):
    return jnp.where(row >= d, pltpu.roll(x, d, 0), fill)


def _sublane_scan(e, pe, row):
    d = 1
    while d < SUBLANES:
        e = e + pe * _shift_rows_fill(e, d, 0.0, row)
        pe = pe * _shift_rows_fill(pe, d, 1.0, row)
        d *= 2
    return e, pe


def _linear_scan(a_scr, u_scr, h0):
    n_slab, t, _ = a_scr.shape
    seg = SCAN_SEG if t % SCAN_ROWS == 0 else 1
    row = lax.broadcasted_iota(jnp.int32, (SUBLANES, LANES), 0)

    def block(i, h_prev):
        base = pl.multiple_of(i * (SUBLANES * seg), SUBLANES)
        out = []
        for c in range(n_slab):
            rows = [pl.ds(base + j, SUBLANES, stride=seg) if seg > 1 else pl.ds(base, SUBLANES) for j in range(seg)]
            hs, ps = [], []
            for j in range(seg):
                a, u = a_scr[c, rows[j], :], u_scr[c, rows[j], :]
                hs.append(u if j == 0 else a * hs[-1] + u)
                ps.append(a if j == 0 else a * ps[-1])
            e, pe = _sublane_scan(hs[-1], ps[-1], row)
            g = e + pe * h_prev[c]
            carry = jnp.where(row == 0, h_prev[c], pltpu.roll(g, 1, 0))
            for j in range(seg):
                u_scr[c, rows[j], :] = hs[j] + ps[j] * carry
            out.append(jnp.broadcast_to(g[SUBLANES - 1:], g.shape))
        return tuple(out)

    h_init = tuple(jnp.broadcast_to(h0[:, c * LANES:(c + 1) * LANES], (SUBLANES, LANES)) for c in range(n_slab))
    h_fin = lax.fori_loop(0, t // (SUBLANES * seg), block, h_init)
    return jnp.concatenate([h[:1] for h in h_fin], axis=1)


def _gelu_tanh(x):
    c0 = math.sqrt(2.0 / math.pi)
    hx = 0.5 * x
    return hx + hx * jnp.tanh(x * (c0 + (c0 * 0.044715) * (x * x)))


def _lru_kernel(gate_ref, xr_ref, buf_ref, h0_ref, cw_ref, cb_ref, wa_ref, ba_ref, wx_ref, bx_ref,
                lam_ref, y_ref, hl_ref, tail_scr, a_scr, u_scr):
    c = pl.program_id(1)

    @pl.when(c == 0)
    def _():
        tail_scr[...] = buf_ref[0]
        hl_ref[0] = h0_ref[0]

    xc, tail = _dwconv(xr_ref[0], tail_scr[...], cw_ref[...], cb_ref[...])
    tail_scr[...] = tail
    sp = _softplus(-lam_ref[...])
    slabs_per_block = LRU_BLOCK // LANES
    for k in range(LRU_BLOCKS):
        k0 = k * LRU_BLOCK
        xk = xc[:, k0:k0 + LRU_BLOCK]
        xkb = xk.astype(BF16)
        gr = _sigmoid(jnp.dot(xkb, wa_ref[k], preferred_element_type=F32) + ba_ref[:, k0:k0 + LRU_BLOCK])
        gi = _sigmoid(jnp.dot(xkb, wx_ref[k], preferred_element_type=F32) + bx_ref[:, k0:k0 + LRU_BLOCK])
        log_a = -LRU_C * gr * sp[:, k0:k0 + LRU_BLOCK]
        a = jnp.exp(log_a)
        u = _sqrt_neg_expm1(2.0 * log_a) * gi * xk
        for s in range(slabs_per_block):
            a_scr[k * slabs_per_block + s] = a[:, s * LANES:(s + 1) * LANES]
            u_scr[k * slabs_per_block + s] = u[:, s * LANES:(s + 1) * LANES]

    hl_ref[0] = _linear_scan(a_scr, u_scr, hl_ref[0])
    for s in range(a_scr.shape[0]):
        gate = gate_ref[0, :, s * LANES:(s + 1) * LANES]
        y_ref[0, :, s * LANES:(s + 1) * LANES] = (u_scr[s] * _gelu_tanh(gate)).astype(y_ref.dtype)


def _lru(proj, buf8, h0, conv_w, conv_b, w_a, b_a, w_x, b_x, lam, rows, col0):
    b_, l, _ = proj.shape
    w = LRU_WIDTH
    nb = _seqs_per_step(b_, l // rows)
    full2 = lambda shape: pl.BlockSpec(shape, lambda b, c: (0,) * len(shape))
    return pl.pallas_call(
        _per_sequence(_lru_kernel, nb, [True] * 4 + [False] * 7 + [True] * 2 + [False] * 3),
        grid=(b_ // nb, l // rows),
        in_specs=[pl.BlockSpec((nb, rows, w), lambda b, c: (b, c, col0)),
                  pl.BlockSpec((nb, rows, w), lambda b, c: (b, c, col0 + 1)),
                  pl.BlockSpec((nb, TAIL_ROWS, w), lambda b, c: (b, 0, 0)),
                  pl.BlockSpec((nb, 1, w), lambda b, c: (b, 0, 0)),
                  full2((SSM_CONV, w)), full2((1, w)),
                  full2((LRU_BLOCKS, LRU_BLOCK, LRU_BLOCK)), full2((1, w)),
                  full2((LRU_BLOCKS, LRU_BLOCK, LRU_BLOCK)), full2((1, w)),
                  full2((1, w))],
        out_specs=[pl.BlockSpec((nb, rows, w), lambda b, c: (b, c, 0)),
                   pl.BlockSpec((nb, 1, w), lambda b, c: (b, 0, 0))],
        out_shape=[jax.ShapeDtypeStruct((b_, l, w), BF16),
                   jax.ShapeDtypeStruct((b_, 1, w), F32)],
        scratch_shapes=[pltpu.VMEM((TAIL_ROWS, w), F32), pltpu.VMEM((w // LANES, rows, LANES), F32),
                        pltpu.VMEM((w // LANES, rows, LANES), F32)],
        compiler_params=_cparams(("parallel", "arbitrary")),
        name="lru",
    )(proj, proj, buf8, h0, conv_w, conv_b, w_a, b_a, w_x, b_x, lam)


def _head_variants(slab, half):
    lane = lax.broadcasted_iota(jnp.int32, slab.shape, 1)
    keep = (lane < HEAD_DIM) if half == 0 else (lane >= HEAD_DIM)
    own = jnp.where(keep, slab, 0.0)
    other = pltpu.roll(own, HEAD_DIM, 1)
    return (own, other) if half == 0 else (other, own)


def _slab_attention(q_ref, o_ref, k2, v2, valid, sink_ref):
    w = q_ref.shape[1]
    nk = k2.shape[0]
    nt = (((1,), (1,)), ((), ()))
    scale = HEAD_DIM ** -0.5
    ones = jnp.ones((nk, LANES), BF16)
    slabs = range(nk // LANES)
    heads = range(KV_HEADS)
    halves = [(e, c) for e in heads for c in range(2)]
    vs, ss = [], []
    for e in heads:
        slab, half = divmod(e, 2)
        k_lo, k_hi = _head_variants(k2[:, slab * LANES:(slab + 1) * LANES], half)
        vs.extend(v.astype(BF16) for v in _head_variants(v2[:, slab * LANES:(slab + 1) * LANES], half))
        kk = jnp.concatenate([k_lo, k_hi], axis=0).astype(BF16)
        c0 = 2 * e * LANES
        qq = jnp.concatenate([q_ref[0, :, c0:c0 + LANES], q_ref[0, :, c0 + LANES:c0 + 2 * LANES]], axis=0)
        ss.append(lax.dot_general((qq * scale).astype(BF16), kk, nt, preferred_element_type=F32))
    scs = [jnp.where(valid, ss[e][:, c * nk:(c + 1) * nk], -jnp.inf) for e, c in halves]
    sinks = [jnp.concatenate([jnp.full((w, LANES), sink_ref[Q_PER_KV * e + c], F32),
                              jnp.full((w, LANES), sink_ref[Q_PER_KV * e + 2 + c], F32)], axis=0) for e, c in halves]
    ms = [jnp.maximum(jnp.broadcast_to(jnp.max(sc, axis=1, keepdims=True), sk.shape), sk)
          for sc, sk in zip(scs, sinks)]
    ps = [jnp.concatenate([jnp.exp(sc[:, i * LANES:(i + 1) * LANES] - m) for i in slabs], axis=1).astype(BF16)
          for sc, m in zip(scs, ms)]
    dens = [jnp.dot(p, ones, preferred_element_type=F32) + jnp.exp(sk - m) for p, sk, m in zip(ps, sinks, ms)]
    outs = [jnp.dot(p, v, preferred_element_type=F32) * (1.0 / den) for p, v, den in zip(ps, vs, dens)]
    for e in heads:
        o = outs[2 * e] + outs[2 * e + 1]
        c0 = 2 * e * LANES
        o_ref[0, :, c0:c0 + LANES] = o[:w].astype(o_ref.dtype)
        o_ref[0, :, c0 + LANES:c0 + 2 * LANES] = o[w:].astype(o_ref.dtype)


ATTN_BLOCKS = 2


def _attn_prompt_kernel(sink_ref, q_ref, kc_ref, kp_ref, vc_ref, vp_ref, o_ref):
    w = WINDOW
    has_prev = pl.program_id(1) > 0
    k_prev = jnp.where(has_prev, kp_ref[0], 0.0)
    v_prev = jnp.where(has_prev, vp_ref[0], 0.0)
    t = lax.broadcasted_iota(jnp.int32, (2 * w, 2 * w), 0) % w
    j = lax.broadcasted_iota(jnp.int32, (2 * w, 2 * w), 1)
    valid = (j > t) & (j <= t + w)
    for s in range(q_ref.shape[1] // w):
        rows = pl.ds(s * w, w)
        k_cur, v_cur = kc_ref[0, rows, :], vc_ref[0, rows, :]
        _slab_attention(q_ref.at[:, rows], o_ref.at[:, rows], jnp.concatenate([k_prev, k_cur], axis=0),
                        jnp.concatenate([v_prev, v_cur], axis=0), valid, sink_ref)
        k_prev, v_prev = k_cur, v_cur


def _attn_prompt(qkv, sinks):
    b_, l, _ = qkv.shape
    w = WINDOW
    rows = ATTN_BLOCKS * w
    qw = ATT_HEADS * HEAD_DIM
    kvw = KV_HEADS * HEAD_DIM
    k_blk = qw // kvw
    prev = lambda n: jnp.maximum(ATTN_BLOCKS * n - 1, 0)
    return pl.pallas_call(
        _attn_prompt_kernel,
        grid=(b_, l // rows),
        in_specs=[pl.BlockSpec(memory_space=pltpu.SMEM),
                  pl.BlockSpec((1, rows, qw), lambda b, n: (b, n, 0)),
                  pl.BlockSpec((1, rows, kvw), lambda b, n: (b, n, k_blk)),
                  pl.BlockSpec((1, w, kvw), lambda b, n: (b, prev(n), k_blk)),
                  pl.BlockSpec((1, rows, kvw), lambda b, n: (b, n, k_blk + 1)),
                  pl.BlockSpec((1, w, kvw), lambda b, n: (b, prev(n), k_blk + 1))],
        out_specs=pl.BlockSpec((1, rows, qw), lambda b, n: (b, n, 0)),
        out_shape=jax.ShapeDtypeStruct((b_, l, qw), BF16),
        compiler_params=_cparams(("parallel", "parallel")),
        name="attn_prompt",
    )(sinks, qkv, qkv, qkv, qkv, qkv)


def _attn_sample_kernel(sink_ref, qkv_ref, ck_ref, cv_ref, o_ref, nk_ref, nv_ref):
    l = qkv_ref.shape[1]
    n_keep = ck_ref.shape[1]
    kcol = ATT_HEADS * HEAD_DIM
    vcol = kcol + KV_HEADS * HEAD_DIM
    qkv = qkv_ref[0]
    k_all = jnp.concatenate([ck_ref[0], qkv[:, kcol:vcol]], axis=0)
    v_all = jnp.concatenate([cv_ref[0], qkv[:, vcol:]], axis=0)
    nk_ref[0] = k_all[l:]
    nv_ref[0] = v_all[l:]
    nk = -(-(n_keep + l) // LANES) * LANES
    pad = jnp.zeros((nk - n_keep - l, k_all.shape[1]), F32)
    t = lax.broadcasted_iota(jnp.int32, (2 * l, nk), 0) % l
    j = lax.broadcasted_iota(jnp.int32, (2 * l, nk), 1)
    rel = jnp.where(j < n_keep, t + n_keep - j, t - (j - n_keep))
    valid = (rel >= 0) & (rel < WINDOW) & (j < n_keep + l)
    _slab_attention(qkv_ref, o_ref, jnp.concatenate([k_all, pad], axis=0), jnp.concatenate([v_all, pad], axis=0),
                    valid, sink_ref)


def _attn_sample(qkv, cache_k, cache_v, sinks):
    b_, l, n = qkv.shape
    n_keep, kvw = cache_k.shape[1:]
    nb = _seqs_per_step(b_, 1)
    return pl.pallas_call(
        _per_sequence(_attn_sample_kernel, nb, [False] + [True] * 6),
        grid=(b_ // nb,),
        in_specs=[pl.BlockSpec(memory_space=pltpu.SMEM),
                  pl.BlockSpec((nb, l, n), lambda b: (b, 0, 0)),
                  pl.BlockSpec((nb, n_keep, kvw), lambda b: (b, 0, 0)),
                  pl.BlockSpec((nb, n_keep, kvw), lambda b: (b, 0, 0))],
        out_specs=[pl.BlockSpec((nb, l, ATT_HEADS * HEAD_DIM), lambda b: (b, 0, 0)),
                   pl.BlockSpec((nb, n_keep, kvw), lambda b: (b, 0, 0)),
                   pl.BlockSpec((nb, n_keep, kvw), lambda b: (b, 0, 0))],
        out_shape=[jax.ShapeDtypeStruct((b_, l, ATT_HEADS * HEAD_DIM), BF16),
                   jax.ShapeDtypeStruct((b_, n_keep, kvw), F32),
                   jax.ShapeDtypeStruct((b_, n_keep, kvw), F32)],
        compiler_params=_cparams(("parallel",)),
        name="attn_sample",
    )(sinks, qkv, cache_k, cache_v)


def _rope_tables(pos):
    half = HEAD_DIM // 2
    inv_freq = ROPE_THETA ** (-jnp.arange(half, dtype=F32) / half)
    ang = pos.astype(F32)[:, None] * inv_freq[None, :]
    cos = jnp.cos(ang)
    sin = jnp.sin(ang)
    reps = LANES // HEAD_DIM
    return (jnp.tile(jnp.concatenate([cos, cos], axis=1), (1, reps)),
            jnp.tile(jnp.concatenate([-sin, sin], axis=1), (1, reps)))


def _tail_pad(buf):
    return jnp.pad(buf, ((0, 0), (TAIL_ROWS - buf.shape[1], 0), (0, 0)))


def _trunk(x, mods, pos, st, pw, seq_len):
    g_, m, d = x.shape
    nseq = g_ * m // seq_len
    blk = _block_rows(m)
    chunk = min(seq_len, SSM_CHUNK)
    lru_rows = min(seq_len, LRU_ROWS)

    proj, dtp = _inproj(x, pw['norms'][0, 0:1], mods[0], pw['w_in_t'], SSM_INNER + SSM_CONV_DIM, SSM_HEADS,
                        blk['inproj'], INPROJ_COLS)
    n_main = proj.shape[-1]
    proj_s = proj.reshape(nseq, seq_len, n_main)
    dtp_s = dtp.reshape(nseq, seq_len, LANES)
    y_ssm, ssm_h = _ssd(proj_s, dtp_s, _tail_pad(st['ssm_conv']), st['ssm'].reshape(nseq, SSM_INNER, SSM_STATE),
                        pw['ssm_conv_w'], pw['ssm_conv_b'], pw['ssm_dt_bias'], pw['ssm_a_log'], pw['ssm_d_full'],
                        pw['ssm_norm'], pw['expand'], chunk)
    y_lru, lru_h = _lru(proj_s, _tail_pad(st['lru_conv']), st['lru'].reshape(nseq, 1, LRU_WIDTH),
                        pw['lru_conv_w'], pw['lru_conv_b'], pw['lru_w_a'], pw['lru_b_a'], pw['lru_w_x'],
                        pw['lru_b_x'], pw['lru_lambda'], lru_rows, 3)
    keep = SSM_CONV - 1
    new_ssm_conv = proj_s[:, seq_len - keep:, SSM_INNER:SSM_INNER + SSM_CONV_DIM]
    new_lru_conv = proj_s[:, seq_len - keep:, n_main - LRU_WIDTH:]
    x = _outproj([y_ssm.reshape(g_, m, SSM_INNER), y_lru.reshape(g_, m, LRU_WIDTH)],
                 pw['w_out_hyb'], x, pw['norms'][0, 1:2], mods[0], blk['outproj'])
    x = _mlp(x, pw['norms'][0, 2:3], pw['norms'][0, 3:4], mods[0], pw['w_up'], pw['w_down'], 0, blk['mlp'],
             MLP_COLS)

    cos_t, sin_t = _rope_tables(pos)
    if seq_len < blk['qkv']:
        cos_t = jnp.tile(cos_t, (blk['qkv'] // seq_len, 1))
        sin_t = jnp.tile(sin_t, (blk['qkv'] // seq_len, 1))
    qkv = _qkv(x, pw['norms'][1, 0:1], mods[1], pw['w_qkv'], pw['b_qkv'], cos_t, sin_t, blk['qkv'], QKV_COLS)
    qkv_s = qkv.reshape(nseq, seq_len, qkv.shape[-1])
    kcol = ATT_HEADS * HEAD_DIM
    vcol = kcol + KV_HEADS * HEAD_DIM
    if st['k'] is None:
        o = _attn_prompt(qkv_s, pw['attn_sinks'])
        n_keep = min(WINDOW, seq_len)
        k_new = qkv_s[:, seq_len - n_keep:, kcol:vcol]
        v_new = qkv_s[:, seq_len - n_keep:, vcol:]
    else:
        n_keep = st['k'].shape[1]
        o, k_new, v_new = _attn_sample(qkv_s, st['k'].reshape(nseq, n_keep, KV_HEADS * HEAD_DIM),
                                       st['v'].reshape(nseq, n_keep, KV_HEADS * HEAD_DIM), pw['attn_sinks'])
    k_new = k_new.reshape(nseq, n_keep, KV_HEADS, HEAD_DIM)
    v_new = v_new.reshape(nseq, n_keep, KV_HEADS, HEAD_DIM)
    x = _outproj([o.reshape(g_, m, kcol)], pw['w_out_attn'], x, pw['norms'][1, 1:2], mods[1], blk['outproj'])
    x = _mlp(x, pw['norms'][1, 2:3], pw['norms'][1, 3:4], mods[1], pw['w_up'], pw['w_down'], 1, blk['mlp'],
             MLP_COLS)

    states = (new_ssm_conv[None], ssm_h.reshape(1, nseq, SSM_HEADS, SSM_HEAD_DIM, SSM_STATE),
              new_lru_conv[None], lru_h.reshape(1, nseq, LRU_WIDTH), k_new[None], v_new[None])
    return x, states


def kernel(x_prompt, x_sample, state_ssm_conv, state_ssm, state_lru_conv, state_lru, cache_k, cache_v, c_prompt, c_sample, w_mod, b_mod, norms, w_in_hyb, ssm_conv_w, ssm_conv_b, ssm_dt_bias, ssm_a_log, ssm_d, ssm_norm, lru_conv_w, lru_conv_b, lru_w_a, lru_b_a, lru_w_x, lru_b_x, lru_lambda, w_out_hyb, w_qkv, b_qkv, attn_sinks, w_out_attn, w_up, w_down):
    bp, lp, d = x_prompt.shape
    bs, ls, _ = x_sample.shape
    depth = w_mod.shape[0]

    pad_lanes = lambda v: jnp.pad(v.reshape(1, -1), ((0, 0), (0, LANES - v.size)))
    head_of_lane = jnp.arange(SSM_INNER) // SSM_HEAD_DIM
    pw = {
        'norms': norms,
        'w_in_t': jnp.swapaxes(w_in_hyb, 1, 2),
        'ssm_conv_w': ssm_conv_w[0], 'ssm_conv_b': ssm_conv_b[0].reshape(1, -1),
        'ssm_dt_bias': pad_lanes(ssm_dt_bias[0]), 'ssm_a_log': pad_lanes(ssm_a_log[0]),
        'ssm_d_full': jnp.repeat(ssm_d[0], SSM_HEAD_DIM).reshape(1, -1),
        'ssm_norm': ssm_norm[0].reshape(1, -1),
        'expand': (jnp.arange(LANES)[:, None] == head_of_lane[None, :]).astype(BF16),
        'lru_conv_w': lru_conv_w[0], 'lru_conv_b': lru_conv_b[0].reshape(1, -1),
        'lru_w_a': lru_w_a[0].astype(BF16), 'lru_b_a': lru_b_a[0].reshape(1, -1),
        'lru_w_x': lru_w_x[0].astype(BF16), 'lru_b_x': lru_b_x[0].reshape(1, -1),
        'lru_lambda': lru_lambda[0].reshape(1, -1),
        'w_out_hyb': w_out_hyb[0].astype(BF16),
        'w_qkv': w_qkv[0], 'b_qkv': b_qkv[0].reshape(1, -1),
        'attn_sinks': attn_sinks[0],
        'w_out_attn': w_out_attn[0],
        'w_up': w_up, 'w_down': w_down,
    }

    mod = _adaln(jnp.concatenate([c_prompt, c_sample], axis=0), w_mod, b_mod)
    mods_p = [mod[l, :bp][:, None, :] for l in range(depth)]
    mods_s = [jnp.repeat(mod[l, bp:], ls, axis=0)[None] for l in range(depth)]

    zeros = lambda *shape: jnp.zeros(shape, F32)
    st_p = {'ssm_conv': zeros(bp, SSM_CONV - 1, SSM_CONV_DIM), 'ssm': zeros(bp, SSM_INNER, SSM_STATE),
            'lru_conv': zeros(bp, SSM_CONV - 1, LRU_WIDTH), 'lru': zeros(bp, LRU_WIDTH), 'k': None, 'v': None}
    st_s = {'ssm_conv': state_ssm_conv[0], 'ssm': state_ssm[0], 'lru_conv': state_lru_conv[0],
            'lru': state_lru[0], 'k': cache_k[0], 'v': cache_v[0]}

    pos_p = jnp.arange(lp, dtype=jnp.int32)
    pos_s = PAST_LEN + jnp.arange(ls, dtype=jnp.int32)
    y_p, sp = _trunk(x_prompt, mods_p, pos_p, st_p, pw, lp)
    y_s, ss = _trunk(x_sample.reshape(1, bs * ls, d), mods_s, pos_s, st_s, pw, ls)
    return (y_p, y_s.reshape(bs, ls, d)) + sp + ss
```

```python
import functools
import math

import jax
import jax.numpy as jnp
from jax import lax
from jax.experimental import pallas as pl
from jax.experimental.pallas import tpu as pltpu

F32 = jnp.float32
BF16 = jnp.bfloat16

D_MODEL = 2048
PAST_LEN = 16384
SSM_HEADS = 32
SSM_HEAD_DIM = 64
SSM_INNER = SSM_HEADS * SSM_HEAD_DIM
SSM_GROUPS = 8
SSM_STATE = 128
SSM_CONV = 4
SSM_CHUNK = 128
SSM_CONV_DIM = SSM_INNER + 2 * SSM_GROUPS * SSM_STATE
HEADS_PER_GROUP = SSM_HEADS // SSM_GROUPS
GROUP_WIDTH = HEADS_PER_GROUP * SSM_HEAD_DIM
LRU_WIDTH = D_MODEL
LRU_BLOCKS = 8
LRU_BLOCK = LRU_WIDTH // LRU_BLOCKS
LRU_C = 8.0
ATT_HEADS = 32
KV_HEADS = 8
HEAD_DIM = 64
Q_PER_KV = ATT_HEADS // KV_HEADS
WINDOW = 128
ROPE_THETA = 10000.0
D_FF = 4 * D_MODEL
EPS = 1e-6

LANES = 128
SUBLANES = 8
BF16_SUBLANES = 16
TAIL_ROWS = SUBLANES
VMEM_LIMIT = 56 * 1024 * 1024


ADALN_COLS = 1024
INPROJ_COLS = 1024
MLP_COLS = 512
QKV_COLS = 512
LRU_ROWS = 512


def _block_rows(m):
    return {'inproj': min(m, 1024), 'mlp': min(m, 1024), 'qkv': min(m, 512), 'outproj': min(m, 512)}


SEQS_PER_STEP = 4
SSD_CHUNKS_PER_STEP = 2


def _seqs_per_step(n_seq, n_chunks):
    return SEQS_PER_STEP if n_chunks == 1 and n_seq % SEQS_PER_STEP == 0 and n_seq > SEQS_PER_STEP else 1


def _per_sequence(body, nb, per_seq):
    if nb == 1:
        return body

    def kernel(*refs):
        def one(n, carry):
            body(*[r.at[pl.ds(n, 1)] if flag else r for r, flag in zip(refs, per_seq)])
            return carry

        lax.fori_loop(0, nb, one, 0)

    return kernel


def _cparams(sem):
    return pltpu.CompilerParams(dimension_semantics=sem, vmem_limit_bytes=VMEM_LIMIT)


def _rms(x, g):
    return x * lax.rsqrt(jnp.mean(x * x, axis=-1, keepdims=True) + EPS) * g


def _gated_postnorm(x, acc, g_ref, gt_ref):
    if gt_ref.shape[1] == 1:
        return x + _rms(acc, g_ref[...] * gt_ref[0])
    return x + gt_ref[0] * _rms(acc, g_ref[...])


ROW_CHUNK = 16
ROW_UNROLL = 4


def _for_row_chunks(n_rows, body):
    def step(i, carry):
        body(pl.multiple_of(i * ROW_CHUNK, ROW_CHUNK))
        return carry

    lax.fori_loop(0, n_rows // ROW_CHUNK, step, 0, unroll=ROW_UNROLL)


def _mod_rows(ref, r0):
    return ref[0] if ref.shape[1] == 1 else ref[0, pl.ds(r0, ROW_CHUNK), :]


def _prenorm_to(h_scr, x_ref, g_ref, sc_ref, sh_ref):
    one_seq = sc_ref.shape[1] == 1
    gain = g_ref[...] * (1.0 + sc_ref[0]) if one_seq else g_ref[...]

    def body(r0):
        x = x_ref[0, pl.ds(r0, ROW_CHUNK), :]
        h = _rms(x, gain) if one_seq else _rms(x, gain) * (1.0 + _mod_rows(sc_ref, r0))
        h_scr[pl.ds(r0, ROW_CHUNK), :] = (h + _mod_rows(sh_ref, r0)).astype(BF16)

    _for_row_chunks(x_ref.shape[1], body)


def _sigmoid(x):
    return 0.5 * (jnp.tanh(0.5 * x) + 1.0)


def _silu(x):
    return x * _sigmoid(x)


def _softplus(x):
    return jnp.maximum(x, 0.0) + jnp.log1p(jnp.exp(-jnp.abs(x)))


def _sqrt_neg_expm1(x):
    s = jnp.tanh(-0.5 * x)
    t = (2.0 * s) * (1.0 + s)
    return jnp.where(t > 0.0, (2.0 * s) * lax.rsqrt(t), 0.0)


def _split3(x):
    hi = x.astype(BF16)
    r = x - hi.astype(F32)
    mid = r.astype(BF16)
    lo = (r - mid.astype(F32)).astype(BF16)
    return [hi, mid, lo]


def _adaln_kernel(c_ref, w_ref, b_ref, o_ref):
    s = _silu(c_ref[...]).astype(BF16)
    o_ref[0] = jnp.dot(s, w_ref[0].astype(BF16), preferred_element_type=F32) + b_ref[0]


def _adaln(c_all, w_mod, b_mod):
    nb, d = c_all.shape
    depth, _, n = w_mod.shape
    bn = ADALN_COLS
    return pl.pallas_call(
        _adaln_kernel,
        grid=(depth, n // bn),
        in_specs=[pl.BlockSpec((nb, d), lambda l, j: (0, 0)),
                  pl.BlockSpec((1, d, bn), lambda l, j: (l, 0, j)),
                  pl.BlockSpec((1, 1, bn), lambda l, j: (l, 0, j))],
        out_specs=pl.BlockSpec((1, nb, bn), lambda l, j: (l, 0, j)),
        out_shape=jax.ShapeDtypeStruct((depth, nb, n), F32),
        compiler_params=_cparams(("parallel", "parallel")),
        name="adaln",
    )(c_all, w_mod, b_mod.reshape(depth, 1, n))


def _mod_spec(mod, bm, k):
    if mod.shape[1] == 1:
        return pl.BlockSpec((1, 1, D_MODEL), lambda g, i, *_: (g, 0, k))
    return pl.BlockSpec((1, bm, D_MODEL), lambda g, i, *_: (g, i, k))


def _inproj_kernel(x_ref, g_ref, sh_ref, sc_ref, wt_ref, wdt_ref, o_ref, dt_ref, h_scr, *, n_dt):
    nt = (((1,), (1,)), ((), ()))

    @pl.when(pl.program_id(2) == 0)
    def _():
        _prenorm_to(h_scr, x_ref, g_ref, sc_ref, sh_ref)
        dt = lax.dot_general(h_scr[...], wdt_ref[...].astype(BF16), nt, preferred_element_type=F32)
        lane = lax.broadcasted_iota(jnp.int32, dt.shape, 1)
        dt_ref[0] = jnp.where(lane < n_dt, dt, 0.0)

    o_ref[0] = lax.dot_general(h_scr[...], wt_ref[...].astype(BF16), nt, preferred_element_type=F32)


def _inproj(x, gain, mod, w_t, c_dt, n_dt, bm, bn):
    g_, m, d = x.shape
    n = w_t.shape[1] - n_dt
    row0 = lambda j: pl.multiple_of(jnp.where(j * bn < c_dt, j * bn, j * bn + n_dt), SUBLANES)
    return pl.pallas_call(
        functools.partial(_inproj_kernel, n_dt=n_dt),
        grid=(g_, m // bm, n // bn),
        in_specs=[pl.BlockSpec((1, bm, d), lambda g, i, j: (g, i, 0)),
                  pl.BlockSpec((1, d), lambda g, i, j: (0, 0)),
                  _mod_spec(mod, bm, 0), _mod_spec(mod, bm, 1),
                  pl.BlockSpec((None, pl.Element(bn), pl.Element(d)), lambda g, i, j: (0, row0(j), 0)),
                  pl.BlockSpec((None, pl.Element(LANES), pl.Element(d)), lambda g, i, j: (0, c_dt, 0))],
        out_specs=[pl.BlockSpec((1, bm, bn), lambda g, i, j: (g, i, j)),
                   pl.BlockSpec((1, bm, LANES), lambda g, i, j: (g, i, 0))],
        out_shape=[jax.ShapeDtypeStruct((g_, m, n), F32),
                   jax.ShapeDtypeStruct((g_, m, LANES), F32)],
        scratch_shapes=[pltpu.VMEM((bm, d), BF16)],
        compiler_params=_cparams(("parallel", "parallel", "arbitrary")),
        name="inproj",
    )(x, gain, mod, mod, w_t, w_t)


def _swap_halves(x):
    lane = lax.broadcasted_iota(jnp.int32, x.shape, 1)
    first = (lane % HEAD_DIM) < (HEAD_DIM // 2)
    return jnp.where(first, pltpu.roll(x, LANES - HEAD_DIM // 2, 1), pltpu.roll(x, HEAD_DIM // 2, 1))


def _qkv_kernel(x_ref, g_ref, sh_ref, sc_ref, w_ref, b_ref, cos_ref, sin_ref, o_ref, h_scr, *, rope_cols, bn):
    _prenorm_to(h_scr, x_ref, g_ref, sc_ref, sh_ref)
    h = h_scr[...]
    cos = cos_ref[...]
    sin = sin_ref[...]
    for c0 in range(0, w_ref.shape[1], bn):
        acc = jnp.dot(h, w_ref[:, c0:c0 + bn].astype(BF16), preferred_element_type=F32) + b_ref[:, c0:c0 + bn]
        if c0 < rope_cols:
            for c in range(c0, c0 + bn, LANES):
                a = acc[:, c - c0:c - c0 + LANES]
                o_ref[0, :, c:c + LANES] = a * cos + _swap_halves(a) * sin
        else:
            o_ref[0, :, c0:c0 + bn] = acc


def _qkv(x, gain, mod, w, b, cos_t, sin_t, bm, bn):
    g_, m, d = x.shape
    n = w.shape[1]
    rope_cols = (ATT_HEADS + KV_HEADS) * HEAD_DIM
    rows_per_g = cos_t.shape[0] // bm
    return pl.pallas_call(
        functools.partial(_qkv_kernel, rope_cols=rope_cols, bn=bn),
        grid=(g_, m // bm),
        in_specs=[pl.BlockSpec((1, bm, d), lambda g, i: (g, i, 0)),
                  pl.BlockSpec((1, d), lambda g, i: (0, 0)),
                  _mod_spec(mod, bm, 0), _mod_spec(mod, bm, 1),
                  pl.BlockSpec((d, n), lambda g, i: (0, 0), pipeline_mode=pl.Buffered(1)),
                  pl.BlockSpec((1, n), lambda g, i: (0, 0)),
                  pl.BlockSpec((bm, LANES), lambda g, i: (i % rows_per_g, 0)),
                  pl.BlockSpec((bm, LANES), lambda g, i: (i % rows_per_g, 0))],
        out_specs=pl.BlockSpec((1, bm, n), lambda g, i: (g, i, 0)),
        out_shape=jax.ShapeDtypeStruct((g_, m, n), F32),
        scratch_shapes=[pltpu.VMEM((bm, d), BF16)],
        compiler_params=_cparams(("parallel", "parallel")),
        name="qkv",
    )(x, gain, mod, mod, w, b, cos_t, sin_t)


def _outproj_kernel(*refs, n_lhs):
    lhs = refs[:n_lhs]
    ws = refs[n_lhs:2 * n_lhs]
    x_ref, g_ref, gt_ref, o_ref = refs[2 * n_lhs:]
    acc = jnp.dot(lhs[0][0], ws[0][...].astype(BF16), preferred_element_type=F32)
    for a, w in zip(lhs[1:], ws[1:]):
        acc = acc + jnp.dot(a[0], w[...].astype(BF16), preferred_element_type=F32)
    o_ref[0] = _gated_postnorm(x_ref[0], acc, g_ref, gt_ref)


def _outproj(lhs_list, w, x, gain, mod, bm):
    g_, m, d = x.shape
    n_lhs = len(lhs_list)
    kw = w.shape[0] // n_lhs
    lhs_specs = [pl.BlockSpec((1, bm, kw), lambda g, i: (g, i, 0)) for _ in lhs_list]
    w_specs = [pl.BlockSpec((kw, d), lambda g, i, k=k: (k, 0), pipeline_mode=pl.Buffered(1)) for k in range(n_lhs)]
    w_list = [w] * n_lhs
    if mod.shape[1] == 1:
        gt_spec = pl.BlockSpec((1, 1, d), lambda g, i: (g, 0, 2))
    else:
        gt_spec = pl.BlockSpec((1, bm, d), lambda g, i: (g, i, 2))
    return pl.pallas_call(
        functools.partial(_outproj_kernel, n_lhs=n_lhs),
        grid=(g_, m // bm),
        in_specs=lhs_specs + w_specs + [pl.BlockSpec((1, bm, d), lambda g, i: (g, i, 0)),
                                        pl.BlockSpec((1, d), lambda g, i: (0, 0)), gt_spec],
        out_specs=pl.BlockSpec((1, bm, d), lambda g, i: (g, i, 0)),
        out_shape=jax.ShapeDtypeStruct((g_, m, d), F32),
        compiler_params=_cparams(("parallel", "parallel")),
        name="outproj",
    )(*lhs_list, *w_list, x, gain, mod)


def _mlp_kernel(x_ref, g2_ref, sh_ref, sc_ref, gt_ref, wu_ref, wd_ref, g3_ref, o_ref, h_scr):
    f = pl.program_id(2)

    @pl.when(f == 0)
    def _():
        _prenorm_to(h_scr, x_ref, g2_ref, sc_ref, sh_ref)
        o_ref[0] = jnp.zeros(o_ref.shape[1:], F32)

    u = jnp.dot(h_scr[...], wu_ref[...].astype(BF16), preferred_element_type=F32)
    u = jnp.square(jnp.maximum(u, 0.0)).astype(BF16)
    o_ref[0] += jnp.dot(u, wd_ref[...].astype(BF16), preferred_element_type=F32)

    @pl.when(f == pl.num_programs(2) - 1)
    def _():
        o_ref[0] = _gated_postnorm(x_ref[0], o_ref[0], g3_ref, gt_ref)


def _mlp(x, g2, g3, mod, w_up, w_down, layer, bm, bf):
    g_, m, d = x.shape
    dff = w_up.shape[2]
    if mod.shape[1] == 1:
        mspec = lambda k: pl.BlockSpec((1, 1, d), lambda g, i, f: (g, 0, k))
    else:
        mspec = lambda k: pl.BlockSpec((1, bm, d), lambda g, i, f: (g, i, k))
    return pl.pallas_call(
        _mlp_kernel,
        grid=(g_, m // bm, dff // bf),
        in_specs=[pl.BlockSpec((1, bm, d), lambda g, i, f: (g, i, 0), pipeline_mode=pl.Buffered(1)),
                  pl.BlockSpec((1, d), lambda g, i, f: (0, 0)),
                  mspec(3), mspec(4), mspec(5),
                  pl.BlockSpec((None, d, bf), lambda g, i, f: (layer, 0, f)),
                  pl.BlockSpec((None, bf, d), lambda g, i, f: (layer, f, 0)),
                  pl.BlockSpec((1, d), lambda g, i, f: (0, 0))],
        out_specs=pl.BlockSpec((1, bm, d), lambda g, i, f: (g, i, 0)),
        out_shape=jax.ShapeDtypeStruct((g_, m, d), F32),
        scratch_shapes=[pltpu.VMEM((bm, d), BF16)],
        compiler_params=_cparams(("parallel", "parallel", "arbitrary")),
        name="mlp",
    )(x, g2, mod, mod, mod, w_up, w_down, g3)


def _dwconv(raw, tail, w, b):
    t = raw.shape[0]
    assert w.shape[0] == 4
    ext = jnp.concatenate([tail, raw], axis=0)
    z1 = pltpu.roll(ext, 1, 0)
    near = ext * w[3:4] + z1 * w[2:3]
    far = ext * w[1:2] + z1 * w[0:1]
    y = near + pltpu.roll(far, 2, 0)
    return y[TAIL_ROWS:] + b, ext[t:t + TAIL_ROWS]


def _cumsum_rows(x):
    n = x.shape[0]
    row = lax.broadcasted_iota(jnp.int32, x.shape, 0)
    d = 1
    while d < n:
        x = x + jnp.where(row >= d, pltpu.roll(x, d, 0), 0.0)
        d *= 2
    return x


def _ssd_kernel(z_ref, xs_ref, bc_ref, dt_ref, buf_ref, h0_ref, cw_ref, cb_ref, dtb_ref, alog_ref,
                dsk_ref, ng_ref, exp_ref, y_ref, st_ref, tail_scr, *, q_valid, q, starts_step):
    if starts_step:
        @pl.when(pl.program_id(1) == 0)
        def _():
            tail_scr[...] = buf_ref[0]
            st_ref[0] = h0_ref[0]

    def padded(v):
        if q_valid == q:
            return v
        return jnp.concatenate([v, jnp.zeros((q - q_valid, v.shape[1]), v.dtype)], axis=0)

    def transposed(v):
        if q == LANES:
            return v.T
        return jnp.concatenate([v, jnp.zeros((LANES - q, LANES), v.dtype)], axis=0).T[:, :q]

    cw = cw_ref[...]
    cb = cb_ref[...]
    xs, tail_x = _dwconv(xs_ref[0], tail_scr[:, :SSM_INNER], cw[:, :SSM_INNER], cb[:, :SSM_INNER])
    bcm, tail_b = _dwconv(bc_ref[0], tail_scr[:, SSM_INNER:], cw[:, SSM_INNER:], cb[:, SSM_INNER:])
    tail_scr[:, :SSM_INNER] = tail_x
    tail_scr[:, SSM_INNER:] = tail_b
    xs = padded(_silu(xs))
    bcm = padded(_silu(bcm))
    z = padded(z_ref[0])

    dt = _softplus(dt_ref[0] + dtb_ref[...])
    dt = padded(dt)
    a = -jnp.exp(alog_ref[...])
    acs = _cumsum_rows(dt * a)
    acs_t = transposed(acs)

    full = jnp.dot(jnp.concatenate(_split3(dt) + _split3(acs), axis=0), exp_ref[...], preferred_element_type=F32)
    dt_full = (full[0:q] + full[q:2 * q]) + full[2 * q:3 * q]
    acs_full = (full[3 * q:4 * q] + full[4 * q:5 * q]) + full[5 * q:6 * q]
    xdt = xs * dt_full
    dec_out = jnp.exp(acs_full)
    xw = (xdt * jnp.exp(acs_full[q - 1:q] - acs_full)).astype(BF16)
    xdt_b = xdt.astype(BF16)
    bc_b = bcm.astype(BF16)

    row = lax.broadcasted_iota(jnp.int32, (q, q), 0)
    col = lax.broadcasted_iota(jnp.int32, (q, q), 1)
    causal = row >= col
    lane = lax.broadcasted_iota(jnp.int32, (q, LANES), 1)
    lo_half = lane < SSM_HEAD_DIM
    nt = (((1,), (1,)), ((), ()))
    tn = (((0,), (0,)), ((), ()))
    n_bc = SSM_GROUPS * SSM_STATE

    groups = range(SSM_GROUPS)
    cols = [slice(g * GROUP_WIDTH, (g + 1) * GROUP_WIDTH) for g in groups]
    b_gs = [bc_b[:, g * SSM_STATE:(g + 1) * SSM_STATE] for g in groups]
    c_gs = [bc_b[:, n_bc + g * SSM_STATE:n_bc + (g + 1) * SSM_STATE] for g in groups]
    h_gs = [st_ref[0, cols[g], :] for g in groups]
    cbms = [lax.dot_general(c_gs[g], b_gs[g], nt, preferred_element_type=F32) for g in groups]
    y_offs = [lax.dot_general(c_gs[g], h_gs[g].astype(BF16), nt, preferred_element_type=F32) for g in groups]
    st_news = [lax.dot_general(xw[:, cols[g]], b_gs[g], tn, preferred_element_type=F32) for g in groups]
    ms = [(cbms[h // HEADS_PER_GROUP]
           * jnp.exp(jnp.where(causal, acs[:, h:h + 1] - acs_t[h:h + 1, :], -jnp.inf))).astype(BF16)
          for h in range(SSM_HEADS)]
    y_diags = []
    for pair in range(SSM_HEADS // 2):
        x_pair = xdt_b[:, pair * LANES:(pair + 1) * LANES]
        zero = jnp.zeros_like(x_pair)
        y_diags.append(jnp.dot(ms[2 * pair], jnp.where(lo_half, x_pair, zero), preferred_element_type=F32)
                       + jnp.dot(ms[2 * pair + 1], jnp.where(lo_half, zero, x_pair), preferred_element_type=F32))
    pairs_per_group = HEADS_PER_GROUP // 2
    for g in groups:
        y_g = y_offs[g] * dec_out[:, cols[g]]
        y_g = y_g + jnp.concatenate(y_diags[g * pairs_per_group:(g + 1) * pairs_per_group], axis=1)
        y_g = y_g + xs[:, cols[g]] * dsk_ref[:, cols[g]]
        y_g = y_g * _silu(z[:, cols[g]])
        y_g = y_g * lax.rsqrt(jnp.mean(y_g * y_g, axis=-1, keepdims=True) + EPS)
        y_g = y_g * ng_ref[:, cols[g]]
        y_ref[0, :, cols[g]] = y_g[:q_valid].astype(y_ref.dtype)
    for h in range(SSM_HEADS):
        g, r = divmod(h, HEADS_PER_GROUP)
        rows = slice(r * SSM_HEAD_DIM, (r + 1) * SSM_HEAD_DIM)
        decay = jnp.exp(acs_t[h:h + 1, q - 1:q])
        st_ref[0, h * SSM_HEAD_DIM:(h + 1) * SSM_HEAD_DIM, :] = h_gs[g][rows] * decay + st_news[g][rows]


def _ssd(proj, dtp, buf8, h0, conv_w, conv_b, dt_bias, a_log, d_full, norm_g, expand, rows):
    b_, l, _ = proj.shape
    chunk = rows
    n_sub = min(SSD_CHUNKS_PER_STEP, l // chunk)
    rows = n_sub * chunk
    nc = l // rows
    nb = _seqs_per_step(b_, nc)
    st_rows = SSM_INNER

    def kernel_fn(z, x, bc, dt, *rest):
        *fixed, y, st, tail = rest
        for s in range(n_sub):
            sub = lambda ref: ref.at[:, pl.ds(s * chunk, chunk)]
            _ssd_kernel(sub(z), sub(x), sub(bc), sub(dt), *fixed, sub(y), st, tail,
                        q_valid=chunk, q=max(chunk, BF16_SUBLANES), starts_step=s == 0)

    return pl.pallas_call(
        _per_sequence(kernel_fn, nb, [True] * 6 + [False] * 7 + [True] * 2 + [False]),
        grid=(b_ // nb, nc),
        in_specs=[pl.BlockSpec((nb, rows, SSM_INNER), lambda b, c: (b, c, 0)),
                  pl.BlockSpec((nb, rows, SSM_INNER), lambda b, c: (b, c, 1)),
                  pl.BlockSpec((nb, rows, SSM_INNER), lambda b, c: (b, c, 2)),
                  pl.BlockSpec((nb, rows, LANES), lambda b, c: (b, c, 0)),
                  pl.BlockSpec((nb, TAIL_ROWS, SSM_CONV_DIM), lambda b, c: (b, 0, 0)),
                  pl.BlockSpec((nb, st_rows, SSM_STATE), lambda b, c: (b, 0, 0)),
                  pl.BlockSpec((SSM_CONV, SSM_CONV_DIM), lambda b, c: (0, 0)),
                  pl.BlockSpec((1, SSM_CONV_DIM), lambda b, c: (0, 0)),
                  pl.BlockSpec((1, LANES), lambda b, c: (0, 0)),
                  pl.BlockSpec((1, LANES), lambda b, c: (0, 0)),
                  pl.BlockSpec((1, SSM_INNER), lambda b, c: (0, 0)),
                  pl.BlockSpec((1, SSM_INNER), lambda b, c: (0, 0)),
                  pl.BlockSpec((LANES, SSM_INNER), lambda b, c: (0, 0))],
        out_specs=[pl.BlockSpec((nb, rows, SSM_INNER), lambda b, c: (b, c, 0)),
                   pl.BlockSpec((nb, st_rows, SSM_STATE), lambda b, c: (b, 0, 0))],
        out_shape=[jax.ShapeDtypeStruct((b_, l, SSM_INNER), BF16),
                   jax.ShapeDtypeStruct((b_, st_rows, SSM_STATE), F32)],
        scratch_shapes=[pltpu.VMEM((TAIL_ROWS, SSM_CONV_DIM), F32)],
        compiler_params=_cparams(("parallel", "arbitrary")),
        name="ssd",
    )(proj, proj, proj, dtp, buf8, h0, conv_w, conv_b, dt_bias, a_log, d_full, norm_g, expand)


SCAN_SEG = 4
SCAN_ROWS = SUBLANES * SCAN_SEG


def _shift_rows_fill(x, d, fill, row):
    return jnp.where(row >= d, pltpu.roll(x, d, 0), fill)


def _sublane_scan(e, pe, row):
    d = 1
    while d < SUBLANES:
        e = e + pe * _shift_rows_fill(e, d, 0.0, row)
        pe = pe * _shift_rows_fill(pe, d, 1.0, row)
        d *= 2
    return e, pe


def _linear_scan(a_scr, u_scr, h0):
    n_slab, t, _ = a_scr.shape
    seg = SCAN_SEG if t % SCAN_ROWS == 0 else 1
    row = lax.broadcasted_iota(jnp.int32, (SUBLANES, LANES), 0)

    def block(i, h_prev):
        base = pl.multiple_of(i * (SUBLANES * seg), SUBLANES)
        out = []
        for c in range(n_slab):
            rows = [pl.ds(base + j, SUBLANES, stride=seg) if seg > 1 else pl.ds(base, SUBLANES) for j in range(seg)]
            hs, ps = [], []
            for j in range(seg):
                a, u = a_scr[c, rows[j], :], u_scr[c, rows[j], :]
                hs.append(u if j == 0 else a * hs[-1] + u)
                ps.append(a if j == 0 else a * ps[-1])
            e, pe = _sublane_scan(hs[-1], ps[-1], row)
            g = e + pe * h_prev[c]
            carry = jnp.where(row == 0, h_prev[c], pltpu.roll(g, 1, 0))
            for j in range(seg):
                u_scr[c, rows[j], :] = hs[j] + ps[j] * carry
            out.append(jnp.broadcast_to(g[SUBLANES - 1:], g.shape))
        return tuple(out)

    h_init = tuple(jnp.broadcast_to(h0[:, c * LANES:(c + 1) * LANES], (SUBLANES, LANES)) for c in range(n_slab))
    h_fin = lax.fori_loop(0, t // (SUBLANES * seg), block, h_init)
    return jnp.concatenate([h[:1] for h in h_fin], axis=1)


def _gelu_tanh(x):
    c0 = math.sqrt(2.0 / math.pi)
    hx = 0.5 * x
    return hx + hx * jnp.tanh(x * (c0 + (c0 * 0.044715) * (x * x)))


def _lru_kernel(gate_ref, xr_ref, buf_ref, h0_ref, cw_ref, cb_ref, wa_ref, ba_ref, wx_ref, bx_ref,
                lam_ref, y_ref, hl_ref, tail_scr, a_scr, u_scr):
    c = pl.program_id(1)

    @pl.when(c == 0)
    def _():
        tail_scr[...] = buf_ref[0]
        hl_ref[0] = h0_ref[0]

    xc, tail = _dwconv(xr_ref[0], tail_scr[...], cw_ref[...], cb_ref[...])
    tail_scr[...] = tail
    sp = _softplus(-lam_ref[...])
    slabs_per_block = LRU_BLOCK // LANES
    for k in range(LRU_BLOCKS):
        k0 = k * LRU_BLOCK
        xk = xc[:, k0:k0 + LRU_BLOCK]
        xkb = xk.astype(BF16)
        gr = _sigmoid(jnp.dot(xkb, wa_ref[k], preferred_element_type=F32) + ba_ref[:, k0:k0 + LRU_BLOCK])
        gi = _sigmoid(jnp.dot(xkb, wx_ref[k], preferred_element_type=F32) + bx_ref[:, k0:k0 + LRU_BLOCK])
        log_a = -LRU_C * gr * sp[:, k0:k0 + LRU_BLOCK]
        a = jnp.exp(log_a)
        u = _sqrt_neg_expm1(2.0 * log_a) * gi * xk
        for s in range(slabs_per_block):
            a_scr[k * slabs_per_block + s] = a[:, s * LANES:(s + 1) * LANES]
            u_scr[k * slabs_per_block + s] = u[:, s * LANES:(s + 1) * LANES]

    hl_ref[0] = _linear_scan(a_scr, u_scr, hl_ref[0])
    for s in range(a_scr.shape[0]):
        gate = gate_ref[0, :, s * LANES:(s + 1) * LANES]
        y_ref[0, :, s * LANES:(s + 1) * LANES] = (u_scr[s] * _gelu_tanh(gate)).astype(y_ref.dtype)


def _lru(proj, buf8, h0, conv_w, conv_b, w_a, b_a, w_x, b_x, lam, rows, col0):
    b_, l, _ = proj.shape
    w = LRU_WIDTH
    nb = _seqs_per_step(b_, l // rows)
    full2 = lambda shape: pl.BlockSpec(shape, lambda b, c: (0,) * len(shape))
    return pl.pallas_call(
        _per_sequence(_lru_kernel, nb, [True] * 4 + [False] * 7 + [True] * 2 + [False] * 3),
        grid=(b_ // nb, l // rows),
        in_specs=[pl.BlockSpec((nb, rows, w), lambda b, c: (b, c, col0)),
                  pl.BlockSpec((nb, rows, w), lambda b, c: (b, c, col0 + 1)),
                  pl.BlockSpec((nb, TAIL_ROWS, w), lambda b, c: (b, 0, 0)),
                  pl.BlockSpec((nb, 1, w), lambda b, c: (b, 0, 0)),
                  full2((SSM_CONV, w)), full2((1, w)),
                  full2((LRU_BLOCKS, LRU_BLOCK, LRU_BLOCK)), full2((1, w)),
                  full2((LRU_BLOCKS, LRU_BLOCK, LRU_BLOCK)), full2((1, w)),
                  full2((1, w))],
        out_specs=[pl.BlockSpec((nb, rows, w), lambda b, c: (b, c, 0)),
                   pl.BlockSpec((nb, 1, w), lambda b, c: (b, 0, 0))],
        out_shape=[jax.ShapeDtypeStruct((b_, l, w), BF16),
                   jax.ShapeDtypeStruct((b_, 1, w), F32)],
        scratch_shapes=[pltpu.VMEM((TAIL_ROWS, w), F32), pltpu.VMEM((w // LANES, rows, LANES), F32),
                        pltpu.VMEM((w // LANES, rows, LANES), F32)],
        compiler_params=_cparams(("parallel", "arbitrary")),
        name="lru",
    )(proj, proj, buf8, h0, conv_w, conv_b, w_a, b_a, w_x, b_x, lam)


def _head_variants(slab, half):
    lane = lax.broadcasted_iota(jnp.int32, slab.shape, 1)
    keep = (lane < HEAD_DIM) if half == 0 else (lane >= HEAD_DIM)
    own = jnp.where(keep, slab, 0.0)
    other = pltpu.roll(own, HEAD_DIM, 1)
    return (own, other) if half == 0 else (other, own)


def _slab_attention(q_ref, o_ref, k2, v2, valid, sink_ref):
    w = q_ref.shape[1]
    nk = k2.shape[0]
    nt = (((1,), (1,)), ((), ()))
    scale = HEAD_DIM ** -0.5
    ones = jnp.ones((nk, LANES), BF16)
    slabs = range(nk // LANES)
    heads = range(KV_HEADS)
    halves = [(e, c) for e in heads for c in range(2)]
    vs, ss = [], []
    for e in heads:
        slab, half = divmod(e, 2)
        k_lo, k_hi = _head_variants(k2[:, slab * LANES:(slab + 1) * LANES], half)
        vs.extend(v.astype(BF16) for v in _head_variants(v2[:, slab * LANES:(slab + 1) * LANES], half))
        kk = jnp.concatenate([k_lo, k_hi], axis=0).astype(BF16)
        c0 = 2 * e * LANES
        qq = jnp.concatenate([q_ref[0, :, c0:c0 + LANES], q_ref[0, :, c0 + LANES:c0 + 2 * LANES]], axis=0)
        ss.append(lax.dot_general((qq * scale).astype(BF16), kk, nt, preferred_element_type=F32))
    scs = [jnp.where(valid, ss[e][:, c * nk:(c + 1) * nk], -jnp.inf) for e, c in halves]
    sinks = [jnp.concatenate([jnp.full((w, LANES), sink_ref[Q_PER_KV * e + c], F32),
                              jnp.full((w, LANES), sink_ref[Q_PER_KV * e + 2 + c], F32)], axis=0) for e, c in halves]
    ms = [jnp.maximum(jnp.broadcast_to(jnp.max(sc, axis=1, keepdims=True), sk.shape), sk)
          for sc, sk in zip(scs, sinks)]
    ps = [jnp.concatenate([jnp.exp(sc[:, i * LANES:(i + 1) * LANES] - m) for i in slabs], axis=1).astype(BF16)
          for sc, m in zip(scs, ms)]
    dens = [jnp.dot(p, ones, preferred_element_type=F32) + jnp.exp(sk - m) for p, sk, m in zip(ps, sinks, ms)]
    outs = [jnp.dot(p, v, preferred_element_type=F32) * (1.0 / den) for p, v, den in zip(ps, vs, dens)]
    for e in heads:
        o = outs[2 * e] + outs[2 * e + 1]
        c0 = 2 * e * LANES
        o_ref[0, :, c0:c0 + LANES] = o[:w].astype(o_ref.dtype)
        o_ref[0, :, c0 + LANES:c0 + 2 * LANES] = o[w:].astype(o_ref.dtype)


ATTN_BLOCKS = 2


def _attn_prompt_kernel(sink_ref, q_ref, kc_ref, kp_ref, vc_ref, vp_ref, o_ref):
    w = WINDOW
    has_prev = pl.program_id(1) > 0
    k_prev = jnp.where(has_prev, kp_ref[0], 0.0)
    v_prev = jnp.where(has_prev, vp_ref[0], 0.0)
    t = lax.broadcasted_iota(jnp.int32, (2 * w, 2 * w), 0) % w
    j = lax.broadcasted_iota(jnp.int32, (2 * w, 2 * w), 1)
    valid = (j > t) & (j <= t + w)
    for s in range(q_ref.shape[1] // w):
        rows = pl.ds(s * w, w)
        k_cur, v_cur = kc_ref[0, rows, :], vc_ref[0, rows, :]
        _slab_attention(q_ref.at[:, rows], o_ref.at[:, rows], jnp.concatenate([k_prev, k_cur], axis=0),
                        jnp.concatenate([v_prev, v_cur], axis=0), valid, sink_ref)
        k_prev, v_prev = k_cur, v_cur


def _attn_prompt(qkv, sinks):
    b_, l, _ = qkv.shape
    w = WINDOW
    rows = ATTN_BLOCKS * w
    qw = ATT_HEADS * HEAD_DIM
    kvw = KV_HEADS * HEAD_DIM
    k_blk = qw // kvw
    prev = lambda n: jnp.maximum(ATTN_BLOCKS * n - 1, 0)
    return pl.pallas_call(
        _attn_prompt_kernel,
        grid=(b_, l // rows),
        in_specs=[pl.BlockSpec(memory_space=pltpu.SMEM),
                  pl.BlockSpec((1, rows, qw), lambda b, n: (b, n, 0)),
                  pl.BlockSpec((1, rows, kvw), lambda b, n: (b, n, k_blk)),
                  pl.BlockSpec((1, w, kvw), lambda b, n: (b, prev(n), k_blk)),
                  pl.BlockSpec((1, rows, kvw), lambda b, n: (b, n, k_blk + 1)),
                  pl.BlockSpec((1, w, kvw), lambda b, n: (b, prev(n), k_blk + 1))],
        out_specs=pl.BlockSpec((1, rows, qw), lambda b, n: (b, n, 0)),
        out_shape=jax.ShapeDtypeStruct((b_, l, qw), BF16),
        compiler_params=_cparams(("parallel", "parallel")),
        name="attn_prompt",
    )(sinks, qkv, qkv, qkv, qkv, qkv)


def _attn_sample_kernel(sink_ref, qkv_ref, ck_ref, cv_ref, o_ref, nk_ref, nv_ref):
    l = qkv_ref.shape[1]
    n_keep = ck_ref.shape[1]
    kcol = ATT_HEADS * HEAD_DIM
    vcol = kcol + KV_HEADS * HEAD_DIM
    qkv = qkv_ref[0]
    k_all = jnp.concatenate([ck_ref[0], qkv[:, kcol:vcol]], axis=0)
    v_all = jnp.concatenate([cv_ref[0], qkv[:, vcol:]], axis=0)
    nk_ref[0] = k_all[l:]
    nv_ref[0] = v_all[l:]
    nk = -(-(n_keep + l) // LANES) * LANES
    pad = jnp.zeros((nk - n_keep - l, k_all.shape[1]), F32)
    t = lax.broadcasted_iota(jnp.int32, (2 * l, nk), 0) % l
    j = lax.broadcasted_iota(jnp.int32, (2 * l, nk), 1)
    rel = jnp.where(j < n_keep, t + n_keep - j, t - (j - n_keep))
    valid = (rel >= 0) & (rel < WINDOW) & (j < n_keep + l)
    _slab_attention(qkv_ref, o_ref, jnp.concatenate([k_all, pad], axis=0), jnp.concatenate([v_all, pad], axis=0),
                    valid, sink_ref)


def _attn_sample(qkv, cache_k, cache_v, sinks):
    b_, l, n = qkv.shape
    n_keep, kvw = cache_k.shape[1:]
    nb = _seqs_per_step(b_, 1)
    return pl.pallas_call(
        _per_sequence(_attn_sample_kernel, nb, [False] + [True] * 6),
        grid=(b_ // nb,),
        in_specs=[pl.BlockSpec(memory_space=pltpu.SMEM),
                  pl.BlockSpec((nb, l, n), lambda b: (b, 0, 0)),
                  pl.BlockSpec((nb, n_keep, kvw), lambda b: (b, 0, 0)),
                  pl.BlockSpec((nb, n_keep, kvw), lambda b: (b, 0, 0))],
        out_specs=[pl.BlockSpec((nb, l, ATT_HEADS * HEAD_DIM), lambda b: (b, 0, 0)),
                   pl.BlockSpec((nb, n_keep, kvw), lambda b: (b, 0, 0)),
                   pl.BlockSpec((nb, n_keep, kvw), lambda b: (b, 0, 0))],
        out_shape=[jax.ShapeDtypeStruct((b_, l, ATT_HEADS * HEAD_DIM), BF16),
                   jax.ShapeDtypeStruct((b_, n_keep, kvw), F32),
                   jax.ShapeDtypeStruct((b_, n_keep, kvw), F32)],
        compiler_params=_cparams(("parallel",)),
        name="attn_sample",
    )(sinks, qkv, cache_k, cache_v)


def _rope_tables(pos):
    half = HEAD_DIM // 2
    inv_freq = ROPE_THETA ** (-jnp.arange(half, dtype=F32) / half)
    ang = pos.astype(F32)[:, None] * inv_freq[None, :]
    cos = jnp.cos(ang)
    sin = jnp.sin(ang)
    reps = LANES // HEAD_DIM
    return (jnp.tile(jnp.concatenate([cos, cos], axis=1), (1, reps)),
            jnp.tile(jnp.concatenate([-sin, sin], axis=1), (1, reps)))


def _tail_pad(buf):
    return jnp.pad(buf, ((0, 0), (TAIL_ROWS - buf.shape[1], 0), (0, 0)))


def _trunk(x, mods, pos, st, pw, seq_len):
    g_, m, d = x.shape
    nseq = g_ * m // seq_len
    blk = _block_rows(m)
    chunk = min(seq_len, SSM_CHUNK)
    lru_rows = min(seq_len, LRU_ROWS)

    proj, dtp = _inproj(x, pw['norms'][0, 0:1], mods[0], pw['w_in_t'], SSM_INNER + SSM_CONV_DIM, SSM_HEADS,
                        blk['inproj'], INPROJ_COLS)
    n_main = proj.shape[-1]
    proj_s = proj.reshape(nseq, seq_len, n_main)
    dtp_s = dtp.reshape(nseq, seq_len, LANES)
    y_ssm, ssm_h = _ssd(proj_s, dtp_s, _tail_pad(st['ssm_conv']), st['ssm'].reshape(nseq, SSM_INNER, SSM_STATE),
                        pw['ssm_conv_w'], pw['ssm_conv_b'], pw['ssm_dt_bias'], pw['ssm_a_log'], pw['ssm_d_full'],
                        pw['ssm_norm'], pw['expand'], chunk)
    y_lru, lru_h = _lru(proj_s, _tail_pad(st['lru_conv']), st['lru'].reshape(nseq, 1, LRU_WIDTH),
                        pw['lru_conv_w'], pw['lru_conv_b'], pw['lru_w_a'], pw['lru_b_a'], pw['lru_w_x'],
                        pw['lru_b_x'], pw['lru_lambda'], lru_rows, 3)
    keep = SSM_CONV - 1
    new_ssm_conv = proj_s[:, seq_len - keep:, SSM_INNER:SSM_INNER + SSM_CONV_DIM]
    new_lru_conv = proj_s[:, seq_len - keep:, n_main - LRU_WIDTH:]
    x = _outproj([y_ssm.reshape(g_, m, SSM_INNER), y_lru.reshape(g_, m, LRU_WIDTH)],
                 pw['w_out_hyb'], x, pw['norms'][0, 1:2], mods[0], blk['outproj'])
    x = _mlp(x, pw['norms'][0, 2:3], pw['norms'][0, 3:4], mods[0], pw['w_up'], pw['w_down'], 0, blk['mlp'],
             MLP_COLS)

    cos_t, sin_t = _rope_tables(pos)
    if seq_len < blk['qkv']:
        cos_t = jnp.tile(cos_t, (blk['qkv'] // seq_len, 1))
        sin_t = jnp.tile(sin_t, (blk['qkv'] // seq_len, 1))
    qkv = _qkv(x, pw['norms'][1, 0:1], mods[1], pw['w_qkv'], pw['b_qkv'], cos_t, sin_t, blk['qkv'], QKV_COLS)
    qkv_s = qkv.reshape(nseq, seq_len, qkv.shape[-1])
    kcol = ATT_HEADS * HEAD_DIM
    vcol = kcol + KV_HEADS * HEAD_DIM
    if st['k'] is None:
        o = _attn_prompt(qkv_s, pw['attn_sinks'])
        n_keep = min(WINDOW, seq_len)
        k_new = qkv_s[:, seq_len - n_keep:, kcol:vcol]
        v_new = qkv_s[:, seq_len - n_keep:, vcol:]
    else:
        n_keep = st['k'].shape[1]
        o, k_new, v_new = _attn_sample(qkv_s, st['k'].reshape(nseq, n_keep, KV_HEADS * HEAD_DIM),
                                       st['v'].reshape(nseq, n_keep, KV_HEADS * HEAD_DIM), pw['attn_sinks'])
    k_new = k_new.reshape(nseq, n_keep, KV_HEADS, HEAD_DIM)
    v_new = v_new.reshape(nseq, n_keep, KV_HEADS, HEAD_DIM)
    x = _outproj([o.reshape(g_, m, kcol)], pw['w_out_attn'], x, pw['norms'][1, 1:2], mods[1], blk['outproj'])
    x = _mlp(x, pw['norms'][1, 2:3], pw['norms'][1, 3:4], mods[1], pw['w_up'], pw['w_down'], 1, blk['mlp'],
             MLP_COLS)

    states = (new_ssm_conv[None], ssm_h.reshape(1, nseq, SSM_HEADS, SSM_HEAD_DIM, SSM_STATE),
              new_lru_conv[None], lru_h.reshape(1, nseq, LRU_WIDTH), k_new[None], v_new[None])
    return x, states


def kernel(x_prompt, x_sample, state_ssm_conv, state_ssm, state_lru_conv, state_lru, cache_k, cache_v, c_prompt, c_sample, w_mod, b_mod, norms, w_in_hyb, ssm_conv_w, ssm_conv_b, ssm_dt_bias, ssm_a_log, ssm_d, ssm_norm, lru_conv_w, lru_conv_b, lru_w_a, lru_b_a, lru_w_x, lru_b_x, lru_lambda, w_out_hyb, w_qkv, b_qkv, attn_sinks, w_out_attn, w_up, w_down):
    bp, lp, d = x_prompt.shape
    bs, ls, _ = x_sample.shape
    depth = w_mod.shape[0]

    pad_lanes = lambda v: jnp.pad(v.reshape(1, -1), ((0, 0), (0, LANES - v.size)))
    head_of_lane = jnp.arange(SSM_INNER) // SSM_HEAD_DIM
    pw = {
        'norms': norms,
        'w_in_t': jnp.swapaxes(w_in_hyb, 1, 2),
        'ssm_conv_w': ssm_conv_w[0], 'ssm_conv_b': ssm_conv_b[0].reshape(1, -1),
        'ssm_dt_bias': pad_lanes(ssm_dt_bias[0]), 'ssm_a_log': pad_lanes(ssm_a_log[0]),
        'ssm_d_full': jnp.repeat(ssm_d[0], SSM_HEAD_DIM).reshape(1, -1),
        'ssm_norm': ssm_norm[0].reshape(1, -1),
        'expand': (jnp.arange(LANES)[:, None] == head_of_lane[None, :]).astype(BF16),
        'lru_conv_w': lru_conv_w[0], 'lru_conv_b': lru_conv_b[0].reshape(1, -1),
        'lru_w_a': lru_w_a[0].astype(BF16), 'lru_b_a': lru_b_a[0].reshape(1, -1),
        'lru_w_x': lru_w_x[0].astype(BF16), 'lru_b_x': lru_b_x[0].reshape(1, -1),
        'lru_lambda': lru_lambda[0].reshape(1, -1),
        'w_out_hyb': w_out_hyb[0].astype(BF16),
        'w_qkv': w_qkv[0], 'b_qkv': b_qkv[0].reshape(1, -1),
        'attn_sinks': attn_sinks[0],
        'w_out_attn': w_out_attn[0],
        'w_up': w_up, 'w_down': w_down,
    }

    mod = _adaln(jnp.concatenate([c_prompt, c_sample], axis=0), w_mod, b_mod)
    mods_p = [mod[l, :bp][:, None, :] for l in range(depth)]
    mods_s = [jnp.repeat(mod[l, bp:], ls, axis=0)[None] for l in range(depth)]

    zeros = lambda *shape: jnp.zeros(shape, F32)
    st_p = {'ssm_conv': zeros(bp, SSM_CONV - 1, SSM_CONV_DIM), 'ssm': zeros(bp, SSM_INNER, SSM_STATE),
            'lru_conv': zeros(bp, SSM_CONV - 1, LRU_WIDTH), 'lru': zeros(bp, LRU_WIDTH), 'k': None, 'v': None}
    st_s = {'ssm_conv': state_ssm_conv[0], 'ssm': state_ssm[0], 'lru_conv': state_lru_conv[0],
            'lru': state_lru[0], 'k': cache_k[0], 'v': cache_v[0]}

    pos_p = jnp.arange(lp, dtype=jnp.int32)
    pos_s = PAST_LEN + jnp.arange(ls, dtype=jnp.int32)
    y_p, sp = _trunk(x_prompt, mods_p, pos_p, st_p, pw, lp)
    y_s, ss = _trunk(x_sample.reshape(1, bs * ls, d), mods_s, pos_s, st_s, pw, ls)
    return (y_p, y_s.reshape(bs, ls, d)) + sp + ss
```

```python
import functools
import math

import jax
import jax.numpy as jnp
from jax import lax
from jax.experimental import pallas as pl
from jax.experimental.pallas import tpu as pltpu

F32 = jnp.float32
BF16 = jnp.bfloat16

D_MODEL = 2048
PAST_LEN = 16384
SSM_HEADS = 32
SSM_HEAD_DIM = 64
SSM_INNER = SSM_HEADS * SSM_HEAD_DIM
SSM_GROUPS = 8
SSM_STATE = 128
SSM_CONV = 4
SSM_CHUNK = 128
SSM_CONV_DIM = SSM_INNER + 2 * SSM_GROUPS * SSM_STATE
HEADS_PER_GROUP = SSM_HEADS // SSM_GROUPS
GROUP_WIDTH = HEADS_PER_GROUP * SSM_HEAD_DIM
LRU_WIDTH = D_MODEL
LRU_BLOCKS = 8
LRU_BLOCK = LRU_WIDTH // LRU_BLOCKS
LRU_C = 8.0
ATT_HEADS = 32
KV_HEADS = 8
HEAD_DIM = 64
Q_PER_KV = ATT_HEADS // KV_HEADS
WINDOW = 128
ROPE_THETA = 10000.0
D_FF = 4 * D_MODEL
EPS = 1e-6

LANES = 128
SUBLANES = 8
BF16_SUBLANES = 16
TAIL_ROWS = SUBLANES
VMEM_LIMIT = 56 * 1024 * 1024


ADALN_COLS = 1024
INPROJ_COLS = 1024
MLP_COLS = 512
QKV_COLS = 512
LRU_ROWS = 512


def _block_rows(m):
    return {'inproj': min(m, 1024), 'mlp': min(m, 1024), 'qkv': min(m, 512), 'outproj': min(m, 512)}


SEQS_PER_STEP = 8
SSD_CHUNKS_PER_STEP = 2


def _seqs_per_step(n_seq, n_chunks):
    return SEQS_PER_STEP if n_chunks == 1 and n_seq % SEQS_PER_STEP == 0 and n_seq > SEQS_PER_STEP else 1


def _per_sequence(body, nb, per_seq):
    if nb == 1:
        return body

    def kernel(*refs):
        def one(n, carry):
            body(*[r.at[pl.ds(n, 1)] if flag else r for r, flag in zip(refs, per_seq)])
            return carry

        lax.fori_loop(0, nb, one, 0)

    return kernel


def _cparams(sem):
    return pltpu.CompilerParams(dimension_semantics=sem, vmem_limit_bytes=VMEM_LIMIT)


def _rms(x, g):
    return x * lax.rsqrt(jnp.mean(x * x, axis=-1, keepdims=True) + EPS) * g


def _gated_postnorm(x, acc, g_ref, gt_ref):
    if gt_ref.shape[1] == 1:
        return x + _rms(acc, g_ref[...] * gt_ref[0])
    return x + gt_ref[0] * _rms(acc, g_ref[...])


ROW_CHUNK = 16
ROW_UNROLL = 4


def _for_row_chunks(n_rows, body):
    def step(i, carry):
        body(pl.multiple_of(i * ROW_CHUNK, ROW_CHUNK))
        return carry

    lax.fori_loop(0, n_rows // ROW_CHUNK, step, 0, unroll=ROW_UNROLL)


def _mod_rows(ref, r0):
    return ref[0] if ref.shape[1] == 1 else ref[0, pl.ds(r0, ROW_CHUNK), :]


def _prenorm_to(h_scr, x_ref, g_ref, sc_ref, sh_ref):
    one_seq = sc_ref.shape[1] == 1
    gain = g_ref[...] * (1.0 + sc_ref[0]) if one_seq else g_ref[...]

    def body(r0):
        x = x_ref[0, pl.ds(r0, ROW_CHUNK), :]
        h = _rms(x, gain) if one_seq else _rms(x, gain) * (1.0 + _mod_rows(sc_ref, r0))
        h_scr[pl.ds(r0, ROW_CHUNK), :] = (h + _mod_rows(sh_ref, r0)).astype(BF16)

    _for_row_chunks(x_ref.shape[1], body)


def _sigmoid(x):
    return 0.5 * (jnp.tanh(0.5 * x) + 1.0)


def _silu(x):
    return x * _sigmoid(x)


def _softplus(x):
    return jnp.maximum(x, 0.0) + jnp.log1p(jnp.exp(-jnp.abs(x)))


def _sqrt_neg_expm1(x):
    s = jnp.tanh(-0.5 * x)
    t = (2.0 * s) * (1.0 + s)
    return jnp.where(t > 0.0, (2.0 * s) * lax.rsqrt(t), 0.0)


def _split3(x):
    hi = x.astype(BF16)
    r = x - hi.astype(F32)
    mid = r.astype(BF16)
    lo = (r - mid.astype(F32)).astype(BF16)
    return [hi, mid, lo]


def _adaln_kernel(c_ref, w_ref, b_ref, o_ref):
    s = _silu(c_ref[...]).astype(BF16)
    o_ref[0] = jnp.dot(s, w_ref[0].astype(BF16), preferred_element_type=F32) + b_ref[0]


def _adaln(c_all, w_mod, b_mod):
    nb, d = c_all.shape
    depth, _, n = w_mod.shape
    bn = ADALN_COLS
    return pl.pallas_call(
        _adaln_kernel,
        grid=(depth, n // bn),
        in_specs=[pl.BlockSpec((nb, d), lambda l, j: (0, 0)),
                  pl.BlockSpec((1, d, bn), lambda l, j: (l, 0, j)),
                  pl.BlockSpec((1, 1, bn), lambda l, j: (l, 0, j))],
        out_specs=pl.BlockSpec((1, nb, bn), lambda l, j: (l, 0, j)),
        out_shape=jax.ShapeDtypeStruct((depth, nb, n), F32),
        compiler_params=_cparams(("parallel", "parallel")),
        name="adaln",
    )(c_all, w_mod, b_mod.reshape(depth, 1, n))


def _mod_spec(mod, bm, k):
    if mod.shape[1] == 1:
        return pl.BlockSpec((1, 1, D_MODEL), lambda g, i, *_: (g, 0, k))
    return pl.BlockSpec((1, bm, D_MODEL), lambda g, i, *_: (g, i, k))


def _inproj_kernel(x_ref, g_ref, sh_ref, sc_ref, wt_ref, wdt_ref, o_ref, dt_ref, h_scr, *, n_dt):
    nt = (((1,), (1,)), ((), ()))

    @pl.when(pl.program_id(2) == 0)
    def _():
        _prenorm_to(h_scr, x_ref, g_ref, sc_ref, sh_ref)
        dt = lax.dot_general(h_scr[...], wdt_ref[...].astype(BF16), nt, preferred_element_type=F32)
        lane = lax.broadcasted_iota(jnp.int32, dt.shape, 1)
        dt_ref[0] = jnp.where(lane < n_dt, dt, 0.0)

    o_ref[0] = lax.dot_general(h_scr[...], wt_ref[...].astype(BF16), nt, preferred_element_type=F32)


def _inproj(x, gain, mod, w_t, c_dt, n_dt, bm, bn):
    g_, m, d = x.shape
    n = w_t.shape[1] - n_dt
    row0 = lambda j: pl.multiple_of(jnp.where(j * bn < c_dt, j * bn, j * bn + n_dt), SUBLANES)
    return pl.pallas_call(
        functools.partial(_inproj_kernel, n_dt=n_dt),
        grid=(g_, m // bm, n // bn),
        in_specs=[pl.BlockSpec((1, bm, d), lambda g, i, j: (g, i, 0)),
                  pl.BlockSpec((1, d), lambda g, i, j: (0, 0)),
                  _mod_spec(mod, bm, 0), _mod_spec(mod, bm, 1),
                  pl.BlockSpec((None, pl.Element(bn), pl.Element(d)), lambda g, i, j: (0, row0(j), 0)),
                  pl.BlockSpec((None, pl.Element(LANES), pl.Element(d)), lambda g, i, j: (0, c_dt, 0))],
        out_specs=[pl.BlockSpec((1, bm, bn), lambda g, i, j: (g, i, j)),
                   pl.BlockSpec((1, bm, LANES), lambda g, i, j: (g, i, 0))],
        out_shape=[jax.ShapeDtypeStruct((g_, m, n), F32),
                   jax.ShapeDtypeStruct((g_, m, LANES), F32)],
        scratch_shapes=[pltpu.VMEM((bm, d), BF16)],
        compiler_params=_cparams(("parallel", "parallel", "arbitrary")),
        name="inproj",
    )(x, gain, mod, mod, w_t, w_t)


def _swap_halves(x):
    lane = lax.broadcasted_iota(jnp.int32, x.shape, 1)
    first = (lane % HEAD_DIM) < (HEAD_DIM // 2)
    return jnp.where(first, pltpu.roll(x, LANES - HEAD_DIM // 2, 1), pltpu.roll(x, HEAD_DIM // 2, 1))


def _qkv_kernel(x_ref, g_ref, sh_ref, sc_ref, w_ref, b_ref, cos_ref, sin_ref, o_ref, h_scr, *, rope_cols, bn):
    _prenorm_to(h_scr, x_ref, g_ref, sc_ref, sh_ref)
    h = h_scr[...]
    cos = cos_ref[...]
    sin = sin_ref[...]
    for c0 in range(0, w_ref.shape[1], bn):
        acc = jnp.dot(h, w_ref[:, c0:c0 + bn].astype(BF16), preferred_element_type=F32) + b_ref[:, c0:c0 + bn]
        if c0 < rope_cols:
            for c in range(c0, c0 + bn, LANES):
                a = acc[:, c - c0:c - c0 + LANES]
                o_ref[0, :, c:c + LANES] = a * cos + _swap_halves(a) * sin
        else:
            o_ref[0, :, c0:c0 + bn] = acc


def _qkv(x, gain, mod, w, b, cos_t, sin_t, bm, bn):
    g_, m, d = x.shape
    n = w.shape[1]
    rope_cols = (ATT_HEADS + KV_HEADS) * HEAD_DIM
    rows_per_g = cos_t.shape[0] // bm
    return pl.pallas_call(
        functools.partial(_qkv_kernel, rope_cols=rope_cols, bn=bn),
        grid=(g_, m // bm),
        in_specs=[pl.BlockSpec((1, bm, d), lambda g, i: (g, i, 0)),
                  pl.BlockSpec((1, d), lambda g, i: (0, 0)),
                  _mod_spec(mod, bm, 0), _mod_spec(mod, bm, 1),
                  pl.BlockSpec((d, n), lambda g, i: (0, 0), pipeline_mode=pl.Buffered(1)),
                  pl.BlockSpec((1, n), lambda g, i: (0, 0)),
                  pl.BlockSpec((bm, LANES), lambda g, i: (i % rows_per_g, 0)),
                  pl.BlockSpec((bm, LANES), lambda g, i: (i % rows_per_g, 0))],
        out_specs=pl.BlockSpec((1, bm, n), lambda g, i: (g, i, 0)),
        out_shape=jax.ShapeDtypeStruct((g_, m, n), F32),
        scratch_shapes=[pltpu.VMEM((bm, d), BF16)],
        compiler_params=_cparams(("parallel", "parallel")),
        name="qkv",
    )(x, gain, mod, mod, w, b, cos_t, sin_t)


def _outproj_kernel(*refs, n_lhs):
    lhs = refs[:n_lhs]
    ws = refs[n_lhs:2 * n_lhs]
    x_ref, g_ref, gt_ref, o_ref = refs[2 * n_lhs:]
    acc = jnp.dot(lhs[0][0], ws[0][...].astype(BF16), preferred_element_type=F32)
    for a, w in zip(lhs[1:], ws[1:]):
        acc = acc + jnp.dot(a[0], w[...].astype(BF16), preferred_element_type=F32)
    o_ref[0] = _gated_postnorm(x_ref[0], acc, g_ref, gt_ref)


def _outproj(lhs_list, w, x, gain, mod, bm):
    g_, m, d = x.shape
    n_lhs = len(lhs_list)
    kw = w.shape[0] // n_lhs
    lhs_specs = [pl.BlockSpec((1, bm, kw), lambda g, i: (g, i, 0)) for _ in lhs_list]
    w_specs = [pl.BlockSpec((kw, d), lambda g, i, k=k: (k, 0), pipeline_mode=pl.Buffered(1)) for k in range(n_lhs)]
    w_list = [w] * n_lhs
    if mod.shape[1] == 1:
        gt_spec = pl.BlockSpec((1, 1, d), lambda g, i: (g, 0, 2))
    else:
        gt_spec = pl.BlockSpec((1, bm, d), lambda g, i: (g, i, 2))
    return pl.pallas_call(
        functools.partial(_outproj_kernel, n_lhs=n_lhs),
        grid=(g_, m // bm),
        in_specs=lhs_specs + w_specs + [pl.BlockSpec((1, bm, d), lambda g, i: (g, i, 0)),
                                        pl.BlockSpec((1, d), lambda g, i: (0, 0)), gt_spec],
        out_specs=pl.BlockSpec((1, bm, d), lambda g, i: (g, i, 0)),
        out_shape=jax.ShapeDtypeStruct((g_, m, d), F32),
        compiler_params=_cparams(("parallel", "parallel")),
        name="outproj",
    )(*lhs_list, *w_list, x, gain, mod)


def _mlp_kernel(x_ref, g2_ref, sh_ref, sc_ref, gt_ref, wu_ref, wd_ref, g3_ref, o_ref, h_scr):
    f = pl.program_id(2)

    @pl.when(f == 0)
    def _():
        _prenorm_to(h_scr, x_ref, g2_ref, sc_ref, sh_ref)
        o_ref[0] = jnp.zeros(o_ref.shape[1:], F32)

    u = jnp.dot(h_scr[...], wu_ref[...].astype(BF16), preferred_element_type=F32)
    u = jnp.square(jnp.maximum(u, 0.0)).astype(BF16)
    o_ref[0] += jnp.dot(u, wd_ref[...].astype(BF16), preferred_element_type=F32)

    @pl.when(f == pl.num_programs(2) - 1)
    def _():
        o_ref[0] = _gated_postnorm(x_ref[0], o_ref[0], g3_ref, gt_ref)


def _mlp(x, g2, g3, mod, w_up, w_down, layer, bm, bf):
    g_, m, d = x.shape
    dff = w_up.shape[2]
    if mod.shape[1] == 1:
        mspec = lambda k: pl.BlockSpec((1, 1, d), lambda g, i, f: (g, 0, k))
    else:
        mspec = lambda k: pl.BlockSpec((1, bm, d), lambda g, i, f: (g, i, k))
    return pl.pallas_call(
        _mlp_kernel,
        grid=(g_, m // bm, dff // bf),
        in_specs=[pl.BlockSpec((1, bm, d), lambda g, i, f: (g, i, 0), pipeline_mode=pl.Buffered(1)),
                  pl.BlockSpec((1, d), lambda g, i, f: (0, 0)),
                  mspec(3), mspec(4), mspec(5),
                  pl.BlockSpec((None, d, bf), lambda g, i, f: (layer, 0, f)),
                  pl.BlockSpec((None, bf, d), lambda g, i, f: (layer, f, 0)),
                  pl.BlockSpec((1, d), lambda g, i, f: (0, 0))],
        out_specs=pl.BlockSpec((1, bm, d), lambda g, i, f: (g, i, 0)),
        out_shape=jax.ShapeDtypeStruct((g_, m, d), F32),
        scratch_shapes=[pltpu.VMEM((bm, d), BF16)],
        compiler_params=_cparams(("parallel", "parallel", "arbitrary")),
        name="mlp",
    )(x, g2, mod, mod, mod, w_up, w_down, g3)


def _dwconv(raw, tail, w, b):
    t = raw.shape[0]
    assert w.shape[0] == 4
    ext = jnp.concatenate([tail, raw], axis=0)
    z1 = pltpu.roll(ext, 1, 0)
    near = ext * w[3:4] + z1 * w[2:3]
    far = ext * w[1:2] + z1 * w[0:1]
    y = near + pltpu.roll(far, 2, 0)
    return y[TAIL_ROWS:] + b, ext[t:t + TAIL_ROWS]


def _cumsum_rows(x):
    n = x.shape[0]
    row = lax.broadcasted_iota(jnp.int32, x.shape, 0)
    d = 1
    while d < n:
        x = x + jnp.where(row >= d, pltpu.roll(x, d, 0), 0.0)
        d *= 2
    return x


def _ssd_kernel(z_ref, xs_ref, bc_ref, dt_ref, buf_ref, h0_ref, cw_ref, cb_ref, dtb_ref, alog_ref,
                dsk_ref, ng_ref, exp_ref, y_ref, st_ref, tail_scr, *, q_valid, q, starts_step):
    if starts_step:
        @pl.when(pl.program_id(1) == 0)
        def _():
            tail_scr[...] = buf_ref[0]
            st_ref[0] = h0_ref[0]

    def padded(v):
        if q_valid == q:
            return v
        return jnp.concatenate([v, jnp.zeros((q - q_valid, v.shape[1]), v.dtype)], axis=0)

    def transposed(v):
        if q == LANES:
            return v.T
        return jnp.concatenate([v, jnp.zeros((LANES - q, LANES), v.dtype)], axis=0).T[:, :q]

    cw = cw_ref[...]
    cb = cb_ref[...]
    xs, tail_x = _dwconv(xs_ref[0], tail_scr[:, :SSM_INNER], cw[:, :SSM_INNER], cb[:, :SSM_INNER])
    bcm, tail_b = _dwconv(bc_ref[0], tail_scr[:, SSM_INNER:], cw[:, SSM_INNER:], cb[:, SSM_INNER:])
    tail_scr[:, :SSM_INNER] = tail_x
    tail_scr[:, SSM_INNER:] = tail_b
    xs = padded(_silu(xs))
    bcm = padded(_silu(bcm))
    z = padded(z_ref[0])

    dt = _softplus(dt_ref[0] + dtb_ref[...])
    dt = padded(dt)
    a = -jnp.exp(alog_ref[...])
    acs = _cumsum_rows(dt * a)
    acs_t = transposed(acs)

    full = jnp.dot(jnp.concatenate(_split3(dt) + _split3(acs), axis=0), exp_ref[...], preferred_element_type=F32)
    dt_full = (full[0:q] + full[q:2 * q]) + full[2 * q:3 * q]
    acs_full = (full[3 * q:4 * q] + full[4 * q:5 * q]) + full[5 * q:6 * q]
    xdt = xs * dt_full
    dec_out = jnp.exp(acs_full)
    xw = (xdt * jnp.exp(acs_full[q - 1:q] - acs_full)).astype(BF16)
    xdt_b = xdt.astype(BF16)
    bc_b = bcm.astype(BF16)

    row = lax.broadcasted_iota(jnp.int32, (q, q), 0)
    col = lax.broadcasted_iota(jnp.int32, (q, q), 1)
    causal = row >= col
    lane = lax.broadcasted_iota(jnp.int32, (q, LANES), 1)
    lo_half = lane < SSM_HEAD_DIM
    nt = (((1,), (1,)), ((), ()))
    tn = (((0,), (0,)), ((), ()))
    n_bc = SSM_GROUPS * SSM_STATE

    groups = range(SSM_GROUPS)
    cols = [slice(g * GROUP_WIDTH, (g + 1) * GROUP_WIDTH) for g in groups]
    b_gs = [bc_b[:, g * SSM_STATE:(g + 1) * SSM_STATE] for g in groups]
    c_gs = [bc_b[:, n_bc + g * SSM_STATE:n_bc + (g + 1) * SSM_STATE] for g in groups]
    h_gs = [st_ref[0, cols[g], :] for g in groups]
    cbms = [lax.dot_general(c_gs[g], b_gs[g], nt, preferred_element_type=F32) for g in groups]
    y_offs = [lax.dot_general(c_gs[g], h_gs[g].astype(BF16), nt, preferred_element_type=F32) for g in groups]
    st_news = [lax.dot_general(xw[:, cols[g]], b_gs[g], tn, preferred_element_type=F32) for g in groups]
    ms = [(cbms[h // HEADS_PER_GROUP]
           * jnp.exp(jnp.where(causal, acs[:, h:h + 1] - acs_t[h:h + 1, :], -jnp.inf))).astype(BF16)
          for h in range(SSM_HEADS)]
    y_diags = []
    for pair in range(SSM_HEADS // 2):
        x_pair = xdt_b[:, pair * LANES:(pair + 1) * LANES]
        zero = jnp.zeros_like(x_pair)
        y_diags.append(jnp.dot(ms[2 * pair], jnp.where(lo_half, x_pair, zero), preferred_element_type=F32)
                       + jnp.dot(ms[2 * pair + 1], jnp.where(lo_half, zero, x_pair), preferred_element_type=F32))
    pairs_per_group = HEADS_PER_GROUP // 2
    for g in groups:
        y_g = y_offs[g] * dec_out[:, cols[g]]
        y_g = y_g + jnp.concatenate(y_diags[g * pairs_per_group:(g + 1) * pairs_per_group], axis=1)
        y_g = y_g + xs[:, cols[g]] * dsk_ref[:, cols[g]]
        y_g = y_g * _silu(z[:, cols[g]])
        y_g = y_g * lax.rsqrt(jnp.mean(y_g * y_g, axis=-1, keepdims=True) + EPS)
        y_g = y_g * ng_ref[:, cols[g]]
        y_ref[0, :, cols[g]] = y_g[:q_valid].astype(y_ref.dtype)
    for h in range(SSM_HEADS):
        g, r = divmod(h, HEADS_PER_GROUP)
        rows = slice(r * SSM_HEAD_DIM, (r + 1) * SSM_HEAD_DIM)
        decay = jnp.exp(acs_t[h:h + 1, q - 1:q])
        st_ref[0, h * SSM_HEAD_DIM:(h + 1) * SSM_HEAD_DIM, :] = h_gs[g][rows] * decay + st_news[g][rows]


def _ssd(proj, dtp, buf8, h0, conv_w, conv_b, dt_bias, a_log, d_full, norm_g, expand, rows):
    b_, l, _ = proj.shape
    chunk = rows
    n_sub = min(SSD_CHUNKS_PER_STEP, l // chunk)
    rows = n_sub * chunk
    nc = l // rows
    nb = _seqs_per_step(b_, nc)
    st_rows = SSM_INNER

    def kernel_fn(z, x, bc, dt, *rest):
        *fixed, y, st, tail = rest
        for s in range(n_sub):
            sub = lambda ref: ref.at[:, pl.ds(s * chunk, chunk)]
            _ssd_kernel(sub(z), sub(x), sub(bc), sub(dt), *fixed, sub(y), st, tail,
                        q_valid=chunk, q=max(chunk, BF16_SUBLANES), starts_step=s == 0)

    return pl.pallas_call(
        _per_sequence(kernel_fn, nb, [True] * 6 + [False] * 7 + [True] * 2 + [False]),
        grid=(b_ // nb, nc),
        in_specs=[pl.BlockSpec((nb, rows, SSM_INNER), lambda b, c: (b, c, 0)),
                  pl.BlockSpec((nb, rows, SSM_INNER), lambda b, c: (b, c, 1)),
                  pl.BlockSpec((nb, rows, SSM_INNER), lambda b, c: (b, c, 2)),
                  pl.BlockSpec((nb, rows, LANES), lambda b, c: (b, c, 0)),
                  pl.BlockSpec((nb, TAIL_ROWS, SSM_CONV_DIM), lambda b, c: (b, 0, 0)),
                  pl.BlockSpec((nb, st_rows, SSM_STATE), lambda b, c: (b, 0, 0)),
                  pl.BlockSpec((SSM_CONV, SSM_CONV_DIM), lambda b, c: (0, 0)),
                  pl.BlockSpec((1, SSM_CONV_DIM), lambda b, c: (0, 0)),
                  pl.BlockSpec((1, LANES), lambda b, c: (0, 0)),
                  pl.BlockSpec((1, LANES), lambda b, c: (0, 0)),
                  pl.BlockSpec((1, SSM_INNER), lambda b, c: (0, 0)),
                  pl.BlockSpec((1, SSM_INNER), lambda b, c: (0, 0)),
                  pl.BlockSpec((LANES, SSM_INNER), lambda b, c: (0, 0))],
        out_specs=[pl.BlockSpec((nb, rows, SSM_INNER), lambda b, c: (b, c, 0)),
                   pl.BlockSpec((nb, st_rows, SSM_STATE), lambda b, c: (b, 0, 0))],
        out_shape=[jax.ShapeDtypeStruct((b_, l, SSM_INNER), BF16),
                   jax.ShapeDtypeStruct((b_, st_rows, SSM_STATE), F32)],
        scratch_shapes=[pltpu.VMEM((TAIL_ROWS, SSM_CONV_DIM), F32)],
        compiler_params=_cparams(("parallel", "arbitrary")),
        name="ssd",
    )(proj, proj, proj, dtp, buf8, h0, conv_w, conv_b, dt_bias, a_log, d_full, norm_g, expand)


SCAN_SEG = 4
SCAN_ROWS = SUBLANES * SCAN_SEG


def _shift_rows_fill(x, d, fill, row):
    return jnp.where(row >= d, pltpu.roll(x, d, 0), fill)


def _sublane_scan(e, pe, row):
    d = 1
    while d < SUBLANES:
        e = e + pe * _shift_rows_fill(e, d, 0.0, row)
        pe = pe * _shift_rows_fill(pe, d, 1.0, row)
        d *= 2
    return e, pe


def _linear_scan(a_scr, u_scr, h0):
    n_slab, t, _ = a_scr.shape
    seg = SCAN_SEG if t % SCAN_ROWS == 0 else 1
    row = lax.broadcasted_iota(jnp.int32, (SUBLANES, LANES), 0)

    def block(i, h_prev):
        base = pl.multiple_of(i * (SUBLANES * seg), SUBLANES)
        out = []
        for c in range(n_slab):
            rows = [pl.ds(base + j, SUBLANES, stride=seg) if seg > 1 else pl.ds(base, SUBLANES) for j in range(seg)]
            hs, ps = [], []
            for j in range(seg):
                a, u = a_scr[c, rows[j], :], u_scr[c, rows[j], :]
                hs.append(u if j == 0 else a * hs[-1] + u)
                ps.append(a if j == 0 else a * ps[-1])
            e, pe = _sublane_scan(hs[-1], ps[-1], row)
            g = e + pe * h_prev[c]
            carry = jnp.where(row == 0, h_prev[c], pltpu.roll(g, 1, 0))
            for j in range(seg):
                u_scr[c, rows[j], :] = hs[j] + ps[j] * carry
            out.append(jnp.broadcast_to(g[SUBLANES - 1:], g.shape))
        return tuple(out)

    h_init = tuple(jnp.broadcast_to(h0[:, c * LANES:(c + 1) * LANES], (SUBLANES, LANES)) for c in range(n_slab))
    h_fin = lax.fori_loop(0, t // (SUBLANES * seg), block, h_init)
    return jnp.concatenate([h[:1] for h in h_fin], axis=1)


def _gelu_tanh(x):
    c0 = math.sqrt(2.0 / math.pi)
    hx = 0.5 * x
    return hx + hx * jnp.tanh(x * (c0 + (c0 * 0.044715) * (x * x)))


def _lru_kernel(gate_ref, xr_ref, buf_ref, h0_ref, cw_ref, cb_ref, wa_ref, ba_ref, wx_ref, bx_ref,
                lam_ref, y_ref, hl_ref, tail_scr, a_scr, u_scr):
    c = pl.program_id(1)

    @pl.when(c == 0)
    def _():
        tail_scr[...] = buf_ref[0]
        hl_ref[0] = h0_ref[0]

    xc, tail = _dwconv(xr_ref[0], tail_scr[...], cw_ref[...], cb_ref[...])
    tail_scr[...] = tail
    sp = _softplus(-lam_ref[...])
    slabs_per_block = LRU_BLOCK // LANES
    for k in range(LRU_BLOCKS):
        k0 = k * LRU_BLOCK
        xk = xc[:, k0:k0 + LRU_BLOCK]
        xkb = xk.astype(BF16)
        gr = _sigmoid(jnp.dot(xkb, wa_ref[k], preferred_element_type=F32) + ba_ref[:, k0:k0 + LRU_BLOCK])
        gi = _sigmoid(jnp.dot(xkb, wx_ref[k], preferred_element_type=F32) + bx_ref[:, k0:k0 + LRU_BLOCK])
        log_a = -LRU_C * gr * sp[:, k0:k0 + LRU_BLOCK]
        a = jnp.exp(log_a)
        u = _sqrt_neg_expm1(2.0 * log_a) * gi * xk
        for s in range(slabs_per_block):
            a_scr[k * slabs_per_block + s] = a[:, s * LANES:(s + 1) * LANES]
            u_scr[k * slabs_per_block + s] = u[:, s * LANES:(s + 1) * LANES]

    hl_ref[0] = _linear_scan(a_scr, u_scr, hl_ref[0])
    for s in range(a_scr.shape[0]):
        gate = gate_ref[0, :, s * LANES:(s + 1) * LANES]
        y_ref[0, :, s * LANES:(s + 1) * LANES] = (u_scr[s] * _gelu_tanh(gate)).astype(y_ref.dtype)


def _lru(proj, buf8, h0, conv_w, conv_b, w_a, b_a, w_x, b_x, lam, rows, col0):
    b_, l, _ = proj.shape
    w = LRU_WIDTH
    nb = _seqs_per_step(b_, l // rows)
    full2 = lambda shape: pl.BlockSpec(shape, lambda b, c: (0,) * len(shape))
    return pl.pallas_call(
        _per_sequence(_lru_kernel, nb, [True] * 4 + [False] * 7 + [True] * 2 + [False] * 3),
        grid=(b_ // nb, l // rows),
        in_specs=[pl.BlockSpec((nb, rows, w), lambda b, c: (b, c, col0)),
                  pl.BlockSpec((nb, rows, w), lambda b, c: (b, c, col0 + 1)),
                  pl.BlockSpec((nb, TAIL_ROWS, w), lambda b, c: (b, 0, 0)),
                  pl.BlockSpec((nb, 1, w), lambda b, c: (b, 0, 0)),
                  full2((SSM_CONV, w)), full2((1, w)),
                  full2((LRU_BLOCKS, LRU_BLOCK, LRU_BLOCK)), full2((1, w)),
                  full2((LRU_BLOCKS, LRU_BLOCK, LRU_BLOCK)), full2((1, w)),
                  full2((1, w))],
        out_specs=[pl.BlockSpec((nb, rows, w), lambda b, c: (b, c, 0)),
                   pl.BlockSpec((nb, 1, w), lambda b, c: (b, 0, 0))],
        out_shape=[jax.ShapeDtypeStruct((b_, l, w), BF16),
                   jax.ShapeDtypeStruct((b_, 1, w), F32)],
        scratch_shapes=[pltpu.VMEM((TAIL_ROWS, w), F32), pltpu.VMEM((w // LANES, rows, LANES), F32),
                        pltpu.VMEM((w // LANES, rows, LANES), F32)],
        compiler_params=_cparams(("parallel", "arbitrary")),
        name="lru",
    )(proj, proj, buf8, h0, conv_w, conv_b, w_a, b_a, w_x, b_x, lam)


def _head_variants(slab, half):
    lane = lax.broadcasted_iota(jnp.int32, slab.shape, 1)
    keep = (lane < HEAD_DIM) if half == 0 else (lane >= HEAD_DIM)
    own = jnp.where(keep, slab, 0.0)
    other = pltpu.roll(own, HEAD_DIM, 1)
    return (own, other) if half == 0 else (other, own)


def _slab_attention(q_ref, o_ref, k2, v2, valid, sink_ref):
    w = q_ref.shape[1]
    nk = k2.shape[0]
    nt = (((1,), (1,)), ((), ()))
    scale = HEAD_DIM ** -0.5
    ones = jnp.ones((nk, LANES), BF16)
    slabs = range(nk // LANES)
    heads = range(KV_HEADS)
    halves = [(e, c) for e in heads for c in range(2)]
    vs, ss = [], []
    for e in heads:
        slab, half = divmod(e, 2)
        k_lo, k_hi = _head_variants(k2[:, slab * LANES:(slab + 1) * LANES], half)
        vs.extend(v.astype(BF16) for v in _head_variants(v2[:, slab * LANES:(slab + 1) * LANES], half))
        kk = jnp.concatenate([k_lo, k_hi], axis=0).astype(BF16)
        c0 = 2 * e * LANES
        qq = jnp.concatenate([q_ref[0, :, c0:c0 + LANES], q_ref[0, :, c0 + LANES:c0 + 2 * LANES]], axis=0)
        ss.append(lax.dot_general((qq * scale).astype(BF16), kk, nt, preferred_element_type=F32))
    scs = [jnp.where(valid, ss[e][:, c * nk:(c + 1) * nk], -jnp.inf) for e, c in halves]
    sinks = [jnp.concatenate([jnp.full((w, LANES), sink_ref[Q_PER_KV * e + c], F32),
                              jnp.full((w, LANES), sink_ref[Q_PER_KV * e + 2 + c], F32)], axis=0) for e, c in halves]
    ms = [jnp.maximum(jnp.broadcast_to(jnp.max(sc, axis=1, keepdims=True), sk.shape), sk)
          for sc, sk in zip(scs, sinks)]
    ps = [jnp.concatenate([jnp.exp(sc[:, i * LANES:(i + 1) * LANES] - m) for i in slabs], axis=1).astype(BF16)
          for sc, m in zip(scs, ms)]
    dens = [jnp.dot(p, ones, preferred_element_type=F32) + jnp.exp(sk - m) for p, sk, m in zip(ps, sinks, ms)]
    outs = [jnp.dot(p, v, preferred_element_type=F32) * (1.0 / den) for p, v, den in zip(ps, vs, dens)]
    for e in heads:
        o = outs[2 * e] + outs[2 * e + 1]
        c0 = 2 * e * LANES
        o_ref[0, :, c0:c0 + LANES] = o[:w].astype(o_ref.dtype)
        o_ref[0, :, c0 + LANES:c0 + 2 * LANES] = o[w:].astype(o_ref.dtype)


ATTN_BLOCKS = 4


def _attn_prompt_kernel(sink_ref, q_ref, kc_ref, kp_ref, vc_ref, vp_ref, o_ref):
    w = WINDOW
    has_prev = pl.program_id(1) > 0
    k_prev = jnp.where(has_prev, kp_ref[0], 0.0)
    v_prev = jnp.where(has_prev, vp_ref[0], 0.0)
    t = lax.broadcasted_iota(jnp.int32, (2 * w, 2 * w), 0) % w
    j = lax.broadcasted_iota(jnp.int32, (2 * w, 2 * w), 1)
    valid = (j > t) & (j <= t + w)
    for s in range(q_ref.shape[1] // w):
        rows = pl.ds(s * w, w)
        k_cur, v_cur = kc_ref[0, rows, :], vc_ref[0, rows, :]
        _slab_attention(q_ref.at[:, rows], o_ref.at[:, rows], jnp.concatenate([k_prev, k_cur], axis=0),
                        jnp.concatenate([v_prev, v_cur], axis=0), valid, sink_ref)
        k_prev, v_prev = k_cur, v_cur


def _attn_prompt(qkv, sinks):
    b_, l, _ = qkv.shape
    w = WINDOW
    rows = ATTN_BLOCKS * w
    qw = ATT_HEADS * HEAD_DIM
    kvw = KV_HEADS * HEAD_DIM
    k_blk = qw // kvw
    prev = lambda n: jnp.maximum(ATTN_BLOCKS * n - 1, 0)
    return pl.pallas_call(
        _attn_prompt_kernel,
        grid=(b_, l // rows),
        in_specs=[pl.BlockSpec(memory_space=pltpu.SMEM),
                  pl.BlockSpec((1, rows, qw), lambda b, n: (b, n, 0)),
                  pl.BlockSpec((1, rows, kvw), lambda b, n: (b, n, k_blk)),
                  pl.BlockSpec((1, w, kvw), lambda b, n: (b, prev(n), k_blk)),
                  pl.BlockSpec((1, rows, kvw), lambda b, n: (b, n, k_blk + 1)),
                  pl.BlockSpec((1, w, kvw), lambda b, n: (b, prev(n), k_blk + 1))],
        out_specs=pl.BlockSpec((1, rows, qw), lambda b, n: (b, n, 0)),
        out_shape=jax.ShapeDtypeStruct((b_, l, qw), BF16),
        compiler_params=_cparams(("parallel", "parallel")),
        name="attn_prompt",
    )(sinks, qkv, qkv, qkv, qkv, qkv)


def _attn_sample_kernel(sink_ref, qkv_ref, ck_ref, cv_ref, o_ref, nk_ref, nv_ref):
    l = qkv_ref.shape[1]
    n_keep = ck_ref.shape[1]
    kcol = ATT_HEADS * HEAD_DIM
    vcol = kcol + KV_HEADS * HEAD_DIM
    qkv = qkv_ref[0]
    k_all = jnp.concatenate([ck_ref[0], qkv[:, kcol:vcol]], axis=0)
    v_all = jnp.concatenate([cv_ref[0], qkv[:, vcol:]], axis=0)
    nk_ref[0] = k_all[l:]
    nv_ref[0] = v_all[l:]
    nk = -(-(n_keep + l) // LANES) * LANES
    pad = jnp.zeros((nk - n_keep - l, k_all.shape[1]), F32)
    t = lax.broadcasted_iota(jnp.int32, (2 * l, nk), 0) % l
    j = lax.broadcasted_iota(jnp.int32, (2 * l, nk), 1)
    rel = jnp.where(j < n_keep, t + n_keep - j, t - (j - n_keep))
    valid = (rel >= 0) & (rel < WINDOW) & (j < n_keep + l)
    _slab_attention(qkv_ref, o_ref, jnp.concatenate([k_all, pad], axis=0), jnp.concatenate([v_all, pad], axis=0),
                    valid, sink_ref)


def _attn_sample(qkv, cache_k, cache_v, sinks):
    b_, l, n = qkv.shape
    n_keep, kvw = cache_k.shape[1:]
    nb = _seqs_per_step(b_, 1)
    return pl.pallas_call(
        _per_sequence(_attn_sample_kernel, nb, [False] + [True] * 6),
        grid=(b_ // nb,),
        in_specs=[pl.BlockSpec(memory_space=pltpu.SMEM),
                  pl.BlockSpec((nb, l, n), lambda b: (b, 0, 0)),
                  pl.BlockSpec((nb, n_keep, kvw), lambda b: (b, 0, 0)),
                  pl.BlockSpec((nb, n_keep, kvw), lambda b: (b, 0, 0))],
        out_specs=[pl.BlockSpec((nb, l, ATT_HEADS * HEAD_DIM), lambda b: (b, 0, 0)),
                   pl.BlockSpec((nb, n_keep, kvw), lambda b: (b, 0, 0)),
                   pl.BlockSpec((nb, n_keep, kvw), lambda b: (b, 0, 0))],
        out_shape=[jax.ShapeDtypeStruct((b_, l, ATT_HEADS * HEAD_DIM), BF16),
                   jax.ShapeDtypeStruct((b_, n_keep, kvw), F32),
                   jax.ShapeDtypeStruct((b_, n_keep, kvw), F32)],
        compiler_params=_cparams(("parallel",)),
        name="attn_sample",
    )(sinks, qkv, cache_k, cache_v)


def _rope_tables(pos):
    half = HEAD_DIM // 2
    inv_freq = ROPE_THETA ** (-jnp.arange(half, dtype=F32) / half)
    ang = pos.astype(F32)[:, None] * inv_freq[None, :]
    cos = jnp.cos(ang)
    sin = jnp.sin(ang)
    reps = LANES // HEAD_DIM
    return (jnp.tile(jnp.concatenate([cos, cos], axis=1), (1, reps)),
            jnp.tile(jnp.concatenate([-sin, sin], axis=1), (1, reps)))


def _tail_pad(buf):
    return jnp.pad(buf, ((0, 0), (TAIL_ROWS - buf.shape[1], 0), (0, 0)))


def _trunk(x, mods, pos, st, pw, seq_len):
    g_, m, d = x.shape
    nseq = g_ * m // seq_len
    blk = _block_rows(m)
    chunk = min(seq_len, SSM_CHUNK)
    lru_rows = min(seq_len, LRU_ROWS)

    proj, dtp = _inproj(x, pw['norms'][0, 0:1], mods[0], pw['w_in_t'], SSM_INNER + SSM_CONV_DIM, SSM_HEADS,
                        blk['inproj'], INPROJ_COLS)
    n_main = proj.shape[-1]
    proj_s = proj.reshape(nseq, seq_len, n_main)
    dtp_s = dtp.reshape(nseq, seq_len, LANES)
    y_ssm, ssm_h = _ssd(proj_s, dtp_s, _tail_pad(st['ssm_conv']), st['ssm'].reshape(nseq, SSM_INNER, SSM_STATE),
                        pw['ssm_conv_w'], pw['ssm_conv_b'], pw['ssm_dt_bias'], pw['ssm_a_log'], pw['ssm_d_full'],
                        pw['ssm_norm'], pw['expand'], chunk)
    y_lru, lru_h = _lru(proj_s, _tail_pad(st['lru_conv']), st['lru'].reshape(nseq, 1, LRU_WIDTH),
                        pw['lru_conv_w'], pw['lru_conv_b'], pw['lru_w_a'], pw['lru_b_a'], pw['lru_w_x'],
                        pw['lru_b_x'], pw['lru_lambda'], lru_rows, 3)
    keep = SSM_CONV - 1
    new_ssm_conv = proj_s[:, seq_len - keep:, SSM_INNER:SSM_INNER + SSM_CONV_DIM]
    new_lru_conv = proj_s[:, seq_len - keep:, n_main - LRU_WIDTH:]
    x = _outproj([y_ssm.reshape(g_, m, SSM_INNER), y_lru.reshape(g_, m, LRU_WIDTH)],
                 pw['w_out_hyb'], x, pw['norms'][0, 1:2], mods[0], blk['outproj'])
    x = _mlp(x, pw['norms'][0, 2:3], pw['norms'][0, 3:4], mods[0], pw['w_up'], pw['w_down'], 0, blk['mlp'],
             MLP_COLS)

    cos_t, sin_t = _rope_tables(pos)
    if seq_len < blk['qkv']:
        cos_t = jnp.tile(cos_t, (blk['qkv'] // seq_len, 1))
        sin_t = jnp.tile(sin_t, (blk['qkv'] // seq_len, 1))
    qkv = _qkv(x, pw['norms'][1, 0:1], mods[1], pw['w_qkv'], pw['b_qkv'], cos_t, sin_t, blk['qkv'], QKV_COLS)
    qkv_s = qkv.reshape(nseq, seq_len, qkv.shape[-1])
    kcol = ATT_HEADS * HEAD_DIM
    vcol = kcol + KV_HEADS * HEAD_DIM
    if st['k'] is None:
        o = _attn_prompt(qkv_s, pw['attn_sinks'])
        n_keep = min(WINDOW, seq_len)
        k_new = qkv_s[:, seq_len - n_keep:, kcol:vcol]
        v_new = qkv_s[:, seq_len - n_keep:, vcol:]
    else:
        n_keep = st['k'].shape[1]
        o, k_new, v_new = _attn_sample(qkv_s, st['k'].reshape(nseq, n_keep, KV_HEADS * HEAD_DIM),
                                       st['v'].reshape(nseq, n_keep, KV_HEADS * HEAD_DIM), pw['attn_sinks'])
    k_new = k_new.reshape(nseq, n_keep, KV_HEADS, HEAD_DIM)
    v_new = v_new.reshape(nseq, n_keep, KV_HEADS, HEAD_DIM)
    x = _outproj([o.reshape(g_, m, kcol)], pw['w_out_attn'], x, pw['norms'][1, 1:2], mods[1], blk['outproj'])
    x = _mlp(x, pw['norms'][1, 2:3], pw['norms'][1, 3:4], mods[1], pw['w_up'], pw['w_down'], 1, blk['mlp'],
             MLP_COLS)

    states = (new_ssm_conv[None], ssm_h.reshape(1, nseq, SSM_HEADS, SSM_HEAD_DIM, SSM_STATE),
              new_lru_conv[None], lru_h.reshape(1, nseq, LRU_WIDTH), k_new[None], v_new[None])
    return x, states


def kernel(x_prompt, x_sample, state_ssm_conv, state_ssm, state_lru_conv, state_lru, cache_k, cache_v, c_prompt, c_sample, w_mod, b_mod, norms, w_in_hyb, ssm_conv_w, ssm_conv_b, ssm_dt_bias, ssm_a_log, ssm_d, ssm_norm, lru_conv_w, lru_conv_b, lru_w_a, lru_b_a, lru_w_x, lru_b_x, lru_lambda, w_out_hyb, w_qkv, b_qkv, attn_sinks, w_out_attn, w_up, w_down):
    bp, lp, d = x_prompt.shape
    bs, ls, _ = x_sample.shape
    depth = w_mod.shape[0]

    pad_lanes = lambda v: jnp.pad(v.reshape(1, -1), ((0, 0), (0, LANES - v.size)))
    head_of_lane = jnp.arange(SSM_INNER) // SSM_HEAD_DIM
    pw = {
        'norms': norms,
        'w_in_t': jnp.swapaxes(w_in_hyb, 1, 2),
        'ssm_conv_w': ssm_conv_w[0], 'ssm_conv_b': ssm_conv_b[0].reshape(1, -1),
        'ssm_dt_bias': pad_lanes(ssm_dt_bias[0]), 'ssm_a_log': pad_lanes(ssm_a_log[0]),
        'ssm_d_full': jnp.repeat(ssm_d[0], SSM_HEAD_DIM).reshape(1, -1),
        'ssm_norm': ssm_norm[0].reshape(1, -1),
        'expand': (jnp.arange(LANES)[:, None] == head_of_lane[None, :]).astype(BF16),
        'lru_conv_w': lru_conv_w[0], 'lru_conv_b': lru_conv_b[0].reshape(1, -1),
        'lru_w_a': lru_w_a[0].astype(BF16), 'lru_b_a': lru_b_a[0].reshape(1, -1),
        'lru_w_x': lru_w_x[0].astype(BF16), 'lru_b_x': lru_b_x[0].reshape(1, -1),
        'lru_lambda': lru_lambda[0].reshape(1, -1),
        'w_out_hyb': w_out_hyb[0].astype(BF16),
        'w_qkv': w_qkv[0], 'b_qkv': b_qkv[0].reshape(1, -1),
        'attn_sinks': attn_sinks[0],
        'w_out_attn': w_out_attn[0],
        'w_up': w_up, 'w_down': w_down,
    }

    mod = _adaln(jnp.concatenate([c_prompt, c_sample], axis=0), w_mod, b_mod)
    mods_p = [mod[l, :bp][:, None, :] for l in range(depth)]
    mods_s = [jnp.repeat(mod[l, bp:], ls, axis=0)[None] for l in range(depth)]

    zeros = lambda *shape: jnp.zeros(shape, F32)
    st_p = {'ssm_conv': zeros(bp, SSM_CONV - 1, SSM_CONV_DIM), 'ssm': zeros(bp, SSM_INNER, SSM_STATE),
            'lru_conv': zeros(bp, SSM_CONV - 1, LRU_WIDTH), 'lru': zeros(bp, LRU_WIDTH), 'k': None, 'v': None}
    st_s = {'ssm_conv': state_ssm_conv[0], 'ssm': state_ssm[0], 'lru_conv': state_lru_conv[0],
            'lru': state_lru[0], 'k': cache_k[0], 'v': cache_v[0]}

    pos_p = jnp.arange(lp, dtype=jnp.int32)
    pos_s = PAST_LEN + jnp.arange(ls, dtype=jnp.int32)
    y_p, sp = _trunk(x_prompt, mods_p, pos_p, st_p, pw, lp)
    y_s, ss = _trunk(x_sample.reshape(1, bs * ls, d), mods_s, pos_s, st_s, pw, ls)
    return (y_p, y_s.reshape(bs, ls, d)) + sp + ss
```

```python
import functools
import math

import jax
import jax.numpy as jnp
from jax import lax
from jax.experimental import pallas as pl
from jax.experimental.pallas import tpu as pltpu

F32 = jnp.float32
BF16 = jnp.bfloat16

D_MODEL = 2048
PAST_LEN = 16384
SSM_HEADS = 32
SSM_HEAD_DIM = 64
SSM_INNER = SSM_HEADS * SSM_HEAD_DIM
SSM_GROUPS = 8
SSM_STATE = 128
SSM_CONV = 4
SSM_CHUNK = 128
SSM_CONV_DIM = SSM_INNER + 2 * SSM_GROUPS * SSM_STATE
HEADS_PER_GROUP = SSM_HEADS // SSM_GROUPS
GROUP_WIDTH = HEADS_PER_GROUP * SSM_HEAD_DIM
LRU_WIDTH = D_MODEL
LRU_BLOCKS = 8
LRU_BLOCK = LRU_WIDTH // LRU_BLOCKS
LRU_C = 8.0
ATT_HEADS = 32
KV_HEADS = 8
HEAD_DIM = 64
Q_PER_KV = ATT_HEADS // KV_HEADS
WINDOW = 128
ROPE_THETA = 10000.0
D_FF = 4 * D_MODEL
EPS = 1e-6

LANES = 128
SUBLANES = 8
BF16_SUBLANES = 16
TAIL_ROWS = SUBLANES
VMEM_LIMIT = 56 * 1024 * 1024


ADALN_COLS = 1024
INPROJ_COLS = 1024
MLP_COLS = 512
QKV_COLS = 512
LRU_ROWS = 512


def _block_rows(m):
    return {'inproj': min(m, 1024), 'mlp': min(m, 1024), 'qkv': min(m, 512), 'outproj': min(m, 512)}


SEQS_PER_STEP = 4
SSD_CHUNKS_PER_STEP = 2


def _seqs_per_step(n_seq, n_chunks):
    return SEQS_PER_STEP if n_chunks == 1 and n_seq % SEQS_PER_STEP == 0 and n_seq > SEQS_PER_STEP else 1


def _per_sequence(body, nb, per_seq):
    if nb == 1:
        return body

    def kernel(*refs):
        def one(n, carry):
            body(*[r.at[pl.ds(n, 1)] if flag else r for r, flag in zip(refs, per_seq)])
            return carry

        lax.fori_loop(0, nb, one, 0)

    return kernel


def _cparams(sem):
    return pltpu.CompilerParams(dimension_semantics=sem, vmem_limit_bytes=VMEM_LIMIT)


def _rms(x, g):
    return x * lax.rsqrt(jnp.mean(x * x, axis=-1, keepdims=True) + EPS) * g


def _gated_postnorm(x, acc, g_ref, gt_ref):
    if gt_ref.shape[1] == 1:
        return x + _rms(acc, g_ref[...] * gt_ref[0])
    return x + gt_ref[0] * _rms(acc, g_ref[...])


ROW_CHUNK = 16
ROW_UNROLL = 8


def _for_row_chunks(n_rows, body):
    def step(i, carry):
        body(pl.multiple_of(i * ROW_CHUNK, ROW_CHUNK))
        return carry

    lax.fori_loop(0, n_rows // ROW_CHUNK, step, 0, unroll=ROW_UNROLL)


def _mod_rows(ref, r0):
    return ref[0] if ref.shape[1] == 1 else ref[0, pl.ds(r0, ROW_CHUNK), :]


def _prenorm_to(h_scr, x_ref, g_ref, sc_ref, sh_ref):
    one_seq = sc_ref.shape[1] == 1
    gain = g_ref[...] * (1.0 + sc_ref[0]) if one_seq else g_ref[...]

    def body(r0):
        x = x_ref[0, pl.ds(r0, ROW_CHUNK), :]
        h = _rms(x, gain) if one_seq else _rms(x, gain) * (1.0 + _mod_rows(sc_ref, r0))
        h_scr[pl.ds(r0, ROW_CHUNK), :] = (h + _mod_rows(sh_ref, r0)).astype(BF16)

    _for_row_chunks(x_ref.shape[1], body)


def _sigmoid(x):
    return 0.5 * (jnp.tanh(0.5 * x) + 1.0)


def _silu(x):
    return x * _sigmoid(x)


def _softplus(x):
    return jnp.maximum(x, 0.0) + jnp.log1p(jnp.exp(-jnp.abs(x)))


def _sqrt_neg_expm1(x):
    s = jnp.tanh(-0.5 * x)
    t = (2.0 * s) * (1.0 + s)
    return jnp.where(t > 0.0, (2.0 * s) * lax.rsqrt(t), 0.0)


def _split3(x):
    hi = x.astype(BF16)
    r = x - hi.astype(F32)
    mid = r.astype(BF16)
    lo = (r - mid.astype(F32)).astype(BF16)
    return [hi, mid, lo]


def _adaln_kernel(c_ref, w_ref, b_ref, o_ref):
    s = _silu(c_ref[...]).astype(BF16)
    o_ref[0] = jnp.dot(s, w_ref[0].astype(BF16), preferred_element_type=F32) + b_ref[0]


def _adaln(c_all, w_mod, b_mod):
    nb, d = c_all.shape
    depth, _, n = w_mod.shape
    bn = ADALN_COLS
    return pl.pallas_call(
        _adaln_kernel,
        grid=(depth, n // bn),
        in_specs=[pl.BlockSpec((nb, d), lambda l, j: (0, 0)),
                  pl.BlockSpec((1, d, bn), lambda l, j: (l, 0, j)),
                  pl.BlockSpec((1, 1, bn), lambda l, j: (l, 0, j))],
        out_specs=pl.BlockSpec((1, nb, bn), lambda l, j: (l, 0, j)),
        out_shape=jax.ShapeDtypeStruct((depth, nb, n), F32),
        compiler_params=_cparams(("parallel", "parallel")),
        name="adaln",
    )(c_all, w_mod, b_mod.reshape(depth, 1, n))


def _mod_spec(mod, bm, k):
    if mod.shape[1] == 1:
        return pl.BlockSpec((1, 1, D_MODEL), lambda g, i, *_: (g, 0, k))
    return pl.BlockSpec((1, bm, D_MODEL), lambda g, i, *_: (g, i, k))


def _inproj_kernel(x_ref, g_ref, sh_ref, sc_ref, wt_ref, wdt_ref, o_ref, dt_ref, h_scr, *, n_dt):
    nt = (((1,), (1,)), ((), ()))

    @pl.when(pl.program_id(2) == 0)
    def _():
        _prenorm_to(h_scr, x_ref, g_ref, sc_ref, sh_ref)
        dt = lax.dot_general(h_scr[...], wdt_ref[...].astype(BF16), nt, preferred_element_type=F32)
        lane = lax.broadcasted_iota(jnp.int32, dt.shape, 1)
        dt_ref[0] = jnp.where(lane < n_dt, dt, 0.0)

    o_ref[0] = lax.dot_general(h_scr[...], wt_ref[...].astype(BF16), nt, preferred_element_type=F32)


def _inproj(x, gain, mod, w_t, c_dt, n_dt, bm, bn):
    g_, m, d = x.shape
    n = w_t.shape[1] - n_dt
    row0 = lambda j: pl.multiple_of(jnp.where(j * bn < c_dt, j * bn, j * bn + n_dt), SUBLANES)
    return pl.pallas_call(
        functools.partial(_inproj_kernel, n_dt=n_dt),
        grid=(g_, m // bm, n // bn),
        in_specs=[pl.BlockSpec((1, bm, d), lambda g, i, j: (g, i, 0)),
                  pl.BlockSpec((1, d), lambda g, i, j: (0, 0)),
                  _mod_spec(mod, bm, 0), _mod_spec(mod, bm, 1),
                  pl.BlockSpec((None, pl.Element(bn), pl.Element(d)), lambda g, i, j: (0, row0(j), 0)),
                  pl.BlockSpec((None, pl.Element(LANES), pl.Element(d)), lambda g, i, j: (0, c_dt, 0))],
        out_specs=[pl.BlockSpec((1, bm, bn), lambda g, i, j: (g, i, j)),
                   pl.BlockSpec((1, bm, LANES), lambda g, i, j: (g, i, 0))],
        out_shape=[jax.ShapeDtypeStruct((g_, m, n), F32),
                   jax.ShapeDtypeStruct((g_, m, LANES), F32)],
        scratch_shapes=[pltpu.VMEM((bm, d), BF16)],
        compiler_params=_cparams(("parallel", "parallel", "arbitrary")),
        name="inproj",
    )(x, gain, mod, mod, w_t, w_t)


def _swap_halves(x):
    lane = lax.broadcasted_iota(jnp.int32, x.shape, 1)
    first = (lane % HEAD_DIM) < (HEAD_DIM // 2)
    return jnp.where(first, pltpu.roll(x, LANES - HEAD_DIM // 2, 1), pltpu.roll(x, HEAD_DIM // 2, 1))


def _qkv_kernel(x_ref, g_ref, sh_ref, sc_ref, w_ref, b_ref, cos_ref, sin_ref, o_ref, h_scr, *, rope_cols, bn):
    _prenorm_to(h_scr, x_ref, g_ref, sc_ref, sh_ref)
    h = h_scr[...]
    cos = cos_ref[...]
    sin = sin_ref[...]
    for c0 in range(0, w_ref.shape[1], bn):
        acc = jnp.dot(h, w_ref[:, c0:c0 + bn].astype(BF16), preferred_element_type=F32) + b_ref[:, c0:c0 + bn]
        if c0 < rope_cols:
            for c in range(c0, c0 + bn, LANES):
                a = acc[:, c - c0:c - c0 + LANES]
                o_ref[0, :, c:c + LANES] = a * cos + _swap_halves(a) * sin
        else:
            o_ref[0, :, c0:c0 + bn] = acc


def _qkv(x, gain, mod, w, b, cos_t, sin_t, bm, bn):
    g_, m, d = x.shape
    n = w.shape[1]
    rope_cols = (ATT_HEADS + KV_HEADS) * HEAD_DIM
    rows_per_g = cos_t.shape[0] // bm
    return pl.pallas_call(
        functools.partial(_qkv_kernel, rope_cols=rope_cols, bn=bn),
        grid=(g_, m // bm),
        in_specs=[pl.BlockSpec((1, bm, d), lambda g, i: (g, i, 0)),
                  pl.BlockSpec((1, d), lambda g, i: (0, 0)),
                  _mod_spec(mod, bm, 0), _mod_spec(mod, bm, 1),
                  pl.BlockSpec((d, n), lambda g, i: (0, 0), pipeline_mode=pl.Buffered(1)),
                  pl.BlockSpec((1, n), lambda g, i: (0, 0)),
                  pl.BlockSpec((bm, LANES), lambda g, i: (i % rows_per_g, 0)),
                  pl.BlockSpec((bm, LANES), lambda g, i: (i % rows_per_g, 0))],
        out_specs=pl.BlockSpec((1, bm, n), lambda g, i: (g, i, 0)),
        out_shape=jax.ShapeDtypeStruct((g_, m, n), F32),
        scratch_shapes=[pltpu.VMEM((bm, d), BF16)],
        compiler_params=_cparams(("parallel", "parallel")),
        name="qkv",
    )(x, gain, mod, mod, w, b, cos_t, sin_t)


def _outproj_kernel(*refs, n_lhs):
    lhs = refs[:n_lhs]
    ws = refs[n_lhs:2 * n_lhs]
    x_ref, g_ref, gt_ref, o_ref = refs[2 * n_lhs:]
    acc = jnp.dot(lhs[0][0], ws[0][...].astype(BF16), preferred_element_type=F32)
    for a, w in zip(lhs[1:], ws[1:]):
        acc = acc + jnp.dot(a[0], w[...].astype(BF16), preferred_element_type=F32)
    o_ref[0] = _gated_postnorm(x_ref[0], acc, g_ref, gt_ref)


def _outproj(lhs_list, w, x, gain, mod, bm):
    g_, m, d = x.shape
    n_lhs = len(lhs_list)
    kw = w.shape[0] // n_lhs
    lhs_specs = [pl.BlockSpec((1, bm, kw), lambda g, i: (g, i, 0)) for _ in lhs_list]
    w_specs = [pl.BlockSpec((kw, d), lambda g, i, k=k: (k, 0), pipeline_mode=pl.Buffered(1)) for k in range(n_lhs)]
    w_list = [w] * n_lhs
    if mod.shape[1] == 1:
        gt_spec = pl.BlockSpec((1, 1, d), lambda g, i: (g, 0, 2))
    else:
        gt_spec = pl.BlockSpec((1, bm, d), lambda g, i: (g, i, 2))
    return pl.pallas_call(
        functools.partial(_outproj_kernel, n_lhs=n_lhs),
        grid=(g_, m // bm),
        in_specs=lhs_specs + w_specs + [pl.BlockSpec((1, bm, d), lambda g, i: (g, i, 0)),
                                        pl.BlockSpec((1, d), lambda g, i: (0, 0)), gt_spec],
        out_specs=pl.BlockSpec((1, bm, d), lambda g, i: (g, i, 0)),
        out_shape=jax.ShapeDtypeStruct((g_, m, d), F32),
        compiler_params=_cparams(("parallel", "parallel")),
        name="outproj",
    )(*lhs_list, *w_list, x, gain, mod)


def _mlp_kernel(x_ref, g2_ref, sh_ref, sc_ref, gt_ref, wu_ref, wd_ref, g3_ref, o_ref, h_scr):
    f = pl.program_id(2)

    @pl.when(f == 0)
    def _():
        _prenorm_to(h_scr, x_ref, g2_ref, sc_ref, sh_ref)
        o_ref[0] = jnp.zeros(o_ref.shape[1:], F32)

    u = jnp.dot(h_scr[...], wu_ref[...].astype(BF16), preferred_element_type=F32)
    u = jnp.square(jnp.maximum(u, 0.0)).astype(BF16)
    o_ref[0] += jnp.dot(u, wd_ref[...].astype(BF16), preferred_element_type=F32)

    @pl.when(f == pl.num_programs(2) - 1)
    def _():
        o_ref[0] = _gated_postnorm(x_ref[0], o_ref[0], g3_ref, gt_ref)


def _mlp(x, g2, g3, mod, w_up, w_down, layer, bm, bf):
    g_, m, d = x.shape
    dff = w_up.shape[2]
    if mod.shape[1] == 1:
        mspec = lambda k: pl.BlockSpec((1, 1, d), lambda g, i, f: (g, 0, k))
    else:
        mspec = lambda k: pl.BlockSpec((1, bm, d), lambda g, i, f: (g, i, k))
    return pl.pallas_call(
        _mlp_kernel,
        grid=(g_, m // bm, dff // bf),
        in_specs=[pl.BlockSpec((1, bm, d), lambda g, i, f: (g, i, 0), pipeline_mode=pl.Buffered(1)),
                  pl.BlockSpec((1, d), lambda g, i, f: (0, 0)),
                  mspec(3), mspec(4), mspec(5),
                  pl.BlockSpec((None, d, bf), lambda g, i, f: (layer, 0, f)),
                  pl.BlockSpec((None, bf, d), lambda g, i, f: (layer, f, 0)),
                  pl.BlockSpec((1, d), lambda g, i, f: (0, 0))],
        out_specs=pl.BlockSpec((1, bm, d), lambda g, i, f: (g, i, 0)),
        out_shape=jax.ShapeDtypeStruct((g_, m, d), F32),
        scratch_shapes=[pltpu.VMEM((bm, d), BF16)],
        compiler_params=_cparams(("parallel", "parallel", "arbitrary")),
        name="mlp",
    )(x, g2, mod, mod, mod, w_up, w_down, g3)


def _dwconv(raw, tail, w, b):
    t = raw.shape[0]
    assert w.shape[0] == 4
    ext = jnp.concatenate([tail, raw], axis=0)
    z1 = pltpu.roll(ext, 1, 0)
    near = ext * w[3:4] + z1 * w[2:3]
    far = ext * w[1:2] + z1 * w[0:1]
    y = near + pltpu.roll(far, 2, 0)
    return y[TAIL_ROWS:] + b, ext[t:t + TAIL_ROWS]


def _cumsum_rows(x):
    n = x.shape[0]
    row = lax.broadcasted_iota(jnp.int32, x.shape, 0)
    d = 1
    while d < n:
        x = x + jnp.where(row >= d, pltpu.roll(x, d, 0), 0.0)
        d *= 2
    return x


def _ssd_kernel(z_ref, xs_ref, bc_ref, dt_ref, buf_ref, h0_ref, cw_ref, cb_ref, dtb_ref, alog_ref,
                dsk_ref, ng_ref, exp_ref, y_ref, st_ref, tail_scr, *, q_valid, q, starts_step):
    if starts_step:
        @pl.when(pl.program_id(1) == 0)
        def _():
            tail_scr[...] = buf_ref[0]
            st_ref[0] = h0_ref[0]

    def padded(v):
        if q_valid == q:
            return v
        return jnp.concatenate([v, jnp.zeros((q - q_valid, v.shape[1]), v.dtype)], axis=0)

    def transposed(v):
        if q == LANES:
            return v.T
        return jnp.concatenate([v, jnp.zeros((LANES - q, LANES), v.dtype)], axis=0).T[:, :q]

    cw = cw_ref[...]
    cb = cb_ref[...]
    xs, tail_x = _dwconv(xs_ref[0], tail_scr[:, :SSM_INNER], cw[:, :SSM_INNER], cb[:, :SSM_INNER])
    bcm, tail_b = _dwconv(bc_ref[0], tail_scr[:, SSM_INNER:], cw[:, SSM_INNER:], cb[:, SSM_INNER:])
    tail_scr[:, :SSM_INNER] = tail_x
    tail_scr[:, SSM_INNER:] = tail_b
    xs = padded(_silu(xs))
    bcm = padded(_silu(bcm))
    z = padded(z_ref[0])

    dt = _softplus(dt_ref[0] + dtb_ref[...])
    dt = padded(dt)
    a = -jnp.exp(alog_ref[...])
    acs = _cumsum_rows(dt * a)
    acs_t = transposed(acs)

    full = jnp.dot(jnp.concatenate(_split3(dt) + _split3(acs), axis=0), exp_ref[...], preferred_element_type=F32)
    dt_full = (full[0:q] + full[q:2 * q]) + full[2 * q:3 * q]
    acs_full = (full[3 * q:4 * q] + full[4 * q:5 * q]) + full[5 * q:6 * q]
    xdt = xs * dt_full
    dec_out = jnp.exp(acs_full)
    xw = (xdt * jnp.exp(acs_full[q - 1:q] - acs_full)).astype(BF16)
    xdt_b = xdt.astype(BF16)
    bc_b = bcm.astype(BF16)

    row = lax.broadcasted_iota(jnp.int32, (q, q), 0)
    col = lax.broadcasted_iota(jnp.int32, (q, q), 1)
    causal = row >= col
    lane = lax.broadcasted_iota(jnp.int32, (q, LANES), 1)
    lo_half = lane < SSM_HEAD_DIM
    nt = (((1,), (1,)), ((), ()))
    tn = (((0,), (0,)), ((), ()))
    n_bc = SSM_GROUPS * SSM_STATE

    groups = range(SSM_GROUPS)
    cols = [slice(g * GROUP_WIDTH, (g + 1) * GROUP_WIDTH) for g in groups]
    b_gs = [bc_b[:, g * SSM_STATE:(g + 1) * SSM_STATE] for g in groups]
    c_gs = [bc_b[:, n_bc + g * SSM_STATE:n_bc + (g + 1) * SSM_STATE] for g in groups]
    h_gs = [st_ref[0, cols[g], :] for g in groups]
    cbms = [lax.dot_general(c_gs[g], b_gs[g], nt, preferred_element_type=F32) for g in groups]
    y_offs = [lax.dot_general(c_gs[g], h_gs[g].astype(BF16), nt, preferred_element_type=F32) for g in groups]
    st_news = [lax.dot_general(xw[:, cols[g]], b_gs[g], tn, preferred_element_type=F32) for g in groups]
    ms = [(cbms[h // HEADS_PER_GROUP]
           * jnp.exp(jnp.where(causal, acs[:, h:h + 1] - acs_t[h:h + 1, :], -jnp.inf))).astype(BF16)
          for h in range(SSM_HEADS)]
    y_diags = []
    for pair in range(SSM_HEADS // 2):
        x_pair = xdt_b[:, pair * LANES:(pair + 1) * LANES]
        zero = jnp.zeros_like(x_pair)
        y_diags.append(jnp.dot(ms[2 * pair], jnp.where(lo_half, x_pair, zero), preferred_element_type=F32)
                       + jnp.dot(ms[2 * pair + 1], jnp.where(lo_half, zero, x_pair), preferred_element_type=F32))
    pairs_per_group = HEADS_PER_GROUP // 2
    for g in groups:
        y_g = y_offs[g] * dec_out[:, cols[g]]
        y_g = y_g + jnp.concatenate(y_diags[g * pairs_per_group:(g + 1) * pairs_per_group], axis=1)
        y_g = y_g + xs[:, cols[g]] * dsk_ref[:, cols[g]]
        y_g = y_g * _silu(z[:, cols[g]])
        y_g = y_g * lax.rsqrt(jnp.mean(y_g * y_g, axis=-1, keepdims=True) + EPS)
        y_g = y_g * ng_ref[:, cols[g]]
        y_ref[0, :, cols[g]] = y_g[:q_valid].astype(y_ref.dtype)
    for h in range(SSM_HEADS):
        g, r = divmod(h, HEADS_PER_GROUP)
        rows = slice(r * SSM_HEAD_DIM, (r + 1) * SSM_HEAD_DIM)
        decay = jnp.exp(acs_t[h:h + 1, q - 1:q])
        st_ref[0, h * SSM_HEAD_DIM:(h + 1) * SSM_HEAD_DIM, :] = h_gs[g][rows] * decay + st_news[g][rows]


def _ssd(proj, dtp, buf8, h0, conv_w, conv_b, dt_bias, a_log, d_full, norm_g, expand, rows):
    b_, l, _ = proj.shape
    chunk = rows
    n_sub = min(SSD_CHUNKS_PER_STEP, l // chunk)
    rows = n_sub * chunk
    nc = l // rows
    nb = _seqs_per_step(b_, nc)
    st_rows = SSM_INNER

    def kernel_fn(z, x, bc, dt, *rest):
        *fixed, y, st, tail = rest
        for s in range(n_sub):
            sub = lambda ref: ref.at[:, pl.ds(s * chunk, chunk)]
            _ssd_kernel(sub(z), sub(x), sub(bc), sub(dt), *fixed, sub(y), st, tail,
                        q_valid=chunk, q=max(chunk, BF16_SUBLANES), starts_step=s == 0)

    return pl.pallas_call(
        _per_sequence(kernel_fn, nb, [True] * 6 + [False] * 7 + [True] * 2 + [False]),
        grid=(b_ // nb, nc),
        in_specs=[pl.BlockSpec((nb, rows, SSM_INNER), lambda b, c: (b, c, 0)),
                  pl.BlockSpec((nb, rows, SSM_INNER), lambda b, c: (b, c, 1)),
                  pl.BlockSpec((nb, rows, SSM_INNER), lambda b, c: (b, c, 2)),
                  pl.BlockSpec((nb, rows, LANES), lambda b, c: (b, c, 0)),
                  pl.BlockSpec((nb, TAIL_ROWS, SSM_CONV_DIM), lambda b, c: (b, 0, 0)),
                  pl.BlockSpec((nb, st_rows, SSM_STATE), lambda b, c: (b, 0, 0)),
                  pl.BlockSpec((SSM_CONV, SSM_CONV_DIM), lambda b, c: (0, 0)),
                  pl.BlockSpec((1, SSM_CONV_DIM), lambda b, c: (0, 0)),
                  pl.BlockSpec((1, LANES), lambda b, c: (0, 0)),
                  pl.BlockSpec((1, LANES), lambda b, c: (0, 0)),
                  pl.BlockSpec((1, SSM_INNER), lambda b, c: (0, 0)),
                  pl.BlockSpec((1, SSM_INNER), lambda b, c: (0, 0)),
                  pl.BlockSpec((LANES, SSM_INNER), lambda b, c: (0, 0))],
        out_specs=[pl.BlockSpec((nb, rows, SSM_INNER), lambda b, c: (b, c, 0)),
                   pl.BlockSpec((nb, st_rows, SSM_STATE), lambda b, c: (b, 0, 0))],
        out_shape=[jax.ShapeDtypeStruct((b_, l, SSM_INNER), BF16),
                   jax.ShapeDtypeStruct((b_, st_rows, SSM_STATE), F32)],
        scratch_shapes=[pltpu.VMEM((TAIL_ROWS, SSM_CONV_DIM), F32)],
        compiler_params=_cparams(("parallel", "arbitrary")),
        name="ssd",
    )(proj, proj, proj, dtp, buf8, h0, conv_w, conv_b, dt_bias, a_log, d_full, norm_g, expand)


SCAN_SEG = 4
SCAN_ROWS = SUBLANES * SCAN_SEG


def _shift_rows_fill(x, d, fill, row):
    return jnp.where(row >= d, pltpu.roll(x, d, 0), fill)


def _sublane_scan(e, pe, row):
    d = 1
    while d < SUBLANES:
        e = e + pe * _shift_rows_fill(e, d, 0.0, row)
        pe = pe * _shift_rows_fill(pe, d, 1.0, row)
        d *= 2
    return e, pe


def _linear_scan(a_scr, u_scr, h0):
    n_slab, t, _ = a_scr.shape
    seg = SCAN_SEG if t % SCAN_ROWS == 0 else 1
    row = lax.broadcasted_iota(jnp.int32, (SUBLANES, LANES), 0)

    def block(i, h_prev):
        base = pl.multiple_of(i * (SUBLANES * seg), SUBLANES)
        out = []
        for c in range(n_slab):
            rows = [pl.ds(base + j, SUBLANES, stride=seg) if seg > 1 else pl.ds(base, SUBLANES) for j in range(seg)]
            hs, ps = [], []
            for j in range(seg):
                a, u = a_scr[c, rows[j], :], u_scr[c, rows[j], :]
                hs.append(u if j == 0 else a * hs[-1] + u)
                ps.append(a if j == 0 else a * ps[-1])
            e, pe = _sublane_scan(hs[-1], ps[-1], row)
            g = e + pe * h_prev[c]
            carry = jnp.where(row == 0, h_prev[c], pltpu.roll(g, 1, 0))
            for j in range(seg):
                u_scr[c, rows[j], :] = hs[j] + ps[j] * carry
            out.append(jnp.broadcast_to(g[SUBLANES - 1:], g.shape))
        return tuple(out)

    h_init = tuple(jnp.broadcast_to(h0[:, c * LANES:(c + 1) * LANES], (SUBLANES, LANES)) for c in range(n_slab))
    h_fin = lax.fori_loop(0, t // (SUBLANES * seg), block, h_init)
    return jnp.concatenate([h[:1] for h in h_fin], axis=1)


def _gelu_tanh(x):
    c0 = math.sqrt(2.0 / math.pi)
    hx = 0.5 * x
    return hx + hx * jnp.tanh(x * (c0 + (c0 * 0.044715) * (x * x)))


def _lru_kernel(gate_ref, xr_ref, buf_ref, h0_ref, cw_ref, cb_ref, wa_ref, ba_ref, wx_ref, bx_ref,
                lam_ref, y_ref, hl_ref, tail_scr, a_scr, u_scr):
    c = pl.program_id(1)

    @pl.when(c == 0)
    def _():
        tail_scr[...] = buf_ref[0]
        hl_ref[0] = h0_ref[0]

    xc, tail = _dwconv(xr_ref[0], tail_scr[...], cw_ref[...], cb_ref[...])
    tail_scr[...] = tail
    sp = _softplus(-lam_ref[...])
    slabs_per_block = LRU_BLOCK // LANES
    for k in range(LRU_BLOCKS):
        k0 = k * LRU_BLOCK
        xk = xc[:, k0:k0 + LRU_BLOCK]
        xkb = xk.astype(BF16)
        gr = _sigmoid(jnp.dot(xkb, wa_ref[k], preferred_element_type=F32) + ba_ref[:, k0:k0 + LRU_BLOCK])
        gi = _sigmoid(jnp.dot(xkb, wx_ref[k], preferred_element_type=F32) + bx_ref[:, k0:k0 + LRU_BLOCK])
        log_a = -LRU_C * gr * sp[:, k0:k0 + LRU_BLOCK]
        a = jnp.exp(log_a)
        u = _sqrt_neg_expm1(2.0 * log_a) * gi * xk
        for s in range(slabs_per_block):
            a_scr[k * slabs_per_block + s] = a[:, s * LANES:(s + 1) * LANES]
            u_scr[k * slabs_per_block + s] = u[:, s * LANES:(s + 1) * LANES]

    hl_ref[0] = _linear_scan(a_scr, u_scr, hl_ref[0])
    for s in range(a_scr.shape[0]):
        gate = gate_ref[0, :, s * LANES:(s + 1) * LANES]
        y_ref[0, :, s * LANES:(s + 1) * LANES] = (u_scr[s] * _gelu_tanh(gate)).astype(y_ref.dtype)


def _lru(proj, buf8, h0, conv_w, conv_b, w_a, b_a, w_x, b_x, lam, rows, col0):
    b_, l, _ = proj.shape
    w = LRU_WIDTH
    nb = _seqs_per_step(b_, l // rows)
    full2 = lambda shape: pl.BlockSpec(shape, lambda b, c: (0,) * len(shape))
    return pl.pallas_call(
        _per_sequence(_lru_kernel, nb, [True] * 4 + [False] * 7 + [True] * 2 + [False] * 3),
        grid=(b_ // nb, l // rows),
        in_specs=[pl.BlockSpec((nb, rows, w), lambda b, c: (b, c, col0)),
                  pl.BlockSpec((nb, rows, w), lambda b, c: (b, c, col0 + 1)),
                  pl.BlockSpec((nb, TAIL_ROWS, w), lambda b, c: (b, 0, 0)),
                  pl.BlockSpec((nb, 1, w), lambda b, c: (b, 0, 0)),
                  full2((SSM_CONV, w)), full2((1, w)),
                  full2((LRU_BLOCKS, LRU_BLOCK, LRU_BLOCK)), full2((1, w)),
                  full2((LRU_BLOCKS, LRU_BLOCK, LRU_BLOCK)), full2((1, w)),
                  full2((1, w))],
        out_specs=[pl.BlockSpec((nb, rows, w), lambda b, c: (b, c, 0)),
                   pl.BlockSpec((nb, 1, w), lambda b, c: (b, 0, 0))],
        out_shape=[jax.ShapeDtypeStruct((b_, l, w), BF16),
                   jax.ShapeDtypeStruct((b_, 1, w), F32)],
        scratch_shapes=[pltpu.VMEM((TAIL_ROWS, w), F32), pltpu.VMEM((w // LANES, rows, LANES), F32),
                        pltpu.VMEM((w // LANES, rows, LANES), F32)],
        compiler_params=_cparams(("parallel", "arbitrary")),
        name="lru",
    )(proj, proj, buf8, h0, conv_w, conv_b, w_a, b_a, w_x, b_x, lam)


def _head_variants(slab, half):
    lane = lax.broadcasted_iota(jnp.int32, slab.shape, 1)
    keep = (lane < HEAD_DIM) if half == 0 else (lane >= HEAD_DIM)
    own = jnp.where(keep, slab, 0.0)
    other = pltpu.roll(own, HEAD_DIM, 1)
    return (own, other) if half == 0 else (other, own)


def _slab_attention(q_ref, o_ref, k2, v2, valid, sink_ref):
    w = q_ref.shape[1]
    nk = k2.shape[0]
    nt = (((1,), (1,)), ((), ()))
    scale = HEAD_DIM ** -0.5
    ones = jnp.ones((nk, LANES), BF16)
    slabs = range(nk // LANES)
    heads = range(KV_HEADS)
    halves = [(e, c) for e in heads for c in range(2)]
    vs, ss = [], []
    for e in heads:
        slab, half = divmod(e, 2)
        k_lo, k_hi = _head_variants(k2[:, slab * LANES:(slab + 1) * LANES], half)
        vs.extend(v.astype(BF16) for v in _head_variants(v2[:, slab * LANES:(slab + 1) * LANES], half))
        kk = jnp.concatenate([k_lo, k_hi], axis=0).astype(BF16)
        c0 = 2 * e * LANES
        qq = jnp.concatenate([q_ref[0, :, c0:c0 + LANES], q_ref[0, :, c0 + LANES:c0 + 2 * LANES]], axis=0)
        ss.append(lax.dot_general((qq * scale).astype(BF16), kk, nt, preferred_element_type=F32))
    scs = [jnp.where(valid, ss[e][:, c * nk:(c + 1) * nk], -jnp.inf) for e, c in halves]
    sinks = [jnp.concatenate([jnp.full((w, LANES), sink_ref[Q_PER_KV * e + c], F32),
                              jnp.full((w, LANES), sink_ref[Q_PER_KV * e + 2 + c], F32)], axis=0) for e, c in halves]
    ms = [jnp.maximum(jnp.broadcast_to(jnp.max(sc, axis=1, keepdims=True), sk.shape), sk)
          for sc, sk in zip(scs, sinks)]
    ps = [jnp.concatenate([jnp.exp(sc[:, i * LANES:(i + 1) * LANES] - m) for i in slabs], axis=1).astype(BF16)
          for sc, m in zip(scs, ms)]
    dens = [jnp.dot(p, ones, preferred_element_type=F32) + jnp.exp(sk - m) for p, sk, m in zip(ps, sinks, ms)]
    outs = [jnp.dot(p, v, preferred_element_type=F32) * (1.0 / den) for p, v, den in zip(ps, vs, dens)]
    for e in heads:
        o = outs[2 * e] + outs[2 * e + 1]
        c0 = 2 * e * LANES
        o_ref[0, :, c0:c0 + LANES] = o[:w].astype(o_ref.dtype)
        o_ref[0, :, c0 + LANES:c0 + 2 * LANES] = o[w:].astype(o_ref.dtype)


ATTN_BLOCKS = 2


def _attn_prompt_kernel(sink_ref, q_ref, kc_ref, kp_ref, vc_ref, vp_ref, o_ref):
    w = WINDOW
    has_prev = pl.program_id(1) > 0
    k_prev = jnp.where(has_prev, kp_ref[0], 0.0)
    v_prev = jnp.where(has_prev, vp_ref[0], 0.0)
    t = lax.broadcasted_iota(jnp.int32, (2 * w, 2 * w), 0) % w
    j = lax.broadcasted_iota(jnp.int32, (2 * w, 2 * w), 1)
    valid = (j > t) & (j <= t + w)
    for s in range(q_ref.shape[1] // w):
        rows = pl.ds(s * w, w)
        k_cur, v_cur = kc_ref[0, rows, :], vc_ref[0, rows, :]
        _slab_attention(q_ref.at[:, rows], o_ref.at[:, rows], jnp.concatenate([k_prev, k_cur], axis=0),
                        jnp.concatenate([v_prev, v_cur], axis=0), valid, sink_ref)
        k_prev, v_prev = k_cur, v_cur


def _attn_prompt(qkv, sinks):
    b_, l, _ = qkv.shape
    w = WINDOW
    rows = ATTN_BLOCKS * w
    qw = ATT_HEADS * HEAD_DIM
    kvw = KV_HEADS * HEAD_DIM
    k_blk = qw // kvw
    prev = lambda n: jnp.maximum(ATTN_BLOCKS * n - 1, 0)
    return pl.pallas_call(
        _attn_prompt_kernel,
        grid=(b_, l // rows),
        in_specs=[pl.BlockSpec(memory_space=pltpu.SMEM),
                  pl.BlockSpec((1, rows, qw), lambda b, n: (b, n, 0)),
                  pl.BlockSpec((1, rows, kvw), lambda b, n: (b, n, k_blk)),
                  pl.BlockSpec((1, w, kvw), lambda b, n: (b, prev(n), k_blk)),
                  pl.BlockSpec((1, rows, kvw), lambda b, n: (b, n, k_blk + 1)),
                  pl.BlockSpec((1, w, kvw), lambda b, n: (b, prev(n), k_blk + 1))],
        out_specs=pl.BlockSpec((1, rows, qw), lambda b, n: (b, n, 0)),
        out_shape=jax.ShapeDtypeStruct((b_, l, qw), BF16),
        compiler_params=_cparams(("parallel", "parallel")),
        name="attn_prompt",
    )(sinks, qkv, qkv, qkv, qkv, qkv)


def _attn_sample_kernel(sink_ref, qkv_ref, ck_ref, cv_ref, o_ref, nk_ref, nv_ref):
    l = qkv_ref.shape[1]
    n_keep = ck_ref.shape[1]
    kcol = ATT_HEADS * HEAD_DIM
    vcol = kcol + KV_HEADS * HEAD_DIM
    qkv = qkv_ref[0]
    k_all = jnp.concatenate([ck_ref[0], qkv[:, kcol:vcol]], axis=0)
    v_all = jnp.concatenate([cv_ref[0], qkv[:, vcol:]], axis=0)
    nk_ref[0] = k_all[l:]
    nv_ref[0] = v_all[l:]
    nk = -(-(n_keep + l) // LANES) * LANES
    pad = jnp.zeros((nk - n_keep - l, k_all.shape[1]), F32)
    t = lax.broadcasted_iota(jnp.int32, (2 * l, nk), 0) % l
    j = lax.broadcasted_iota(jnp.int32, (2 * l, nk), 1)
    rel = jnp.where(j < n_keep, t + n_keep - j, t - (j - n_keep))
    valid = (rel >= 0) & (rel < WINDOW) & (j < n_keep + l)
    _slab_attention(qkv_ref, o_ref, jnp.concatenate([k_all, pad], axis=0), jnp.concatenate([v_all, pad], axis=0),
                    valid, sink_ref)


def _attn_sample(qkv, cache_k, cache_v, sinks):
    b_, l, n = qkv.shape
    n_keep, kvw = cache_k.shape[1:]
    nb = _seqs_per_step(b_, 1)
    return pl.pallas_call(
        _per_sequence(_attn_sample_kernel, nb, [False] + [True] * 6),
        grid=(b_ // nb,),
        in_specs=[pl.BlockSpec(memory_space=pltpu.SMEM),
                  pl.BlockSpec((nb, l, n), lambda b: (b, 0, 0)),
                  pl.BlockSpec((nb, n_keep, kvw), lambda b: (b, 0, 0)),
                  pl.BlockSpec((nb, n_keep, kvw), lambda b: (b, 0, 0))],
        out_specs=[pl.BlockSpec((nb, l, ATT_HEADS * HEAD_DIM), lambda b: (b, 0, 0)),
                   pl.BlockSpec((nb, n_keep, kvw), lambda b: (b, 0, 0)),
                   pl.BlockSpec((nb, n_keep, kvw), lambda b: (b, 0, 0))],
        out_shape=[jax.ShapeDtypeStruct((b_, l, ATT_HEADS * HEAD_DIM), BF16),
                   jax.ShapeDtypeStruct((b_, n_keep, kvw), F32),
                   jax.ShapeDtypeStruct((b_, n_keep, kvw), F32)],
        compiler_params=_cparams(("parallel",)),
        name="attn_sample",
    )(sinks, qkv, cache_k, cache_v)


def _rope_tables(pos):
    half = HEAD_DIM // 2
    inv_freq = ROPE_THETA ** (-jnp.arange(half, dtype=F32) / half)
    ang = pos.astype(F32)[:, None] * inv_freq[None, :]
    cos = jnp.cos(ang)
    sin = jnp.sin(ang)
    reps = LANES // HEAD_DIM
    return (jnp.tile(jnp.concatenate([cos, cos], axis=1), (1, reps)),
            jnp.tile(jnp.concatenate([-sin, sin], axis=1), (1, reps)))


def _tail_pad(buf):
    return jnp.pad(buf, ((0, 0), (TAIL_ROWS - buf.shape[1], 0), (0, 0)))


def _trunk(x, mods, pos, st, pw, seq_len):
    g_, m, d = x.shape
    nseq = g_ * m // seq_len
    blk = _block_rows(m)
    chunk = min(seq_len, SSM_CHUNK)
    lru_rows = min(seq_len, LRU_ROWS)

    proj, dtp = _inproj(x, pw['norms'][0, 0:1], mods[0], pw['w_in_t'], SSM_INNER + SSM_CONV_DIM, SSM_HEADS,
                        blk['inproj'], INPROJ_COLS)
    n_main = proj.shape[-1]
    proj_s = proj.reshape(nseq, seq_len, n_main)
    dtp_s = dtp.reshape(nseq, seq_len, LANES)
    y_ssm, ssm_h = _ssd(proj_s, dtp_s, _tail_pad(st['ssm_conv']), st['ssm'].reshape(nseq, SSM_INNER, SSM_STATE),
                        pw['ssm_conv_w'], pw['ssm_conv_b'], pw['ssm_dt_bias'], pw['ssm_a_log'], pw['ssm_d_full'],
                        pw['ssm_norm'], pw['expand'], chunk)
    y_lru, lru_h = _lru(proj_s, _tail_pad(st['lru_conv']), st['lru'].reshape(nseq, 1, LRU_WIDTH),
                        pw['lru_conv_w'], pw['lru_conv_b'], pw['lru_w_a'], pw['lru_b_a'], pw['lru_w_x'],
                        pw['lru_b_x'], pw['lru_lambda'], lru_rows, 3)
    keep = SSM_CONV - 1
    new_ssm_conv = proj_s[:, seq_len - keep:, SSM_INNER:SSM_INNER + SSM_CONV_DIM]
    new_lru_conv = proj_s[:, seq_len - keep:, n_main - LRU_WIDTH:]
    x = _outproj([y_ssm.reshape(g_, m, SSM_INNER), y_lru.reshape(g_, m, LRU_WIDTH)],
                 pw['w_out_hyb'], x, pw['norms'][0, 1:2], mods[0], blk['outproj'])
    x = _mlp(x, pw['norms'][0, 2:3], pw['norms'][0, 3:4], mods[0], pw['w_up'], pw['w_down'], 0, blk['mlp'],
             MLP_COLS)

    cos_t, sin_t = _rope_tables(pos)
    if seq_len < blk['qkv']:
        cos_t = jnp.tile(cos_t, (blk['qkv'] // seq_len, 1))
        sin_t = jnp.tile(sin_t, (blk['qkv'] // seq_len, 1))
    qkv = _qkv(x, pw['norms'][1, 0:1], mods[1], pw['w_qkv'], pw['b_qkv'], cos_t, sin_t, blk['qkv'], QKV_COLS)
    qkv_s = qkv.reshape(nseq, seq_len, qkv.shape[-1])
    kcol = ATT_HEADS * HEAD_DIM
    vcol = kcol + KV_HEADS * HEAD_DIM
    if st['k'] is None:
        o = _attn_prompt(qkv_s, pw['attn_sinks'])
        n_keep = min(WINDOW, seq_len)
        k_new = qkv_s[:, seq_len - n_keep:, kcol:vcol]
        v_new = qkv_s[:, seq_len - n_keep:, vcol:]
    else:
        n_keep = st['k'].shape[1]
        o, k_new, v_new = _attn_sample(qkv_s, st['k'].reshape(nseq, n_keep, KV_HEADS * HEAD_DIM),
                                       st['v'].reshape(nseq, n_keep, KV_HEADS * HEAD_DIM), pw['attn_sinks'])
    k_new = k_new.reshape(nseq, n_keep, KV_HEADS, HEAD_DIM)
    v_new = v_new.reshape(nseq, n_keep, KV_HEADS, HEAD_DIM)
    x = _outproj([o.reshape(g_, m, kcol)], pw['w_out_attn'], x, pw['norms'][1, 1:2], mods[1], blk['outproj'])
    x = _mlp(x, pw['norms'][1, 2:3], pw['norms'][1, 3:4], mods[1], pw['w_up'], pw['w_down'], 1, blk['mlp'],
             MLP_COLS)

    states = (new_ssm_conv[None], ssm_h.reshape(1, nseq, SSM_HEADS, SSM_HEAD_DIM, SSM_STATE),
              new_lru_conv[None], lru_h.reshape(1, nseq, LRU_WIDTH), k_new[None], v_new[None])
    return x, states


def kernel(x_prompt, x_sample, state_ssm_conv, state_ssm, state_lru_conv, state_lru, cache_k, cache_v, c_prompt, c_sample, w_mod, b_mod, norms, w_in_hyb, ssm_conv_w, ssm_conv_b, ssm_dt_bias, ssm_a_log, ssm_d, ssm_norm, lru_conv_w, lru_conv_b, lru_w_a, lru_b_a, lru_w_x, lru_b_x, lru_lambda, w_out_hyb, w_qkv, b_qkv, attn_sinks, w_out_attn, w_up, w_down):
    bp, lp, d = x_prompt.shape
    bs, ls, _ = x_sample.shape
    depth = w_mod.shape[0]

    pad_lanes = lambda v: jnp.pad(v.reshape(1, -1), ((0, 0), (0, LANES - v.size)))
    head_of_lane = jnp.arange(SSM_INNER) // SSM_HEAD_DIM
    pw = {
        'norms': norms,
        'w_in_t': jnp.swapaxes(w_in_hyb, 1, 2),
        'ssm_conv_w': ssm_conv_w[0], 'ssm_conv_b': ssm_conv_b[0].reshape(1, -1),
        'ssm_dt_bias': pad_lanes(ssm_dt_bias[0]), 'ssm_a_log': pad_lanes(ssm_a_log[0]),
        'ssm_d_full': jnp.repeat(ssm_d[0], SSM_HEAD_DIM).reshape(1, -1),
        'ssm_norm': ssm_norm[0].reshape(1, -1),
        'expand': (jnp.arange(LANES)[:, None] == head_of_lane[None, :]).astype(BF16),
        'lru_conv_w': lru_conv_w[0], 'lru_conv_b': lru_conv_b[0].reshape(1, -1),
        'lru_w_a': lru_w_a[0].astype(BF16), 'lru_b_a': lru_b_a[0].reshape(1, -1),
        'lru_w_x': lru_w_x[0].astype(BF16), 'lru_b_x': lru_b_x[0].reshape(1, -1),
        'lru_lambda': lru_lambda[0].reshape(1, -1),
        'w_out_hyb': w_out_hyb[0].astype(BF16),
        'w_qkv': w_qkv[0], 'b_qkv': b_qkv[0].reshape(1, -1),
        'attn_sinks': attn_sinks[0],
        'w_out_attn': w_out_attn[0],
        'w_up': w_up, 'w_down': w_down,
    }

    mod = _adaln(jnp.concatenate([c_prompt, c_sample], axis=0), w_mod, b_mod)
    mods_p = [mod[l, :bp][:, None, :] for l in range(depth)]
    mods_s = [jnp.repeat(mod[l, bp:], ls, axis=0)[None] for l in range(depth)]

    zeros = lambda *shape: jnp.zeros(shape, F32)
    st_p = {'ssm_conv': zeros(bp, SSM_CONV - 1, SSM_CONV_DIM), 'ssm': zeros(bp, SSM_INNER, SSM_STATE),
            'lru_conv': zeros(bp, SSM_CONV - 1, LRU_WIDTH), 'lru': zeros(bp, LRU_WIDTH), 'k': None, 'v': None}
    st_s = {'ssm_conv': state_ssm_conv[0], 'ssm': state_ssm[0], 'lru_conv': state_lru_conv[0],
            'lru': state_lru[0], 'k': cache_k[0], 'v': cache_v[0]}

    pos_p = jnp.arange(lp, dtype=jnp.int32)
    pos_s = PAST_LEN + jnp.arange(ls, dtype=jnp.int32)
    y_p, sp = _trunk(x_prompt, mods_p, pos_p, st_p, pw, lp)
    y_s, ss = _trunk(x_sample.reshape(1, bs * ls, d), mods_s, pos_s, st_s, pw, ls)
    return (y_p, y_s.reshape(bs, ls, d)) + sp + ss
```
